```python
import math
import jax
import jax.numpy as jnp
from jax import lax
import numpy as np

D_MODEL = 2048
BATCH = 16
SEQ = 256
DEPTH = 2
DEC_BATCH = 8
DEC_SEQ = 1024
PAST_LEN = 512

GRID_W = 64
N_EVEN = (DEPTH + 1) // 2
N_ODD = DEPTH // 2
MIX_W = D_MODEL
HEAD_DIM = 128
HY_W = MIX_W // 2
HY_ORDER = 2
SHORT_CONV = 3
POS_EMB = 33
FILTER_ORDER = 64
HY_FAST_DECAY = 0.3
HY_SLOW_DECAY = 1.5
HY_TARGET = 1e-2
ATT_HEADS = (MIX_W // 2) // HEAD_DIM
ATT_KV_HEADS = ATT_HEADS // 4
ATT_GROUP = ATT_HEADS // ATT_KV_HEADS
WINDOW = 128
BLK = 128
EV_IN = (HY_ORDER + 1) * HY_W + (ATT_HEADS + 2 * ATT_KV_HEADS) * HEAD_DIM
HG_W = MIX_W // 2
HG_HEADS = HG_W // HEAD_DIM
HG_DK = HEAD_DIM
HG_DV = HEAD_DIM
CHUNK = 64
Q_LORA = 512
KV_LORA = 256
NOPE = 128
ROPE = 64
V_DIM = 128
MLA_HEADS = (MIX_W // 2) // V_DIM
OD_IN = 5 * HG_W + Q_LORA + KV_LORA + ROPE
D_FF = 5632
MACARON_W = 0.5
N_MOD = 9
ROPE_BASE = 10000.0
EPS = 1e-6
QBLK = 128
ATT_SCALE = HEAD_DIM ** -0.5
MLA_SCALE = (NOPE + ROPE) ** -0.5

kernel_name = 'hybrid_diffusion_prefix_step'


def _rmsnorm(x, g):
    xf = x.astype(jnp.float32)
    y = xf * lax.rsqrt(jnp.mean(xf * xf, axis=-1, keepdims=True) + EPS)
    return (y * g.astype(jnp.float32)).astype(x.dtype)


def _modulation(cvec, w, b):
    m = jax.nn.silu(cvec) @ w + b
    return m.reshape(cvec.shape[0], N_MOD, 1, D_MODEL)


def _modulate(x, g, mods, j):
    return _rmsnorm(x, g) * (1.0 + mods[:, 3 * j + 1]) + mods[:, 3 * j]


def _residual(x, out, g, mods, j, w):
    return x + w * mods[:, 3 * j + 2] * _rmsnorm(out.astype(x.dtype), g)


def _ffn_sublayer(x, mods, j, g_pre, g_post, wg, wu, wd):
    h = _modulate(x, g_pre, mods, j)
    out = (jax.nn.silu(h @ wg) * (h @ wu)) @ wd
    return _residual(x, out, g_post, mods, j, MACARON_W)


def _axial_angles(L, rot_dim):
    rows = L // GRID_W
    half = rot_dim // 2
    inv = ROPE_BASE ** (-jnp.arange(0, half, 2, dtype=jnp.float32) / half)
    row = jnp.repeat(jnp.arange(rows), GRID_W).astype(jnp.float32)
    col = jnp.tile(jnp.arange(GRID_W), rows).astype(jnp.float32)
    return row[:, None] * inv, col[:, None] * inv


def _rot_half(x, ang):
    x1, x2 = jnp.split(x, 2, axis=-1)
    cos = jnp.cos(ang).astype(x.dtype)
    sin = jnp.sin(ang).astype(x.dtype)
    return jnp.concatenate([x1 * cos - x2 * sin, x2 * cos + x1 * sin], axis=-1)


def _rope2d(x):
    row, col = _axial_angles(x.shape[-2], x.shape[-1])
    xr, xc = jnp.split(x, 2, axis=-1)
    return jnp.concatenate([_rot_half(xr, row), _rot_half(xc, col)], axis=-1)


def _probs(s, sink):
    if sink is None:
        return jax.nn.softmax(s, axis=-1)
    sk = sink.astype(jnp.float32)[None, :, :, None, None]
    m = jnp.maximum(jnp.max(s, axis=-1, keepdims=True), sk)
    p = jnp.exp(s - m)
    return p / (jnp.sum(p, axis=-1, keepdims=True) + jnp.exp(sk - m))


def _attend_dense(q, k, v, scale, sink=None):
    B, Hk, G, Lq, dq = q.shape
    nb = Lq // QBLK
    qb = jnp.moveaxis(q.reshape(B, Hk, G, nb, QBLK, dq), 3, 0)

    def one(qi):
        s = jnp.einsum('bhgqd,bhkd->bhgqk', qi, k).astype(jnp.float32) * scale
        p = _probs(s, sink)
        return jnp.einsum('bhgqk,bhkd->bhgqd', p.astype(v.dtype), v)

    o = lax.map(one, qb)
    return jnp.moveaxis(o, 0, 3).reshape(B, Hk, G, Lq, v.shape[-1])


def _attend_window_ctx(q, k, v, k_ctx, v_ctx, sink, scale):
    B, Hk, G, L, d = q.shape
    nb = L // BLK

    def band(a):
        ap = jnp.pad(a, ((0, 0), (0, 0), (BLK, BLK), (0, 0))).reshape(B, Hk, nb + 2, BLK, a.shape[-1])
        return jnp.concatenate([ap[:, :, :-2], ap[:, :, 1:-1], ap[:, :, 2:]], axis=3)

    kb, vb = band(k), band(v)
    jb = jnp.arange(nb)[:, None, None]
    qpos = jb * BLK + jnp.arange(BLK)[None, :, None]
    kpos = (jb - 1) * BLK + jnp.arange(3 * BLK)[None, None, :]
    mask = (jnp.abs(kpos - qpos) <= WINDOW) & (kpos >= 0) & (kpos < L)
    xs = (jnp.moveaxis(q.reshape(B, Hk, G, nb, BLK, d), 3, 0), jnp.moveaxis(kb, 2, 0),
          jnp.moveaxis(vb, 2, 0), mask)

    def one(args):
        qi, ki, vi, mk = args
        s_loc = jnp.einsum('bhgqd,bhkd->bhgqk', qi, ki).astype(jnp.float32) * scale
        s_loc = jnp.where(mk, s_loc, -jnp.inf)
        s_ctx = jnp.einsum('bhgqd,bhkd->bhgqk', qi, k_ctx).astype(jnp.float32) * scale
        p = _probs(jnp.concatenate([s_loc, s_ctx], axis=-1), sink)
        p_loc, p_ctx = p[..., :3 * BLK], p[..., 3 * BLK:]
        return (jnp.einsum('bhgqk,bhkd->bhgqd', p_loc.astype(vi.dtype), vi)
                + jnp.einsum('bhgqk,bhkd->bhgqd', p_ctx.astype(v_ctx.dtype), v_ctx))

    o = lax.map(one, xs)
    return jnp.moveaxis(o, 0, 3).reshape(B, Hk, G, L, d)


def _short_conv(u, w, b):
    L = u.shape[1]
    pad = SHORT_CONV // 2
    up = jnp.pad(u, ((0, 0), (pad, pad), (0, 0)))
    out = b
    for j in range(SHORT_CONV):
        out = out + up[:, j:j + L] * w[j]
    return out


def _hyena_filters(L, w1, b1, w2, b2, w3, freq):
    f32 = jnp.float32
    t = jnp.linspace(0.0, 1.0, L, dtype=f32)[:, None]
    bands = (POS_EMB - 1) // 2
    w = 2.0 * math.pi * jnp.arange(L, dtype=f32)[:, None] / L
    fb = jnp.linspace(1e-4, bands - 1, bands, dtype=f32)[None, :]
    z = jnp.concatenate([t, jnp.cos(fb * w), -jnp.sin(fb * w)], axis=-1)
    fr = freq.astype(f32)
    h = jnp.sin(fr * (z @ w1.astype(f32) + b1.astype(f32)))
    h = jnp.sin(fr * (h @ w2.astype(f32) + b2.astype(f32)))
    h = (h @ w3.astype(f32)).reshape(L, HY_ORDER, 2, HY_W)
    deltas = jnp.abs(jnp.linspace(math.log(HY_TARGET) / HY_SLOW_DECAY,
                                  math.log(HY_TARGET) / HY_FAST_DECAY, HY_W, dtype=f32))
    decay = jnp.exp(-t * deltas)
    return jnp.moveaxis(h * decay[:, None, None, :], 0, 2)


def _long_conv(z, hf, hb):
    L = z.shape[1]
    filt = jnp.concatenate([hf, jnp.zeros_like(hf[:1]), hb[:0:-1]], axis=0)
    Z = jnp.fft.rfft(z.astype(jnp.float32), n=2 * L, axis=1)
    K = jnp.fft.rfft(filt, n=2 * L, axis=0)
    return jnp.fft.irfft(Z * K[None], n=2 * L, axis=1)[:, :L]


def _hyena(u, conv_w, conv_b, f_w1, f_b1, f_w2, f_b2, f_w3, f_freq, h_bias):
    L = u.shape[1]
    uc = _short_conv(u, conv_w, conv_b).astype(jnp.float32)
    v, *gates = jnp.split(uc, HY_ORDER + 1, axis=-1)
    h = _hyena_filters(L, f_w1, f_b1, f_w2, f_b2, f_w3, f_freq)
    z = v
    for n in range(HY_ORDER):
        z = gates[n] * (_long_conv(z, h[n, 0], h[n, 1]) + h_bias[n].astype(jnp.float32) * z)
    return z


def _gla_chunk(q, k, v, logf, S0):
    B, H, L, dk = q.shape
    n = L // CHUNK

    def blocks(a):
        return a.reshape(B, H, n, CHUNK, a.shape[-1])

    q, k, v, logf = blocks(q), blocks(k), blocks(v), blocks(logf)
    b = jnp.cumsum(logf, axis=3)
    b_end = b[:, :, :, -1:]
    qd = q * jnp.exp(b)
    kd = k * jnp.exp(-b)
    kend = k * jnp.exp(b_end - b)
    causal = jnp.tril(jnp.ones((CHUNK, CHUNK), dtype=bool))
    A = jnp.where(causal, jnp.einsum('bhncd,bhnsd->bhncs', qd, kd), 0.0)
    o_intra = jnp.einsum('bhncs,bhnsv->bhncv', A, v)

    def step(S, xs):
        qc, kc, vc, dc = xs
        o = jnp.einsum('bhcd,bhdv->bhcv', qc, S)
        S = dc[..., None] * S + jnp.einsum('bhcd,bhcv->bhdv', kc, vc)
        return S, o

    xs = tuple(jnp.moveaxis(a, 2, 0) for a in (qd, kend, v, jnp.exp(b_end[:, :, :, 0])))
    S, o_inter = lax.scan(step, S0.astype(jnp.float32), xs)
    o = o_intra + jnp.moveaxis(o_inter, 0, 2)
    return o.reshape(B, H, L, -1), S


def _heads(a):
    B, L, _ = a.shape
    return a.reshape(B, L, HG_HEADS, -1).transpose(0, 2, 1, 3).astype(jnp.float32)


def _hgrn(q_h, f_f, f_b, i_h, g_h, lb, norm_g, S0):
    q = jax.nn.silu(_heads(q_h))
    v = _heads(i_h)
    o = 0.0
    states = []
    for d, fz in enumerate((f_f, f_b)):
        lbd = lb[d].reshape(HG_HEADS, 1, HG_DK)
        f = lbd + (1.0 - lbd) * jax.nn.sigmoid(_heads(fz))
        args = (q, 1.0 - f, v, jnp.log(f))
        if d == 1:
            args = tuple(a[:, :, ::-1] for a in args)
        od, Sd = _gla_chunk(*args, S0[:, d])
        o = o + (od[:, :, ::-1] if d == 1 else od)
        states.append(Sd)
    o = _rmsnorm(o, norm_g) * jax.nn.silu(_heads(g_h))
    B, H, L, dv = o.shape
    return o.transpose(0, 2, 1, 3).reshape(B, L, H * dv), jnp.stack(states, axis=1)


def _mla_q(q_lat, q_norm, w_qb):
    B, L, _ = q_lat.shape
    q = (_rmsnorm(q_lat, q_norm) @ w_qb).reshape(B, L, MLA_HEADS, NOPE + ROPE).transpose(0, 2, 1, 3)
    return q[..., :NOPE], q[..., NOPE:]


def _mla_kv(ckv, krope, w_kvb):
    B, L, _ = ckv.shape
    kv = (ckv @ w_kvb).reshape(B, L, MLA_HEADS, NOPE + V_DIM).transpose(0, 2, 1, 3)
    kr = jnp.broadcast_to(krope[:, None], (B, MLA_HEADS, L, ROPE)).astype(kv.dtype)
    return jnp.concatenate([kv[..., :NOPE], kr], axis=-1), kv[..., NOPE:]


def _merge(y_a, attn, w_out, dtype):
    B, Hk, G, L, dv = attn.shape
    a = attn.transpose(0, 3, 1, 2, 4).reshape(B, L, Hk * G * dv)
    return jnp.concatenate([y_a.astype(dtype), a.astype(dtype)], axis=-1) @ w_out


def _even_project(h, w_in):
    B, L, _ = h.shape
    p = h @ w_in
    n_hy = (HY_ORDER + 1) * HY_W
    n_q = ATT_HEADS * HEAD_DIM
    n_kv = ATT_KV_HEADS * HEAD_DIM
    u = p[..., :n_hy]
    q = p[..., n_hy:n_hy + n_q].reshape(B, L, ATT_KV_HEADS, ATT_GROUP, HEAD_DIM).transpose(0, 2, 3, 1, 4)
    k = p[..., n_hy + n_q:n_hy + n_q + n_kv].reshape(B, L, ATT_KV_HEADS, HEAD_DIM).transpose(0, 2, 1, 3)
    v = p[..., n_hy + n_q + n_kv:].reshape(B, L, ATT_KV_HEADS, HEAD_DIM).transpose(0, 2, 1, 3)
    return u, q, k, v


def _even_context(h, w_in, w_out, sink, hy_w):
    u, q, k, v = _even_project(h, w_in)
    a = _attend_dense(q, k, v, ATT_SCALE, sink)
    return _merge(_hyena(u, *hy_w), a, w_out, h.dtype), k, v


def _even_latent(h, k_ctx, v_ctx, w_in, w_out, sink, hy_w):
    u, q, k, v = _even_project(h, w_in)
    a = _attend_window_ctx(_rope2d(q), _rope2d(k), v, k_ctx, v_ctx, sink, ATT_SCALE)
    return _merge(_hyena(u, *hy_w), a, w_out, h.dtype)


def _odd_project(h, w_in):
    sizes = [HG_W] * 5 + [Q_LORA, KV_LORA, ROPE]
    idx = np.cumsum(sizes)[:-1].tolist()
    return jnp.split(h @ w_in, idx, axis=-1)


def _odd_context(h, lb, w_in, w_out, hg_g, qn, wqb, kvn, wkvb):
    q_h, f_f, f_b, i_h, g_h, q_lat, kv_lat, k_rope = _odd_project(h, w_in)
    S0 = jnp.zeros((h.shape[0], 2, HG_HEADS, HG_DK, HG_DV), jnp.float32)
    o_hg, S = _hgrn(q_h, f_f, f_b, i_h, g_h, lb, hg_g, S0)
    q_nope, q_rope = _mla_q(q_lat, qn, wqb)
    ckv = _rmsnorm(kv_lat, kvn)
    k, v = _mla_kv(ckv, k_rope, wkvb)
    q = jnp.concatenate([q_nope, q_rope], axis=-1)[:, :, None]
    a = _attend_dense(q, k, v, MLA_SCALE)
    return _merge(o_hg, a, w_out, h.dtype), S, ckv, k_rope


def _odd_latent(h, S_ctx, ckv_ctx, kr_ctx, lb, w_in, w_out, hg_g, qn, wqb, kvn, wkvb):
    q_h, f_f, f_b, i_h, g_h, q_lat, kv_lat, k_rope = _odd_project(h, w_in)
    o_hg, _ = _hgrn(q_h, f_f, f_b, i_h, g_h, lb, hg_g, S_ctx)
    q_nope, q_rope = _mla_q(q_lat, qn, wqb)
    q = jnp.concatenate([q_nope, _rope2d(q_rope)], axis=-1)[:, :, None]
    k_l, v_l = _mla_kv(_rmsnorm(kv_lat, kvn), _rope2d(k_rope), wkvb)
    k_c, v_c = _mla_kv(ckv_ctx, kr_ctx, wkvb)
    a = _attend_dense(q, jnp.concatenate([k_l, k_c.astype(k_l.dtype)], axis=2),
                      jnp.concatenate([v_l, v_c.astype(v_l.dtype)], axis=2), MLA_SCALE)
    return _merge(o_hg, a, w_out, h.dtype)


def setup_inputs(seed: int = 0) -> dict:
    key = jax.random.key(seed)
    ks = iter(jax.random.split(key, 64))
    D = D_MODEL

    def nrm(shape, s):
        return jax.random.normal(next(ks), shape, jnp.float32) * s

    def gain(shape):
        return 1.0 + nrm(shape, 0.05)

    return {
        'x_prompt': nrm((BATCH, SEQ, D), 1.0),
        'x_sample': nrm((DEC_BATCH, DEC_SEQ, D), 1.0),
        'c': nrm((DEC_BATCH, D), 1.0),
        'c_ctx': nrm((D,), 1.0),
        'cache_attn_k': nrm((DEC_BATCH, N_EVEN, ATT_KV_HEADS, PAST_LEN, HEAD_DIM), 1.0),
        'cache_attn_v': nrm((DEC_BATCH, N_EVEN, ATT_KV_HEADS, PAST_LEN, HEAD_DIM), 1.0),
        'cache_mla_ckv': nrm((DEC_BATCH, N_ODD, PAST_LEN, KV_LORA), 1.0),
        'cache_mla_krope': nrm((DEC_BATCH, N_ODD, PAST_LEN, ROPE), 1.0),
        'state_hgrn': nrm((DEC_BATCH, N_ODD, 2, HG_HEADS, HG_DK, HG_DV), 0.5),
        'mod_w': nrm((DEPTH, D, N_MOD * D), 0.5 * D ** -0.5),
        'mod_b': nrm((DEPTH, N_MOD * D), 0.02),
        'norm_g': gain((DEPTH, 6, D)),
        'ffn_wg': nrm((DEPTH, 2, D, D_FF), D ** -0.5),
        'ffn_wu': nrm((DEPTH, 2, D, D_FF), D ** -0.5),
        'ffn_wd': nrm((DEPTH, 2, D_FF, D), D_FF ** -0.5),
        'ev_w_in': nrm((N_EVEN, D, EV_IN), D ** -0.5),
        'ev_w_out': nrm((N_EVEN, MIX_W, D), MIX_W ** -0.5),
        'hy_conv_w': nrm((N_EVEN, SHORT_CONV, (HY_ORDER + 1) * HY_W), SHORT_CONV ** -0.5),
        'hy_conv_b': nrm((N_EVEN, (HY_ORDER + 1) * HY_W), 0.02),
        'hy_f_w1': nrm((N_EVEN, POS_EMB, FILTER_ORDER), POS_EMB ** -0.5),
        'hy_f_b1': nrm((N_EVEN, FILTER_ORDER), 0.1),
        'hy_f_w2': nrm((N_EVEN, FILTER_ORDER, FILTER_ORDER), FILTER_ORDER ** -0.5),
        'hy_f_b2': nrm((N_EVEN, FILTER_ORDER), 0.1),
        'hy_f_w3': nrm((N_EVEN, FILTER_ORDER, HY_ORDER * 2 * HY_W), 0.1 * FILTER_ORDER ** -0.5),
        'hy_f_freq': 1.0 + nrm((N_EVEN, FILTER_ORDER), 0.1),
        'hy_bias': nrm((N_EVEN, HY_ORDER, HY_W), 0.5),
        'attn_sink': nrm((N_EVEN, ATT_HEADS), 0.5),
        'od_w_in': nrm((N_ODD, D, OD_IN), D ** -0.5),
        'od_w_out': nrm((N_ODD, MIX_W, D), MIX_W ** -0.5),
        'hg_lb': nrm((DEPTH, 2, HG_W), 0.1),
        'hg_norm': gain((N_ODD, HG_DV)),
        'mla_q_norm': gain((N_ODD, Q_LORA)),
        'mla_w_qb': nrm((N_ODD, Q_LORA, MLA_HEADS * (NOPE + ROPE)), Q_LORA ** -0.5),
        'mla_kv_norm': gain((N_ODD, KV_LORA)),
        'mla_w_kvb': nrm((N_ODD, KV_LORA, MLA_HEADS * (NOPE + V_DIM)), KV_LORA ** -0.5),
    }


def reference(x_prompt, x_sample, c, c_ctx, cache_attn_k, cache_attn_v, cache_mla_ckv, cache_mla_krope,
              state_hgrn, mod_w, mod_b, norm_g, ffn_wg, ffn_wu, ffn_wd, ev_w_in, ev_w_out, hy_conv_w,
              hy_conv_b, hy_f_w1, hy_f_b1, hy_f_w2, hy_f_b2, hy_f_w3, hy_f_freq, hy_bias, attn_sink,
              od_w_in, od_w_out, hg_lb, hg_norm, mla_q_norm, mla_w_qb, mla_kv_norm, mla_w_kvb):
    lb_all = jnp.cumsum(jax.nn.softmax(hg_lb.astype(jnp.float32), axis=0), axis=0)
    lb_all = lb_all - lb_all[:1]
    xp, xs = x_prompt, x_sample
    new_k, new_v, new_ckv, new_kr, new_s = [], [], [], [], []
    for l in range(DEPTH):
        mp = _modulation(c_ctx[None, :], mod_w[l], mod_b[l])
        ms = _modulation(c, mod_w[l], mod_b[l])
        xp = _ffn_sublayer(xp, mp, 0, norm_g[l, 0], norm_g[l, 1], ffn_wg[l, 0], ffn_wu[l, 0], ffn_wd[l, 0])
        xs = _ffn_sublayer(xs, ms, 0, norm_g[l, 0], norm_g[l, 1], ffn_wg[l, 0], ffn_wu[l, 0], ffn_wd[l, 0])
        hp = _modulate(xp, norm_g[l, 2], mp, 1)
        hs = _modulate(xs, norm_g[l, 2], ms, 1)
        if l % 2 == 0:
            e = l // 2
            hy_w = (hy_conv_w[e], hy_conv_b[e], hy_f_w1[e], hy_f_b1[e], hy_f_w2[e], hy_f_b2[e],
                    hy_f_w3[e], hy_f_freq[e], hy_bias[e])
            sink = attn_sink[e].reshape(ATT_KV_HEADS, ATT_GROUP)
            op, kc, vc = _even_context(hp, ev_w_in[e], ev_w_out[e], sink, hy_w)
            osm = _even_latent(hs, cache_attn_k[:, e], cache_attn_v[:, e], ev_w_in[e], ev_w_out[e], sink, hy_w)
            new_k.append(kc)
            new_v.append(vc)
        else:
            o = l // 2
            ow = (od_w_in[o], od_w_out[o], hg_norm[o], mla_q_norm[o], mla_w_qb[o], mla_kv_norm[o], mla_w_kvb[o])
            op, sc, ckv, kr = _odd_context(hp, lb_all[l], *ow)
            osm = _odd_latent(hs, state_hgrn[:, o], cache_mla_ckv[:, o], cache_mla_krope[:, o], lb_all[l], *ow)
            new_s.append(sc)
            new_ckv.append(ckv)
            new_kr.append(kr)
        xp = _residual(xp, op, norm_g[l, 3], mp, 1, 1.0)
        xs = _residual(xs, osm, norm_g[l, 3], ms, 1, 1.0)
        xp = _ffn_sublayer(xp, mp, 2, norm_g[l, 4], norm_g[l, 5], ffn_wg[l, 1], ffn_wu[l, 1], ffn_wd[l, 1])
        xs = _ffn_sublayer(xs, ms, 2, norm_g[l, 4], norm_g[l, 5], ffn_wg[l, 1], ffn_wu[l, 1], ffn_wd[l, 1])
    return (xp, xs, jnp.stack(new_k, axis=1), jnp.stack(new_v, axis=1), jnp.stack(new_ckv, axis=1),
            jnp.stack(new_kr, axis=1), jnp.stack(new_s, axis=1))
```

```python
import functools
import math

import jax
import jax.numpy as jnp
from jax import lax
from jax.experimental import pallas as pl
from jax.experimental.pallas import tpu as pltpu

D_MODEL = 2048
BATCH = 16
SEQ = 256
DEC_BATCH = 8
DEC_SEQ = 1024
PAST_LEN = 512
GRID_W = 64
HEAD_DIM = 128
HY_W = 1024
HY_ORDER = 2
SHORT_CONV = 3
POS_EMB = 33
FILTER_ORDER = 64
HY_FAST_DECAY = 0.3
HY_SLOW_DECAY = 1.5
HY_TARGET = 1e-2
ATT_HEADS = 8
ATT_KV_HEADS = 2
ATT_GROUP = 4
WINDOW = 128
EV_IN = 3 * HY_W + (ATT_HEADS + 2 * ATT_KV_HEADS) * HEAD_DIM
HG_W = 1024
HG_HEADS = 8
CHUNK = 64
Q_LORA = 512
KV_LORA = 256
NOPE = 128
ROPE = 64
V_DIM = 128
MLA_HEADS = 8
OD_IN = 5 * HG_W + Q_LORA + KV_LORA + ROPE
OD_IN_PAD = 6144
D_FF = 5632
MACARON_W = 0.5
N_MOD = 9
ROPE_BASE = 10000.0
EPS = 1e-6
ATT_SCALE = HEAD_DIM ** -0.5
MLA_SCALE = (NOPE + ROPE) ** -0.5

T_PROMPT = BATCH * SEQ
T_SAMPLE = DEC_BATCH * DEC_SEQ
T_ALL = T_PROMPT + T_SAMPLE
MOD_ROWS = 16

V7X_VMEM_LIMIT = 56 * 1024 * 1024
TM = 512
TF = 512
TN = 512
MOD_TN = 1024

BF16 = jnp.bfloat16
F32 = jnp.float32
HIGHEST = lax.Precision.HIGHEST


def _cparams(*sem):
    return pltpu.CompilerParams(dimension_semantics=sem, vmem_limit_bytes=V7X_VMEM_LIMIT)


def _mod_row(i):
    return jnp.maximum(0, 1 + (i - T_PROMPT // TM) // (DEC_SEQ // TM))


def _rms(x):
    return x * lax.rsqrt(jnp.mean(x * x, axis=-1, keepdims=True) + EPS)


def _silu(x):
    return x * jax.nn.sigmoid(x)


def _dot(a, b):
    return jnp.dot(a, b, preferred_element_type=F32)


def _dot_nt(a, b):
    return lax.dot_general(a, b, (((1,), (1,)), ((), ())), preferred_element_type=F32)


def _dot_tn(a, b):
    return lax.dot_general(a, b, (((0,), (0,)), ((), ())), preferred_element_type=F32)


def _mod_kernel(c_ref, w_ref, b_ref, o_ref):
    s = _silu(c_ref[...]).astype(BF16)
    o_ref[...] = _dot(s, w_ref[...].astype(BF16)) + b_ref[...]


def _modulation(cvec, mod_w, mod_b):
    depth = mod_w.shape[0]
    n = N_MOD * D_MODEL
    out = pl.pallas_call(
        _mod_kernel,
        out_shape=jax.ShapeDtypeStruct((depth, MOD_ROWS, n), F32),
        grid=(depth, n // MOD_TN),
        in_specs=[
            pl.BlockSpec((MOD_ROWS, D_MODEL), lambda l, j: (0, 0)),
            pl.BlockSpec((None, D_MODEL, MOD_TN), lambda l, j: (l, 0, j)),
            pl.BlockSpec((None, 1, MOD_TN), lambda l, j: (l, 0, j)),
        ],
        out_specs=pl.BlockSpec((None, MOD_ROWS, MOD_TN), lambda l, j: (l, 0, j)),
        compiler_params=_cparams("parallel", "parallel"),
        name="modulation",
    )(cvec, mod_w, mod_b.reshape(depth, 1, n))
    return out.reshape(depth, MOD_ROWS, N_MOD, D_MODEL)


def _ffn_kernel(j, x_ref, mod_ref, gpre_ref, gpost_ref, wg_ref, wu_ref, wd_ref, o_ref, h_scr, acc_scr):
    f = pl.program_id(1)

    @pl.when(f == 0)
    def _():
        y = _rms(x_ref[...]) * gpre_ref[...]
        h = y * (1.0 + mod_ref[3 * j + 1:3 * j + 2, :]) + mod_ref[3 * j:3 * j + 1, :]
        h_scr[...] = h.astype(BF16)
        acc_scr[...] = jnp.zeros_like(acc_scr)

    h = h_scr[...]
    g = _dot(h, wg_ref[...])
    u = _dot(h, wu_ref[...])
    a = (_silu(g) * u).astype(BF16)
    acc_scr[...] += _dot(a, wd_ref[...])

    @pl.when(f == pl.num_programs(1) - 1)
    def _():
        out = _rms(acc_scr[...]) * gpost_ref[...]
        o_ref[...] = x_ref[...] + MACARON_W * mod_ref[3 * j + 2:3 * j + 3, :] * out


def _ffn(x, mods, j, g_pre, g_post, wg, wu, wd, l, s):
    T = x.shape[0]
    return pl.pallas_call(
        functools.partial(_ffn_kernel, j),
        out_shape=jax.ShapeDtypeStruct((T, D_MODEL), F32),
        grid=(T // TM, D_FF // TF),
        in_specs=[
            pl.BlockSpec((TM, D_MODEL), lambda i, f: (i, 0)),
            pl.BlockSpec((None, N_MOD, D_MODEL), lambda i, f: (_mod_row(i), 0, 0)),
            pl.BlockSpec((1, D_MODEL), lambda i, f: (0, 0)),
            pl.BlockSpec((1, D_MODEL), lambda i, f: (0, 0)),
            pl.BlockSpec((None, None, D_MODEL, TF), lambda i, f: (l, s, 0, f)),
            pl.BlockSpec((None, None, D_MODEL, TF), lambda i, f: (l, s, 0, f)),
            pl.BlockSpec((None, None, TF, D_MODEL), lambda i, f: (l, s, f, 0)),
        ],
        out_specs=pl.BlockSpec((TM, D_MODEL), lambda i, f: (i, 0)),
        scratch_shapes=[pltpu.VMEM((TM, D_MODEL), BF16), pltpu.VMEM((TM, D_MODEL), F32)],
        compiler_params=_cparams("parallel", "arbitrary"),
        name="ffn",
    )(x, mods, g_pre.reshape(1, D_MODEL), g_post.reshape(1, D_MODEL), wg, wu, wd)


def _inproj_kernel(x_ref, mod_ref, g_ref, w_ref, o_ref, h_scr):
    @pl.when(pl.program_id(1) == 0)
    def _():
        y = _rms(x_ref[...]) * g_ref[...]
        h_scr[...] = (y * (1.0 + mod_ref[4:5, :]) + mod_ref[3:4, :]).astype(BF16)

    o_ref[...] = _dot(h_scr[...], w_ref[...])


def _inproj(x, mods, g, w):
    T = x.shape[0]
    n = w.shape[1]
    return pl.pallas_call(
        _inproj_kernel,
        out_shape=jax.ShapeDtypeStruct((T, n), F32),
        grid=(T // TM, n // TN),
        in_specs=[
            pl.BlockSpec((TM, D_MODEL), lambda i, k: (i, 0)),
            pl.BlockSpec((None, N_MOD, D_MODEL), lambda i, k: (_mod_row(i), 0, 0)),
            pl.BlockSpec((1, D_MODEL), lambda i, k: (0, 0)),
            pl.BlockSpec((D_MODEL, TN), lambda i, k: (0, k)),
        ],
        out_specs=pl.BlockSpec((TM, TN), lambda i, k: (i, k)),
        scratch_shapes=[pltpu.VMEM((TM, D_MODEL), BF16)],
        compiler_params=_cparams("parallel", "arbitrary"),
        name="inproj",
    )(x, mods, g.reshape(1, D_MODEL), w)


def _outproj_kernel(a_ref, b_ref, wa_ref, wb_ref, x_ref, mod_ref, g_ref, o_ref):
    y = _dot(a_ref[...], wa_ref[...]) + _dot(b_ref[...], wb_ref[...])
    o_ref[...] = x_ref[...] + mod_ref[5:6, :] * (_rms(y) * g_ref[...])


def _outproj(a, b, wa, wb, x, mods, g):
    T = x.shape[0]
    half = a.shape[1]
    return pl.pallas_call(
        _outproj_kernel,
        out_shape=jax.ShapeDtypeStruct((T, D_MODEL), F32),
        grid=(T // TM,),
        in_specs=[
            pl.BlockSpec((TM, half), lambda i: (i, 0)),
            pl.BlockSpec((TM, half), lambda i: (i, 0)),
            pl.BlockSpec((half, D_MODEL), lambda i: (0, 0)),
            pl.BlockSpec((half, D_MODEL), lambda i: (0, 0)),
            pl.BlockSpec((TM, D_MODEL), lambda i: (i, 0)),
            pl.BlockSpec((None, N_MOD, D_MODEL), lambda i: (_mod_row(i), 0, 0)),
            pl.BlockSpec((1, D_MODEL), lambda i: (0, 0)),
        ],
        out_specs=pl.BlockSpec((TM, D_MODEL), lambda i: (i, 0)),
        compiler_params=_cparams("parallel"),
        name="outproj",
    )(a, b, wa, wb, x, mods, g.reshape(1, D_MODEL))


def _rope_tables(L, rot_dim, lane0, width):
    half = rot_dim // 2
    inv = ROPE_BASE ** (-jnp.arange(0, half, 2, dtype=F32) / half)
    pos = jnp.arange(L)
    ang_r = (pos // GRID_W).astype(F32)[:, None] * inv
    ang_c = (pos % GRID_W).astype(F32)[:, None] * inv
    cr, sr, cc, sc = jnp.cos(ang_r), jnp.sin(ang_r), jnp.cos(ang_c), jnp.sin(ang_c)
    z = jnp.zeros_like(sr)
    cos = jnp.concatenate([cr, cr, cc, cc], axis=-1)
    sin_a = jnp.concatenate([-sr, z, -sc, z], axis=-1)
    sin_b = jnp.concatenate([z, sr, z, sc], axis=-1)
    pad = ((0, 0), (lane0, width - lane0 - rot_dim))
    return jnp.pad(cos, pad, constant_values=1.0), jnp.pad(sin_a, pad), jnp.pad(sin_b, pad)


def _rope(x, cos, sin_a, sin_b, nf):
    w = x.shape[-1]
    return x * cos + pltpu.roll(x, w - nf, 1) * sin_a + pltpu.roll(x, nf, 1) * sin_b


def _dft_matrix(L):
    r = jnp.arange(2 * L, dtype=jnp.int32)[:, None]
    s = jnp.arange(L, dtype=jnp.int32)[None, :]
    k = r % L
    ang = ((k * s) % (2 * L)).astype(F32) * (math.pi / L)
    nyq = (1 - 2 * (s % 2)).astype(F32)
    bottom = jnp.where(k == 0, nyq, -jnp.sin(ang))
    return jnp.where(r < L, jnp.cos(ang), bottom)


def _filter_kernel(L, z_ref, w1_ref, b1_ref, w2_ref, b2_ref, fr_ref, w3f_ref, w3b_ref, dl_ref, dft_ref,
                   ka_ref, kb_ref, kc_ref):
    fr = fr_ref[...]
    z = z_ref[...]
    h = jnp.sin(fr * (jnp.dot(z, w1_ref[...], precision=HIGHEST, preferred_element_type=F32) + b1_ref[...]))
    h = jnp.sin(fr * (jnp.dot(h, w2_ref[...], precision=HIGHEST, preferred_element_type=F32) + b2_ref[...]))
    decay = jnp.exp(-z[:, 0:1] * dl_ref[...])
    hf = jnp.dot(h, w3f_ref[...], precision=HIGHEST, preferred_element_type=F32) * decay
    hb = jnp.dot(h, w3b_ref[...], precision=HIGHEST, preferred_element_type=F32) * decay
    row0 = lax.broadcasted_iota(jnp.int32, hf.shape, 0) == 0
    hb = jnp.where(row0, 0.0, hb)
    dft = dft_ref[...]
    p1 = _dot(dft, (hf + hb).astype(BF16))
    p2 = _dot(dft, (hf - hb).astype(BF16))
    sc = jnp.where(row0, 0.5 / L, 1.0 / L)
    ka_ref[...] = p1[:L] * sc
    kb_ref[...] = jnp.where(row0, 0.0, p2[L:]) * sc
    kc_ref[...] = jnp.where(row0, p1[L:L + 1], p1[:L]) * sc


def _hyena_spectra(L, tc, dft, f_w1, f_b1, f_w2, f_b2, f_w3, f_freq):
    t = jnp.linspace(0.0, 1.0, L, dtype=F32)[:, None]
    bands = (POS_EMB - 1) // 2
    w = 2.0 * math.pi * jnp.arange(L, dtype=F32)[:, None] / L
    fb = jnp.linspace(1e-4, bands - 1, bands, dtype=F32)[None, :]
    z = jnp.concatenate([t, jnp.cos(fb * w), -jnp.sin(fb * w)], axis=-1)
    z = jnp.pad(z, ((0, 0), (0, 128 - POS_EMB)))
    w1 = jnp.pad(f_w1, ((0, 128 - POS_EMB), (0, 0)))
    deltas = jnp.abs(jnp.linspace(math.log(HY_TARGET) / HY_SLOW_DECAY,
                                  math.log(HY_TARGET) / HY_FAST_DECAY, HY_W, dtype=F32))[None, :]
    nct = HY_W // tc
    fo = FILTER_ORDER
    row = lambda a: a.reshape(1, fo)
    kshape = jax.ShapeDtypeStruct((HY_ORDER, L, HY_W), F32)
    kspec = pl.BlockSpec((None, L, tc), lambda n, c: (n, 0, c))
    const = lambda shape: pl.BlockSpec(shape, lambda n, c: (0, 0))
    return pl.pallas_call(
        functools.partial(_filter_kernel, L),
        out_shape=(kshape, kshape, kshape),
        grid=(HY_ORDER, nct),
        in_specs=[
            const((L, 128)), const((128, fo)), const((1, fo)), const((fo, fo)), const((1, fo)), const((1, fo)),
            pl.BlockSpec((fo, tc), lambda n, c: (0, 2 * n * nct + c)),
            pl.BlockSpec((fo, tc), lambda n, c: (0, (2 * n + 1) * nct + c)),
            pl.BlockSpec((1, tc), lambda n, c: (0, c)),
            const((2 * L, L)),
        ],
        out_specs=(kspec, kspec, kspec),
        compiler_params=_cparams("parallel", "parallel"),
        name="hyena_filter",
    )(z, w1, row(f_b1), f_w2, row(f_b2), row(f_freq), f_w3, f_w3, deltas, dft)


def _hyena_kernel(L, uv_ref, ug0_ref, ug1_ref, cwv_ref, cwg0_ref, cwg1_ref, cbv_ref, cbg0_ref, cbg1_ref,
                  hb_ref, ka_ref, kb_ref, kc_ref, dft_ref, dftt_ref, o_ref):
    row = lax.broadcasted_iota(jnp.int32, (L, uv_ref.shape[1]), 0)
    first, last = row == 0, row == L - 1

    def short_conv(u_ref, w_ref, b_ref):
        u = u_ref[...]
        prev = jnp.where(first, 0.0, pltpu.roll(u, 1, 0))
        nxt = jnp.where(last, 0.0, pltpu.roll(u, L - 1, 0))
        return b_ref[...] + prev * w_ref[0:1, :] + u * w_ref[1:2, :] + nxt * w_ref[2:3, :]

    z = short_conv(uv_ref, cwv_ref, cbv_ref)
    gates = (short_conv(ug0_ref, cwg0_ref, cbg0_ref), short_conv(ug1_ref, cwg1_ref, cbg1_ref))
    dft = dft_ref[...]
    dftt = dftt_ref[...]
    for n in range(HY_ORDER):
        zf = _dot(dft, z.astype(BF16))
        zre, zim = zf[:L], zf[L:]
        ka, kb, kc = ka_ref[n], kb_ref[n], kc_ref[n]
        y = jnp.concatenate([zre * ka - zim * kb, zre * kb + zim * kc], axis=0).astype(BF16)
        conv = _dot(dftt, y)
        z = gates[n] * (conv + hb_ref[n:n + 1, :] * z)
    o_ref[...] = z.astype(o_ref.dtype)


def _hyena(p, row0, nb, L, tc, spectra, dft, dftt, conv_w, conv_b, h_bias):
    nct = HY_W // tc
    rb = row0 // L
    ka, kb, kc = spectra
    u_spec = lambda j: pl.BlockSpec((L, tc), lambda c, b: (rb + b, j * nct + c))
    cw_spec = lambda j: pl.BlockSpec((SHORT_CONV, tc), lambda c, b: (0, j * nct + c))
    cb_spec = lambda j: pl.BlockSpec((1, tc), lambda c, b: (0, j * nct + c))
    k_spec = pl.BlockSpec((HY_ORDER, L, tc), lambda c, b: (0, 0, c))
    cb = conv_b.reshape(1, -1)
    return pl.pallas_call(
        functools.partial(_hyena_kernel, L),
        out_shape=jax.ShapeDtypeStruct((nb * L, HY_W), BF16),
        grid=(nct, nb),
        in_specs=[
            u_spec(0), u_spec(1), u_spec(2), cw_spec(0), cw_spec(1), cw_spec(2),
            cb_spec(0), cb_spec(1), cb_spec(2),
            pl.BlockSpec((HY_ORDER, tc), lambda c, b: (0, c)),
            k_spec, k_spec, k_spec,
            pl.BlockSpec((2 * L, L), lambda c, b: (0, 0)),
            pl.BlockSpec((L, 2 * L), lambda c, b: (0, 0)),
        ],
        out_specs=pl.BlockSpec((L, tc), lambda c, b: (b, c)),
        compiler_params=_cparams("parallel", "arbitrary"),
        name="hyena",
    )(p, p, p, conv_w, conv_w, conv_w, cb, cb, cb, h_bias, ka, kb, kc, dft, dftt)


def _gqa_kernel(L, window, has_ctx, has_rope, emit_kv, *refs):
    it = iter(refs)
    q_ref, k_ref, v_ref, sink_ref = next(it), next(it), next(it), next(it)
    if has_ctx:
        kc_ref, vc_ref = next(it), next(it)
    if has_rope:
        cos_ref, sa_ref, sb_ref = next(it), next(it), next(it)
    o_ref = next(it)
    if emit_kv:
        kn_ref, vn_ref = next(it), next(it)
        kn_ref[...] = k_ref[...]
        vn_ref[...] = v_ref[...]

    hk = pl.program_id(1)
    qb = HEAD_DIM
    nf = HEAD_DIM // 4
    k = k_ref[...]
    if has_rope:
        k = _rope(k, cos_ref[...], sa_ref[...], sb_ref[...], nf)
    k = k.astype(BF16)
    v = v_ref[...].astype(BF16)
    if has_ctx:
        kc = kc_ref[...].astype(BF16)
        vc = vc_ref[...].astype(BF16)
    sink_row = sink_ref[...]
    lane = lax.broadcasted_iota(jnp.int32, sink_row.shape, 1)
    sinks = [jnp.sum(jnp.where(lane == hk * ATT_GROUP + g, sink_row, 0.0), axis=1, keepdims=True)
             for g in range(ATT_GROUP)]
    sink = jnp.concatenate([jnp.broadcast_to(s, (qb, 1)) for s in sinks], axis=0)

    for i in range(L // qb):
        rows = slice(i * qb, (i + 1) * qb)
        qs = []
        for g in range(ATT_GROUP):
            qg = q_ref[rows, g * HEAD_DIM:(g + 1) * HEAD_DIM]
            if has_rope:
                qg = _rope(qg, cos_ref[rows, :], sa_ref[rows, :], sb_ref[rows, :], nf)
            qs.append((qg * ATT_SCALE).astype(BF16))
        q = jnp.concatenate(qs, axis=0)
        if window is None:
            lo, hi = 0, L
        else:
            lo, hi = max(0, (i - 1) * qb), min(L, (i + 2) * qb)
        s = _dot_nt(q, k[lo:hi])
        if window is not None:
            qpos = i * qb + (lax.broadcasted_iota(jnp.int32, s.shape, 0) % qb)
            kpos = lo + lax.broadcasted_iota(jnp.int32, s.shape, 1)
            s = jnp.where(jnp.abs(kpos - qpos) <= window, s, -1e30)
        m = jnp.maximum(jnp.max(s, axis=-1, keepdims=True), sink)
        if has_ctx:
            sc = _dot_nt(q, kc)
            m = jnp.maximum(m, jnp.max(sc, axis=-1, keepdims=True))
        e = jnp.exp(s - m)
        den = jnp.sum(e, axis=-1, keepdims=True) + jnp.exp(sink - m)
        o = _dot(e.astype(BF16), v[lo:hi])
        if has_ctx:
            ec = jnp.exp(sc - m)
            den = den + jnp.sum(ec, axis=-1, keepdims=True)
            o = o + _dot(ec.astype(BF16), vc)
        o = o / den
        for g in range(ATT_GROUP):
            o_ref[rows, g * HEAD_DIM:(g + 1) * HEAD_DIM] = o[g * qb:(g + 1) * qb].astype(o_ref.dtype)


def _gqa(p, row0, nb, L, sink, window=None, ctx=None, rope=None, emit_kv=False):
    rb = row0 // L
    gw = ATT_GROUP * HEAD_DIM
    q0 = 3 * HY_W // gw
    k0 = (3 * HY_W + ATT_HEADS * HEAD_DIM) // HEAD_DIM
    v0 = k0 + ATT_KV_HEADS
    in_specs = [
        pl.BlockSpec((L, gw), lambda b, h: (rb + b, q0 + h)),
        pl.BlockSpec((L, HEAD_DIM), lambda b, h: (rb + b, k0 + h)),
        pl.BlockSpec((L, HEAD_DIM), lambda b, h: (rb + b, v0 + h)),
        pl.BlockSpec((1, ATT_HEADS), lambda b, h: (0, 0)),
    ]
    args = [p, p, p, sink.reshape(1, ATT_HEADS)]
    if ctx is not None:
        kc, vc, e = ctx
        spec = pl.BlockSpec((None, None, None, PAST_LEN, HEAD_DIM), lambda b, h: (b, e, h, 0, 0))
        in_specs += [spec, spec]
        args += [kc, vc]
    if rope is not None:
        in_specs += [pl.BlockSpec((L, HEAD_DIM), lambda b, h: (0, 0))] * 3
        args += list(rope)
    out_shape = [jax.ShapeDtypeStruct((nb * L, ATT_HEADS * HEAD_DIM), BF16)]
    out_specs = [pl.BlockSpec((L, gw), lambda b, h: (b, h))]
    if emit_kv:
        kv_shape = jax.ShapeDtypeStruct((nb, 1, ATT_KV_HEADS, L, HEAD_DIM), F32)
        kv_spec = pl.BlockSpec((None, None, None, L, HEAD_DIM), lambda b, h: (b, 0, h, 0, 0))
        out_shape += [kv_shape, kv_shape]
        out_specs += [kv_spec, kv_spec]
    return pl.pallas_call(
        functools.partial(_gqa_kernel, L, window, ctx is not None, rope is not None, emit_kv),
        out_shape=tuple(out_shape),
        grid=(nb, ATT_KV_HEADS),
        in_specs=in_specs,
        out_specs=tuple(out_specs),
        compiler_params=_cparams("parallel", "parallel"),
        name="gqa",
    )(*args)


def _hgrn_kernel(L, has_state, emit_state, *refs):
    it = iter(refs)
    q_ref, ff_ref, fb_ref, i_ref, g_ref, lb_ref, norm_ref = (next(it) for _ in range(7))
    if has_state:
        s0_ref = next(it)
    o_ref = next(it)
    if emit_state:
        s_ref = next(it)
    acc_scr = next(it)

    q = _silu(q_ref[...])
    v = i_ref[...].astype(BF16)
    a = lb_ref[...]
    e = jnp.exp(a - jnp.max(a, axis=0, keepdims=True))
    lb = e[1] / (e[0] + e[1])
    ci = lax.broadcasted_iota(jnp.int32, (CHUNK, CHUNK), 0)
    cj = lax.broadcasted_iota(jnp.int32, (CHUNK, CHUNK), 1)
    n_chunks = L // CHUNK

    for d, fz_ref in enumerate((ff_ref, fb_ref)):
        lbd = lb[d:d + 1]
        f = lbd + (1.0 - lbd) * jax.nn.sigmoid(fz_ref[...])
        kk = 1.0 - f
        logf = jnp.log(f)
        keep = (cj <= ci) if d == 0 else (cj >= ci)
        tri = keep.astype(F32)
        st = s0_ref[d].T if has_state else jnp.zeros((HEAD_DIM, HEAD_DIM), F32)
        order = range(n_chunks) if d == 0 else range(n_chunks - 1, -1, -1)
        for n in order:
            rows = slice(n * CHUNK, (n + 1) * CHUNK)
            b = jnp.dot(tri, logf[rows], precision=HIGHEST, preferred_element_type=F32)
            b_end = b[CHUNK - 1:CHUNK] if d == 0 else b[0:1]
            qd = (q[rows] * jnp.exp(b)).astype(BF16)
            kd = (kk[rows] * jnp.exp(-b)).astype(BF16)
            kend = (kk[rows] * jnp.exp(b_end - b)).astype(BF16)
            att = jnp.where(keep, _dot_nt(qd, kd), 0.0).astype(BF16)
            o = _dot(att, v[rows]) + _dot_nt(qd, st.astype(BF16))
            st = st * jnp.exp(b_end) + _dot_tn(v[rows], kend)
            if d == 0:
                acc_scr[rows, :] = o
            else:
                acc_scr[rows, :] += o
        if emit_state:
            s_ref[d] = st.T

    o = _rms(acc_scr[...]) * norm_ref[...] * _silu(g_ref[...])
    o_ref[...] = o.astype(o_ref.dtype)


def _hgrn(p, row0, nb, L, hg_lb, norm_g, state=None, emit_state=False):
    rb = row0 // L
    col = lambda j: pl.BlockSpec((L, HEAD_DIM), lambda b, h: (rb + b, j * HG_HEADS + h))
    in_specs = [col(0), col(1), col(2), col(3), col(4),
                pl.BlockSpec((hg_lb.shape[0], 2, HEAD_DIM), lambda b, h: (0, 0, h)),
                pl.BlockSpec((1, HEAD_DIM), lambda b, h: (0, 0))]
    args = [p, p, p, p, p, hg_lb, norm_g.reshape(1, HEAD_DIM)]
    st_spec = lambda o: pl.BlockSpec((None, None, 2, None, HEAD_DIM, HEAD_DIM), lambda b, h: (b, o, 0, h, 0, 0))
    if state is not None:
        s0, o = state
        in_specs.append(st_spec(o))
        args.append(s0)
    out_shape = [jax.ShapeDtypeStruct((nb * L, HG_W), BF16)]
    out_specs = [pl.BlockSpec((L, HEAD_DIM), lambda b, h: (b, h))]
    if emit_state:
        out_shape.append(jax.ShapeDtypeStruct((nb, 1, 2, HG_HEADS, HEAD_DIM, HEAD_DIM), F32))
        out_specs.append(st_spec(0))
    return pl.pallas_call(
        functools.partial(_hgrn_kernel, L, state is not None, emit_state),
        out_shape=tuple(out_shape),
        grid=(nb, HG_HEADS),
        in_specs=in_specs,
        out_specs=tuple(out_specs),
        scratch_shapes=[pltpu.VMEM((L, HEAD_DIM), F32)],
        compiler_params=_cparams("parallel", "parallel"),
        name="hgrn",
    )(*args)


MLA_QW = 256
KR_W = 128


def _mla_prep_kernel(has_rope, emit_kr, *refs):
    it = iter(refs)
    ql_ref, kvl_ref, kr_ref, qn_ref, kvn_ref, wq_ref = (next(it) for _ in range(6))
    if has_rope:
        qc_ref, qsa_ref, qsb_ref, kc_ref, ksa_ref, ksb_ref = (next(it) for _ in range(6))
    q_ref, ckv_ref, kro_ref = next(it), next(it), next(it)
    if emit_kr:
        krn_ref = next(it)
        krn_ref[...] = kr_ref[:, :ROPE]

    nf = ROPE // 4
    qn = (_rms(ql_ref[...]) * qn_ref[...]).astype(BF16)
    q = _dot(qn, wq_ref[...])
    for h in range(MLA_HEADS):
        cols = slice(h * MLA_QW, (h + 1) * MLA_QW)
        qh = q[:, cols]
        if has_rope:
            qh = _rope(qh, qc_ref[...], qsa_ref[...], qsb_ref[...], nf)
        q_ref[:, cols] = qh.astype(q_ref.dtype)
    ckv_ref[...] = _rms(kvl_ref[...]) * kvn_ref[...]
    kr = kr_ref[...]
    if has_rope:
        kr = _rope(kr, kc_ref[...], ksa_ref[...], ksb_ref[...], nf)
    kro_ref[...] = kr.astype(kro_ref.dtype)


def _mla_prep(p, row0, n_rows, L, q_norm, kv_norm, wq, rope=None, emit_kr=False):
    tm = min(TM, L)
    rb = row0 // tm
    per = L // tm
    in_specs = [
        pl.BlockSpec((tm, Q_LORA), lambda i: (rb + i, 5 * HG_W // Q_LORA)),
        pl.BlockSpec((tm, KV_LORA), lambda i: (rb + i, (5 * HG_W + Q_LORA) // KV_LORA)),
        pl.BlockSpec((tm, KR_W), lambda i: (rb + i, (5 * HG_W + Q_LORA + KV_LORA) // KR_W)),
        pl.BlockSpec((1, Q_LORA), lambda i: (0, 0)),
        pl.BlockSpec((1, KV_LORA), lambda i: (0, 0)),
        pl.BlockSpec((Q_LORA, MLA_HEADS * MLA_QW), lambda i: (0, 0)),
    ]
    args = [p, p, p, q_norm.reshape(1, Q_LORA), kv_norm.reshape(1, KV_LORA), wq]
    if rope is not None:
        in_specs += [pl.BlockSpec((tm, MLA_QW), lambda i: (i % per, 0))] * 3
        in_specs += [pl.BlockSpec((tm, KR_W), lambda i: (i % per, 0))] * 3
        args += list(rope)
    out_shape = [jax.ShapeDtypeStruct((n_rows, MLA_HEADS * MLA_QW), BF16),
                 jax.ShapeDtypeStruct((n_rows, KV_LORA), F32),
                 jax.ShapeDtypeStruct((n_rows, KR_W), BF16)]
    out_specs = [pl.BlockSpec((tm, MLA_HEADS * MLA_QW), lambda i: (i, 0)),
                 pl.BlockSpec((tm, KV_LORA), lambda i: (i, 0)),
                 pl.BlockSpec((tm, KR_W), lambda i: (i, 0))]
    if emit_kr:
        out_shape.append(jax.ShapeDtypeStruct((n_rows, ROPE), F32))
        out_specs.append(pl.BlockSpec((tm, ROPE), lambda i: (i, 0)))
    return pl.pallas_call(
        functools.partial(_mla_prep_kernel, rope is not None, emit_kr),
        out_shape=tuple(out_shape),
        grid=(n_rows // tm,),
        in_specs=in_specs,
        out_specs=tuple(out_specs),
        compiler_params=_cparams("parallel"),
        name="mla_prep",
    )(*args)


def _mla_attn_kernel(L, has_ctx, *refs):
    it = iter(refs)
    q_ref, ckv_ref, kr_ref, wk_ref, wv_ref = (next(it) for _ in range(5))
    if has_ctx:
        cckv_ref, ckr_ref = next(it), next(it)
    o_ref = next(it)

    ckv = ckv_ref[...].astype(BF16)
    kr = kr_ref[...]
    if has_ctx:
        ckv = jnp.concatenate([ckv, cckv_ref[...].astype(BF16)], axis=0)
        kr = jnp.concatenate([kr, ckr_ref[...].astype(BF16)], axis=0)
    kn = _dot(ckv, wk_ref[...]).astype(BF16)
    vh = _dot(ckv, wv_ref[...]).astype(BF16)
    kh = jnp.concatenate([kn, kr], axis=1)
    qb = min(L, 256)
    for i in range(L // qb):
        rows = slice(i * qb, (i + 1) * qb)
        s = _dot_nt(q_ref[rows, :], kh) * MLA_SCALE
        m = jnp.max(s, axis=-1, keepdims=True)
        e = jnp.exp(s - m)
        o = _dot(e.astype(BF16), vh) / jnp.sum(e, axis=-1, keepdims=True)
        o_ref[rows, :] = o.astype(o_ref.dtype)


def _mla_attn(q, ckv, kr, wk, wv, nb, L, ctx=None):
    in_specs = [
        pl.BlockSpec((L, MLA_QW), lambda b, h: (b, h)),
        pl.BlockSpec((L, KV_LORA), lambda b, h: (b, 0)),
        pl.BlockSpec((L, KR_W), lambda b, h: (b, 0)),
        pl.BlockSpec((KV_LORA, NOPE), lambda b, h: (0, h)),
        pl.BlockSpec((KV_LORA, V_DIM), lambda b, h: (0, h)),
    ]
    args = [q, ckv, kr, wk, wv]
    if ctx is not None:
        cckv, ckr, o = ctx
        in_specs += [pl.BlockSpec((None, None, PAST_LEN, KV_LORA), lambda b, h: (b, o, 0, 0)),
                     pl.BlockSpec((None, None, PAST_LEN, KR_W), lambda b, h: (b, o, 0, 0))]
        args += [cckv, ckr]
    return pl.pallas_call(
        functools.partial(_mla_attn_kernel, L, ctx is not None),
        out_shape=jax.ShapeDtypeStruct((nb * L, MLA_HEADS * V_DIM), BF16),
        grid=(nb, MLA_HEADS),
        in_specs=in_specs,
        out_specs=pl.BlockSpec((L, V_DIM), lambda b, h: (b, h)),
        compiler_params=_cparams("parallel", "parallel"),
        name="mla_attn",
    )(*args)


def kernel(x_prompt, x_sample, c, c_ctx, cache_attn_k, cache_attn_v, cache_mla_ckv, cache_mla_krope, state_hgrn, mod_w, mod_b, norm_g, ffn_wg, ffn_wu, ffn_wd, ev_w_in, ev_w_out, hy_conv_w, hy_conv_b, hy_f_w1, hy_f_b1, hy_f_w2, hy_f_b2, hy_f_w3, hy_f_freq, hy_bias, attn_sink, od_w_in, od_w_out, hg_lb, hg_norm, mla_q_norm, mla_w_qb, mla_kv_norm, mla_w_kvb):
    depth = mod_w.shape[0]
    x = jnp.concatenate([x_prompt.reshape(T_PROMPT, D_MODEL), x_sample.reshape(T_SAMPLE, D_MODEL)], axis=0)
    cvec = jnp.concatenate([c_ctx[None, :], c, jnp.zeros((MOD_ROWS - 1 - DEC_BATCH, D_MODEL), F32)], axis=0)
    mods_all = _modulation(cvec, mod_w, mod_b)

    wg, wu, wd = ffn_wg.astype(BF16), ffn_wu.astype(BF16), ffn_wd.astype(BF16)
    dft = {L: _dft_matrix(L).astype(BF16) for L in (SEQ, DEC_SEQ)}
    dftt = {L: m.T for L, m in dft.items()}
    hy_tc = {SEQ: 512, DEC_SEQ: 256}

    new_k = new_v = new_ckv = new_kr = new_s = None
    for l in range(depth):
        mods = mods_all[l]
        x = _ffn(x, mods, 0, norm_g[l, 0], norm_g[l, 1], wg, wu, wd, l, 0)
        if l % 2 == 0:
            e = l // 2
            p = _inproj(x, mods, norm_g[l, 2], ev_w_in[e].astype(BF16))
            hy_out, at_out = [], []
            for row0, nb, L in ((0, BATCH, SEQ), (T_PROMPT, DEC_BATCH, DEC_SEQ)):
                spectra = _hyena_spectra(L, hy_tc[L], dft[L], hy_f_w1[e], hy_f_b1[e], hy_f_w2[e], hy_f_b2[e],
                                         hy_f_w3[e], hy_f_freq[e])
                hy_out.append(_hyena(p, row0, nb, L, hy_tc[L], spectra, dft[L], dftt[L], hy_conv_w[e],
                                     hy_conv_b[e], hy_bias[e]))
            a_p, new_k, new_v = _gqa(p, 0, BATCH, SEQ, attn_sink[e], emit_kv=True)
            rope = _rope_tables(DEC_SEQ, HEAD_DIM, 0, HEAD_DIM)
            (a_s,) = _gqa(p, T_PROMPT, DEC_BATCH, DEC_SEQ, attn_sink[e], window=WINDOW,
                          ctx=(cache_attn_k, cache_attn_v, e), rope=rope)
            mix_a = jnp.concatenate(hy_out, axis=0)
            mix_b = jnp.concatenate([a_p, a_s], axis=0)
            w_out = ev_w_out[e].astype(BF16)
        else:
            o = l // 2
            w_in = jnp.pad(od_w_in[o], ((0, 0), (0, OD_IN_PAD - OD_IN))).astype(BF16)
            p = _inproj(x, mods, norm_g[l, 2], w_in)
            hg_p, new_s = _hgrn(p, 0, BATCH, SEQ, hg_lb, hg_norm[o], emit_state=True)
            (hg_s,) = _hgrn(p, T_PROMPT, DEC_BATCH, DEC_SEQ, hg_lb, hg_norm[o], state=(state_hgrn, o))
            wq = mla_w_qb[o].reshape(Q_LORA, MLA_HEADS, NOPE + ROPE)
            wq = jnp.pad(wq, ((0, 0), (0, 0), (0, MLA_QW - NOPE - ROPE))).reshape(Q_LORA, -1).astype(BF16)
            wkv = mla_w_kvb[o].reshape(KV_LORA, MLA_HEADS, NOPE + V_DIM)
            wk = wkv[:, :, :NOPE].reshape(KV_LORA, -1).astype(BF16)
            wv = wkv[:, :, NOPE:].reshape(KV_LORA, -1).astype(BF16)
            q_p, ckv_p, kr_p, new_kr = _mla_prep(p, 0, T_PROMPT, SEQ, mla_q_norm[o], mla_kv_norm[o], wq,
                                                 emit_kr=True)
            rope = (_rope_tables(DEC_SEQ, ROPE, NOPE, MLA_QW) + _rope_tables(DEC_SEQ, ROPE, 0, KR_W))
            q_s, ckv_s, kr_s = _mla_prep(p, T_PROMPT, T_SAMPLE, DEC_SEQ, mla_q_norm[o], mla_kv_norm[o], wq,
                                         rope=rope)
            a_p = _mla_attn(q_p, ckv_p, kr_p, wk, wv, BATCH, SEQ)
            ckr = jnp.pad(cache_mla_krope, ((0, 0), (0, 0), (0, 0), (0, KR_W - ROPE)))
            a_s = _mla_attn(q_s, ckv_s, kr_s, wk, wv, DEC_BATCH, DEC_SEQ, ctx=(cache_mla_ckv, ckr, o))
            new_ckv = ckv_p.reshape(BATCH, 1, SEQ, KV_LORA)
            new_kr = new_kr.reshape(BATCH, 1, SEQ, ROPE)
            mix_a = jnp.concatenate([hg_p, hg_s], axis=0)
            mix_b = jnp.concatenate([a_p, a_s], axis=0)
            w_out = od_w_out[o].astype(BF16)
        half = mix_a.shape[1]
        x = _outproj(mix_a, mix_b, w_out[:half], w_out[half:], x, mods, norm_g[l, 3])
        x = _ffn(x, mods, 2, norm_g[l, 4], norm_g[l, 5], wg, wu, wd, l, 1)

    y_prompt = x[:T_PROMPT].reshape(BATCH, SEQ, D_MODEL)
    y_sample = x[T_PROMPT:].reshape(DEC_BATCH, DEC_SEQ, D_MODEL)
    return (y_prompt, y_sample, new_k, new_v, new_ckv, new_kr, new_s)
```

```python
import functools
import math
from typing import NamedTuple

import jax
import jax.numpy as jnp
from jax import lax
from jax.experimental import pallas as pl
from jax.experimental.pallas import tpu as pltpu

D_MODEL = 2048
BATCH = 16
SEQ = 256
DEC_BATCH = 8
DEC_SEQ = 1024
PAST_LEN = 512
GRID_W = 64
HEAD_DIM = 128
HY_W = 1024
HY_ORDER = 2
SHORT_CONV = 3
POS_EMB = 33
FILTER_ORDER = 64
HY_FAST_DECAY = 0.3
HY_SLOW_DECAY = 1.5
HY_TARGET = 1e-2
ATT_HEADS = 8
ATT_KV_HEADS = 2
ATT_GROUP = 4
WINDOW = 128
EV_IN = 3 * HY_W + (ATT_HEADS + 2 * ATT_KV_HEADS) * HEAD_DIM
HG_W = 1024
HG_HEADS = 8
CHUNK = 64
Q_LORA = 512
KV_LORA = 256
NOPE = 128
ROPE = 64
V_DIM = 128
MLA_HEADS = 8
OD_IN = 5 * HG_W + Q_LORA + KV_LORA + ROPE
OD_IN_PAD = 6144
D_FF = 5632
MACARON_W = 0.5
N_MOD = 9
ROPE_BASE = 10000.0
EPS = 1e-6
ATT_SCALE = HEAD_DIM ** -0.5
MLA_SCALE = (NOPE + ROPE) ** -0.5

MOD_ROWS = 16

V7X_VMEM_LIMIT = 56 * 1024 * 1024
TM = 1024
RC = 512
TM_OUT = 512
TF = 512
TN = 512
MOD_TN = 1024

BF16 = jnp.bfloat16
F32 = jnp.float32
HIGHEST = lax.Precision.HIGHEST


class Group(NamedTuple):
    nb: int
    L: int
    latent: bool

    @property
    def rows(self):
        return self.nb * self.L

    def mod_row(self, i, tm):
        return 1 + i // (self.L // tm) if self.latent else 0


PROMPT = Group(BATCH, SEQ, False)
LATENT = Group(DEC_BATCH, DEC_SEQ, True)


def _cparams(*sem):
    return pltpu.CompilerParams(dimension_semantics=sem, vmem_limit_bytes=V7X_VMEM_LIMIT)


def _rms(x):
    return x * lax.rsqrt(jnp.mean(x * x, axis=-1, keepdims=True) + EPS)


def _silu(x):
    return x * jax.nn.sigmoid(x)


def _dot(a, b):
    return jnp.dot(a, b, preferred_element_type=F32)


def _dot_nt(a, b):
    return lax.dot_general(a, b, (((1,), (1,)), ((), ())), preferred_element_type=F32)


def _dot_tn(a, b):
    return lax.dot_general(a, b, (((0,), (0,)), ((), ())), preferred_element_type=F32)


def _row_chunks(n_rows, body):
    def step(c, carry):
        body(pl.ds(pl.multiple_of(c * RC, RC), RC))
        return carry
    lax.fori_loop(0, n_rows // RC, step, 0)


def _modulated_norm(x_ref, h_scr, gain, shift):
    def body(rows):
        h_scr[rows, :] = (_rms(x_ref[rows, :]) * gain + shift).astype(BF16)
    _row_chunks(x_ref.shape[0], body)


def _mod_kernel(c_ref, w_ref, b_ref, o_ref):
    s = _silu(c_ref[...]).astype(BF16)
    o_ref[...] = _dot(s, w_ref[...].astype(BF16)) + b_ref[...]


def _modulation(cvec, mod_w, mod_b):
    depth = mod_w.shape[0]
    n = N_MOD * D_MODEL
    out = pl.pallas_call(
        _mod_kernel,
        out_shape=jax.ShapeDtypeStruct((depth, MOD_ROWS, n), F32),
        grid=(depth, n // MOD_TN),
        in_specs=[
            pl.BlockSpec((MOD_ROWS, D_MODEL), lambda l, j: (0, 0)),
            pl.BlockSpec((None, D_MODEL, MOD_TN), lambda l, j: (l, 0, j)),
            pl.BlockSpec((None, 1, MOD_TN), lambda l, j: (l, 0, j)),
        ],
        out_specs=pl.BlockSpec((None, MOD_ROWS, MOD_TN), lambda l, j: (l, 0, j)),
        compiler_params=_cparams("parallel", "parallel"),
        name="modulation",
    )(cvec, mod_w, mod_b.reshape(depth, 1, n))
    return out.reshape(depth, MOD_ROWS, N_MOD, D_MODEL)


def _ffn_kernel(j, x_ref, mod_ref, gpre_ref, gpost_ref, wg_ref, wu_ref, wd_ref, o_ref, h_scr):
    f = pl.program_id(1)

    @pl.when(f == 0)
    def _():
        gain = gpre_ref[...] * (1.0 + mod_ref[3 * j + 1:3 * j + 2, :])
        _modulated_norm(x_ref, h_scr, gain, mod_ref[3 * j:3 * j + 1, :])
        o_ref[...] = jnp.zeros_like(o_ref)

    def body(rows):
        h = h_scr[rows, :]
        a = (_silu(_dot(h, wg_ref[...])) * _dot(h, wu_ref[...])).astype(BF16)
        o_ref[rows, :] += _dot(a, wd_ref[...])
    _row_chunks(o_ref.shape[0], body)

    @pl.when(f == pl.num_programs(1) - 1)
    def _():
        gain = (MACARON_W * mod_ref[3 * j + 2:3 * j + 3, :]) * gpost_ref[...]

        def residual(rows):
            o_ref[rows, :] = x_ref[rows, :] + _rms(o_ref[rows, :]) * gain
        _row_chunks(o_ref.shape[0], residual)


def _ffn(grp, x, mods, j, g_pre, g_post, wg, wu, wd, l, s):
    return pl.pallas_call(
        functools.partial(_ffn_kernel, j),
        out_shape=jax.ShapeDtypeStruct((grp.rows, D_MODEL), F32),
        grid=(grp.rows // TM, D_FF // TF),
        in_specs=[
            pl.BlockSpec((TM, D_MODEL), lambda i, f: (i, 0), pipeline_mode=pl.Buffered(1)),
            pl.BlockSpec((None, N_MOD, D_MODEL), lambda i, f: (grp.mod_row(i, TM), 0, 0)),
            pl.BlockSpec((1, D_MODEL), lambda i, f: (0, 0)),
            pl.BlockSpec((1, D_MODEL), lambda i, f: (0, 0)),
            pl.BlockSpec((None, None, D_MODEL, TF), lambda i, f: (l, s, 0, f)),
            pl.BlockSpec((None, None, D_MODEL, TF), lambda i, f: (l, s, 0, f)),
            pl.BlockSpec((None, None, TF, D_MODEL), lambda i, f: (l, s, f, 0)),
        ],
        out_specs=pl.BlockSpec((TM, D_MODEL), lambda i, f: (i, 0)),
        scratch_shapes=[pltpu.VMEM((TM, D_MODEL), BF16)],
        compiler_params=_cparams("parallel", "arbitrary"),
        name="ffn",
    )(x, mods, g_pre.reshape(1, D_MODEL), g_post.reshape(1, D_MODEL), wg, wu, wd)


def _inproj_kernel(k0, x_ref, mod_ref, g_ref, w_ref, o_ref, t_ref, h_scr):
    k = pl.program_id(1)

    @pl.when(k == 0)
    def _():
        _modulated_norm(x_ref, h_scr, g_ref[...] * (1.0 + mod_ref[4:5, :]), mod_ref[3:4, :])

    y = _dot(h_scr[...], w_ref[...])
    o_ref[...] = y.astype(o_ref.dtype)

    @pl.when(k >= k0)
    def _():
        t_ref[...] = y


def _inproj(grp, x, mods, g, w, n_tail):
    n = w.shape[1]
    k0 = (n - n_tail) // TN
    return pl.pallas_call(
        functools.partial(_inproj_kernel, k0),
        out_shape=(jax.ShapeDtypeStruct((grp.rows, n), BF16), jax.ShapeDtypeStruct((grp.rows, n_tail), F32)),
        grid=(grp.rows // TM, n // TN),
        in_specs=[
            pl.BlockSpec((TM, D_MODEL), lambda i, k: (i, 0)),
            pl.BlockSpec((None, N_MOD, D_MODEL), lambda i, k: (grp.mod_row(i, TM), 0, 0)),
            pl.BlockSpec((1, D_MODEL), lambda i, k: (0, 0)),
            pl.BlockSpec((D_MODEL, TN), lambda i, k: (0, k)),
        ],
        out_specs=(pl.BlockSpec((TM, TN), lambda i, k: (i, k)),
                   pl.BlockSpec((TM, TN), lambda i, k: (i, jnp.maximum(k - k0, 0)))),
        scratch_shapes=[pltpu.VMEM((TM, D_MODEL), BF16)],
        compiler_params=_cparams("parallel", "arbitrary"),
        name="inproj",
    )(x, mods, g.reshape(1, D_MODEL), w)


def _outproj_kernel(a_ref, b_ref, wa_ref, wb_ref, x_ref, mod_ref, g_ref, o_ref):
    y = _dot(a_ref[...], wa_ref[...]) + _dot(b_ref[...], wb_ref[...])
    o_ref[...] = x_ref[...] + _rms(y) * (mod_ref[5:6, :] * g_ref[...])


def _outproj(grp, a, b, wa, wb, x, mods, g):
    half = a.shape[1]
    tm = TM_OUT
    return pl.pallas_call(
        _outproj_kernel,
        out_shape=jax.ShapeDtypeStruct((grp.rows, D_MODEL), F32),
        grid=(grp.rows // tm,),
        in_specs=[
            pl.BlockSpec((tm, half), lambda i: (i, 0)),
            pl.BlockSpec((tm, half), lambda i: (i, 0)),
            pl.BlockSpec((half, D_MODEL), lambda i: (0, 0)),
            pl.BlockSpec((half, D_MODEL), lambda i: (0, 0)),
            pl.BlockSpec((tm, D_MODEL), lambda i: (i, 0)),
            pl.BlockSpec((None, N_MOD, D_MODEL), lambda i: (grp.mod_row(i, tm), 0, 0)),
            pl.BlockSpec((1, D_MODEL), lambda i: (0, 0)),
        ],
        out_specs=pl.BlockSpec((tm, D_MODEL), lambda i: (i, 0)),
        compiler_params=_cparams("parallel"),
        name="outproj",
    )(a, b, wa, wb, x, mods, g.reshape(1, D_MODEL))


def _rope_tables(L, rot_dim, lane0, width):
    half = rot_dim // 2
    inv = ROPE_BASE ** (-jnp.arange(0, half, 2, dtype=F32) / half)
    pos = jnp.arange(L)
    ang_r = (pos // GRID_W).astype(F32)[:, None] * inv
    ang_c = (pos % GRID_W).astype(F32)[:, None] * inv
    cr, sr, cc, sc = jnp.cos(ang_r), jnp.sin(ang_r), jnp.cos(ang_c), jnp.sin(ang_c)
    z = jnp.zeros_like(sr)
    cos = jnp.concatenate([cr, cr, cc, cc], axis=-1)
    sin_a = jnp.concatenate([-sr, z, -sc, z], axis=-1)
    sin_b = jnp.concatenate([z, sr, z, sc], axis=-1)
    pad = ((0, 0), (lane0, width - lane0 - rot_dim))
    return jnp.pad(cos, pad, constant_values=1.0), jnp.pad(sin_a, pad), jnp.pad(sin_b, pad)


def _rope(x, cos, sin_a, sin_b, nf):
    w = x.shape[-1]
    return x * cos + pltpu.roll(x, w - nf, 1) * sin_a + pltpu.roll(x, nf, 1) * sin_b


DFT_SPLIT = 32


def _dft_matrices(L):
    s = jnp.arange(L, dtype=jnp.int32)[None, :]

    def trig(k):
        ang = ((k[:, None] * s) % (2 * L)).astype(F32) * (math.pi / L)
        return jnp.cos(ang), jnp.sin(ang)

    c1, s1 = trig(jnp.arange(0, L, DFT_SPLIT, dtype=jnp.int32))
    c0, s0 = trig(jnp.arange(DFT_SPLIT, dtype=jnp.int32))
    cos = (c1[:, None, :] * c0[None] - s1[:, None, :] * s0[None]).reshape(L, L)
    sin = (s1[:, None, :] * c0[None] + c1[:, None, :] * s0[None]).reshape(L, L)
    nyq = (1 - 2 * (jnp.arange(L, dtype=jnp.int32) % 2)).astype(F32)
    k = jnp.arange(L, dtype=jnp.int32)[:, None]
    msin = jnp.where(k == 0, nyq[None, :], -sin)
    msin_t = jnp.where(s == 0, nyq[:, None], -sin)
    return cos.astype(BF16), msin.astype(BF16), msin_t.astype(BF16)


def _filter_kernel(L, z_ref, w1_ref, b1_ref, w2_ref, b2_ref, fr_ref, w3f_ref, w3b_ref, dl_ref, cos_ref, msin_ref,
                   ka_ref, kb_ref, kc_ref):
    fr = fr_ref[...]
    z = z_ref[...]
    h = jnp.sin(fr * (jnp.dot(z, w1_ref[...], precision=HIGHEST, preferred_element_type=F32) + b1_ref[...]))
    h = jnp.sin(fr * (jnp.dot(h, w2_ref[...], precision=HIGHEST, preferred_element_type=F32) + b2_ref[...]))
    decay = jnp.exp(-z[:, 0:1] * dl_ref[...])
    hf = jnp.dot(h, w3f_ref[...], precision=HIGHEST, preferred_element_type=F32) * decay
    hb = jnp.dot(h, w3b_ref[...], precision=HIGHEST, preferred_element_type=F32) * decay
    row = lax.broadcasted_iota(jnp.int32, hf.shape, 0)
    row0 = row == 0
    hb = jnp.where(row0, 0.0, hb)
    even = hf + hb
    re = _dot(cos_ref[...], even.astype(BF16))
    im = _dot(msin_ref[...], (hf - hb).astype(BF16))
    nyq = jnp.sum(jnp.where(row % 2 == 0, even, -even), axis=0, keepdims=True)
    sc = jnp.where(row0, 0.5 / L, 1.0 / L)
    ka_ref[...] = re * sc
    kb_ref[...] = jnp.where(row0, 0.0, im) * sc
    kc_ref[...] = jnp.where(row0, nyq, re) * sc


def _hyena_spectra(L, tc, dft, f_w1, f_b1, f_w2, f_b2, f_w3, f_freq):
    t = jnp.linspace(0.0, 1.0, L, dtype=F32)[:, None]
    bands = (POS_EMB - 1) // 2
    w = 2.0 * math.pi * jnp.arange(L, dtype=F32)[:, None] / L
    fb = jnp.linspace(1e-4, bands - 1, bands, dtype=F32)[None, :]
    z = jnp.concatenate([t, jnp.cos(fb * w), -jnp.sin(fb * w)], axis=-1)
    z = jnp.pad(z, ((0, 0), (0, 128 - POS_EMB)))
    w1 = jnp.pad(f_w1, ((0, 128 - POS_EMB), (0, 0)))
    deltas = jnp.abs(jnp.linspace(math.log(HY_TARGET) / HY_SLOW_DECAY,
                                  math.log(HY_TARGET) / HY_FAST_DECAY, HY_W, dtype=F32))[None, :]
    nct = HY_W // tc
    fo = FILTER_ORDER
    row = lambda a: a.reshape(1, fo)
    kshape = jax.ShapeDtypeStruct((HY_ORDER, L, HY_W), F32)
    kspec = pl.BlockSpec((None, L, tc), lambda n, c: (n, 0, c))
    const = lambda shape: pl.BlockSpec(shape, lambda n, c: (0, 0))
    return pl.pallas_call(
        functools.partial(_filter_kernel, L),
        out_shape=(kshape, kshape, kshape),
        grid=(HY_ORDER, nct),
        in_specs=[
            const((L, 128)), const((128, fo)), const((1, fo)), const((fo, fo)), const((1, fo)), const((1, fo)),
            pl.BlockSpec((fo, tc), lambda n, c: (0, 2 * n * nct + c)),
            pl.BlockSpec((fo, tc), lambda n, c: (0, (2 * n + 1) * nct + c)),
            pl.BlockSpec((1, tc), lambda n, c: (0, c)),
            const((L, L)), const((L, L)),
        ],
        out_specs=(kspec, kspec, kspec),
        compiler_params=_cparams("parallel", "parallel"),
        name="hyena_filter",
    )(z, w1, row(f_b1), f_w2, row(f_b2), row(f_freq), f_w3, f_w3, deltas, dft[0], dft[1])


def _hyena_kernel(L, uv_ref, ug0_ref, ug1_ref, cwv_ref, cwg0_ref, cwg1_ref, cbv_ref, cbg0_ref, cbg1_ref,
                  hb_ref, ka_ref, kb_ref, kc_ref, cos_ref, msin_ref, msint_ref, o_ref):
    row = lax.broadcasted_iota(jnp.int32, (L, uv_ref.shape[1]), 0)
    first, last = row == 0, row == L - 1

    def short_conv(u_ref, w_ref, b_ref):
        u = u_ref[...].astype(F32)
        prev = jnp.where(first, 0.0, pltpu.roll(u, 1, 0))
        nxt = jnp.where(last, 0.0, pltpu.roll(u, L - 1, 0))
        return b_ref[...] + prev * w_ref[0:1, :] + u * w_ref[1:2, :] + nxt * w_ref[2:3, :]

    z = short_conv(uv_ref, cwv_ref, cbv_ref)
    gates = (short_conv(ug0_ref, cwg0_ref, cbg0_ref), short_conv(ug1_ref, cwg1_ref, cbg1_ref))
    for n in range(HY_ORDER):
        zb = z.astype(BF16)
        zre = _dot(cos_ref[...], zb)
        zim = _dot(msin_ref[...], zb)
        ka, kb, kc = ka_ref[n], kb_ref[n], kc_ref[n]
        yre = (zre * ka - zim * kb).astype(BF16)
        yim = (zre * kb + zim * kc).astype(BF16)
        conv = _dot(cos_ref[...], yre) + _dot(msint_ref[...], yim)
        z = gates[n] * (conv + hb_ref[n:n + 1, :] * z)
    o_ref[...] = z.astype(o_ref.dtype)


def _hyena(grp, p, tc, spectra, dft, conv_w, conv_b, h_bias):
    L = grp.L
    nct = HY_W // tc
    ka, kb, kc = spectra
    u_spec = lambda j: pl.BlockSpec((L, tc), lambda c, b: (b, j * nct + c))
    cw_spec = lambda j: pl.BlockSpec((SHORT_CONV, tc), lambda c, b: (0, j * nct + c))
    cb_spec = lambda j: pl.BlockSpec((1, tc), lambda c, b: (0, j * nct + c))
    k_spec = pl.BlockSpec((HY_ORDER, L, tc), lambda c, b: (0, 0, c))
    m_spec = pl.BlockSpec((L, L), lambda c, b: (0, 0))
    cb = conv_b.reshape(1, -1)
    return pl.pallas_call(
        functools.partial(_hyena_kernel, L),
        out_shape=jax.ShapeDtypeStruct((grp.rows, HY_W), BF16),
        grid=(nct, grp.nb),
        in_specs=[
            u_spec(0), u_spec(1), u_spec(2), cw_spec(0), cw_spec(1), cw_spec(2),
            cb_spec(0), cb_spec(1), cb_spec(2),
            pl.BlockSpec((HY_ORDER, tc), lambda c, b: (0, c)),
            k_spec, k_spec, k_spec, m_spec, m_spec, m_spec,
        ],
        out_specs=pl.BlockSpec((L, tc), lambda c, b: (b, c)),
        compiler_params=_cparams("parallel", "arbitrary"),
        name="hyena",
    )(p, p, p, conv_w, conv_w, conv_w, cb, cb, cb, h_bias, ka, kb, kc, *dft)


def _gqa_kernel(L, window, has_ctx, has_rope, emit_kv, *refs):
    it = iter(refs)
    q_ref, k_ref, v_ref, sink_ref = next(it), next(it), next(it), next(it)
    if has_ctx:
        kc_ref, vc_ref = next(it), next(it)
    if has_rope:
        cos_ref, sa_ref, sb_ref = next(it), next(it), next(it)
    o_ref = next(it)
    if emit_kv:
        kn_ref, vn_ref = next(it), next(it)
        kn_ref[...] = k_ref[...]
        vn_ref[...] = v_ref[...]

    hk = pl.program_id(1)
    qb = HEAD_DIM
    nf = HEAD_DIM // 4
    k = k_ref[...]
    if has_rope:
        k = _rope(k, cos_ref[...], sa_ref[...], sb_ref[...], nf)
    k = k.astype(BF16)
    v = v_ref[...].astype(BF16)
    if has_ctx:
        kc = kc_ref[...].astype(BF16)
        vc = vc_ref[...].astype(BF16)
    sink_row = sink_ref[...]
    lane = lax.broadcasted_iota(jnp.int32, sink_row.shape, 1)
    sinks = [jnp.sum(jnp.where(lane == hk * ATT_GROUP + g, sink_row, 0.0), axis=1, keepdims=True)
             for g in range(ATT_GROUP)]
    sink = jnp.concatenate([jnp.broadcast_to(s, (qb, 1)) for s in sinks], axis=0)

    for i in range(L // qb):
        rows = slice(i * qb, (i + 1) * qb)
        qs = []
        for g in range(ATT_GROUP):
            qg = q_ref[rows, g * HEAD_DIM:(g + 1) * HEAD_DIM].astype(F32)
            if has_rope:
                qg = _rope(qg, cos_ref[rows, :], sa_ref[rows, :], sb_ref[rows, :], nf)
            qs.append((qg * ATT_SCALE).astype(BF16))
        q = jnp.concatenate(qs, axis=0)
        if window is None:
            lo, hi = 0, L
        else:
            lo, hi = max(0, (i - 1) * qb), min(L, (i + 2) * qb)
        s = _dot_nt(q, k[lo:hi])
        if window is not None:
            qpos = i * qb + (lax.broadcasted_iota(jnp.int32, s.shape, 0) % qb)
            kpos = lo + lax.broadcasted_iota(jnp.int32, s.shape, 1)
            s = jnp.where(jnp.abs(kpos - qpos) <= window, s, -1e30)
        m = jnp.maximum(jnp.max(s, axis=-1, keepdims=True), sink)
        if has_ctx:
            sc = _dot_nt(q, kc)
            m = jnp.maximum(m, jnp.max(sc, axis=-1, keepdims=True))
        e = jnp.exp(s - m)
        den = jnp.sum(e, axis=-1, keepdims=True) + jnp.exp(sink - m)
        o = _dot(e.astype(BF16), v[lo:hi])
        if has_ctx:
            ec = jnp.exp(sc - m)
            den = den + jnp.sum(ec, axis=-1, keepdims=True)
            o = o + _dot(ec.astype(BF16), vc)
        o = o / den
        for g in range(ATT_GROUP):
            o_ref[rows, g * HEAD_DIM:(g + 1) * HEAD_DIM] = o[g * qb:(g + 1) * qb].astype(o_ref.dtype)


def _gqa(grp, p, kv, sink, window=None, ctx=None, rope=None, emit_kv=False):
    L = grp.L
    gw = ATT_GROUP * HEAD_DIM
    q0 = 3 * HY_W // gw
    in_specs = [
        pl.BlockSpec((L, gw), lambda b, h: (b, q0 + h)),
        pl.BlockSpec((L, HEAD_DIM), lambda b, h: (b, h)),
        pl.BlockSpec((L, HEAD_DIM), lambda b, h: (b, ATT_KV_HEADS + h)),
        pl.BlockSpec((1, ATT_HEADS), lambda b, h: (0, 0)),
    ]
    args = [p, kv, kv, sink.reshape(1, ATT_HEADS)]
    if ctx is not None:
        kc, vc, e = ctx
        spec = pl.BlockSpec((None, None, None, PAST_LEN, HEAD_DIM), lambda b, h: (b, e, h, 0, 0))
        in_specs += [spec, spec]
        args += [kc, vc]
    if rope is not None:
        in_specs += [pl.BlockSpec((L, HEAD_DIM), lambda b, h: (0, 0))] * 3
        args += list(rope)
    out_shape = [jax.ShapeDtypeStruct((grp.rows, ATT_HEADS * HEAD_DIM), BF16)]
    out_specs = [pl.BlockSpec((L, gw), lambda b, h: (b, h))]
    if emit_kv:
        kv_shape = jax.ShapeDtypeStruct((grp.nb, 1, ATT_KV_HEADS, L, HEAD_DIM), F32)
        kv_spec = pl.BlockSpec((None, None, None, L, HEAD_DIM), lambda b, h: (b, 0, h, 0, 0))
        out_shape += [kv_shape, kv_shape]
        out_specs += [kv_spec, kv_spec]
    return pl.pallas_call(
        functools.partial(_gqa_kernel, L, window, ctx is not None, rope is not None, emit_kv),
        out_shape=tuple(out_shape),
        grid=(grp.nb, ATT_KV_HEADS),
        in_specs=in_specs,
        out_specs=tuple(out_specs),
        compiler_params=_cparams("parallel", "parallel"),
        name="gqa",
    )(*args)


def _chunk_cumsum(x, reverse):
    n = x.shape[0]
    pos = lax.broadcasted_iota(jnp.int32, x.shape, 0) % CHUNK
    s = 1
    while s < CHUNK:
        if reverse:
            x = x + jnp.where(pos < CHUNK - s, pltpu.roll(x, n - s, 0), 0.0)
        else:
            x = x + jnp.where(pos >= s, pltpu.roll(x, s, 0), 0.0)
        s *= 2
    return x


def _hgrn_kernel(L, has_state, emit_state, *refs):
    it = iter(refs)
    q_ref, ff_ref, fb_ref, i_ref, g_ref, lb_ref, norm_ref = (next(it) for _ in range(7))
    if has_state:
        s0_ref = next(it)
    o_ref = next(it)
    if emit_state:
        s_ref = next(it)
    acc_scr = next(it)

    q = _silu(q_ref[...].astype(F32))
    v = i_ref[...]
    a = lb_ref[...]
    e = jnp.exp(a - jnp.max(a, axis=0, keepdims=True))
    lb = e[1] / (e[0] + e[1])
    ci = lax.broadcasted_iota(jnp.int32, (CHUNK, CHUNK), 0)
    cj = lax.broadcasted_iota(jnp.int32, (CHUNK, CHUNK), 1)
    n_chunks = L // CHUNK
    chunks = [slice(n * CHUNK, (n + 1) * CHUNK) for n in range(n_chunks)]

    for d, fz_ref in enumerate((ff_ref, fb_ref)):
        lbd = lb[d:d + 1]
        f = lbd + (1.0 - lbd) * jax.nn.sigmoid(fz_ref[...].astype(F32))
        b = _chunk_cumsum(jnp.log(f), reverse=(d == 1))
        qd = (q * jnp.exp(b)).astype(BF16)
        kd32 = (1.0 - f) * jnp.exp(-b)
        kd = kd32.astype(BF16)
        keep = (cj <= ci) if d == 0 else (cj >= ci)
        decay, own = [], []
        for rows in chunks:
            end = rows.stop - 1 if d == 0 else rows.start
            dc = jnp.exp(b[end:end + 1])
            decay.append(dc)
            own.append(_dot_tn(v[rows], (kd32[rows] * dc).astype(BF16)))
        st = s0_ref[d].T if has_state else jnp.zeros((HEAD_DIM, HEAD_DIM), F32)
        entering = [None] * n_chunks
        for n in (range(n_chunks) if d == 0 else range(n_chunks - 1, -1, -1)):
            entering[n] = st.astype(BF16)
            st = st * decay[n] + own[n]
        if emit_state:
            s_ref[d] = st.T
        for n, rows in enumerate(chunks):
            att = jnp.where(keep, _dot_nt(qd[rows], kd[rows]), 0.0).astype(BF16)
            o = _dot(att, v[rows]) + _dot_nt(qd[rows], entering[n])
            if d == 0:
                acc_scr[rows, :] = o
            else:
                acc_scr[rows, :] += o

    o = _rms(acc_scr[...]) * norm_ref[...] * _silu(g_ref[...].astype(F32))
    o_ref[...] = o.astype(o_ref.dtype)


def _hgrn(grp, p, hg_lb, norm_g, state=None, emit_state=False):
    L = grp.L
    col = lambda j: pl.BlockSpec((L, HEAD_DIM), lambda b, h: (b, j * HG_HEADS + h))
    in_specs = [col(0), col(1), col(2), col(3), col(4),
                pl.BlockSpec((hg_lb.shape[0], 2, HEAD_DIM), lambda b, h: (0, 0, h)),
                pl.BlockSpec((1, HEAD_DIM), lambda b, h: (0, 0))]
    args = [p, p, p, p, p, hg_lb, norm_g.reshape(1, HEAD_DIM)]
    st_spec = lambda o: pl.BlockSpec((None, None, 2, None, HEAD_DIM, HEAD_DIM), lambda b, h: (b, o, 0, h, 0, 0))
    if state is not None:
        s0, o = state
        in_specs.append(st_spec(o))
        args.append(s0)
    out_shape = [jax.ShapeDtypeStruct((grp.rows, HG_W), BF16)]
    out_specs = [pl.BlockSpec((L, HEAD_DIM), lambda b, h: (b, h))]
    if emit_state:
        out_shape.append(jax.ShapeDtypeStruct((grp.nb, 1, 2, HG_HEADS, HEAD_DIM, HEAD_DIM), F32))
        out_specs.append(st_spec(0))
    return pl.pallas_call(
        functools.partial(_hgrn_kernel, L, state is not None, emit_state),
        out_shape=tuple(out_shape),
        grid=(grp.nb, HG_HEADS),
        in_specs=in_specs,
        out_specs=tuple(out_specs),
        scratch_shapes=[pltpu.VMEM((L, HEAD_DIM), F32)],
        compiler_params=_cparams("parallel", "parallel"),
        name="hgrn",
    )(*args)


MLA_QW = 256
KR_W = 128
OD_TAIL = 1024


def _mla_prep_kernel(has_rope, emit_kr, *refs):
    it = iter(refs)
    ql_ref, kvl_ref, kr_ref, qn_ref, kvn_ref, wq_ref = (next(it) for _ in range(6))
    if has_rope:
        qc_ref, qsa_ref, qsb_ref, kc_ref, ksa_ref, ksb_ref = (next(it) for _ in range(6))
    q_ref, ckv_ref, kro_ref = next(it), next(it), next(it)
    if emit_kr:
        krn_ref = next(it)
        krn_ref[...] = kr_ref[:, :ROPE]

    nf = ROPE // 4
    qn = (_rms(ql_ref[...]) * qn_ref[...]).astype(BF16)
    q = _dot(qn, wq_ref[...])
    for h in range(MLA_HEADS):
        cols = slice(h * MLA_QW, (h + 1) * MLA_QW)
        qh = q[:, cols]
        if has_rope:
            qh = _rope(qh, qc_ref[...], qsa_ref[...], qsb_ref[...], nf)
        q_ref[:, cols] = qh.astype(q_ref.dtype)
    ckv_ref[...] = _rms(kvl_ref[...]) * kvn_ref[...]
    kr = kr_ref[...]
    if has_rope:
        kr = _rope(kr, kc_ref[...], ksa_ref[...], ksb_ref[...], nf)
    kro_ref[...] = kr.astype(kro_ref.dtype)


def _mla_prep(grp, tail, q_norm, kv_norm, wq, rope=None, emit_kr=False):
    tm = min(512, grp.L)
    per = grp.L // tm
    n_rows = grp.rows
    in_specs = [
        pl.BlockSpec((tm, Q_LORA), lambda i: (i, 0)),
        pl.BlockSpec((tm, KV_LORA), lambda i: (i, Q_LORA // KV_LORA)),
        pl.BlockSpec((tm, KR_W), lambda i: (i, (Q_LORA + KV_LORA) // KR_W)),
        pl.BlockSpec((1, Q_LORA), lambda i: (0, 0)),
        pl.BlockSpec((1, KV_LORA), lambda i: (0, 0)),
        pl.BlockSpec((Q_LORA, MLA_HEADS * MLA_QW), lambda i: (0, 0)),
    ]
    args = [tail, tail, tail, q_norm.reshape(1, Q_LORA), kv_norm.reshape(1, KV_LORA), wq]
    if rope is not None:
        in_specs += [pl.BlockSpec((tm, MLA_QW), lambda i: (i % per, 0))] * 3
        in_specs += [pl.BlockSpec((tm, KR_W), lambda i: (i % per, 0))] * 3
        args += list(rope)
    out_shape = [jax.ShapeDtypeStruct((n_rows, MLA_HEADS * MLA_QW), BF16),
                 jax.ShapeDtypeStruct((n_rows, KV_LORA), F32),
                 jax.ShapeDtypeStruct((n_rows, KR_W), BF16)]
    out_specs = [pl.BlockSpec((tm, MLA_HEADS * MLA_QW), lambda i: (i, 0)),
                 pl.BlockSpec((tm, KV_LORA), lambda i: (i, 0)),
                 pl.BlockSpec((tm, KR_W), lambda i: (i, 0))]
    if emit_kr:
        out_shape.append(jax.ShapeDtypeStruct((n_rows, ROPE), F32))
        out_specs.append(pl.BlockSpec((tm, ROPE), lambda i: (i, 0)))
    return pl.pallas_call(
        functools.partial(_mla_prep_kernel, rope is not None, emit_kr),
        out_shape=tuple(out_shape),
        grid=(n_rows // tm,),
        in_specs=in_specs,
        out_specs=tuple(out_specs),
        compiler_params=_cparams("parallel"),
        name="mla_prep",
    )(*args)


def _mla_attn_kernel(L, has_ctx, *refs):
    it = iter(refs)
    q_ref, ckv_ref, kr_ref, wk_ref, wv_ref = (next(it) for _ in range(5))
    if has_ctx:
        cckv_ref, ckr_ref = next(it), next(it)
    o_ref = next(it)

    ckv = ckv_ref[...].astype(BF16)
    kr = kr_ref[...]
    if has_ctx:
        ckv = jnp.concatenate([ckv, cckv_ref[...].astype(BF16)], axis=0)
        kr = jnp.concatenate([kr, ckr_ref[...].astype(BF16)], axis=0)
    kn = _dot(ckv, wk_ref[...]).astype(BF16)
    vh = _dot(ckv, wv_ref[...]).astype(BF16)
    kh = jnp.concatenate([kn, kr], axis=1)
    qb = min(L, 256)
    for i in range(L // qb):
        rows = slice(i * qb, (i + 1) * qb)
        s = _dot_nt(q_ref[rows, :], kh) * MLA_SCALE
        m = jnp.max(s, axis=-1, keepdims=True)
        e = jnp.exp(s - m)
        o = _dot(e.astype(BF16), vh) / jnp.sum(e, axis=-1, keepdims=True)
        o_ref[rows, :] = o.astype(o_ref.dtype)


def _mla_attn(grp, q, ckv, kr, wk, wv, ctx=None):
    L = grp.L
    in_specs = [
        pl.BlockSpec((L, MLA_QW), lambda b, h: (b, h)),
        pl.BlockSpec((L, KV_LORA), lambda b, h: (b, 0)),
        pl.BlockSpec((L, KR_W), lambda b, h: (b, 0)),
        pl.BlockSpec((KV_LORA, NOPE), lambda b, h: (0, h)),
        pl.BlockSpec((KV_LORA, V_DIM), lambda b, h: (0, h)),
    ]
    args = [q, ckv, kr, wk, wv]
    if ctx is not None:
        cckv, ckr, o = ctx
        in_specs += [pl.BlockSpec((None, None, PAST_LEN, KV_LORA), lambda b, h: (b, o, 0, 0)),
                     pl.BlockSpec((None, None, PAST_LEN, KR_W), lambda b, h: (b, o, 0, 0))]
        args += [cckv, ckr]
    return pl.pallas_call(
        functools.partial(_mla_attn_kernel, L, ctx is not None),
        out_shape=jax.ShapeDtypeStruct((grp.rows, MLA_HEADS * V_DIM), BF16),
        grid=(grp.nb, MLA_HEADS),
        in_specs=in_specs,
        out_specs=pl.BlockSpec((L, V_DIM), lambda b, h: (b, h)),
        compiler_params=_cparams("parallel", "parallel"),
        name="mla_attn",
    )(*args)


def kernel(x_prompt, x_sample, c, c_ctx, cache_attn_k, cache_attn_v, cache_mla_ckv, cache_mla_krope, state_hgrn, mod_w, mod_b, norm_g, ffn_wg, ffn_wu, ffn_wd, ev_w_in, ev_w_out, hy_conv_w, hy_conv_b, hy_f_w1, hy_f_b1, hy_f_w2, hy_f_b2, hy_f_w3, hy_f_freq, hy_bias, attn_sink, od_w_in, od_w_out, hg_lb, hg_norm, mla_q_norm, mla_w_qb, mla_kv_norm, mla_w_kvb):
    depth = mod_w.shape[0]
    groups = (PROMPT, LATENT)
    xs = [x_prompt.reshape(PROMPT.rows, D_MODEL), x_sample.reshape(LATENT.rows, D_MODEL)]
    cvec = jnp.concatenate([c_ctx[None, :], c, jnp.zeros((MOD_ROWS - 1 - DEC_BATCH, D_MODEL), F32)], axis=0)
    mods_all = _modulation(cvec, mod_w, mod_b)

    wg, wu, wd = ffn_wg.astype(BF16), ffn_wu.astype(BF16), ffn_wd.astype(BF16)
    hy_tc = {SEQ: 512, DEC_SEQ: 256}

    new_k = new_v = new_ckv = new_kr = new_s = None
    for l in range(depth):
        mods = mods_all[l]
        xs = [_ffn(grp, x, mods, 0, norm_g[l, 0], norm_g[l, 1], wg, wu, wd, l, 0) for grp, x in zip(groups, xs)]
        if l % 2 == 0:
            e = l // 2
            w_in = ev_w_in[e].astype(BF16)
            w_out = ev_w_out[e].astype(BF16)
            kv_cols = 2 * ATT_KV_HEADS * HEAD_DIM
            mix = []
            for grp, x in zip(groups, xs):
                p, kv = _inproj(grp, x, mods, norm_g[l, 2], w_in, kv_cols)
                dft = _dft_matrices(grp.L)
                tc = hy_tc[grp.L]
                spectra = _hyena_spectra(grp.L, tc, dft, hy_f_w1[e], hy_f_b1[e], hy_f_w2[e], hy_f_b2[e],
                                         hy_f_w3[e], hy_f_freq[e])
                hy = _hyena(grp, p, tc, spectra, dft, hy_conv_w[e], hy_conv_b[e], hy_bias[e])
                if grp.latent:
                    rope = _rope_tables(grp.L, HEAD_DIM, 0, HEAD_DIM)
                    (att,) = _gqa(grp, p, kv, attn_sink[e], window=WINDOW,
                                  ctx=(cache_attn_k, cache_attn_v, e), rope=rope)
                else:
                    att, new_k, new_v = _gqa(grp, p, kv, attn_sink[e], emit_kv=True)
                mix.append((hy, att))
        else:
            o = l // 2
            w_in = jnp.pad(od_w_in[o], ((0, 0), (0, OD_IN_PAD - OD_IN))).astype(BF16)
            w_out = od_w_out[o].astype(BF16)
            wq = mla_w_qb[o].reshape(Q_LORA, MLA_HEADS, NOPE + ROPE)
            wq = jnp.pad(wq, ((0, 0), (0, 0), (0, MLA_QW - NOPE - ROPE))).reshape(Q_LORA, -1).astype(BF16)
            wkv = mla_w_kvb[o].reshape(KV_LORA, MLA_HEADS, NOPE + V_DIM)
            wk = wkv[:, :, :NOPE].reshape(KV_LORA, -1).astype(BF16)
            wv = wkv[:, :, NOPE:].reshape(KV_LORA, -1).astype(BF16)
            mix = []
            for grp, x in zip(groups, xs):
                p, tail = _inproj(grp, x, mods, norm_g[l, 2], w_in, OD_TAIL)
                if grp.latent:
                    (hg,) = _hgrn(grp, p, hg_lb, hg_norm[o], state=(state_hgrn, o))
                    rope = (_rope_tables(grp.L, ROPE, NOPE, MLA_QW) + _rope_tables(grp.L, ROPE, 0, KR_W))
                    q, ckv, kr = _mla_prep(grp, tail, mla_q_norm[o], mla_kv_norm[o], wq, rope=rope)
                    ckr = jnp.pad(cache_mla_krope, ((0, 0), (0, 0), (0, 0), (0, KR_W - ROPE)))
                    att = _mla_attn(grp, q, ckv, kr, wk, wv, ctx=(cache_mla_ckv, ckr, o))
                else:
                    hg, new_s = _hgrn(grp, p, hg_lb, hg_norm[o], emit_state=True)
                    q, ckv, kr, kr_raw = _mla_prep(grp, tail, mla_q_norm[o], mla_kv_norm[o], wq, emit_kr=True)
                    att = _mla_attn(grp, q, ckv, kr, wk, wv)
                    new_ckv = ckv.reshape(BATCH, 1, SEQ, KV_LORA)
                    new_kr = kr_raw.reshape(BATCH, 1, SEQ, ROPE)
                mix.append((hg, att))
        half = w_out.shape[0] // 2
        xs = [_outproj(grp, a, b, w_out[:half], w_out[half:], x, mods, norm_g[l, 3])
              for grp, x, (a, b) in zip(groups, xs, mix)]
        xs = [_ffn(grp, x, mods, 2, norm_g[l, 4], norm_g[l, 5], wg, wu, wd, l, 1) for grp, x in zip(groups, xs)]

    y_prompt = xs[0].reshape(BATCH, SEQ, D_MODEL)
    y_sample = xs[1].reshape(DEC_BATCH, DEC_SEQ, D_MODEL)
    return (y_prompt, y_sample, new_k, new_v, new_ckv, new_kr, new_s)
```

```python
import functools
import math
from typing import NamedTuple

import jax
import jax.numpy as jnp
from jax import lax
from jax.experimental import pallas as pl
from jax.experimental.pallas import tpu as pltpu

D_MODEL = 2048
BATCH = 16
SEQ = 256
DEC_BATCH = 8
DEC_SEQ = 1024
PAST_LEN = 512
GRID_W = 64
HEAD_DIM = 128
HY_W = 1024
HY_ORDER = 2
SHORT_CONV = 3
POS_EMB = 33
FILTER_ORDER = 64
HY_FAST_DECAY = 0.3
HY_SLOW_DECAY = 1.5
HY_TARGET = 1e-2
ATT_HEADS = 8
ATT_KV_HEADS = 2
ATT_GROUP = 4
WINDOW = 128
EV_IN = 3 * HY_W + (ATT_HEADS + 2 * ATT_KV_HEADS) * HEAD_DIM
HG_W = 1024
HG_HEADS = 8
CHUNK = 64
Q_LORA = 512
KV_LORA = 256
NOPE = 128
ROPE = 64
V_DIM = 128
MLA_HEADS = 8
OD_IN = 5 * HG_W + Q_LORA + KV_LORA + ROPE
OD_IN_PAD = 6144
D_FF = 5632
MACARON_W = 0.5
N_MOD = 9
ROPE_BASE = 10000.0
EPS = 1e-6
ATT_SCALE = HEAD_DIM ** -0.5
MLA_SCALE = (NOPE + ROPE) ** -0.5

MOD_ROWS = 16

V7X_VMEM_LIMIT = 56 * 1024 * 1024
TM = 1024
TM_FFN = 512
RC = 512
TM_OUT = 512
TF = 512
TN = 512
MOD_TN = 1024

BF16 = jnp.bfloat16
F32 = jnp.float32
HIGHEST = lax.Precision.HIGHEST


class Group(NamedTuple):
    nb: int
    L: int
    latent: bool

    @property
    def rows(self):
        return self.nb * self.L

    def mod_row(self, i, tm):
        return 1 + i // (self.L // tm) if self.latent else 0


PROMPT = Group(BATCH, SEQ, False)
LATENT = Group(DEC_BATCH, DEC_SEQ, True)


def _cparams(*sem):
    return pltpu.CompilerParams(dimension_semantics=sem, vmem_limit_bytes=V7X_VMEM_LIMIT)


def _rms(x):
    return x * lax.rsqrt(jnp.mean(x * x, axis=-1, keepdims=True) + EPS)


def _silu(x):
    return x * jax.nn.sigmoid(x)


def _dot(a, b):
    return jnp.dot(a, b, preferred_element_type=F32)


def _dot_nt(a, b):
    return lax.dot_general(a, b, (((1,), (1,)), ((), ())), preferred_element_type=F32)


def _dot_tn(a, b):
    return lax.dot_general(a, b, (((0,), (0,)), ((), ())), preferred_element_type=F32)


def _row_chunks(n_rows, body):
    def step(c, carry):
        body(pl.ds(pl.multiple_of(c * RC, RC), RC))
        return carry
    lax.fori_loop(0, n_rows // RC, step, 0)


def _modulated_norm(x_ref, h_scr, gain, shift):
    def body(rows):
        h_scr[rows, :] = (_rms(x_ref[rows, :]) * gain + shift).astype(BF16)
    _row_chunks(x_ref.shape[0], body)


def _mod_kernel(c_ref, w_ref, b_ref, o_ref):
    s = _silu(c_ref[...]).astype(BF16)
    o_ref[...] = _dot(s, w_ref[...].astype(BF16)) + b_ref[...]


def _modulation(cvec, mod_w, mod_b):
    depth = mod_w.shape[0]
    n = N_MOD * D_MODEL
    out = pl.pallas_call(
        _mod_kernel,
        out_shape=jax.ShapeDtypeStruct((depth, MOD_ROWS, n), F32),
        grid=(depth, n // MOD_TN),
        in_specs=[
            pl.BlockSpec((MOD_ROWS, D_MODEL), lambda l, j: (0, 0)),
            pl.BlockSpec((None, D_MODEL, MOD_TN), lambda l, j: (l, 0, j)),
            pl.BlockSpec((None, 1, MOD_TN), lambda l, j: (l, 0, j)),
        ],
        out_specs=pl.BlockSpec((None, MOD_ROWS, MOD_TN), lambda l, j: (l, 0, j)),
        compiler_params=_cparams("parallel", "parallel"),
        name="modulation",
    )(cvec, mod_w, mod_b.reshape(depth, 1, n))
    return out.reshape(depth, MOD_ROWS, N_MOD, D_MODEL)


def _ffn_kernel(j, x_ref, mod_ref, gpre_ref, gpost_ref, wg_ref, wu_ref, wd_ref, o_ref, h_scr):
    f = pl.program_id(1)

    @pl.when(f == 0)
    def _():
        gain = gpre_ref[...] * (1.0 + mod_ref[3 * j + 1:3 * j + 2, :])
        _modulated_norm(x_ref, h_scr, gain, mod_ref[3 * j:3 * j + 1, :])
        o_ref[...] = jnp.zeros_like(o_ref)

    def body(rows):
        h = h_scr[rows, :]
        a = (_silu(_dot(h, wg_ref[...])) * _dot(h, wu_ref[...])).astype(BF16)
        o_ref[rows, :] += _dot(a, wd_ref[...])
    _row_chunks(o_ref.shape[0], body)

    @pl.when(f == pl.num_programs(1) - 1)
    def _():
        gain = (MACARON_W * mod_ref[3 * j + 2:3 * j + 3, :]) * gpost_ref[...]

        def residual(rows):
            o_ref[rows, :] = x_ref[rows, :] + _rms(o_ref[rows, :]) * gain
        _row_chunks(o_ref.shape[0], residual)


def _ffn(grp, x, mods, j, g_pre, g_post, wg, wu, wd, l, s):
    return pl.pallas_call(
        functools.partial(_ffn_kernel, j),
        out_shape=jax.ShapeDtypeStruct((grp.rows, D_MODEL), F32),
        grid=(grp.rows // TM_FFN, D_FF // TF),
        in_specs=[
            pl.BlockSpec((TM_FFN, D_MODEL), lambda i, f: (i, 0)),
            pl.BlockSpec((None, N_MOD, D_MODEL), lambda i, f: (grp.mod_row(i, TM_FFN), 0, 0)),
            pl.BlockSpec((1, D_MODEL), lambda i, f: (0, 0)),
            pl.BlockSpec((1, D_MODEL), lambda i, f: (0, 0)),
            pl.BlockSpec((None, None, D_MODEL, TF), lambda i, f: (l, s, 0, f)),
            pl.BlockSpec((None, None, D_MODEL, TF), lambda i, f: (l, s, 0, f)),
            pl.BlockSpec((None, None, TF, D_MODEL), lambda i, f: (l, s, f, 0)),
        ],
        out_specs=pl.BlockSpec((TM_FFN, D_MODEL), lambda i, f: (i, 0)),
        scratch_shapes=[pltpu.VMEM((TM_FFN, D_MODEL), BF16)],
        compiler_params=_cparams("parallel", "arbitrary"),
        name="ffn",
    )(x, mods, g_pre.reshape(1, D_MODEL), g_post.reshape(1, D_MODEL), wg, wu, wd)


def _inproj_kernel(k0, x_ref, mod_ref, g_ref, w_ref, o_ref, t_ref, h_scr):
    k = pl.program_id(1)

    @pl.when(k == 0)
    def _():
        _modulated_norm(x_ref, h_scr, g_ref[...] * (1.0 + mod_ref[4:5, :]), mod_ref[3:4, :])

    y = _dot(h_scr[...], w_ref[...])
    o_ref[...] = y.astype(o_ref.dtype)

    @pl.when(k >= k0)
    def _():
        t_ref[...] = y


def _inproj(grp, x, mods, g, w, n_tail):
    n = w.shape[1]
    k0 = (n - n_tail) // TN
    return pl.pallas_call(
        functools.partial(_inproj_kernel, k0),
        out_shape=(jax.ShapeDtypeStruct((grp.rows, n), BF16), jax.ShapeDtypeStruct((grp.rows, n_tail), F32)),
        grid=(grp.rows // TM, n // TN),
        in_specs=[
            pl.BlockSpec((TM, D_MODEL), lambda i, k: (i, 0)),
            pl.BlockSpec((None, N_MOD, D_MODEL), lambda i, k: (grp.mod_row(i, TM), 0, 0)),
            pl.BlockSpec((1, D_MODEL), lambda i, k: (0, 0)),
            pl.BlockSpec((D_MODEL, TN), lambda i, k: (0, k)),
        ],
        out_specs=(pl.BlockSpec((TM, TN), lambda i, k: (i, k)),
                   pl.BlockSpec((TM, TN), lambda i, k: (i, jnp.maximum(k - k0, 0)))),
        scratch_shapes=[pltpu.VMEM((TM, D_MODEL), BF16)],
        compiler_params=_cparams("parallel", "arbitrary"),
        name="inproj",
    )(x, mods, g.reshape(1, D_MODEL), w)


def _outproj_kernel(a_ref, b_ref, wa_ref, wb_ref, x_ref, mod_ref, g_ref, o_ref):
    y = _dot(a_ref[...], wa_ref[...]) + _dot(b_ref[...], wb_ref[...])
    o_ref[...] = x_ref[...] + _rms(y) * (mod_ref[5:6, :] * g_ref[...])


def _outproj(grp, a, b, wa, wb, x, mods, g):
    half = a.shape[1]
    tm = TM_OUT
    return pl.pallas_call(
        _outproj_kernel,
        out_shape=jax.ShapeDtypeStruct((grp.rows, D_MODEL), F32),
        grid=(grp.rows // tm,),
        in_specs=[
            pl.BlockSpec((tm, half), lambda i: (i, 0)),
            pl.BlockSpec((tm, half), lambda i: (i, 0)),
            pl.BlockSpec((half, D_MODEL), lambda i: (0, 0)),
            pl.BlockSpec((half, D_MODEL), lambda i: (0, 0)),
            pl.BlockSpec((tm, D_MODEL), lambda i: (i, 0)),
            pl.BlockSpec((None, N_MOD, D_MODEL), lambda i: (grp.mod_row(i, tm), 0, 0)),
            pl.BlockSpec((1, D_MODEL), lambda i: (0, 0)),
        ],
        out_specs=pl.BlockSpec((tm, D_MODEL), lambda i: (i, 0)),
        compiler_params=_cparams("parallel"),
        name="outproj",
    )(a, b, wa, wb, x, mods, g.reshape(1, D_MODEL))


def _rope_tables(L, rot_dim, lane0, width):
    half = rot_dim // 2
    inv = ROPE_BASE ** (-jnp.arange(0, half, 2, dtype=F32) / half)
    pos = jnp.arange(L)
    ang_r = (pos // GRID_W).astype(F32)[:, None] * inv
    ang_c = (pos % GRID_W).astype(F32)[:, None] * inv
    cr, sr, cc, sc = jnp.cos(ang_r), jnp.sin(ang_r), jnp.cos(ang_c), jnp.sin(ang_c)
    z = jnp.zeros_like(sr)
    cos = jnp.concatenate([cr, cr, cc, cc], axis=-1)
    sin_a = jnp.concatenate([-sr, z, -sc, z], axis=-1)
    sin_b = jnp.concatenate([z, sr, z, sc], axis=-1)
    pad = ((0, 0), (lane0, width - lane0 - rot_dim))
    return jnp.pad(cos, pad, constant_values=1.0), jnp.pad(sin_a, pad), jnp.pad(sin_b, pad)


def _rope(x, cos, sin_a, sin_b, nf):
    w = x.shape[-1]
    return x * cos + pltpu.roll(x, w - nf, 1) * sin_a + pltpu.roll(x, nf, 1) * sin_b


DFT_SPLIT = 32


def _dft_matrices(L):
    s = jnp.arange(L, dtype=jnp.int32)[None, :]

    def trig(k):
        ang = ((k[:, None] * s) % (2 * L)).astype(F32) * (math.pi / L)
        return jnp.cos(ang), jnp.sin(ang)

    c1, s1 = trig(jnp.arange(0, L, DFT_SPLIT, dtype=jnp.int32))
    c0, s0 = trig(jnp.arange(DFT_SPLIT, dtype=jnp.int32))
    cos = (c1[:, None, :] * c0[None] - s1[:, None, :] * s0[None]).reshape(L, L)
    sin = (s1[:, None, :] * c0[None] + c1[:, None, :] * s0[None]).reshape(L, L)
    nyq = (1 - 2 * (jnp.arange(L, dtype=jnp.int32) % 2)).astype(F32)
    k = jnp.arange(L, dtype=jnp.int32)[:, None]
    msin = jnp.where(k == 0, nyq[None, :], -sin)
    msin_t = jnp.where(s == 0, nyq[:, None], -sin)
    return cos.astype(BF16), msin.astype(BF16), msin_t.astype(BF16)


def _filter_kernel(L, z_ref, w1_ref, b1_ref, w2_ref, b2_ref, fr_ref, w3f_ref, w3b_ref, dl_ref, cos_ref, msin_ref,
                   ka_ref, kb_ref, kc_ref):
    fr = fr_ref[...]
    z = z_ref[...]
    h = jnp.sin(fr * (jnp.dot(z, w1_ref[...], precision=HIGHEST, preferred_element_type=F32) + b1_ref[...]))
    h = jnp.sin(fr * (jnp.dot(h, w2_ref[...], precision=HIGHEST, preferred_element_type=F32) + b2_ref[...]))
    decay = jnp.exp(-z[:, 0:1] * dl_ref[...])
    hf = jnp.dot(h, w3f_ref[...], precision=HIGHEST, preferred_element_type=F32) * decay
    hb = jnp.dot(h, w3b_ref[...], precision=HIGHEST, preferred_element_type=F32) * decay
    row = lax.broadcasted_iota(jnp.int32, hf.shape, 0)
    row0 = row == 0
    hb = jnp.where(row0, 0.0, hb)
    even = hf + hb
    re = _dot(cos_ref[...], even.astype(BF16))
    im = _dot(msin_ref[...], (hf - hb).astype(BF16))
    nyq = jnp.sum(jnp.where(row % 2 == 0, even, -even), axis=0, keepdims=True)
    sc = jnp.where(row0, 0.5 / L, 1.0 / L)
    ka_ref[...] = re * sc
    kb_ref[...] = jnp.where(row0, 0.0, im) * sc
    kc_ref[...] = jnp.where(row0, nyq, re) * sc


def _hyena_spectra(L, tc, dft, f_w1, f_b1, f_w2, f_b2, f_w3, f_freq):
    t = jnp.linspace(0.0, 1.0, L, dtype=F32)[:, None]
    bands = (POS_EMB - 1) // 2
    w = 2.0 * math.pi * jnp.arange(L, dtype=F32)[:, None] / L
    fb = jnp.linspace(1e-4, bands - 1, bands, dtype=F32)[None, :]
    z = jnp.concatenate([t, jnp.cos(fb * w), -jnp.sin(fb * w)], axis=-1)
    z = jnp.pad(z, ((0, 0), (0, 128 - POS_EMB)))
    w1 = jnp.pad(f_w1, ((0, 128 - POS_EMB), (0, 0)))
    deltas = jnp.abs(jnp.linspace(math.log(HY_TARGET) / HY_SLOW_DECAY,
                                  math.log(HY_TARGET) / HY_FAST_DECAY, HY_W, dtype=F32))[None, :]
    nct = HY_W // tc
    fo = FILTER_ORDER
    row = lambda a: a.reshape(1, fo)
    kshape = jax.ShapeDtypeStruct((HY_ORDER, L, HY_W), F32)
    kspec = pl.BlockSpec((None, L, tc), lambda n, c: (n, 0, c))
    const = lambda shape: pl.BlockSpec(shape, lambda n, c: (0, 0))
    return pl.pallas_call(
        functools.partial(_filter_kernel, L),
        out_shape=(kshape, kshape, kshape),
        grid=(HY_ORDER, nct),
        in_specs=[
            const((L, 128)), const((128, fo)), const((1, fo)), const((fo, fo)), const((1, fo)), const((1, fo)),
            pl.BlockSpec((fo, tc), lambda n, c: (0, 2 * n * nct + c)),
            pl.BlockSpec((fo, tc), lambda n, c: (0, (2 * n + 1) * nct + c)),
            pl.BlockSpec((1, tc), lambda n, c: (0, c)),
            const((L, L)), const((L, L)),
        ],
        out_specs=(kspec, kspec, kspec),
        compiler_params=_cparams("parallel", "parallel"),
        name="hyena_filter",
    )(z, w1, row(f_b1), f_w2, row(f_b2), row(f_freq), f_w3, f_w3, deltas, dft[0], dft[1])


def _hyena_kernel(L, nseq, uv_ref, ug0_ref, ug1_ref, cwv_ref, cwg0_ref, cwg1_ref, cbv_ref, cbg0_ref, cbg1_ref,
                  hb_ref, ka_ref, kb_ref, kc_ref, cos_ref, msin_ref, msint_ref, o_ref):
    row = lax.broadcasted_iota(jnp.int32, (L, uv_ref.shape[1]), 0)
    first, last = row == 0, row == L - 1
    seqs = [slice(i * L, (i + 1) * L) for i in range(nseq)]

    def short_conv(u_ref, rows, w_ref, b_ref):
        u = u_ref[rows, :].astype(F32)
        prev = jnp.where(first, 0.0, pltpu.roll(u, 1, 0))
        nxt = jnp.where(last, 0.0, pltpu.roll(u, L - 1, 0))
        return b_ref[...] + prev * w_ref[0:1, :] + u * w_ref[1:2, :] + nxt * w_ref[2:3, :]

    z = [short_conv(uv_ref, r, cwv_ref, cbv_ref) for r in seqs]
    gates = [(short_conv(ug0_ref, r, cwg0_ref, cbg0_ref), short_conv(ug1_ref, r, cwg1_ref, cbg1_ref))
             for r in seqs]
    for n in range(HY_ORDER):
        zb = [zi.astype(BF16) for zi in z]
        zre = [_dot(cos_ref[...], b) for b in zb]
        zim = [_dot(msin_ref[...], b) for b in zb]
        ka, kb, kc = ka_ref[n], kb_ref[n], kc_ref[n]
        yre = [(re * ka - im * kb).astype(BF16) for re, im in zip(zre, zim)]
        yim = [(re * kb + im * kc).astype(BF16) for re, im in zip(zre, zim)]
        conv = [_dot(cos_ref[...], a) + _dot(msint_ref[...], b) for a, b in zip(yre, yim)]
        z = [g[n] * (cv + hb_ref[n:n + 1, :] * zi) for g, cv, zi in zip(gates, conv, z)]
    for r, zi in zip(seqs, z):
        o_ref[r, :] = zi.astype(o_ref.dtype)


def _hyena(grp, p, tc, nseq, spectra, dft, conv_w, conv_b, h_bias):
    L = grp.L
    nct = HY_W // tc
    ka, kb, kc = spectra
    u_spec = lambda j: pl.BlockSpec((nseq * L, tc), lambda c, b: (b, j * nct + c))
    cw_spec = lambda j: pl.BlockSpec((SHORT_CONV, tc), lambda c, b: (0, j * nct + c))
    cb_spec = lambda j: pl.BlockSpec((1, tc), lambda c, b: (0, j * nct + c))
    k_spec = pl.BlockSpec((HY_ORDER, L, tc), lambda c, b: (0, 0, c))
    m_spec = pl.BlockSpec((L, L), lambda c, b: (0, 0))
    cb = conv_b.reshape(1, -1)
    return pl.pallas_call(
        functools.partial(_hyena_kernel, L, nseq),
        out_shape=jax.ShapeDtypeStruct((grp.rows, HY_W), BF16),
        grid=(nct, grp.nb // nseq),
        in_specs=[
            u_spec(0), u_spec(1), u_spec(2), cw_spec(0), cw_spec(1), cw_spec(2),
            cb_spec(0), cb_spec(1), cb_spec(2),
            pl.BlockSpec((HY_ORDER, tc), lambda c, b: (0, c)),
            k_spec, k_spec, k_spec, m_spec, m_spec, m_spec,
        ],
        out_specs=pl.BlockSpec((nseq * L, tc), lambda c, b: (b, c)),
        compiler_params=_cparams("parallel", "arbitrary"),
        name="hyena",
    )(p, p, p, conv_w, conv_w, conv_w, cb, cb, cb, h_bias, ka, kb, kc, *dft)


def _gqa_kernel(L, window, has_ctx, has_rope, emit_kv, *refs):
    it = iter(refs)
    q_ref, k_ref, v_ref, sink_ref = next(it), next(it), next(it), next(it)
    if has_ctx:
        kc_ref, vc_ref = next(it), next(it)
    if has_rope:
        cos_ref, sa_ref, sb_ref = next(it), next(it), next(it)
    o_ref = next(it)
    if emit_kv:
        kn_ref, vn_ref = next(it), next(it)
        kn_ref[...] = k_ref[...]
        vn_ref[...] = v_ref[...]

    hk = pl.program_id(1)
    qb = HEAD_DIM
    nf = HEAD_DIM // 4
    k = k_ref[...]
    if has_rope:
        k = _rope(k, cos_ref[...], sa_ref[...], sb_ref[...], nf)
    k = k.astype(BF16)
    v = jnp.concatenate([v_ref[...].astype(BF16), jnp.ones((L, HEAD_DIM), BF16)], axis=1)
    if has_ctx:
        kc = kc_ref[...].astype(BF16)
        vc = jnp.concatenate([vc_ref[...].astype(BF16), jnp.ones((PAST_LEN, HEAD_DIM), BF16)], axis=1)
    band_bias = {}

    def bias_for(lo, hi, i):
        key = (lo - i * qb, hi - lo)
        if key not in band_bias:
            shape = (ATT_GROUP * qb, hi - lo)
            rel = key[0] + lax.broadcasted_iota(jnp.int32, shape, 1) - lax.broadcasted_iota(jnp.int32, shape, 0) % qb
            band_bias[key] = jnp.where(jnp.abs(rel) <= window, 0.0, -1e30)
        return band_bias[key]

    sink_row = sink_ref[...]
    lane = lax.broadcasted_iota(jnp.int32, sink_row.shape, 1)
    sinks = [jnp.sum(jnp.where(lane == hk * ATT_GROUP + g, sink_row, 0.0), axis=1, keepdims=True)
             for g in range(ATT_GROUP)]
    sink = jnp.concatenate([jnp.broadcast_to(s, (qb, 1)) for s in sinks], axis=0)

    for i in range(L // qb):
        rows = slice(i * qb, (i + 1) * qb)
        qs = []
        for g in range(ATT_GROUP):
            qg = q_ref[rows, g * HEAD_DIM:(g + 1) * HEAD_DIM].astype(F32)
            if has_rope:
                qg = _rope(qg, cos_ref[rows, :], sa_ref[rows, :], sb_ref[rows, :], nf)
            qs.append((qg * ATT_SCALE).astype(BF16))
        q = jnp.concatenate(qs, axis=0)
        if window is None:
            lo, hi = 0, L
        else:
            lo, hi = max(0, (i - 1) * qb), min(L, (i + 2) * qb)
        s = _dot_nt(q, k[lo:hi])
        if window is not None:
            s = s + bias_for(lo, hi, i)
        m = jnp.maximum(jnp.max(s, axis=-1, keepdims=True), sink)
        if has_ctx:
            sc = _dot_nt(q, kc)
            m = jnp.maximum(m, jnp.max(sc, axis=-1, keepdims=True))
        oa = _dot(jnp.exp(s - m).astype(BF16), v[lo:hi])
        if has_ctx:
            oa = oa + _dot(jnp.exp(sc - m).astype(BF16), vc)
        o = oa[:, :HEAD_DIM] / (oa[:, HEAD_DIM:] + jnp.exp(sink - m))
        for g in range(ATT_GROUP):
            o_ref[rows, g * HEAD_DIM:(g + 1) * HEAD_DIM] = o[g * qb:(g + 1) * qb].astype(o_ref.dtype)


def _gqa(grp, p, kv, sink, window=None, ctx=None, rope=None, emit_kv=False):
    L = grp.L
    gw = ATT_GROUP * HEAD_DIM
    q0 = 3 * HY_W // gw
    in_specs = [
        pl.BlockSpec((L, gw), lambda b, h: (b, q0 + h)),
        pl.BlockSpec((L, HEAD_DIM), lambda b, h: (b, h)),
        pl.BlockSpec((L, HEAD_DIM), lambda b, h: (b, ATT_KV_HEADS + h)),
        pl.BlockSpec((1, ATT_HEADS), lambda b, h: (0, 0)),
    ]
    args = [p, kv, kv, sink.reshape(1, ATT_HEADS)]
    if ctx is not None:
        kc, vc, e = ctx
        spec = pl.BlockSpec((None, None, None, PAST_LEN, HEAD_DIM), lambda b, h: (b, e, h, 0, 0))
        in_specs += [spec, spec]
        args += [kc, vc]
    if rope is not None:
        in_specs += [pl.BlockSpec((L, HEAD_DIM), lambda b, h: (0, 0))] * 3
        args += list(rope)
    out_shape = [jax.ShapeDtypeStruct((grp.rows, ATT_HEADS * HEAD_DIM), BF16)]
    out_specs = [pl.BlockSpec((L, gw), lambda b, h: (b, h))]
    if emit_kv:
        kv_shape = jax.ShapeDtypeStruct((grp.nb, 1, ATT_KV_HEADS, L, HEAD_DIM), F32)
        kv_spec = pl.BlockSpec((None, None, None, L, HEAD_DIM), lambda b, h: (b, 0, h, 0, 0))
        out_shape += [kv_shape, kv_shape]
        out_specs += [kv_spec, kv_spec]
    return pl.pallas_call(
        functools.partial(_gqa_kernel, L, window, ctx is not None, rope is not None, emit_kv),
        out_shape=tuple(out_shape),
        grid=(grp.nb, ATT_KV_HEADS),
        in_specs=in_specs,
        out_specs=tuple(out_specs),
        compiler_params=_cparams("parallel", "parallel"),
        name="gqa",
    )(*args)


def _chunk_cumsum(x, reverse):
    n = x.shape[0]
    pos = lax.broadcasted_iota(jnp.int32, x.shape, 0) % CHUNK
    s = 1
    while s < CHUNK:
        if reverse:
            x = x + jnp.where(pos < CHUNK - s, pltpu.roll(x, n - s, 0), 0.0)
        else:
            x = x + jnp.where(pos >= s, pltpu.roll(x, s, 0), 0.0)
        s *= 2
    return x


def _hgrn_kernel(L, has_state, emit_state, *refs):
    it = iter(refs)
    q_ref, ff_ref, fb_ref, i_ref, g_ref, lb_ref, norm_ref = (next(it) for _ in range(7))
    if has_state:
        s0_ref = next(it)
    o_ref = next(it)
    if emit_state:
        s_ref = next(it)
    acc_scr = next(it)

    q = _silu(q_ref[...].astype(F32))
    v = i_ref[...]
    a = lb_ref[...]
    e = jnp.exp(a - jnp.max(a, axis=0, keepdims=True))
    lb = e[1] / (e[0] + e[1])
    ci = lax.broadcasted_iota(jnp.int32, (CHUNK, CHUNK), 0)
    cj = lax.broadcasted_iota(jnp.int32, (CHUNK, CHUNK), 1)
    n_chunks = L // CHUNK
    chunks = [slice(n * CHUNK, (n + 1) * CHUNK) for n in range(n_chunks)]

    for d, fz_ref in enumerate((ff_ref, fb_ref)):
        lbd = lb[d:d + 1]
        f = lbd + (1.0 - lbd) * jax.nn.sigmoid(fz_ref[...].astype(F32))
        b = _chunk_cumsum(jnp.log(f), reverse=(d == 1))
        qd = (q * jnp.exp(b)).astype(BF16)
        kd32 = (1.0 - f) * jnp.exp(-b)
        kd = kd32.astype(BF16)
        keep = (cj <= ci) if d == 0 else (cj >= ci)
        decay, own = [], []
        for rows in chunks:
            end = rows.stop - 1 if d == 0 else rows.start
            dc = jnp.exp(b[end:end + 1])
            decay.append(dc)
            own.append(_dot_tn(v[rows], (kd32[rows] * dc).astype(BF16)))
        st = s0_ref[d].T if has_state else jnp.zeros((HEAD_DIM, HEAD_DIM), F32)
        entering = [None] * n_chunks
        for n in (range(n_chunks) if d == 0 else range(n_chunks - 1, -1, -1)):
            entering[n] = st.astype(BF16)
            st = st * decay[n] + own[n]
        if emit_state:
            s_ref[d] = st.T
        for n, rows in enumerate(chunks):
            att = jnp.where(keep, _dot_nt(qd[rows], kd[rows]), 0.0).astype(BF16)
            o = _dot(att, v[rows]) + _dot_nt(qd[rows], entering[n])
            if d == 0:
                acc_scr[rows, :] = o
            else:
                acc_scr[rows, :] += o

    o = _rms(acc_scr[...]) * norm_ref[...] * _silu(g_ref[...].astype(F32))
    o_ref[...] = o.astype(o_ref.dtype)


def _hgrn(grp, p, hg_lb, norm_g, state=None, emit_state=False):
    L = grp.L
    col = lambda j: pl.BlockSpec((L, HEAD_DIM), lambda b, h: (b, j * HG_HEADS + h))
    in_specs = [col(0), col(1), col(2), col(3), col(4),
                pl.BlockSpec((hg_lb.shape[0], 2, HEAD_DIM), lambda b, h: (0, 0, h)),
                pl.BlockSpec((1, HEAD_DIM), lambda b, h: (0, 0))]
    args = [p, p, p, p, p, hg_lb, norm_g.reshape(1, HEAD_DIM)]
    st_spec = lambda o: pl.BlockSpec((None, None, 2, None, HEAD_DIM, HEAD_DIM), lambda b, h: (b, o, 0, h, 0, 0))
    if state is not None:
        s0, o = state
        in_specs.append(st_spec(o))
        args.append(s0)
    out_shape = [jax.ShapeDtypeStruct((grp.rows, HG_W), BF16)]
    out_specs = [pl.BlockSpec((L, HEAD_DIM), lambda b, h: (b, h))]
    if emit_state:
        out_shape.append(jax.ShapeDtypeStruct((grp.nb, 1, 2, HG_HEADS, HEAD_DIM, HEAD_DIM), F32))
        out_specs.append(st_spec(0))
    return pl.pallas_call(
        functools.partial(_hgrn_kernel, L, state is not None, emit_state),
        out_shape=tuple(out_shape),
        grid=(grp.nb, HG_HEADS),
        in_specs=in_specs,
        out_specs=tuple(out_specs),
        scratch_shapes=[pltpu.VMEM((L, HEAD_DIM), F32)],
        compiler_params=_cparams("parallel", "parallel"),
        name="hgrn",
    )(*args)


MLA_QW = 256
KR_W = 128
OD_TAIL = 1024


def _mla_prep_kernel(has_rope, emit_kr, *refs):
    it = iter(refs)
    ql_ref, kvl_ref, kr_ref, qn_ref, kvn_ref, wq_ref = (next(it) for _ in range(6))
    if has_rope:
        qc_ref, qsa_ref, qsb_ref, kc_ref, ksa_ref, ksb_ref = (next(it) for _ in range(6))
    q_ref, ckv_ref, kro_ref = next(it), next(it), next(it)
    if emit_kr:
        krn_ref = next(it)
        krn_ref[...] = kr_ref[:, :ROPE]

    nf = ROPE // 4
    qn = (_rms(ql_ref[...]) * qn_ref[...]).astype(BF16)
    q = _dot(qn, wq_ref[...])
    for h in range(MLA_HEADS):
        cols = slice(h * MLA_QW, (h + 1) * MLA_QW)
        qh = q[:, cols]
        if has_rope:
            qh = _rope(qh, qc_ref[...], qsa_ref[...], qsb_ref[...], nf)
        q_ref[:, cols] = qh.astype(q_ref.dtype)
    ckv_ref[...] = _rms(kvl_ref[...]) * kvn_ref[...]
    kr = kr_ref[...]
    if has_rope:
        kr = _rope(kr, kc_ref[...], ksa_ref[...], ksb_ref[...], nf)
    kro_ref[...] = kr.astype(kro_ref.dtype)


def _mla_prep(grp, tail, q_norm, kv_norm, wq, rope=None, emit_kr=False):
    tm = min(512, grp.L)
    per = grp.L // tm
    n_rows = grp.rows
    in_specs = [
        pl.BlockSpec((tm, Q_LORA), lambda i: (i, 0)),
        pl.BlockSpec((tm, KV_LORA), lambda i: (i, Q_LORA // KV_LORA)),
        pl.BlockSpec((tm, KR_W), lambda i: (i, (Q_LORA + KV_LORA) // KR_W)),
        pl.BlockSpec((1, Q_LORA), lambda i: (0, 0)),
        pl.BlockSpec((1, KV_LORA), lambda i: (0, 0)),
        pl.BlockSpec((Q_LORA, MLA_HEADS * MLA_QW), lambda i: (0, 0)),
    ]
    args = [tail, tail, tail, q_norm.reshape(1, Q_LORA), kv_norm.reshape(1, KV_LORA), wq]
    if rope is not None:
        in_specs += [pl.BlockSpec((tm, MLA_QW), lambda i: (i % per, 0))] * 3
        in_specs += [pl.BlockSpec((tm, KR_W), lambda i: (i % per, 0))] * 3
        args += list(rope)
    out_shape = [jax.ShapeDtypeStruct((n_rows, MLA_HEADS * MLA_QW), BF16),
                 jax.ShapeDtypeStruct((n_rows, KV_LORA), F32),
                 jax.ShapeDtypeStruct((n_rows, KR_W), BF16)]
    out_specs = [pl.BlockSpec((tm, MLA_HEADS * MLA_QW), lambda i: (i, 0)),
                 pl.BlockSpec((tm, KV_LORA), lambda i: (i, 0)),
                 pl.BlockSpec((tm, KR_W), lambda i: (i, 0))]
    if emit_kr:
        out_shape.append(jax.ShapeDtypeStruct((n_rows, ROPE), F32))
        out_specs.append(pl.BlockSpec((tm, ROPE), lambda i: (i, 0)))
    return pl.pallas_call(
        functools.partial(_mla_prep_kernel, rope is not None, emit_kr),
        out_shape=tuple(out_shape),
        grid=(n_rows // tm,),
        in_specs=in_specs,
        out_specs=tuple(out_specs),
        compiler_params=_cparams("parallel"),
        name="mla_prep",
    )(*args)


def _mla_attn_kernel(L, has_ctx, *refs):
    it = iter(refs)
    q_ref, ckv_ref, kr_ref, wkv_ref = (next(it) for _ in range(4))
    if has_ctx:
        cckv_ref, ckr_ref = next(it), next(it)
    o_ref = next(it)

    ckv = ckv_ref[...].astype(BF16)
    kr = kr_ref[...]
    if has_ctx:
        ckv = jnp.concatenate([ckv, cckv_ref[...].astype(BF16)], axis=0)
        kr = jnp.concatenate([kr, ckr_ref[...].astype(BF16)], axis=0)
    kv = _dot(ckv, wkv_ref[...])
    kh = jnp.concatenate([kv[:, :NOPE].astype(BF16), kr], axis=1)
    vh = jnp.concatenate([kv[:, NOPE:].astype(BF16), jnp.ones((kv.shape[0], V_DIM), BF16)], axis=1)
    qb = min(L, 256)
    for i in range(L // qb):
        rows = slice(i * qb, (i + 1) * qb)
        s = _dot_nt(q_ref[rows, :], kh) * MLA_SCALE
        m = jnp.max(s, axis=-1, keepdims=True)
        oa = _dot(jnp.exp(s - m).astype(BF16), vh)
        o_ref[rows, :] = (oa[:, :V_DIM] / oa[:, V_DIM:]).astype(o_ref.dtype)


def _mla_attn(grp, q, ckv, kr, wkv, ctx=None):
    L = grp.L
    in_specs = [
        pl.BlockSpec((L, MLA_QW), lambda b, h: (b, h)),
        pl.BlockSpec((L, KV_LORA), lambda b, h: (b, 0)),
        pl.BlockSpec((L, KR_W), lambda b, h: (b, 0)),
        pl.BlockSpec((KV_LORA, NOPE + V_DIM), lambda b, h: (0, h)),
    ]
    args = [q, ckv, kr, wkv]
    if ctx is not None:
        cckv, ckr, o = ctx
        in_specs += [pl.BlockSpec((None, None, PAST_LEN, KV_LORA), lambda b, h: (b, o, 0, 0)),
                     pl.BlockSpec((None, None, PAST_LEN, KR_W), lambda b, h: (b, o, 0, 0))]
        args += [cckv, ckr]
    return pl.pallas_call(
        functools.partial(_mla_attn_kernel, L, ctx is not None),
        out_shape=jax.ShapeDtypeStruct((grp.rows, MLA_HEADS * V_DIM), BF16),
        grid=(grp.nb, MLA_HEADS),
        in_specs=in_specs,
        out_specs=pl.BlockSpec((L, V_DIM), lambda b, h: (b, h)),
        compiler_params=_cparams("parallel", "parallel"),
        name="mla_attn",
    )(*args)


def kernel(x_prompt, x_sample, c, c_ctx, cache_attn_k, cache_attn_v, cache_mla_ckv, cache_mla_krope, state_hgrn, mod_w, mod_b, norm_g, ffn_wg, ffn_wu, ffn_wd, ev_w_in, ev_w_out, hy_conv_w, hy_conv_b, hy_f_w1, hy_f_b1, hy_f_w2, hy_f_b2, hy_f_w3, hy_f_freq, hy_bias, attn_sink, od_w_in, od_w_out, hg_lb, hg_norm, mla_q_norm, mla_w_qb, mla_kv_norm, mla_w_kvb):
    depth = mod_w.shape[0]
    groups = (PROMPT, LATENT)
    xs = [x_prompt.reshape(PROMPT.rows, D_MODEL), x_sample.reshape(LATENT.rows, D_MODEL)]
    cvec = jnp.concatenate([c_ctx[None, :], c, jnp.zeros((MOD_ROWS - 1 - DEC_BATCH, D_MODEL), F32)], axis=0)
    mods_all = _modulation(cvec, mod_w, mod_b)

    wg, wu, wd = ffn_wg.astype(BF16), ffn_wu.astype(BF16), ffn_wd.astype(BF16)
    hy_tc = {SEQ: 512, DEC_SEQ: 256}
    hy_nseq = {SEQ: 4, DEC_SEQ: 2}

    new_k = new_v = new_ckv = new_kr = new_s = None
    for l in range(depth):
        mods = mods_all[l]
        xs = [_ffn(grp, x, mods, 0, norm_g[l, 0], norm_g[l, 1], wg, wu, wd, l, 0) for grp, x in zip(groups, xs)]
        if l % 2 == 0:
            e = l // 2
            w_in = ev_w_in[e].astype(BF16)
            w_out = ev_w_out[e].astype(BF16)
            kv_cols = 2 * ATT_KV_HEADS * HEAD_DIM
            mix = []
            for grp, x in zip(groups, xs):
                p, kv = _inproj(grp, x, mods, norm_g[l, 2], w_in, kv_cols)
                dft = _dft_matrices(grp.L)
                tc = hy_tc[grp.L]
                spectra = _hyena_spectra(grp.L, tc, dft, hy_f_w1[e], hy_f_b1[e], hy_f_w2[e], hy_f_b2[e],
                                         hy_f_w3[e], hy_f_freq[e])
                hy = _hyena(grp, p, tc, hy_nseq[grp.L], spectra, dft, hy_conv_w[e], hy_conv_b[e], hy_bias[e])
                if grp.latent:
                    rope = _rope_tables(grp.L, HEAD_DIM, 0, HEAD_DIM)
                    (att,) = _gqa(grp, p, kv, attn_sink[e], window=WINDOW,
                                  ctx=(cache_attn_k, cache_attn_v, e), rope=rope)
                else:
                    att, new_k, new_v = _gqa(grp, p, kv, attn_sink[e], emit_kv=True)
                mix.append((hy, att))
        else:
            o = l // 2
            w_in = jnp.pad(od_w_in[o], ((0, 0), (0, OD_IN_PAD - OD_IN))).astype(BF16)
            w_out = od_w_out[o].astype(BF16)
            wq = mla_w_qb[o].reshape(Q_LORA, MLA_HEADS, NOPE + ROPE)
            wq = jnp.pad(wq, ((0, 0), (0, 0), (0, MLA_QW - NOPE - ROPE))).reshape(Q_LORA, -1).astype(BF16)
            wkv = mla_w_kvb[o].astype(BF16)
            mix = []
            for grp, x in zip(groups, xs):
                p, tail = _inproj(grp, x, mods, norm_g[l, 2], w_in, OD_TAIL)
                if grp.latent:
                    (hg,) = _hgrn(grp, p, hg_lb, hg_norm[o], state=(state_hgrn, o))
                    rope = (_rope_tables(grp.L, ROPE, NOPE, MLA_QW) + _rope_tables(grp.L, ROPE, 0, KR_W))
                    q, ckv, kr = _mla_prep(grp, tail, mla_q_norm[o], mla_kv_norm[o], wq, rope=rope)
                    ckr = jnp.pad(cache_mla_krope, ((0, 0), (0, 0), (0, 0), (0, KR_W - ROPE)))
                    att = _mla_attn(grp, q, ckv, kr, wkv, ctx=(cache_mla_ckv, ckr, o))
                else:
                    hg, new_s = _hgrn(grp, p, hg_lb, hg_norm[o], emit_state=True)
                    q, ckv, kr, kr_raw = _mla_prep(grp, tail, mla_q_norm[o], mla_kv_norm[o], wq, emit_kr=True)
                    att = _mla_attn(grp, q, ckv, kr, wkv)
                    new_ckv = ckv.reshape(BATCH, 1, SEQ, KV_LORA)
                    new_kr = kr_raw.reshape(BATCH, 1, SEQ, ROPE)
                mix.append((hg, att))
        half = w_out.shape[0] // 2
        xs = [_outproj(grp, a, b, w_out[:half], w_out[half:], x, mods, norm_g[l, 3])
              for grp, x, (a, b) in zip(groups, xs, mix)]
        xs = [_ffn(grp, x, mods, 2, norm_g[l, 4], norm_g[l, 5], wg, wu, wd, l, 1) for grp, x in zip(groups, xs)]

    y_prompt = xs[0].reshape(BATCH, SEQ, D_MODEL)
    y_sample = xs[1].reshape(DEC_BATCH, DEC_SEQ, D_MODEL)
    return (y_prompt, y_sample, new_k, new_v, new_ckv, new_kr, new_s)
```

```python
import functools
import math
from typing import NamedTuple

import jax
import jax.numpy as jnp
from jax import lax
from jax.experimental import pallas as pl
from jax.experimental.pallas import tpu as pltpu

D_MODEL = 2048
BATCH = 16
SEQ = 256
DEC_BATCH = 8
DEC_SEQ = 1024
PAST_LEN = 512
GRID_W = 64
HEAD_DIM = 128
HY_W = 1024
HY_ORDER = 2
SHORT_CONV = 3
POS_EMB = 33
FILTER_ORDER = 64
HY_FAST_DECAY = 0.3
HY_SLOW_DECAY = 1.5
HY_TARGET = 1e-2
ATT_HEADS = 8
ATT_KV_HEADS = 2
ATT_GROUP = 4
WINDOW = 128
EV_IN = 3 * HY_W + (ATT_HEADS + 2 * ATT_KV_HEADS) * HEAD_DIM
HG_W = 1024
HG_HEADS = 8
CHUNK = 64
Q_LORA = 512
KV_LORA = 256
NOPE = 128
ROPE = 64
V_DIM = 128
MLA_HEADS = 8
OD_IN = 5 * HG_W + Q_LORA + KV_LORA + ROPE
OD_IN_PAD = 6144
D_FF = 5632
MACARON_W = 0.5
N_MOD = 9
ROPE_BASE = 10000.0
EPS = 1e-6
ATT_SCALE = HEAD_DIM ** -0.5
MLA_SCALE = (NOPE + ROPE) ** -0.5

MOD_ROWS = 16

V7X_VMEM_LIMIT = 56 * 1024 * 1024
TM = 512
RC = 512
TF = 512
TN = 1536
MOD_TN = 1024

BF16 = jnp.bfloat16
F32 = jnp.float32
HIGHEST = lax.Precision.HIGHEST


class Group(NamedTuple):
    nb: int
    L: int
    latent: bool

    @property
    def rows(self):
        return self.nb * self.L

    def mod_row(self, i, tm):
        return 1 + i // (self.L // tm) if self.latent else 0


PROMPT = Group(BATCH, SEQ, False)
LATENT = Group(DEC_BATCH, DEC_SEQ, True)


def _cparams(*sem):
    return pltpu.CompilerParams(dimension_semantics=sem, vmem_limit_bytes=V7X_VMEM_LIMIT)


def _rms(x):
    return x * lax.rsqrt(jnp.mean(x * x, axis=-1, keepdims=True) + EPS)


def _silu(x):
    return x * jax.nn.sigmoid(x)


def _dot(a, b):
    return jnp.dot(a, b, preferred_element_type=F32)


def _dot_nt(a, b):
    return lax.dot_general(a, b, (((1,), (1,)), ((), ())), preferred_element_type=F32)


def _dot_tn(a, b):
    return lax.dot_general(a, b, (((0,), (0,)), ((), ())), preferred_element_type=F32)


def _row_chunks(n_rows, body, rc=RC):
    def step(c, carry):
        body(pl.ds(pl.multiple_of(c * rc, rc), rc))
        return carry
    lax.fori_loop(0, n_rows // rc, step, 0)


def _modulated_norm(x_ref, h_scr, gain, shift):
    def body(rows):
        h_scr[rows, :] = (_rms(x_ref[rows, :]) * gain + shift).astype(BF16)
    _row_chunks(x_ref.shape[0], body)


def _mod_kernel(c_ref, w_ref, b_ref, o_ref):
    s = _silu(c_ref[...]).astype(BF16)
    o_ref[...] = _dot(s, w_ref[...].astype(BF16)) + b_ref[...]


def _modulation(cvec, mod_w, mod_b):
    depth = mod_w.shape[0]
    n = N_MOD * D_MODEL
    out = pl.pallas_call(
        _mod_kernel,
        out_shape=jax.ShapeDtypeStruct((depth, MOD_ROWS, n), F32),
        grid=(depth, n // MOD_TN),
        in_specs=[
            pl.BlockSpec((MOD_ROWS, D_MODEL), lambda l, j: (0, 0)),
            pl.BlockSpec((None, D_MODEL, MOD_TN), lambda l, j: (l, 0, j)),
            pl.BlockSpec((None, 1, MOD_TN), lambda l, j: (l, 0, j)),
        ],
        out_specs=pl.BlockSpec((None, MOD_ROWS, MOD_TN), lambda l, j: (l, 0, j)),
        compiler_params=_cparams("parallel", "parallel"),
        name="modulation",
    )(cvec, mod_w, mod_b.reshape(depth, 1, n))
    return out.reshape(depth, MOD_ROWS, N_MOD, D_MODEL)


def _ffn_kernel(j, x_ref, mod_ref, gpre_ref, gpost_ref, wg_ref, wu_ref, wd_ref, o_ref, h_scr):
    f = pl.program_id(1)

    @pl.when(f == 0)
    def _():
        gain = gpre_ref[...] * (1.0 + mod_ref[3 * j + 1:3 * j + 2, :])
        _modulated_norm(x_ref, h_scr, gain, mod_ref[3 * j:3 * j + 1, :])

    def partial_down(rows):
        h = h_scr[rows, :]
        a = (_silu(_dot(h, wg_ref[...])) * _dot(h, wu_ref[...])).astype(BF16)
        return _dot(a, wd_ref[...])

    @pl.when(f == 0)
    def _():
        def first(rows):
            o_ref[rows, :] = partial_down(rows)
        _row_chunks(o_ref.shape[0], first)

    @pl.when(f > 0)
    def _():
        def accumulate(rows):
            o_ref[rows, :] += partial_down(rows)
        _row_chunks(o_ref.shape[0], accumulate)

    @pl.when(f == pl.num_programs(1) - 1)
    def _():
        gain = (MACARON_W * mod_ref[3 * j + 2:3 * j + 3, :]) * gpost_ref[...]

        def residual(rows):
            o_ref[rows, :] = x_ref[rows, :] + _rms(o_ref[rows, :]) * gain
        _row_chunks(o_ref.shape[0], residual)


def _ffn(grp, x, mods, j, g_pre, g_post, wg, wu, wd, l, s):
    return pl.pallas_call(
        functools.partial(_ffn_kernel, j),
        out_shape=jax.ShapeDtypeStruct((grp.rows, D_MODEL), F32),
        grid=(grp.rows // TM, D_FF // TF),
        in_specs=[
            pl.BlockSpec((TM, D_MODEL), lambda i, f: (i, 0)),
            pl.BlockSpec((None, N_MOD, D_MODEL), lambda i, f: (grp.mod_row(i, TM), 0, 0)),
            pl.BlockSpec((1, D_MODEL), lambda i, f: (0, 0)),
            pl.BlockSpec((1, D_MODEL), lambda i, f: (0, 0)),
            pl.BlockSpec((None, None, D_MODEL, TF), lambda i, f: (l, s, 0, f)),
            pl.BlockSpec((None, None, D_MODEL, TF), lambda i, f: (l, s, 0, f)),
            pl.BlockSpec((None, None, TF, D_MODEL), lambda i, f: (l, s, f, 0)),
        ],
        out_specs=pl.BlockSpec((TM, D_MODEL), lambda i, f: (i, 0)),
        scratch_shapes=[pltpu.VMEM((TM, D_MODEL), BF16)],
        compiler_params=_cparams("parallel", "arbitrary"),
        name="ffn",
    )(x, mods, g_pre.reshape(1, D_MODEL), g_post.reshape(1, D_MODEL), wg, wu, wd)


def _inproj_kernel(x_ref, mod_ref, g_ref, w_ref, o_ref, t_ref, h_scr):
    k = pl.program_id(1)

    @pl.when(k == 0)
    def _():
        _modulated_norm(x_ref, h_scr, g_ref[...] * (1.0 + mod_ref[4:5, :]), mod_ref[3:4, :])

    y = _dot(h_scr[...], w_ref[...])
    o_ref[...] = y.astype(o_ref.dtype)

    @pl.when(k == pl.num_programs(1) - 1)
    def _():
        t_ref[...] = y[:, y.shape[1] - t_ref.shape[1]:]


def _inproj(grp, x, mods, g, w, n_tail):
    n = w.shape[1]
    return pl.pallas_call(
        _inproj_kernel,
        out_shape=(jax.ShapeDtypeStruct((grp.rows, n), BF16), jax.ShapeDtypeStruct((grp.rows, n_tail), F32)),
        grid=(grp.rows // TM, n // TN),
        in_specs=[
            pl.BlockSpec((TM, D_MODEL), lambda i, k: (i, 0)),
            pl.BlockSpec((None, N_MOD, D_MODEL), lambda i, k: (grp.mod_row(i, TM), 0, 0)),
            pl.BlockSpec((1, D_MODEL), lambda i, k: (0, 0)),
            pl.BlockSpec((D_MODEL, TN), lambda i, k: (0, k)),
        ],
        out_specs=(pl.BlockSpec((TM, TN), lambda i, k: (i, k)),
                   pl.BlockSpec((TM, n_tail), lambda i, k: (i, 0))),
        scratch_shapes=[pltpu.VMEM((TM, D_MODEL), BF16)],
        compiler_params=_cparams("parallel", "arbitrary"),
        name="inproj",
    )(x, mods, g.reshape(1, D_MODEL), w)


def _outproj_kernel(a_ref, b_ref, wa_ref, wb_ref, x_ref, mod_ref, g_ref, o_ref):
    y = _dot(a_ref[...], wa_ref[...]) + _dot(b_ref[...], wb_ref[...])
    o_ref[...] = x_ref[...] + _rms(y) * (mod_ref[5:6, :] * g_ref[...])


def _outproj(grp, a, b, wa, wb, x, mods, g):
    half = a.shape[1]
    tm = TM
    return pl.pallas_call(
        _outproj_kernel,
        out_shape=jax.ShapeDtypeStruct((grp.rows, D_MODEL), F32),
        grid=(grp.rows // tm,),
        in_specs=[
            pl.BlockSpec((tm, half), lambda i: (i, 0)),
            pl.BlockSpec((tm, half), lambda i: (i, 0)),
            pl.BlockSpec((half, D_MODEL), lambda i: (0, 0)),
            pl.BlockSpec((half, D_MODEL), lambda i: (0, 0)),
            pl.BlockSpec((tm, D_MODEL), lambda i: (i, 0)),
            pl.BlockSpec((None, N_MOD, D_MODEL), lambda i: (grp.mod_row(i, tm), 0, 0)),
            pl.BlockSpec((1, D_MODEL), lambda i: (0, 0)),
        ],
        out_specs=pl.BlockSpec((tm, D_MODEL), lambda i: (i, 0)),
        compiler_params=_cparams("parallel"),
        name="outproj",
    )(a, b, wa, wb, x, mods, g.reshape(1, D_MODEL))


def _rope_tables(L, rot_dim, lane0, width):
    half = rot_dim // 2
    inv = ROPE_BASE ** (-jnp.arange(0, half, 2, dtype=F32) / half)
    pos = jnp.arange(L)
    ang_r = (pos // GRID_W).astype(F32)[:, None] * inv
    ang_c = (pos % GRID_W).astype(F32)[:, None] * inv
    cr, sr, cc, sc = jnp.cos(ang_r), jnp.sin(ang_r), jnp.cos(ang_c), jnp.sin(ang_c)
    z = jnp.zeros_like(sr)
    cos = jnp.concatenate([cr, cr, cc, cc], axis=-1)
    sin_a = jnp.concatenate([-sr, z, -sc, z], axis=-1)
    sin_b = jnp.concatenate([z, sr, z, sc], axis=-1)
    pad = ((0, 0), (lane0, width - lane0 - rot_dim))
    return jnp.pad(cos, pad, constant_values=1.0), jnp.pad(sin_a, pad), jnp.pad(sin_b, pad)


def _rope(x, cos, sin_a, sin_b, nf):
    w = x.shape[-1]
    return x * cos + pltpu.roll(x, w - nf, 1) * sin_a + pltpu.roll(x, nf, 1) * sin_b


DFT_SPLIT = 32


def _dft_matrices(L):
    s = jnp.arange(L, dtype=jnp.int32)[None, :]

    def trig(k):
        ang = ((k[:, None] * s) % (2 * L)).astype(F32) * (math.pi / L)
        return jnp.cos(ang), jnp.sin(ang)

    c1, s1 = trig(jnp.arange(0, L, DFT_SPLIT, dtype=jnp.int32))
    c0, s0 = trig(jnp.arange(DFT_SPLIT, dtype=jnp.int32))
    cos = (c1[:, None, :] * c0[None] - s1[:, None, :] * s0[None]).reshape(L, L)
    sin = (s1[:, None, :] * c0[None] + c1[:, None, :] * s0[None]).reshape(L, L)
    nyq = (1 - 2 * (jnp.arange(L, dtype=jnp.int32) % 2)).astype(F32)
    k = jnp.arange(L, dtype=jnp.int32)[:, None]
    msin = jnp.where(k == 0, nyq[None, :], -sin)
    msin_t = jnp.where(s == 0, nyq[:, None], -sin)
    return cos.astype(BF16), msin.astype(BF16), msin_t.astype(BF16)


def _filter_kernel(L, z_ref, w1_ref, b1_ref, w2_ref, b2_ref, fr_ref, w3f_ref, w3b_ref, dl_ref, cos_ref, msin_ref,
                   ka_ref, kb_ref, kc_ref, h_scr):
    z = z_ref[...]

    @pl.when((pl.program_id(0) == 0) & (pl.program_id(1) == 0))
    def _():
        fr = fr_ref[...]
        h1 = jnp.sin(fr * (jnp.dot(z, w1_ref[...], precision=HIGHEST, preferred_element_type=F32) + b1_ref[...]))
        h_scr[...] = jnp.sin(
            fr * (jnp.dot(h1, w2_ref[...], precision=HIGHEST, preferred_element_type=F32) + b2_ref[...]))

    h = h_scr[...]
    decay = jnp.exp(-z[:, 0:1] * dl_ref[...])
    hf = jnp.dot(h, w3f_ref[...], precision=HIGHEST, preferred_element_type=F32) * decay
    hb = jnp.dot(h, w3b_ref[...], precision=HIGHEST, preferred_element_type=F32) * decay
    row = lax.broadcasted_iota(jnp.int32, hf.shape, 0)
    row0 = row == 0
    hb = jnp.where(row0, 0.0, hb)
    even = hf + hb
    re = _dot(cos_ref[...], even.astype(BF16))
    im = _dot(msin_ref[...], (hf - hb).astype(BF16))
    nyq = jnp.sum(jnp.where(row % 2 == 0, even, -even), axis=0, keepdims=True)
    sc = jnp.where(row0, 0.5 / L, 1.0 / L)
    ka_ref[...] = re * sc
    kb_ref[...] = jnp.where(row0, 0.0, im) * sc
    kc_ref[...] = jnp.where(row0, nyq, re) * sc


def _hyena_spectra(L, tc, dft, f_w1, f_b1, f_w2, f_b2, f_w3, f_freq):
    t = jnp.linspace(0.0, 1.0, L, dtype=F32)[:, None]
    bands = (POS_EMB - 1) // 2
    w = 2.0 * math.pi * jnp.arange(L, dtype=F32)[:, None] / L
    fb = jnp.linspace(1e-4, bands - 1, bands, dtype=F32)[None, :]
    z = jnp.concatenate([t, jnp.cos(fb * w), -jnp.sin(fb * w)], axis=-1)
    z = jnp.pad(z, ((0, 0), (0, 128 - POS_EMB)))
    w1 = jnp.pad(f_w1, ((0, 128 - POS_EMB), (0, 0)))
    deltas = jnp.abs(jnp.linspace(math.log(HY_TARGET) / HY_SLOW_DECAY,
                                  math.log(HY_TARGET) / HY_FAST_DECAY, HY_W, dtype=F32))[None, :]
    nct = HY_W // tc
    fo = FILTER_ORDER
    row = lambda a: a.reshape(1, fo)
    kshape = jax.ShapeDtypeStruct((HY_ORDER, L, HY_W), F32)
    kspec = pl.BlockSpec((None, L, tc), lambda n, c: (n, 0, c))
    const = lambda shape: pl.BlockSpec(shape, lambda n, c: (0, 0))
    return pl.pallas_call(
        functools.partial(_filter_kernel, L),
        out_shape=(kshape, kshape, kshape),
        grid=(HY_ORDER, nct),
        in_specs=[
            const((L, 128)), const((128, fo)), const((1, fo)), const((fo, fo)), const((1, fo)), const((1, fo)),
            pl.BlockSpec((fo, tc), lambda n, c: (0, 2 * n * nct + c)),
            pl.BlockSpec((fo, tc), lambda n, c: (0, (2 * n + 1) * nct + c)),
            pl.BlockSpec((1, tc), lambda n, c: (0, c)),
            const((L, L)), const((L, L)),
        ],
        out_specs=(kspec, kspec, kspec),
        scratch_shapes=[pltpu.VMEM((L, fo), F32)],
        compiler_params=_cparams("arbitrary", "arbitrary"),
        name="hyena_filter",
    )(z, w1, row(f_b1), f_w2, row(f_b2), row(f_freq), f_w3, f_w3, deltas, dft[0], dft[1])


def _hyena_kernel(L, nseq, uv_ref, ug0_ref, ug1_ref, cwv_ref, cwg0_ref, cwg1_ref, cbv_ref, cbg0_ref, cbg1_ref,
                  hb_ref, ka_ref, kb_ref, kc_ref, cos_ref, msin_ref, msint_ref, o_ref):
    row = lax.broadcasted_iota(jnp.int32, (L, uv_ref.shape[1]), 0)
    first, last = row == 0, row == L - 1
    seqs = [slice(i * L, (i + 1) * L) for i in range(nseq)]

    def short_conv(u_ref, rows, w_ref, b_ref):
        u = u_ref[rows, :].astype(F32)
        prev = jnp.where(first, 0.0, pltpu.roll(u, 1, 0))
        nxt = jnp.where(last, 0.0, pltpu.roll(u, L - 1, 0))
        return b_ref[...] + prev * w_ref[0:1, :] + u * w_ref[1:2, :] + nxt * w_ref[2:3, :]

    z = [short_conv(uv_ref, r, cwv_ref, cbv_ref) for r in seqs]
    gates = [(short_conv(ug0_ref, r, cwg0_ref, cbg0_ref), short_conv(ug1_ref, r, cwg1_ref, cbg1_ref))
             for r in seqs]
    for n in range(HY_ORDER):
        zb = [zi.astype(BF16) for zi in z]
        zre = [_dot(cos_ref[...], b) for b in zb]
        zim = [_dot(msin_ref[...], b) for b in zb]
        ka, kb, kc = ka_ref[n], kb_ref[n], kc_ref[n]
        yre = [(re * ka - im * kb).astype(BF16) for re, im in zip(zre, zim)]
        yim = [(re * kb + im * kc).astype(BF16) for re, im in zip(zre, zim)]
        conv = [_dot(cos_ref[...], a) + _dot(msint_ref[...], b) for a, b in zip(yre, yim)]
        z = [g[n] * (cv + hb_ref[n:n + 1, :] * zi) for g, cv, zi in zip(gates, conv, z)]
    for r, zi in zip(seqs, z):
        o_ref[r, :] = zi.astype(o_ref.dtype)


def _hyena(grp, p, tc, nseq, spectra, dft, conv_w, conv_b, h_bias):
    L = grp.L
    nct = HY_W // tc
    ka, kb, kc = spectra
    u_spec = lambda j: pl.BlockSpec((nseq * L, tc), lambda c, b: (b, j * nct + c))
    cw_spec = lambda j: pl.BlockSpec((SHORT_CONV, tc), lambda c, b: (0, j * nct + c))
    cb_spec = lambda j: pl.BlockSpec((1, tc), lambda c, b: (0, j * nct + c))
    k_spec = pl.BlockSpec((HY_ORDER, L, tc), lambda c, b: (0, 0, c))
    m_spec = pl.BlockSpec((L, L), lambda c, b: (0, 0))
    cb = conv_b.reshape(1, -1)
    return pl.pallas_call(
        functools.partial(_hyena_kernel, L, nseq),
        out_shape=jax.ShapeDtypeStruct((grp.rows, HY_W), BF16),
        grid=(nct, grp.nb // nseq),
        in_specs=[
            u_spec(0), u_spec(1), u_spec(2), cw_spec(0), cw_spec(1), cw_spec(2),
            cb_spec(0), cb_spec(1), cb_spec(2),
            pl.BlockSpec((HY_ORDER, tc), lambda c, b: (0, c)),
            k_spec, k_spec, k_spec, m_spec, m_spec, m_spec,
        ],
        out_specs=pl.BlockSpec((nseq * L, tc), lambda c, b: (b, c)),
        compiler_params=_cparams("parallel", "arbitrary"),
        name="hyena",
    )(p, p, p, conv_w, conv_w, conv_w, cb, cb, cb, h_bias, ka, kb, kc, *dft)


def _gqa_kernel(L, window, has_ctx, has_rope, emit_kv, *refs):
    it = iter(refs)
    q_ref, k_ref, v_ref, sink_ref = next(it), next(it), next(it), next(it)
    if has_ctx:
        kc_ref, vc_ref = next(it), next(it)
    if has_rope:
        cos_ref, sa_ref, sb_ref = next(it), next(it), next(it)
    o_ref = next(it)
    if emit_kv:
        kn_ref, vn_ref = next(it), next(it)
        kn_ref[...] = k_ref[...]
        vn_ref[...] = v_ref[...]

    hk = pl.program_id(1)
    qb = HEAD_DIM
    nf = HEAD_DIM // 4
    k = k_ref[...]
    if has_rope:
        k = _rope(k, cos_ref[...], sa_ref[...], sb_ref[...], nf)
    k = k.astype(BF16)
    v = jnp.concatenate([v_ref[...].astype(BF16), jnp.ones((L, HEAD_DIM), BF16)], axis=1)
    if has_ctx:
        kc = kc_ref[...].astype(BF16)
        vc = jnp.concatenate([vc_ref[...].astype(BF16), jnp.ones((PAST_LEN, HEAD_DIM), BF16)], axis=1)
    band_bias = {}

    def bias_for(lo, hi, i):
        key = (lo - i * qb, hi - lo)
        if key not in band_bias:
            shape = (ATT_GROUP * qb, hi - lo)
            rel = key[0] + lax.broadcasted_iota(jnp.int32, shape, 1) - lax.broadcasted_iota(jnp.int32, shape, 0) % qb
            band_bias[key] = jnp.where(jnp.abs(rel) <= window, 0.0, -1e30)
        return band_bias[key]

    sink_row = sink_ref[...]
    lane = lax.broadcasted_iota(jnp.int32, sink_row.shape, 1)
    sinks = [jnp.sum(jnp.where(lane == hk * ATT_GROUP + g, sink_row, 0.0), axis=1, keepdims=True)
             for g in range(ATT_GROUP)]
    sink = jnp.concatenate([jnp.broadcast_to(s, (qb, 1)) for s in sinks], axis=0)

    for i in range(L // qb):
        rows = slice(i * qb, (i + 1) * qb)
        qs = []
        for g in range(ATT_GROUP):
            qg = q_ref[rows, g * HEAD_DIM:(g + 1) * HEAD_DIM].astype(F32)
            if has_rope:
                qg = _rope(qg, cos_ref[rows, :], sa_ref[rows, :], sb_ref[rows, :], nf)
            qs.append((qg * ATT_SCALE).astype(BF16))
        q = jnp.concatenate(qs, axis=0)
        if window is None:
            lo, hi = 0, L
        else:
            lo, hi = max(0, (i - 1) * qb), min(L, (i + 2) * qb)
        s = _dot_nt(q, k[lo:hi])
        if window is not None:
            s = s + bias_for(lo, hi, i)
        m = jnp.maximum(jnp.max(s, axis=-1, keepdims=True), sink)
        if has_ctx:
            sc = _dot_nt(q, kc)
            m = jnp.maximum(m, jnp.max(sc, axis=-1, keepdims=True))
        oa = _dot(jnp.exp(s - m).astype(BF16), v[lo:hi])
        if has_ctx:
            oa = oa + _dot(jnp.exp(sc - m).astype(BF16), vc)
        o = oa[:, :HEAD_DIM] / (oa[:, HEAD_DIM:] + jnp.exp(sink - m))
        for g in range(ATT_GROUP):
            o_ref[rows, g * HEAD_DIM:(g + 1) * HEAD_DIM] = o[g * qb:(g + 1) * qb].astype(o_ref.dtype)


def _gqa(grp, p, kv, sink, window=None, ctx=None, rope=None, emit_kv=False):
    L = grp.L
    gw = ATT_GROUP * HEAD_DIM
    q0 = 3 * HY_W // gw
    in_specs = [
        pl.BlockSpec((L, gw), lambda b, h: (b, q0 + h)),
        pl.BlockSpec((L, HEAD_DIM), lambda b, h: (b, h)),
        pl.BlockSpec((L, HEAD_DIM), lambda b, h: (b, ATT_KV_HEADS + h)),
        pl.BlockSpec((1, ATT_HEADS), lambda b, h: (0, 0)),
    ]
    args = [p, kv, kv, sink.reshape(1, ATT_HEADS)]
    if ctx is not None:
        kc, vc, e = ctx
        spec = pl.BlockSpec((None, None, None, PAST_LEN, HEAD_DIM), lambda b, h: (b, e, h, 0, 0))
        in_specs += [spec, spec]
        args += [kc, vc]
    if rope is not None:
        in_specs += [pl.BlockSpec((L, HEAD_DIM), lambda b, h: (0, 0))] * 3
        args += list(rope)
    out_shape = [jax.ShapeDtypeStruct((grp.rows, ATT_HEADS * HEAD_DIM), BF16)]
    out_specs = [pl.BlockSpec((L, gw), lambda b, h: (b, h))]
    if emit_kv:
        kv_shape = jax.ShapeDtypeStruct((grp.nb, 1, ATT_KV_HEADS, L, HEAD_DIM), F32)
        kv_spec = pl.BlockSpec((None, None, None, L, HEAD_DIM), lambda b, h: (b, 0, h, 0, 0))
        out_shape += [kv_shape, kv_shape]
        out_specs += [kv_spec, kv_spec]
    return pl.pallas_call(
        functools.partial(_gqa_kernel, L, window, ctx is not None, rope is not None, emit_kv),
        out_shape=tuple(out_shape),
        grid=(grp.nb, ATT_KV_HEADS),
        in_specs=in_specs,
        out_specs=tuple(out_specs),
        compiler_params=_cparams("parallel", "parallel"),
        name="gqa",
    )(*args)


def _chunk_cumsum(x, reverse):
    n = x.shape[0]
    pos = lax.broadcasted_iota(jnp.int32, x.shape, 0) % CHUNK
    s = 1
    while s < CHUNK:
        if reverse:
            x = x + jnp.where(pos < CHUNK - s, pltpu.roll(x, n - s, 0), 0.0)
        else:
            x = x + jnp.where(pos >= s, pltpu.roll(x, s, 0), 0.0)
        s *= 2
    return x


def _hgrn_kernel(L, has_state, emit_state, *refs):
    it = iter(refs)
    q_ref, ff_ref, fb_ref, i_ref, g_ref, lb_ref, norm_ref = (next(it) for _ in range(7))
    if has_state:
        s0_ref = next(it)
    o_ref = next(it)
    if emit_state:
        s_ref = next(it)
    acc_scr = next(it)

    a = lb_ref[...]
    e = jnp.exp(a - jnp.max(a, axis=0, keepdims=True))
    lb_all = e[1] / (e[0] + e[1])
    ci = lax.broadcasted_iota(jnp.int32, (CHUNK, CHUNK), 0)
    cj = lax.broadcasted_iota(jnp.int32, (CHUNK, CHUNK), 1)
    n_chunks = L // CHUNK
    chunks = [slice(n * CHUNK, (n + 1) * CHUNK) for n in range(n_chunks)]

    for hh in range(q_ref.shape[1] // HEAD_DIM):
        hc = slice(hh * HEAD_DIM, (hh + 1) * HEAD_DIM)
        q = _silu(q_ref[:, hc].astype(F32))
        v = i_ref[:, hc]
        for d, fz_ref in enumerate((ff_ref, fb_ref)):
            lbd = lb_all[d:d + 1, hc]
            f = lbd + (1.0 - lbd) * jax.nn.sigmoid(fz_ref[:, hc].astype(F32))
            b = _chunk_cumsum(jnp.log(f), reverse=(d == 1))
            qd = (q * jnp.exp(b)).astype(BF16)
            kd32 = (1.0 - f) * jnp.exp(-b)
            kd = kd32.astype(BF16)
            keep = (cj <= ci) if d == 0 else (cj >= ci)
            decay, own = [], []
            for rows in chunks:
                end = rows.stop - 1 if d == 0 else rows.start
                dc = jnp.exp(b[end:end + 1])
                decay.append(dc)
                own.append(_dot_tn(v[rows], (kd32[rows] * dc).astype(BF16)))
            st = s0_ref[d, hh].T if has_state else jnp.zeros((HEAD_DIM, HEAD_DIM), F32)
            entering = [None] * n_chunks
            for n in (range(n_chunks) if d == 0 else range(n_chunks - 1, -1, -1)):
                entering[n] = st.astype(BF16)
                st = st * decay[n] + own[n]
            if emit_state:
                s_ref[d, hh] = st.T
            for n, rows in enumerate(chunks):
                att = jnp.where(keep, _dot_nt(qd[rows], kd[rows]), 0.0).astype(BF16)
                o = _dot(att, v[rows]) + _dot_nt(qd[rows], entering[n])
                if d == 0:
                    acc_scr[rows, :] = o
                else:
                    acc_scr[rows, :] += o
        o = _rms(acc_scr[...]) * norm_ref[...] * _silu(g_ref[:, hc].astype(F32))
        o_ref[:, hc] = o.astype(o_ref.dtype)


def _hgrn(grp, p, hg_lb, norm_g, hps, state=None, emit_state=False):
    L = grp.L
    hw = hps * HEAD_DIM
    nh = HG_HEADS // hps
    col = lambda j: pl.BlockSpec((L, hw), lambda b, h: (b, j * nh + h))
    in_specs = [col(0), col(1), col(2), col(3), col(4),
                pl.BlockSpec((hg_lb.shape[0], 2, hw), lambda b, h: (0, 0, h)),
                pl.BlockSpec((1, HEAD_DIM), lambda b, h: (0, 0))]
    args = [p, p, p, p, p, hg_lb, norm_g.reshape(1, HEAD_DIM)]
    st_spec = lambda o: pl.BlockSpec((None, None, 2, hps, HEAD_DIM, HEAD_DIM), lambda b, h: (b, o, 0, h, 0, 0))
    if state is not None:
        s0, o = state
        in_specs.append(st_spec(o))
        args.append(s0)
    out_shape = [jax.ShapeDtypeStruct((grp.rows, HG_W), BF16)]
    out_specs = [pl.BlockSpec((L, hw), lambda b, h: (b, h))]
    if emit_state:
        out_shape.append(jax.ShapeDtypeStruct((grp.nb, 1, 2, HG_HEADS, HEAD_DIM, HEAD_DIM), F32))
        out_specs.append(st_spec(0))
    return pl.pallas_call(
        functools.partial(_hgrn_kernel, L, state is not None, emit_state),
        out_shape=tuple(out_shape),
        grid=(grp.nb, nh),
        in_specs=in_specs,
        out_specs=tuple(out_specs),
        scratch_shapes=[pltpu.VMEM((L, HEAD_DIM), F32)],
        compiler_params=_cparams("parallel", "parallel"),
        name="hgrn",
    )(*args)


MLA_QW = 256
KR_W = 128
OD_TAIL = 1024


def _mla_prep_kernel(has_rope, emit_kr, *refs):
    it = iter(refs)
    ql_ref, kvl_ref, kr_ref, qn_ref, kvn_ref, wq_ref = (next(it) for _ in range(6))
    if has_rope:
        qc_ref, qsa_ref, qsb_ref, kc_ref, ksa_ref, ksb_ref = (next(it) for _ in range(6))
    q_ref, ckv_ref, kro_ref = next(it), next(it), next(it)
    if emit_kr:
        krn_ref = next(it)
        krn_ref[...] = kr_ref[:, :ROPE]

    nf = ROPE // 4
    qn = (_rms(ql_ref[...]) * qn_ref[...]).astype(BF16)
    q = _dot(qn, wq_ref[...])
    for h in range(MLA_HEADS):
        cols = slice(h * MLA_QW, (h + 1) * MLA_QW)
        qh = q[:, cols]
        if has_rope:
            qh = _rope(qh, qc_ref[...], qsa_ref[...], qsb_ref[...], nf)
        q_ref[:, cols] = qh.astype(q_ref.dtype)
    ckv_ref[...] = _rms(kvl_ref[...]) * kvn_ref[...]
    kr = kr_ref[...]
    if has_rope:
        kr = _rope(kr, kc_ref[...], ksa_ref[...], ksb_ref[...], nf)
    kro_ref[...] = kr.astype(kro_ref.dtype)


def _mla_prep(grp, tail, q_norm, kv_norm, wq, rope=None, emit_kr=False):
    tm = min(512, grp.L)
    per = grp.L // tm
    n_rows = grp.rows
    in_specs = [
        pl.BlockSpec((tm, Q_LORA), lambda i: (i, 0)),
        pl.BlockSpec((tm, KV_LORA), lambda i: (i, Q_LORA // KV_LORA)),
        pl.BlockSpec((tm, KR_W), lambda i: (i, (Q_LORA + KV_LORA) // KR_W)),
        pl.BlockSpec((1, Q_LORA), lambda i: (0, 0)),
        pl.BlockSpec((1, KV_LORA), lambda i: (0, 0)),
        pl.BlockSpec((Q_LORA, MLA_HEADS * MLA_QW), lambda i: (0, 0)),
    ]
    args = [tail, tail, tail, q_norm.reshape(1, Q_LORA), kv_norm.reshape(1, KV_LORA), wq]
    if rope is not None:
        in_specs += [pl.BlockSpec((tm, MLA_QW), lambda i: (i % per, 0))] * 3
        in_specs += [pl.BlockSpec((tm, KR_W), lambda i: (i % per, 0))] * 3
        args += list(rope)
    out_shape = [jax.ShapeDtypeStruct((n_rows, MLA_HEADS * MLA_QW), BF16),
                 jax.ShapeDtypeStruct((n_rows, KV_LORA), F32),
                 jax.ShapeDtypeStruct((n_rows, KR_W), BF16)]
    out_specs = [pl.BlockSpec((tm, MLA_HEADS * MLA_QW), lambda i: (i, 0)),
                 pl.BlockSpec((tm, KV_LORA), lambda i: (i, 0)),
                 pl.BlockSpec((tm, KR_W), lambda i: (i, 0))]
    if emit_kr:
        out_shape.append(jax.ShapeDtypeStruct((n_rows, ROPE), F32))
        out_specs.append(pl.BlockSpec((tm, ROPE), lambda i: (i, 0)))
    return pl.pallas_call(
        functools.partial(_mla_prep_kernel, rope is not None, emit_kr),
        out_shape=tuple(out_shape),
        grid=(n_rows // tm,),
        in_specs=in_specs,
        out_specs=tuple(out_specs),
        compiler_params=_cparams("parallel"),
        name="mla_prep",
    )(*args)


def _mla_attn_kernel(L, has_ctx, *refs):
    it = iter(refs)
    q_ref, ckv_ref, kr_ref, wkv_ref = (next(it) for _ in range(4))
    if has_ctx:
        cckv_ref, ckr_ref = next(it), next(it)
    o_ref = next(it)

    ckv = ckv_ref[...].astype(BF16)
    kr = kr_ref[...]
    if has_ctx:
        ckv = jnp.concatenate([ckv, cckv_ref[...].astype(BF16)], axis=0)
        kr = jnp.concatenate([kr, ckr_ref[...].astype(BF16)], axis=0)
    ones = jnp.ones((ckv.shape[0], V_DIM), BF16)
    qb = min(L, 256)
    for hh in range(q_ref.shape[1] // MLA_QW):
        qc = slice(hh * MLA_QW, (hh + 1) * MLA_QW)
        oc = slice(hh * V_DIM, (hh + 1) * V_DIM)
        kv = _dot(ckv, wkv_ref[:, qc])
        kh = jnp.concatenate([kv[:, :NOPE].astype(BF16), kr], axis=1)
        vh = jnp.concatenate([kv[:, NOPE:].astype(BF16), ones], axis=1)
        for i in range(L // qb):
            rows = slice(i * qb, (i + 1) * qb)
            s = _dot_nt(q_ref[rows, qc], kh) * MLA_SCALE
            m = jnp.max(s, axis=-1, keepdims=True)
            oa = _dot(jnp.exp(s - m).astype(BF16), vh)
            o_ref[rows, oc] = (oa[:, :V_DIM] / oa[:, V_DIM:]).astype(o_ref.dtype)


def _mla_attn(grp, q, ckv, kr, wkv, hps, ctx=None):
    L = grp.L
    in_specs = [
        pl.BlockSpec((L, hps * MLA_QW), lambda b, h: (b, h)),
        pl.BlockSpec((L, KV_LORA), lambda b, h: (b, 0)),
        pl.BlockSpec((L, KR_W), lambda b, h: (b, 0)),
        pl.BlockSpec((KV_LORA, hps * (NOPE + V_DIM)), lambda b, h: (0, h)),
    ]
    args = [q, ckv, kr, wkv]
    if ctx is not None:
        cckv, ckr, o = ctx
        in_specs += [pl.BlockSpec((None, None, PAST_LEN, KV_LORA), lambda b, h: (b, o, 0, 0)),
                     pl.BlockSpec((None, None, PAST_LEN, KR_W), lambda b, h: (b, o, 0, 0))]
        args += [cckv, ckr]
    return pl.pallas_call(
        functools.partial(_mla_attn_kernel, L, ctx is not None),
        out_shape=jax.ShapeDtypeStruct((grp.rows, MLA_HEADS * V_DIM), BF16),
        grid=(grp.nb, MLA_HEADS // hps),
        in_specs=in_specs,
        out_specs=pl.BlockSpec((L, hps * V_DIM), lambda b, h: (b, h)),
        compiler_params=_cparams("parallel", "parallel"),
        name="mla_attn",
    )(*args)


def kernel(x_prompt, x_sample, c, c_ctx, cache_attn_k, cache_attn_v, cache_mla_ckv, cache_mla_krope, state_hgrn, mod_w, mod_b, norm_g, ffn_wg, ffn_wu, ffn_wd, ev_w_in, ev_w_out, hy_conv_w, hy_conv_b, hy_f_w1, hy_f_b1, hy_f_w2, hy_f_b2, hy_f_w3, hy_f_freq, hy_bias, attn_sink, od_w_in, od_w_out, hg_lb, hg_norm, mla_q_norm, mla_w_qb, mla_kv_norm, mla_w_kvb):
    depth = mod_w.shape[0]
    groups = (PROMPT, LATENT)
    xs = [x_prompt.reshape(PROMPT.rows, D_MODEL), x_sample.reshape(LATENT.rows, D_MODEL)]
    cvec = jnp.concatenate([c_ctx[None, :], c, jnp.zeros((MOD_ROWS - 1 - DEC_BATCH, D_MODEL), F32)], axis=0)
    mods_all = _modulation(cvec, mod_w, mod_b)

    wg, wu, wd = ffn_wg.astype(BF16), ffn_wu.astype(BF16), ffn_wd.astype(BF16)
    hy_tc = {SEQ: 512, DEC_SEQ: 256}
    hy_nseq = {SEQ: 4, DEC_SEQ: 2}

    new_k = new_v = new_ckv = new_kr = new_s = None
    for l in range(depth):
        mods = mods_all[l]
        xs = [_ffn(grp, x, mods, 0, norm_g[l, 0], norm_g[l, 1], wg, wu, wd, l, 0) for grp, x in zip(groups, xs)]
        if l % 2 == 0:
            e = l // 2
            w_in = ev_w_in[e].astype(BF16)
            w_out = ev_w_out[e].astype(BF16)
            kv_cols = 2 * ATT_KV_HEADS * HEAD_DIM
            mix = []
            for grp, x in zip(groups, xs):
                p, kv = _inproj(grp, x, mods, norm_g[l, 2], w_in, kv_cols)
                dft = _dft_matrices(grp.L)
                tc = hy_tc[grp.L]
                spectra = _hyena_spectra(grp.L, tc, dft, hy_f_w1[e], hy_f_b1[e], hy_f_w2[e], hy_f_b2[e],
                                         hy_f_w3[e], hy_f_freq[e])
                hy = _hyena(grp, p, tc, hy_nseq[grp.L], spectra, dft, hy_conv_w[e], hy_conv_b[e], hy_bias[e])
                if grp.latent:
                    rope = _rope_tables(grp.L, HEAD_DIM, 0, HEAD_DIM)
                    (att,) = _gqa(grp, p, kv, attn_sink[e], window=WINDOW,
                                  ctx=(cache_attn_k, cache_attn_v, e), rope=rope)
                else:
                    att, new_k, new_v = _gqa(grp, p, kv, attn_sink[e], emit_kv=True)
                mix.append((hy, att))
        else:
            o = l // 2
            w_in = jnp.pad(od_w_in[o], ((0, 0), (0, OD_IN_PAD - OD_IN))).astype(BF16)
            w_out = od_w_out[o].astype(BF16)
            wq = mla_w_qb[o].reshape(Q_LORA, MLA_HEADS, NOPE + ROPE)
            wq = jnp.pad(wq, ((0, 0), (0, 0), (0, MLA_QW - NOPE - ROPE))).reshape(Q_LORA, -1).astype(BF16)
            wkv = mla_w_kvb[o].astype(BF16)
            mix = []
            for grp, x in zip(groups, xs):
                p, tail = _inproj(grp, x, mods, norm_g[l, 2], w_in, OD_TAIL)
                if grp.latent:
                    (hg,) = _hgrn(grp, p, hg_lb, hg_norm[o], 1, state=(state_hgrn, o))
                    rope = (_rope_tables(grp.L, ROPE, NOPE, MLA_QW) + _rope_tables(grp.L, ROPE, 0, KR_W))
                    q, ckv, kr = _mla_prep(grp, tail, mla_q_norm[o], mla_kv_norm[o], wq, rope=rope)
                    ckr = jnp.pad(cache_mla_krope, ((0, 0), (0, 0), (0, 0), (0, KR_W - ROPE)))
                    att = _mla_attn(grp, q, ckv, kr, wkv, 1, ctx=(cache_mla_ckv, ckr, o))
                else:
                    hg, new_s = _hgrn(grp, p, hg_lb, hg_norm[o], 2, emit_state=True)
                    q, ckv, kr, kr_raw = _mla_prep(grp, tail, mla_q_norm[o], mla_kv_norm[o], wq, emit_kr=True)
                    att = _mla_attn(grp, q, ckv, kr, wkv, 4)
                    new_ckv = ckv.reshape(BATCH, 1, SEQ, KV_LORA)
                    new_kr = kr_raw.reshape(BATCH, 1, SEQ, ROPE)
                mix.append((hg, att))
        half = w_out.shape[0] // 2
        xs = [_outproj(grp, a, b, w_out[:half], w_out[half:], x, mods, norm_g[l, 3])
              for grp, x, (a, b) in zip(groups, xs, mix)]
        xs = [_ffn(grp, x, mods, 2, norm_g[l, 4], norm_g[l, 5], wg, wu, wd, l, 1) for grp, x in zip(groups, xs)]

    y_prompt = xs[0].reshape(BATCH, SEQ, D_MODEL)
    y_sample = xs[1].reshape(DEC_BATCH, DEC_SEQ, D_MODEL)
    return (y_prompt, y_sample, new_k, new_v, new_ckv, new_kr, new_s)
```

```python
import functools
import math
from typing import NamedTuple

import jax
import jax.numpy as jnp
from jax import lax
from jax.experimental import pallas as pl
from jax.experimental.pallas import tpu as pltpu

D_MODEL = 2048
BATCH = 16
SEQ = 256
DEC_BATCH = 8
DEC_SEQ = 1024
PAST_LEN = 512
GRID_W = 64
HEAD_DIM = 128
HY_W = 1024
HY_ORDER = 2
SHORT_CONV = 3
POS_EMB = 33
FILTER_ORDER = 64
HY_FAST_DECAY = 0.3
HY_SLOW_DECAY = 1.5
HY_TARGET = 1e-2
ATT_HEADS = 8
ATT_KV_HEADS = 2
ATT_GROUP = 4
WINDOW = 128
EV_IN = 3 * HY_W + (ATT_HEADS + 2 * ATT_KV_HEADS) * HEAD_DIM
HG_W = 1024
HG_HEADS = 8
CHUNK = 64
Q_LORA = 512
KV_LORA = 256
NOPE = 128
ROPE = 64
V_DIM = 128
MLA_HEADS = 8
OD_IN = 5 * HG_W + Q_LORA + KV_LORA + ROPE
OD_IN_PAD = 6144
D_FF = 5632
MACARON_W = 0.5
N_MOD = 9
ROPE_BASE = 10000.0
EPS = 1e-6
ATT_SCALE = HEAD_DIM ** -0.5
MLA_SCALE = (NOPE + ROPE) ** -0.5

MOD_ROWS = 16

V7X_VMEM_LIMIT = 56 * 1024 * 1024
TM = 512
RC = 512
TF = 512
TN = 1536
MOD_TN = 1024

BF16 = jnp.bfloat16
F32 = jnp.float32
HIGHEST = lax.Precision.HIGHEST


class Group(NamedTuple):
    nb: int
    L: int
    latent: bool

    @property
    def rows(self):
        return self.nb * self.L

    def mod_row(self, i, tm):
        return 1 + i // (self.L // tm) if self.latent else 0


PROMPT = Group(BATCH, SEQ, False)
LATENT = Group(DEC_BATCH, DEC_SEQ, True)


def _cparams(*sem):
    return pltpu.CompilerParams(dimension_semantics=sem, vmem_limit_bytes=V7X_VMEM_LIMIT)


def _rms(x):
    return x * lax.rsqrt(jnp.mean(x * x, axis=-1, keepdims=True) + EPS)


def _silu(x):
    return x * jax.nn.sigmoid(x)


def _dot(a, b):
    return jnp.dot(a, b, preferred_element_type=F32)


def _dot_nt(a, b):
    return lax.dot_general(a, b, (((1,), (1,)), ((), ())), preferred_element_type=F32)


def _dot_tn(a, b):
    return lax.dot_general(a, b, (((0,), (0,)), ((), ())), preferred_element_type=F32)


def _row_chunks(n_rows, body, rc=RC):
    def step(c, carry):
        body(pl.ds(pl.multiple_of(c * rc, rc), rc))
        return carry
    lax.fori_loop(0, n_rows // rc, step, 0)


def _modulated_norm(x_ref, h_scr, gain, shift):
    def body(rows):
        h_scr[rows, :] = (_rms(x_ref[rows, :]) * gain + shift).astype(BF16)
    _row_chunks(x_ref.shape[0], body)


def _mod_kernel(c_ref, w_ref, b_ref, o_ref):
    s = _silu(c_ref[...]).astype(BF16)
    o_ref[...] = _dot(s, w_ref[...].astype(BF16)) + b_ref[...]


def _modulation(cvec, mod_w, mod_b):
    depth = mod_w.shape[0]
    n = N_MOD * D_MODEL
    out = pl.pallas_call(
        _mod_kernel,
        out_shape=jax.ShapeDtypeStruct((depth, MOD_ROWS, n), F32),
        grid=(depth, n // MOD_TN),
        in_specs=[
            pl.BlockSpec((MOD_ROWS, D_MODEL), lambda l, j: (0, 0)),
            pl.BlockSpec((None, D_MODEL, MOD_TN), lambda l, j: (l, 0, j)),
            pl.BlockSpec((None, 1, MOD_TN), lambda l, j: (l, 0, j)),
        ],
        out_specs=pl.BlockSpec((None, MOD_ROWS, MOD_TN), lambda l, j: (l, 0, j)),
        compiler_params=_cparams("parallel", "parallel"),
        name="modulation",
    )(cvec, mod_w, mod_b.reshape(depth, 1, n))
    return out.reshape(depth, MOD_ROWS, N_MOD, D_MODEL)


def _ffn_kernel(j, x_ref, mod_ref, gpre_ref, gpost_ref, wg_ref, wu_ref, wd_ref, o_ref, h_scr):
    f = pl.program_id(1)

    @pl.when(f == 0)
    def _():
        gain = gpre_ref[...] * (1.0 + mod_ref[3 * j + 1:3 * j + 2, :])
        _modulated_norm(x_ref, h_scr, gain, mod_ref[3 * j:3 * j + 1, :])

    def partial_down(rows):
        h = h_scr[rows, :]
        a = (_silu(_dot(h, wg_ref[...])) * _dot(h, wu_ref[...])).astype(BF16)
        return _dot(a, wd_ref[...])

    @pl.when(f == 0)
    def _():
        def first(rows):
            o_ref[rows, :] = partial_down(rows)
        _row_chunks(o_ref.shape[0], first)

    @pl.when(f > 0)
    def _():
        def accumulate(rows):
            o_ref[rows, :] += partial_down(rows)
        _row_chunks(o_ref.shape[0], accumulate)

    @pl.when(f == pl.num_programs(1) - 1)
    def _():
        gain = (MACARON_W * mod_ref[3 * j + 2:3 * j + 3, :]) * gpost_ref[...]

        def residual(rows):
            o_ref[rows, :] = x_ref[rows, :] + _rms(o_ref[rows, :]) * gain
        _row_chunks(o_ref.shape[0], residual)


def _ffn(grp, x, mods, j, g_pre, g_post, wg, wu, wd, l, s):
    return pl.pallas_call(
        functools.partial(_ffn_kernel, j),
        out_shape=jax.ShapeDtypeStruct((grp.rows, D_MODEL), F32),
        grid=(grp.rows // TM, D_FF // TF),
        in_specs=[
            pl.BlockSpec((TM, D_MODEL), lambda i, f: (i, 0)),
            pl.BlockSpec((None, N_MOD, D_MODEL), lambda i, f: (grp.mod_row(i, TM), 0, 0)),
            pl.BlockSpec((1, D_MODEL), lambda i, f: (0, 0)),
            pl.BlockSpec((1, D_MODEL), lambda i, f: (0, 0)),
            pl.BlockSpec((None, None, None, D_MODEL, TF), lambda i, f: (l, s, f, 0, 0)),
            pl.BlockSpec((None, None, None, D_MODEL, TF), lambda i, f: (l, s, f, 0, 0)),
            pl.BlockSpec((None, None, TF, D_MODEL), lambda i, f: (l, s, f, 0)),
        ],
        out_specs=pl.BlockSpec((TM, D_MODEL), lambda i, f: (i, 0)),
        scratch_shapes=[pltpu.VMEM((TM, D_MODEL), BF16)],
        compiler_params=_cparams("parallel", "arbitrary"),
        name="ffn",
    )(x, mods, g_pre.reshape(1, D_MODEL), g_post.reshape(1, D_MODEL), wg, wu, wd)


def _inproj_kernel(x_ref, mod_ref, g_ref, w_ref, o_ref, t_ref, h_scr):
    k = pl.program_id(1)

    @pl.when(k == 0)
    def _():
        _modulated_norm(x_ref, h_scr, g_ref[...] * (1.0 + mod_ref[4:5, :]), mod_ref[3:4, :])

    y = _dot(h_scr[...], w_ref[...])
    o_ref[...] = y.astype(o_ref.dtype)

    @pl.when(k == pl.num_programs(1) - 1)
    def _():
        t_ref[...] = y[:, y.shape[1] - t_ref.shape[1]:]


def _inproj(grp, x, mods, g, w, n_tail):
    n = w.shape[1]
    return pl.pallas_call(
        _inproj_kernel,
        out_shape=(jax.ShapeDtypeStruct((grp.rows, n), BF16), jax.ShapeDtypeStruct((grp.rows, n_tail), F32)),
        grid=(grp.rows // TM, n // TN),
        in_specs=[
            pl.BlockSpec((TM, D_MODEL), lambda i, k: (i, 0)),
            pl.BlockSpec((None, N_MOD, D_MODEL), lambda i, k: (grp.mod_row(i, TM), 0, 0)),
            pl.BlockSpec((1, D_MODEL), lambda i, k: (0, 0)),
            pl.BlockSpec((D_MODEL, TN), lambda i, k: (0, k)),
        ],
        out_specs=(pl.BlockSpec((TM, TN), lambda i, k: (i, k)),
                   pl.BlockSpec((TM, n_tail), lambda i, k: (i, 0))),
        scratch_shapes=[pltpu.VMEM((TM, D_MODEL), BF16)],
        compiler_params=_cparams("parallel", "arbitrary"),
        name="inproj",
    )(x, mods, g.reshape(1, D_MODEL), w)


def _outproj_kernel(a_ref, b_ref, wa_ref, wb_ref, x_ref, mod_ref, g_ref, o_ref):
    y = _dot(a_ref[...], wa_ref[...]) + _dot(b_ref[...], wb_ref[...])
    o_ref[...] = x_ref[...] + _rms(y) * (mod_ref[5:6, :] * g_ref[...])


def _outproj(grp, a, b, wa, wb, x, mods, g):
    half = a.shape[1]
    tm = TM
    return pl.pallas_call(
        _outproj_kernel,
        out_shape=jax.ShapeDtypeStruct((grp.rows, D_MODEL), F32),
        grid=(grp.rows // tm,),
        in_specs=[
            pl.BlockSpec((tm, half), lambda i: (i, 0)),
            pl.BlockSpec((tm, half), lambda i: (i, 0)),
            pl.BlockSpec((half, D_MODEL), lambda i: (0, 0)),
            pl.BlockSpec((half, D_MODEL), lambda i: (0, 0)),
            pl.BlockSpec((tm, D_MODEL), lambda i: (i, 0)),
            pl.BlockSpec((None, N_MOD, D_MODEL), lambda i: (grp.mod_row(i, tm), 0, 0)),
            pl.BlockSpec((1, D_MODEL), lambda i: (0, 0)),
        ],
        out_specs=pl.BlockSpec((tm, D_MODEL), lambda i: (i, 0)),
        compiler_params=_cparams("parallel"),
        name="outproj",
    )(a, b, wa, wb, x, mods, g.reshape(1, D_MODEL))


def _rope_tables(L, rot_dim, lane0, width):
    half = rot_dim // 2
    inv = ROPE_BASE ** (-jnp.arange(0, half, 2, dtype=F32) / half)
    pos = jnp.arange(L)
    ang_r = (pos // GRID_W).astype(F32)[:, None] * inv
    ang_c = (pos % GRID_W).astype(F32)[:, None] * inv
    cr, sr, cc, sc = jnp.cos(ang_r), jnp.sin(ang_r), jnp.cos(ang_c), jnp.sin(ang_c)
    z = jnp.zeros_like(sr)
    cos = jnp.concatenate([cr, cr, cc, cc], axis=-1)
    sin_a = jnp.concatenate([-sr, z, -sc, z], axis=-1)
    sin_b = jnp.concatenate([z, sr, z, sc], axis=-1)
    pad = ((0, 0), (lane0, width - lane0 - rot_dim))
    return jnp.pad(cos, pad, constant_values=1.0), jnp.pad(sin_a, pad), jnp.pad(sin_b, pad)


def _rope(x, cos, sin_a, sin_b, nf):
    w = x.shape[-1]
    return x * cos + pltpu.roll(x, w - nf, 1) * sin_a + pltpu.roll(x, nf, 1) * sin_b


DFT_SPLIT = 32


def _dft_matrices(L):
    s = jnp.arange(L, dtype=jnp.int32)[None, :]

    def trig(k):
        ang = ((k[:, None] * s) % (2 * L)).astype(F32) * (math.pi / L)
        return jnp.cos(ang), jnp.sin(ang)

    c1, s1 = trig(jnp.arange(0, L, DFT_SPLIT, dtype=jnp.int32))
    c0, s0 = trig(jnp.arange(DFT_SPLIT, dtype=jnp.int32))
    cos = (c1[:, None, :] * c0[None] - s1[:, None, :] * s0[None]).reshape(L, L)
    sin = (s1[:, None, :] * c0[None] + c1[:, None, :] * s0[None]).reshape(L, L)
    nyq = (1 - 2 * (jnp.arange(L, dtype=jnp.int32) % 2)).astype(F32)
    k = jnp.arange(L, dtype=jnp.int32)[:, None]
    msin = jnp.where(k == 0, nyq[None, :], -sin)
    msin_t = jnp.where(s == 0, nyq[:, None], -sin)
    return cos.astype(BF16), msin.astype(BF16), msin_t.astype(BF16)


def _filter_kernel(L, z_ref, w1_ref, b1_ref, w2_ref, b2_ref, fr_ref, w3f_ref, w3b_ref, dl_ref, cos_ref, msin_ref,
                   ka_ref, kb_ref, kc_ref, h_scr):
    z = z_ref[...]

    @pl.when((pl.program_id(0) == 0) & (pl.program_id(1) == 0))
    def _():
        fr = fr_ref[...]
        h1 = jnp.sin(fr * (jnp.dot(z, w1_ref[...], precision=HIGHEST, preferred_element_type=F32) + b1_ref[...]))
        h_scr[...] = jnp.sin(
            fr * (jnp.dot(h1, w2_ref[...], precision=HIGHEST, preferred_element_type=F32) + b2_ref[...]))

    h = h_scr[...]
    decay = jnp.exp(-z[:, 0:1] * dl_ref[...])
    hf = jnp.dot(h, w3f_ref[...], precision=HIGHEST, preferred_element_type=F32) * decay
    hb = jnp.dot(h, w3b_ref[...], precision=HIGHEST, preferred_element_type=F32) * decay
    row = lax.broadcasted_iota(jnp.int32, hf.shape, 0)
    row0 = row == 0
    hb = jnp.where(row0, 0.0, hb)
    even = hf + hb
    re = _dot(cos_ref[...], even.astype(BF16))
    im = _dot(msin_ref[...], (hf - hb).astype(BF16))
    nyq = jnp.sum(jnp.where(row % 2 == 0, even, -even), axis=0, keepdims=True)
    sc = jnp.where(row0, 0.5 / L, 1.0 / L)
    ka_ref[...] = re * sc
    kb_ref[...] = jnp.where(row0, 0.0, im) * sc
    kc_ref[...] = jnp.where(row0, nyq, re) * sc


def _hyena_spectra(L, tc, dft, f_w1, f_b1, f_w2, f_b2, f_w3, f_freq):
    t = jnp.linspace(0.0, 1.0, L, dtype=F32)[:, None]
    bands = (POS_EMB - 1) // 2
    w = 2.0 * math.pi * jnp.arange(L, dtype=F32)[:, None] / L
    fb = jnp.linspace(1e-4, bands - 1, bands, dtype=F32)[None, :]
    z = jnp.concatenate([t, jnp.cos(fb * w), -jnp.sin(fb * w)], axis=-1)
    z = jnp.pad(z, ((0, 0), (0, 128 - POS_EMB)))
    w1 = jnp.pad(f_w1, ((0, 128 - POS_EMB), (0, 0)))
    deltas = jnp.abs(jnp.linspace(math.log(HY_TARGET) / HY_SLOW_DECAY,
                                  math.log(HY_TARGET) / HY_FAST_DECAY, HY_W, dtype=F32))[None, :]
    nct = HY_W // tc
    fo = FILTER_ORDER
    row = lambda a: a.reshape(1, fo)
    kshape = jax.ShapeDtypeStruct((HY_ORDER, L, HY_W), F32)
    kspec = pl.BlockSpec((None, L, tc), lambda n, c: (n, 0, c))
    const = lambda shape: pl.BlockSpec(shape, lambda n, c: (0, 0))
    return pl.pallas_call(
        functools.partial(_filter_kernel, L),
        out_shape=(kshape, kshape, kshape),
        grid=(HY_ORDER, nct),
        in_specs=[
            const((L, 128)), const((128, fo)), const((1, fo)), const((fo, fo)), const((1, fo)), const((1, fo)),
            pl.BlockSpec((fo, tc), lambda n, c: (0, 2 * n * nct + c)),
            pl.BlockSpec((fo, tc), lambda n, c: (0, (2 * n + 1) * nct + c)),
            pl.BlockSpec((1, tc), lambda n, c: (0, c)),
            const((L, L)), const((L, L)),
        ],
        out_specs=(kspec, kspec, kspec),
        scratch_shapes=[pltpu.VMEM((L, fo), F32)],
        compiler_params=_cparams("arbitrary", "arbitrary"),
        name="hyena_filter",
    )(z, w1, row(f_b1), f_w2, row(f_b2), row(f_freq), f_w3, f_w3, deltas, dft[0], dft[1])


def _hyena_kernel(L, nseq, uv_ref, ug0_ref, ug1_ref, cwv_ref, cwg0_ref, cwg1_ref, cbv_ref, cbg0_ref, cbg1_ref,
                  hb_ref, ka_ref, kb_ref, kc_ref, cos_ref, msin_ref, msint_ref, o_ref):
    row = lax.broadcasted_iota(jnp.int32, (L, uv_ref.shape[1]), 0)
    first, last = row == 0, row == L - 1
    seqs = [slice(i * L, (i + 1) * L) for i in range(nseq)]

    def short_conv(u_ref, rows, w_ref, b_ref):
        u = u_ref[rows, :].astype(F32)
        prev = jnp.where(first, 0.0, pltpu.roll(u, 1, 0))
        nxt = jnp.where(last, 0.0, pltpu.roll(u, L - 1, 0))
        return b_ref[...] + prev * w_ref[0:1, :] + u * w_ref[1:2, :] + nxt * w_ref[2:3, :]

    z = [short_conv(uv_ref, r, cwv_ref, cbv_ref) for r in seqs]
    gates = [(short_conv(ug0_ref, r, cwg0_ref, cbg0_ref), short_conv(ug1_ref, r, cwg1_ref, cbg1_ref))
             for r in seqs]
    for n in range(HY_ORDER):
        zb = [zi.astype(BF16) for zi in z]
        zre = [_dot(cos_ref[...], b) for b in zb]
        zim = [_dot(msin_ref[...], b) for b in zb]
        ka, kb, kc = ka_ref[n], kb_ref[n], kc_ref[n]
        yre = [(re * ka - im * kb).astype(BF16) for re, im in zip(zre, zim)]
        yim = [(re * kb + im * kc).astype(BF16) for re, im in zip(zre, zim)]
        conv = [_dot(cos_ref[...], a) + _dot(msint_ref[...], b) for a, b in zip(yre, yim)]
        z = [g[n] * (cv + hb_ref[n:n + 1, :] * zi) for g, cv, zi in zip(gates, conv, z)]
    for r, zi in zip(seqs, z):
        o_ref[r, :] = zi.astype(o_ref.dtype)


def _hyena(grp, p, tc, nseq, spectra, dft, conv_w, conv_b, h_bias):
    L = grp.L
    nct = HY_W // tc
    ka, kb, kc = spectra
    u_spec = lambda j: pl.BlockSpec((nseq * L, tc), lambda c, b: (b, j * nct + c))
    cw_spec = lambda j: pl.BlockSpec((SHORT_CONV, tc), lambda c, b: (0, j * nct + c))
    cb_spec = lambda j: pl.BlockSpec((1, tc), lambda c, b: (0, j * nct + c))
    k_spec = pl.BlockSpec((HY_ORDER, L, tc), lambda c, b: (0, 0, c))
    m_spec = pl.BlockSpec((L, L), lambda c, b: (0, 0))
    cb = conv_b.reshape(1, -1)
    return pl.pallas_call(
        functools.partial(_hyena_kernel, L, nseq),
        out_shape=jax.ShapeDtypeStruct((grp.rows, HY_W), BF16),
        grid=(nct, grp.nb // nseq),
        in_specs=[
            u_spec(0), u_spec(1), u_spec(2), cw_spec(0), cw_spec(1), cw_spec(2),
            cb_spec(0), cb_spec(1), cb_spec(2),
            pl.BlockSpec((HY_ORDER, tc), lambda c, b: (0, c)),
            k_spec, k_spec, k_spec, m_spec, m_spec, m_spec,
        ],
        out_specs=pl.BlockSpec((nseq * L, tc), lambda c, b: (b, c)),
        compiler_params=_cparams("parallel", "arbitrary"),
        name="hyena",
    )(p, p, p, conv_w, conv_w, conv_w, cb, cb, cb, h_bias, ka, kb, kc, *dft)


def _gqa_kernel(L, window, has_ctx, has_rope, emit_kv, *refs):
    it = iter(refs)
    q_ref, k_ref, v_ref, sink_ref = next(it), next(it), next(it), next(it)
    if has_ctx:
        kc_ref, vc_ref = next(it), next(it)
    if has_rope:
        cos_ref, sa_ref, sb_ref = next(it), next(it), next(it)
    o_ref = next(it)
    if emit_kv:
        kn_ref, vn_ref = next(it), next(it)
        kn_ref[...] = k_ref[...]
        vn_ref[...] = v_ref[...]

    hk = pl.program_id(1)
    qb = HEAD_DIM
    nf = HEAD_DIM // 4
    k = k_ref[...]
    if has_rope:
        k = _rope(k, cos_ref[...], sa_ref[...], sb_ref[...], nf)
    k = k.astype(BF16)
    v = jnp.concatenate([v_ref[...].astype(BF16), jnp.ones((L, HEAD_DIM), BF16)], axis=1)
    if has_ctx:
        kc = kc_ref[...].astype(BF16)
        vc = jnp.concatenate([vc_ref[...].astype(BF16), jnp.ones((PAST_LEN, HEAD_DIM), BF16)], axis=1)
    band_bias = {}

    def bias_for(lo, hi, i):
        key = (lo - i * qb, hi - lo)
        if key not in band_bias:
            shape = (ATT_GROUP * qb, hi - lo)
            rel = key[0] + lax.broadcasted_iota(jnp.int32, shape, 1) - lax.broadcasted_iota(jnp.int32, shape, 0) % qb
            band_bias[key] = jnp.where(jnp.abs(rel) <= window, 0.0, -1e30)
        return band_bias[key]

    sink_row = sink_ref[...]
    lane = lax.broadcasted_iota(jnp.int32, sink_row.shape, 1)
    sinks = [jnp.sum(jnp.where(lane == hk * ATT_GROUP + g, sink_row, 0.0), axis=1, keepdims=True)
             for g in range(ATT_GROUP)]
    sink = jnp.concatenate([jnp.broadcast_to(s, (qb, 1)) for s in sinks], axis=0)

    for i in range(L // qb):
        rows = slice(i * qb, (i + 1) * qb)
        qs = []
        for g in range(ATT_GROUP):
            qg = q_ref[rows, g * HEAD_DIM:(g + 1) * HEAD_DIM].astype(F32)
            if has_rope:
                qg = _rope(qg, cos_ref[rows, :], sa_ref[rows, :], sb_ref[rows, :], nf)
            qs.append((qg * ATT_SCALE).astype(BF16))
        q = jnp.concatenate(qs, axis=0)
        if window is None:
            lo, hi = 0, L
        else:
            lo, hi = max(0, (i - 1) * qb), min(L, (i + 2) * qb)
        s = _dot_nt(q, k[lo:hi])
        if window is not None:
            s = s + bias_for(lo, hi, i)
        m = jnp.maximum(jnp.max(s, axis=-1, keepdims=True), sink)
        if has_ctx:
            sc = _dot_nt(q, kc)
            m = jnp.maximum(m, jnp.max(sc, axis=-1, keepdims=True))
        oa = _dot(jnp.exp(s - m).astype(BF16), v[lo:hi])
        if has_ctx:
            oa = oa + _dot(jnp.exp(sc - m).astype(BF16), vc)
        o = oa[:, :HEAD_DIM] / (oa[:, HEAD_DIM:] + jnp.exp(sink - m))
        for g in range(ATT_GROUP):
            o_ref[rows, g * HEAD_DIM:(g + 1) * HEAD_DIM] = o[g * qb:(g + 1) * qb].astype(o_ref.dtype)


def _gqa(grp, p, kv, sink, window=None, ctx=None, rope=None, emit_kv=False):
    L = grp.L
    gw = ATT_GROUP * HEAD_DIM
    q0 = 3 * HY_W // gw
    in_specs = [
        pl.BlockSpec((L, gw), lambda b, h: (b, q0 + h)),
        pl.BlockSpec((L, HEAD_DIM), lambda b, h: (b, h)),
        pl.BlockSpec((L, HEAD_DIM), lambda b, h: (b, ATT_KV_HEADS + h)),
        pl.BlockSpec((1, ATT_HEADS), lambda b, h: (0, 0)),
    ]
    args = [p, kv, kv, sink.reshape(1, ATT_HEADS)]
    if ctx is not None:
        kc, vc, e = ctx
        spec = pl.BlockSpec((None, None, None, PAST_LEN, HEAD_DIM), lambda b, h: (b, e, h, 0, 0))
        in_specs += [spec, spec]
        args += [kc, vc]
    if rope is not None:
        in_specs += [pl.BlockSpec((L, HEAD_DIM), lambda b, h: (0, 0))] * 3
        args += list(rope)
    out_shape = [jax.ShapeDtypeStruct((grp.rows, ATT_HEADS * HEAD_DIM), BF16)]
    out_specs = [pl.BlockSpec((L, gw), lambda b, h: (b, h))]
    if emit_kv:
        kv_shape = jax.ShapeDtypeStruct((grp.nb, 1, ATT_KV_HEADS, L, HEAD_DIM), F32)
        kv_spec = pl.BlockSpec((None, None, None, L, HEAD_DIM), lambda b, h: (b, 0, h, 0, 0))
        out_shape += [kv_shape, kv_shape]
        out_specs += [kv_spec, kv_spec]
    return pl.pallas_call(
        functools.partial(_gqa_kernel, L, window, ctx is not None, rope is not None, emit_kv),
        out_shape=tuple(out_shape),
        grid=(grp.nb, ATT_KV_HEADS),
        in_specs=in_specs,
        out_specs=tuple(out_specs),
        compiler_params=_cparams("parallel", "parallel"),
        name="gqa",
    )(*args)


def _chunk_cumsum(x, reverse):
    n = x.shape[0]
    pos = lax.broadcasted_iota(jnp.int32, x.shape, 0) % CHUNK
    s = 1
    while s < CHUNK:
        if reverse:
            x = x + jnp.where(pos < CHUNK - s, pltpu.roll(x, n - s, 0), 0.0)
        else:
            x = x + jnp.where(pos >= s, pltpu.roll(x, s, 0), 0.0)
        s *= 2
    return x


def _hgrn_kernel(L, has_state, emit_state, *refs):
    it = iter(refs)
    q_ref, ff_ref, fb_ref, i_ref, g_ref, lb_ref, norm_ref = (next(it) for _ in range(7))
    if has_state:
        s0_ref = next(it)
    o_ref = next(it)
    if emit_state:
        s_ref = next(it)
    acc_scr = next(it)

    a = lb_ref[...]
    e = jnp.exp(a - jnp.max(a, axis=0, keepdims=True))
    lb_all = e[1] / (e[0] + e[1])
    ci = lax.broadcasted_iota(jnp.int32, (CHUNK, CHUNK), 0)
    cj = lax.broadcasted_iota(jnp.int32, (CHUNK, CHUNK), 1)
    n_chunks = L // CHUNK
    chunks = [slice(n * CHUNK, (n + 1) * CHUNK) for n in range(n_chunks)]

    for hh in range(q_ref.shape[1] // HEAD_DIM):
        hc = slice(hh * HEAD_DIM, (hh + 1) * HEAD_DIM)
        q = _silu(q_ref[:, hc].astype(F32))
        v = i_ref[:, hc]
        for d, fz_ref in enumerate((ff_ref, fb_ref)):
            lbd = lb_all[d:d + 1, hc]
            f = lbd + (1.0 - lbd) * jax.nn.sigmoid(fz_ref[:, hc].astype(F32))
            b = _chunk_cumsum(jnp.log(f), reverse=(d == 1))
            qd = (q * jnp.exp(b)).astype(BF16)
            kd32 = (1.0 - f) * jnp.exp(-b)
            kd = kd32.astype(BF16)
            keep = (cj <= ci) if d == 0 else (cj >= ci)
            decay, own = [], []
            for rows in chunks:
                end = rows.stop - 1 if d == 0 else rows.start
                dc = jnp.exp(b[end:end + 1])
                decay.append(dc)
                own.append(_dot_tn(v[rows], (kd32[rows] * dc).astype(BF16)))
            st = s0_ref[d, hh].T if has_state else jnp.zeros((HEAD_DIM, HEAD_DIM), F32)
            entering = [None] * n_chunks
            for n in (range(n_chunks) if d == 0 else range(n_chunks - 1, -1, -1)):
                entering[n] = st.astype(BF16)
                st = st * decay[n] + own[n]
            if emit_state:
                s_ref[d, hh] = st.T
            for n, rows in enumerate(chunks):
                att = jnp.where(keep, _dot_nt(qd[rows], kd[rows]), 0.0).astype(BF16)
                o = _dot(att, v[rows]) + _dot_nt(qd[rows], entering[n])
                if d == 0:
                    acc_scr[rows, :] = o
                else:
                    acc_scr[rows, :] += o
        o = _rms(acc_scr[...]) * norm_ref[...] * _silu(g_ref[:, hc].astype(F32))
        o_ref[:, hc] = o.astype(o_ref.dtype)


def _hgrn(grp, p, hg_lb, norm_g, hps, state=None, emit_state=False):
    L = grp.L
    hw = hps * HEAD_DIM
    nh = HG_HEADS // hps
    col = lambda j: pl.BlockSpec((L, hw), lambda b, h: (b, j * nh + h))
    in_specs = [col(0), col(1), col(2), col(3), col(4),
                pl.BlockSpec((hg_lb.shape[0], 2, hw), lambda b, h: (0, 0, h)),
                pl.BlockSpec((1, HEAD_DIM), lambda b, h: (0, 0))]
    args = [p, p, p, p, p, hg_lb, norm_g.reshape(1, HEAD_DIM)]
    st_spec = lambda o: pl.BlockSpec((None, None, 2, hps, HEAD_DIM, HEAD_DIM), lambda b, h: (b, o, 0, h, 0, 0))
    if state is not None:
        s0, o = state
        in_specs.append(st_spec(o))
        args.append(s0)
    out_shape = [jax.ShapeDtypeStruct((grp.rows, HG_W), BF16)]
    out_specs = [pl.BlockSpec((L, hw), lambda b, h: (b, h))]
    if emit_state:
        out_shape.append(jax.ShapeDtypeStruct((grp.nb, 1, 2, HG_HEADS, HEAD_DIM, HEAD_DIM), F32))
        out_specs.append(st_spec(0))
    return pl.pallas_call(
        functools.partial(_hgrn_kernel, L, state is not None, emit_state),
        out_shape=tuple(out_shape),
        grid=(grp.nb, nh),
        in_specs=in_specs,
        out_specs=tuple(out_specs),
        scratch_shapes=[pltpu.VMEM((L, HEAD_DIM), F32)],
        compiler_params=_cparams("parallel", "parallel"),
        name="hgrn",
    )(*args)


MLA_QW = 256
KR_W = 128
OD_TAIL = 1024


def _mla_prep_kernel(has_rope, emit_kr, *refs):
    it = iter(refs)
    ql_ref, kvl_ref, kr_ref, qn_ref, kvn_ref, wq_ref = (next(it) for _ in range(6))
    if has_rope:
        kc_ref, ksa_ref, ksb_ref = (next(it) for _ in range(3))
    q_ref, ckv_ref, kro_ref = next(it), next(it), next(it)
    if emit_kr:
        krn_ref = next(it)
        krn_ref[...] = kr_ref[:, :ROPE]

    nf = ROPE // 4
    qn = (_rms(ql_ref[...]) * qn_ref[...]).astype(BF16)
    q = _dot(qn, wq_ref[...])
    for h in range(MLA_HEADS):
        nope = slice(h * MLA_QW, h * MLA_QW + NOPE)
        rot = slice(h * MLA_QW + NOPE, (h + 1) * MLA_QW)
        q_ref[:, nope] = q[:, nope].astype(q_ref.dtype)
        qr = q[:, rot]
        if has_rope:
            qr = _rope(qr, kc_ref[...], ksa_ref[...], ksb_ref[...], nf)
        q_ref[:, rot] = qr.astype(q_ref.dtype)
    ckv_ref[...] = _rms(kvl_ref[...]) * kvn_ref[...]
    kr = kr_ref[...]
    if has_rope:
        kr = _rope(kr, kc_ref[...], ksa_ref[...], ksb_ref[...], nf)
    kro_ref[...] = kr.astype(kro_ref.dtype)


def _mla_prep(grp, tail, q_norm, kv_norm, wq, rope=None, emit_kr=False):
    tm = min(512, grp.L)
    per = grp.L // tm
    n_rows = grp.rows
    in_specs = [
        pl.BlockSpec((tm, Q_LORA), lambda i: (i, 0)),
        pl.BlockSpec((tm, KV_LORA), lambda i: (i, Q_LORA // KV_LORA)),
        pl.BlockSpec((tm, KR_W), lambda i: (i, (Q_LORA + KV_LORA) // KR_W)),
        pl.BlockSpec((1, Q_LORA), lambda i: (0, 0)),
        pl.BlockSpec((1, KV_LORA), lambda i: (0, 0)),
        pl.BlockSpec((Q_LORA, MLA_HEADS * MLA_QW), lambda i: (0, 0)),
    ]
    args = [tail, tail, tail, q_norm.reshape(1, Q_LORA), kv_norm.reshape(1, KV_LORA), wq]
    if rope is not None:
        in_specs += [pl.BlockSpec((tm, KR_W), lambda i: (i % per, 0))] * 3
        args += list(rope)
    out_shape = [jax.ShapeDtypeStruct((n_rows, MLA_HEADS * MLA_QW), BF16),
                 jax.ShapeDtypeStruct((n_rows, KV_LORA), F32),
                 jax.ShapeDtypeStruct((n_rows, KR_W), BF16)]
    out_specs = [pl.BlockSpec((tm, MLA_HEADS * MLA_QW), lambda i: (i, 0)),
                 pl.BlockSpec((tm, KV_LORA), lambda i: (i, 0)),
                 pl.BlockSpec((tm, KR_W), lambda i: (i, 0))]
    if emit_kr:
        out_shape.append(jax.ShapeDtypeStruct((n_rows, ROPE), F32))
        out_specs.append(pl.BlockSpec((tm, ROPE), lambda i: (i, 0)))
    return pl.pallas_call(
        functools.partial(_mla_prep_kernel, rope is not None, emit_kr),
        out_shape=tuple(out_shape),
        grid=(n_rows // tm,),
        in_specs=in_specs,
        out_specs=tuple(out_specs),
        compiler_params=_cparams("parallel"),
        name="mla_prep",
    )(*args)


def _mla_attn_kernel(L, has_ctx, *refs):
    it = iter(refs)
    q_ref, ckv_ref, kr_ref, wkv_ref = (next(it) for _ in range(4))
    if has_ctx:
        cckv_ref, ckr_ref = next(it), next(it)
    o_ref = next(it)

    ckv = ckv_ref[...].astype(BF16)
    kr = kr_ref[...]
    if has_ctx:
        ckv = jnp.concatenate([ckv, cckv_ref[...].astype(BF16)], axis=0)
        kr = jnp.concatenate([kr, ckr_ref[...].astype(BF16)], axis=0)
    ones = jnp.ones((ckv.shape[0], V_DIM), BF16)
    qb = min(L, 256)
    for hh in range(q_ref.shape[1] // MLA_QW):
        qc = slice(hh * MLA_QW, (hh + 1) * MLA_QW)
        oc = slice(hh * V_DIM, (hh + 1) * V_DIM)
        kv = _dot(ckv, wkv_ref[:, qc])
        kh = jnp.concatenate([kv[:, :NOPE].astype(BF16), kr], axis=1)
        vh = jnp.concatenate([kv[:, NOPE:].astype(BF16), ones], axis=1)
        for i in range(L // qb):
            rows = slice(i * qb, (i + 1) * qb)
            s = _dot_nt(q_ref[rows, qc], kh) * MLA_SCALE
            m = jnp.max(s, axis=-1, keepdims=True)
            oa = _dot(jnp.exp(s - m).astype(BF16), vh)
            o_ref[rows, oc] = (oa[:, :V_DIM] / oa[:, V_DIM:]).astype(o_ref.dtype)


def _mla_attn(grp, q, ckv, kr, wkv, hps, ctx=None):
    L = grp.L
    in_specs = [
        pl.BlockSpec((L, hps * MLA_QW), lambda b, h: (b, h)),
        pl.BlockSpec((L, KV_LORA), lambda b, h: (b, 0)),
        pl.BlockSpec((L, KR_W), lambda b, h: (b, 0)),
        pl.BlockSpec((KV_LORA, hps * (NOPE + V_DIM)), lambda b, h: (0, h)),
    ]
    args = [q, ckv, kr, wkv]
    if ctx is not None:
        cckv, ckr, o = ctx
        in_specs += [pl.BlockSpec((None, None, PAST_LEN, KV_LORA), lambda b, h: (b, o, 0, 0)),
                     pl.BlockSpec((None, None, PAST_LEN, KR_W), lambda b, h: (b, o, 0, 0))]
        args += [cckv, ckr]
    return pl.pallas_call(
        functools.partial(_mla_attn_kernel, L, ctx is not None),
        out_shape=jax.ShapeDtypeStruct((grp.rows, MLA_HEADS * V_DIM), BF16),
        grid=(grp.nb, MLA_HEADS // hps),
        in_specs=in_specs,
        out_specs=pl.BlockSpec((L, hps * V_DIM), lambda b, h: (b, h)),
        compiler_params=_cparams("parallel", "parallel"),
        name="mla_attn",
    )(*args)


def kernel(x_prompt, x_sample, c, c_ctx, cache_attn_k, cache_attn_v, cache_mla_ckv, cache_mla_krope, state_hgrn, mod_w, mod_b, norm_g, ffn_wg, ffn_wu, ffn_wd, ev_w_in, ev_w_out, hy_conv_w, hy_conv_b, hy_f_w1, hy_f_b1, hy_f_w2, hy_f_b2, hy_f_w3, hy_f_freq, hy_bias, attn_sink, od_w_in, od_w_out, hg_lb, hg_norm, mla_q_norm, mla_w_qb, mla_kv_norm, mla_w_kvb):
    depth = mod_w.shape[0]
    groups = (PROMPT, LATENT)
    xs = [x_prompt.reshape(PROMPT.rows, D_MODEL), x_sample.reshape(LATENT.rows, D_MODEL)]
    cvec = jnp.concatenate([c_ctx[None, :], c, jnp.zeros((MOD_ROWS - 1 - DEC_BATCH, D_MODEL), F32)], axis=0)
    mods_all = _modulation(cvec, mod_w, mod_b)

    def column_tiles(w):
        w = w.astype(BF16).reshape(depth, 2, D_MODEL, D_FF // TF, TF)
        return w.transpose(0, 1, 3, 2, 4)

    wg, wu, wd = column_tiles(ffn_wg), column_tiles(ffn_wu), ffn_wd.astype(BF16)
    hy_tc = {SEQ: 512, DEC_SEQ: 256}
    hy_nseq = {SEQ: 4, DEC_SEQ: 2}

    new_k = new_v = new_ckv = new_kr = new_s = None
    for l in range(depth):
        mods = mods_all[l]
        xs = [_ffn(grp, x, mods, 0, norm_g[l, 0], norm_g[l, 1], wg, wu, wd, l, 0) for grp, x in zip(groups, xs)]
        if l % 2 == 0:
            e = l // 2
            w_in = ev_w_in[e].astype(BF16)
            w_out = ev_w_out[e].astype(BF16)
            kv_cols = 2 * ATT_KV_HEADS * HEAD_DIM
            mix = []
            for grp, x in zip(groups, xs):
                p, kv = _inproj(grp, x, mods, norm_g[l, 2], w_in, kv_cols)
                dft = _dft_matrices(grp.L)
                tc = hy_tc[grp.L]
                spectra = _hyena_spectra(grp.L, tc, dft, hy_f_w1[e], hy_f_b1[e], hy_f_w2[e], hy_f_b2[e],
                                         hy_f_w3[e], hy_f_freq[e])
                hy = _hyena(grp, p, tc, hy_nseq[grp.L], spectra, dft, hy_conv_w[e], hy_conv_b[e], hy_bias[e])
                if grp.latent:
                    rope = _rope_tables(grp.L, HEAD_DIM, 0, HEAD_DIM)
                    (att,) = _gqa(grp, p, kv, attn_sink[e], window=WINDOW,
                                  ctx=(cache_attn_k, cache_attn_v, e), rope=rope)
                else:
                    att, new_k, new_v = _gqa(grp, p, kv, attn_sink[e], emit_kv=True)
                mix.append((hy, att))
        else:
            o = l // 2
            w_in = jnp.pad(od_w_in[o], ((0, 0), (0, OD_IN_PAD - OD_IN))).astype(BF16)
            w_out = od_w_out[o].astype(BF16)
            wq = mla_w_qb[o].reshape(Q_LORA, MLA_HEADS, NOPE + ROPE)
            wq = jnp.pad(wq, ((0, 0), (0, 0), (0, MLA_QW - NOPE - ROPE))).reshape(Q_LORA, -1).astype(BF16)
            wkv = mla_w_kvb[o].astype(BF16)
            mix = []
            for grp, x in zip(groups, xs):
                p, tail = _inproj(grp, x, mods, norm_g[l, 2], w_in, OD_TAIL)
                if grp.latent:
                    (hg,) = _hgrn(grp, p, hg_lb, hg_norm[o], 2, state=(state_hgrn, o))
                    rope = _rope_tables(grp.L, ROPE, 0, KR_W)
                    q, ckv, kr = _mla_prep(grp, tail, mla_q_norm[o], mla_kv_norm[o], wq, rope=rope)
                    ckr = jnp.pad(cache_mla_krope, ((0, 0), (0, 0), (0, 0), (0, KR_W - ROPE)))
                    att = _mla_attn(grp, q, ckv, kr, wkv, 1, ctx=(cache_mla_ckv, ckr, o))
                else:
                    hg, new_s = _hgrn(grp, p, hg_lb, hg_norm[o], 2, emit_state=True)
                    q, ckv, kr, kr_raw = _mla_prep(grp, tail, mla_q_norm[o], mla_kv_norm[o], wq, emit_kr=True)
                    att = _mla_attn(grp, q, ckv, kr, wkv, 4)
                    new_ckv = ckv.reshape(BATCH, 1, SEQ, KV_LORA)
                    new_kr = kr_raw.reshape(BATCH, 1, SEQ, ROPE)
                mix.append((hg, att))
        half = w_out.shape[0] // 2
        xs = [_outproj(grp, a, b, w_out[:half], w_out[half:], x, mods, norm_g[l, 3])
              for grp, x, (a, b) in zip(groups, xs, mix)]
        xs = [_ffn(grp, x, mods, 2, norm_g[l, 4], norm_g[l, 5], wg, wu, wd, l, 1) for grp, x in zip(groups, xs)]

    y_prompt = xs[0].reshape(BATCH, SEQ, D_MODEL)
    y_sample = xs[1].reshape(DEC_BATCH, DEC_SEQ, D_MODEL)
    return (y_prompt, y_sample, new_k, new_v, new_ckv, new_kr, new_s)
```

```python
import functools
import math
from typing import NamedTuple

import jax
import jax.numpy as jnp
from jax import lax
from jax.experimental import pallas as pl
from jax.experimental.pallas import tpu as pltpu

D_MODEL = 2048
BATCH = 16
SEQ = 256
DEC_BATCH = 8
DEC_SEQ = 1024
PAST_LEN = 512
GRID_W = 64
HEAD_DIM = 128
HY_W = 1024
HY_ORDER = 2
SHORT_CONV = 3
POS_EMB = 33
FILTER_ORDER = 64
HY_FAST_DECAY = 0.3
HY_SLOW_DECAY = 1.5
HY_TARGET = 1e-2
ATT_HEADS = 8
ATT_KV_HEADS = 2
ATT_GROUP = 4
WINDOW = 128
EV_IN = 3 * HY_W + (ATT_HEADS + 2 * ATT_KV_HEADS) * HEAD_DIM
HG_W = 1024
HG_HEADS = 8
CHUNK = 64
Q_LORA = 512
KV_LORA = 256
NOPE = 128
ROPE = 64
V_DIM = 128
MLA_HEADS = 8
OD_IN = 5 * HG_W + Q_LORA + KV_LORA + ROPE
OD_IN_PAD = 6144
D_FF = 5632
MACARON_W = 0.5
N_MOD = 9
ROPE_BASE = 10000.0
EPS = 1e-6
ATT_SCALE = HEAD_DIM ** -0.5
MLA_SCALE = (NOPE + ROPE) ** -0.5

MOD_ROWS = 16

V7X_VMEM_LIMIT = 56 * 1024 * 1024
TM = 512
TM_IN = 1024
RC = 512
TF = 512
TN_EVEN = 768
TN_ODD = 1024
MOD_TN = 1024

BF16 = jnp.bfloat16
F32 = jnp.float32
HIGHEST = lax.Precision.HIGHEST


class Group(NamedTuple):
    nb: int
    L: int
    latent: bool

    @property
    def rows(self):
        return self.nb * self.L

    def mod_row(self, i, tm):
        return 1 + i // (self.L // tm) if self.latent else 0


PROMPT = Group(BATCH, SEQ, False)
LATENT = Group(DEC_BATCH, DEC_SEQ, True)


def _cparams(*sem):
    return pltpu.CompilerParams(dimension_semantics=sem, vmem_limit_bytes=V7X_VMEM_LIMIT)


def _rms(x):
    return x * lax.rsqrt(jnp.mean(x * x, axis=-1, keepdims=True) + EPS)


def _silu(x):
    return x * jax.nn.sigmoid(x)


def _dot(a, b):
    return jnp.dot(a, b, preferred_element_type=F32)


def _dot_nt(a, b):
    return lax.dot_general(a, b, (((1,), (1,)), ((), ())), preferred_element_type=F32)


def _dot_tn(a, b):
    return lax.dot_general(a, b, (((0,), (0,)), ((), ())), preferred_element_type=F32)


def _row_chunks(n_rows, body, rc=RC):
    def step(c, carry):
        body(pl.ds(pl.multiple_of(c * rc, rc), rc))
        return carry
    lax.fori_loop(0, n_rows // rc, step, 0, unroll=True)


def _modulated_norm(x_ref, h_scr, gain, shift):
    def body(rows):
        h_scr[rows, :] = (_rms(x_ref[rows, :]) * gain + shift).astype(BF16)
    _row_chunks(x_ref.shape[0], body)


def _mod_kernel(c_ref, w_ref, b_ref, o_ref):
    s = _silu(c_ref[...]).astype(BF16)
    o_ref[...] = _dot(s, w_ref[...].astype(BF16)) + b_ref[...]


def _modulation(cvec, mod_w, mod_b):
    depth = mod_w.shape[0]
    n = N_MOD * D_MODEL
    out = pl.pallas_call(
        _mod_kernel,
        out_shape=jax.ShapeDtypeStruct((depth, MOD_ROWS, n), F32),
        grid=(depth, n // MOD_TN),
        in_specs=[
            pl.BlockSpec((MOD_ROWS, D_MODEL), lambda l, j: (0, 0)),
            pl.BlockSpec((None, D_MODEL, MOD_TN), lambda l, j: (l, 0, j)),
            pl.BlockSpec((None, 1, MOD_TN), lambda l, j: (l, 0, j)),
        ],
        out_specs=pl.BlockSpec((None, MOD_ROWS, MOD_TN), lambda l, j: (l, 0, j)),
        compiler_params=_cparams("parallel", "parallel"),
        name="modulation",
    )(cvec, mod_w, mod_b.reshape(depth, 1, n))
    return out.reshape(depth, MOD_ROWS, N_MOD, D_MODEL)


def _ffn_kernel(j, x_ref, mod_ref, gpre_ref, gpost_ref, wg_ref, wu_ref, wd_ref, o_ref, h_scr):
    f = pl.program_id(1)

    @pl.when(f == 0)
    def _():
        gain = gpre_ref[...] * (1.0 + mod_ref[3 * j + 1:3 * j + 2, :])
        _modulated_norm(x_ref, h_scr, gain, mod_ref[3 * j:3 * j + 1, :])

    def partial_down(rows):
        h = h_scr[rows, :]
        a = (_silu(_dot(h, wg_ref[...])) * _dot(h, wu_ref[...])).astype(BF16)
        return _dot(a, wd_ref[...])

    @pl.when(f == 0)
    def _():
        def first(rows):
            o_ref[rows, :] = partial_down(rows)
        _row_chunks(o_ref.shape[0], first)

    @pl.when(f > 0)
    def _():
        def accumulate(rows):
            o_ref[rows, :] += partial_down(rows)
        _row_chunks(o_ref.shape[0], accumulate)

    @pl.when(f == pl.num_programs(1) - 1)
    def _():
        gain = (MACARON_W * mod_ref[3 * j + 2:3 * j + 3, :]) * gpost_ref[...]

        def residual(rows):
            o_ref[rows, :] = x_ref[rows, :] + _rms(o_ref[rows, :]) * gain
        _row_chunks(o_ref.shape[0], residual)


def _ffn(grp, x, mods, j, g_pre, g_post, wg, wu, wd, l, s):
    return pl.pallas_call(
        functools.partial(_ffn_kernel, j),
        out_shape=jax.ShapeDtypeStruct((grp.rows, D_MODEL), F32),
        grid=(grp.rows // TM, D_FF // TF),
        in_specs=[
            pl.BlockSpec((TM, D_MODEL), lambda i, f: (i, 0)),
            pl.BlockSpec((None, N_MOD, D_MODEL), lambda i, f: (grp.mod_row(i, TM), 0, 0)),
            pl.BlockSpec((1, D_MODEL), lambda i, f: (0, 0)),
            pl.BlockSpec((1, D_MODEL), lambda i, f: (0, 0)),
            pl.BlockSpec((None, None, D_MODEL, TF), lambda i, f: (l, s, 0, f)),
            pl.BlockSpec((None, None, D_MODEL, TF), lambda i, f: (l, s, 0, f)),
            pl.BlockSpec((None, None, TF, D_MODEL), lambda i, f: (l, s, f, 0)),
        ],
        out_specs=pl.BlockSpec((TM, D_MODEL), lambda i, f: (i, 0)),
        scratch_shapes=[pltpu.VMEM((TM, D_MODEL), BF16)],
        compiler_params=_cparams("parallel", "arbitrary"),
        name="ffn",
    )(x, mods, g_pre.reshape(1, D_MODEL), g_post.reshape(1, D_MODEL), wg, wu, wd)


def _inproj_kernel(x_ref, mod_ref, g_ref, w_ref, o_ref, t_ref, h_scr):
    k = pl.program_id(1)
    last = pl.num_programs(1) - 1

    def project():
        y = _dot(h_scr[...], w_ref[...])
        o_ref[...] = y.astype(o_ref.dtype)
        return y

    @pl.when(k == 0)
    def _():
        _modulated_norm(x_ref, h_scr, g_ref[...] * (1.0 + mod_ref[4:5, :]), mod_ref[3:4, :])
        project()

    @pl.when((k > 0) & (k < last))
    def _():
        project()

    @pl.when(k == last)
    def _():
        y = project()
        t_ref[...] = y[:, y.shape[1] - t_ref.shape[1]:]


def _inproj(grp, x, mods, g, w, tn, n_tail):
    n = w.shape[1]
    tm = TM_IN
    return pl.pallas_call(
        _inproj_kernel,
        out_shape=(jax.ShapeDtypeStruct((grp.rows, n), BF16), jax.ShapeDtypeStruct((grp.rows, n_tail), F32)),
        grid=(grp.rows // tm, n // tn),
        in_specs=[
            pl.BlockSpec((tm, D_MODEL), lambda i, k: (i, 0)),
            pl.BlockSpec((None, N_MOD, D_MODEL), lambda i, k: (grp.mod_row(i, tm), 0, 0)),
            pl.BlockSpec((1, D_MODEL), lambda i, k: (0, 0)),
            pl.BlockSpec((D_MODEL, tn), lambda i, k: (0, k)),
        ],
        out_specs=(pl.BlockSpec((tm, tn), lambda i, k: (i, k)),
                   pl.BlockSpec((tm, n_tail), lambda i, k: (i, 0))),
        scratch_shapes=[pltpu.VMEM((tm, D_MODEL), BF16)],
        compiler_params=_cparams("parallel", "arbitrary"),
        name="inproj",
    )(x, mods, g.reshape(1, D_MODEL), w)


def _outproj_kernel(a_ref, b_ref, wa_ref, wb_ref, x_ref, mod_ref, g_ref, o_ref):
    y = _dot(a_ref[...], wa_ref[...]) + _dot(b_ref[...], wb_ref[...])
    o_ref[...] = x_ref[...] + _rms(y) * (mod_ref[5:6, :] * g_ref[...])


def _outproj(grp, a, b, wa, wb, x, mods, g):
    half = a.shape[1]
    tm = TM
    return pl.pallas_call(
        _outproj_kernel,
        out_shape=jax.ShapeDtypeStruct((grp.rows, D_MODEL), F32),
        grid=(grp.rows // tm,),
        in_specs=[
            pl.BlockSpec((tm, half), lambda i: (i, 0)),
            pl.BlockSpec((tm, half), lambda i: (i, 0)),
            pl.BlockSpec((half, D_MODEL), lambda i: (0, 0)),
            pl.BlockSpec((half, D_MODEL), lambda i: (0, 0)),
            pl.BlockSpec((tm, D_MODEL), lambda i: (i, 0)),
            pl.BlockSpec((None, N_MOD, D_MODEL), lambda i: (grp.mod_row(i, tm), 0, 0)),
            pl.BlockSpec((1, D_MODEL), lambda i: (0, 0)),
        ],
        out_specs=pl.BlockSpec((tm, D_MODEL), lambda i: (i, 0)),
        compiler_params=_cparams("parallel"),
        name="outproj",
    )(a, b, wa, wb, x, mods, g.reshape(1, D_MODEL))


def _rope_tables(L, rot_dim, lane0, width):
    half = rot_dim // 2
    inv = ROPE_BASE ** (-jnp.arange(0, half, 2, dtype=F32) / half)
    pos = jnp.arange(L)
    ang_r = (pos // GRID_W).astype(F32)[:, None] * inv
    ang_c = (pos % GRID_W).astype(F32)[:, None] * inv
    cr, sr, cc, sc = jnp.cos(ang_r), jnp.sin(ang_r), jnp.cos(ang_c), jnp.sin(ang_c)
    z = jnp.zeros_like(sr)
    cos = jnp.concatenate([cr, cr, cc, cc], axis=-1)
    sin_a = jnp.concatenate([-sr, z, -sc, z], axis=-1)
    sin_b = jnp.concatenate([z, sr, z, sc], axis=-1)
    pad = ((0, 0), (lane0, width - lane0 - rot_dim))
    return jnp.pad(cos, pad, constant_values=1.0), jnp.pad(sin_a, pad), jnp.pad(sin_b, pad)


def _rope(x, cos, sin_a, sin_b, nf):
    w = x.shape[-1]
    return x * cos + pltpu.roll(x, w - nf, 1) * sin_a + pltpu.roll(x, nf, 1) * sin_b


DFT_SPLIT = 32


def _dft_matrices(L):
    s = jnp.arange(L, dtype=jnp.int32)[None, :]

    def trig(k):
        ang = ((k[:, None] * s) % (2 * L)).astype(F32) * (math.pi / L)
        return jnp.cos(ang), jnp.sin(ang)

    c1, s1 = trig(jnp.arange(0, L, DFT_SPLIT, dtype=jnp.int32))
    c0, s0 = trig(jnp.arange(DFT_SPLIT, dtype=jnp.int32))
    cos = (c1[:, None, :] * c0[None] - s1[:, None, :] * s0[None]).reshape(L, L)
    sin = (s1[:, None, :] * c0[None] + c1[:, None, :] * s0[None]).reshape(L, L)
    nyq = (1 - 2 * (jnp.arange(L, dtype=jnp.int32) % 2)).astype(F32)
    k = jnp.arange(L, dtype=jnp.int32)[:, None]
    msin = jnp.where(k == 0, nyq[None, :], -sin)
    msin_t = jnp.where(s == 0, nyq[:, None], -sin)
    return cos.astype(BF16), msin.astype(BF16), msin_t.astype(BF16)


def _filter_kernel(L, z_ref, w1_ref, b1_ref, w2_ref, b2_ref, fr_ref, w3f_ref, w3b_ref, dl_ref, cos_ref, msin_ref,
                   ka_ref, kb_ref, kc_ref, h_scr):
    z = z_ref[...]

    @pl.when((pl.program_id(0) == 0) & (pl.program_id(1) == 0))
    def _():
        fr = fr_ref[...]
        h1 = jnp.sin(fr * (jnp.dot(z, w1_ref[...], precision=HIGHEST, preferred_element_type=F32) + b1_ref[...]))
        h_scr[...] = jnp.sin(
            fr * (jnp.dot(h1, w2_ref[...], precision=HIGHEST, preferred_element_type=F32) + b2_ref[...]))

    h = h_scr[...]
    decay = jnp.exp(-z[:, 0:1] * dl_ref[...])
    hf = jnp.dot(h, w3f_ref[...], precision=HIGHEST, preferred_element_type=F32) * decay
    hb = jnp.dot(h, w3b_ref[...], precision=HIGHEST, preferred_element_type=F32) * decay
    row = lax.broadcasted_iota(jnp.int32, hf.shape, 0)
    row0 = row == 0
    hb = jnp.where(row0, 0.0, hb)
    even = hf + hb
    re = _dot(cos_ref[...], even.astype(BF16))
    im = _dot(msin_ref[...], (hf - hb).astype(BF16))
    nyq = jnp.sum(jnp.where(row % 2 == 0, even, -even), axis=0, keepdims=True)
    sc = jnp.where(row0, 0.5 / L, 1.0 / L)
    ka_ref[...] = re * sc
    kb_ref[...] = jnp.where(row0, 0.0, im) * sc
    kc_ref[...] = jnp.where(row0, nyq, re) * sc


def _hyena_spectra(L, tc, dft, f_w1, f_b1, f_w2, f_b2, f_w3, f_freq):
    t = jnp.linspace(0.0, 1.0, L, dtype=F32)[:, None]
    bands = (POS_EMB - 1) // 2
    w = 2.0 * math.pi * jnp.arange(L, dtype=F32)[:, None] / L
    fb = jnp.linspace(1e-4, bands - 1, bands, dtype=F32)[None, :]
    z = jnp.concatenate([t, jnp.cos(fb * w), -jnp.sin(fb * w)], axis=-1)
    z = jnp.pad(z, ((0, 0), (0, 128 - POS_EMB)))
    w1 = jnp.pad(f_w1, ((0, 128 - POS_EMB), (0, 0)))
    deltas = jnp.abs(jnp.linspace(math.log(HY_TARGET) / HY_SLOW_DECAY,
                                  math.log(HY_TARGET) / HY_FAST_DECAY, HY_W, dtype=F32))[None, :]
    nct = HY_W // tc
    fo = FILTER_ORDER
    row = lambda a: a.reshape(1, fo)
    kshape = jax.ShapeDtypeStruct((HY_ORDER, L, HY_W), F32)
    kspec = pl.BlockSpec((None, L, tc), lambda n, c: (n, 0, c))
    const = lambda shape: pl.BlockSpec(shape, lambda n, c: (0, 0))
    return pl.pallas_call(
        functools.partial(_filter_kernel, L),
        out_shape=(kshape, kshape, kshape),
        grid=(HY_ORDER, nct),
        in_specs=[
            const((L, 128)), const((128, fo)), const((1, fo)), const((fo, fo)), const((1, fo)), const((1, fo)),
            pl.BlockSpec((fo, tc), lambda n, c: (0, 2 * n * nct + c)),
            pl.BlockSpec((fo, tc), lambda n, c: (0, (2 * n + 1) * nct + c)),
            pl.BlockSpec((1, tc), lambda n, c: (0, c)),
            const((L, L)), const((L, L)),
        ],
        out_specs=(kspec, kspec, kspec),
        scratch_shapes=[pltpu.VMEM((L, fo), F32)],
        compiler_params=_cparams("arbitrary", "arbitrary"),
        name="hyena_filter",
    )(z, w1, row(f_b1), f_w2, row(f_b2), row(f_freq), f_w3, f_w3, deltas, dft[0], dft[1])


def _hyena_kernel(L, nseq, uv_ref, ug0_ref, ug1_ref, cwv_ref, cwg0_ref, cwg1_ref, cbv_ref, cbg0_ref, cbg1_ref,
                  hb_ref, ka_ref, kb_ref, kc_ref, cos_ref, msin_ref, msint_ref, o_ref):
    row = lax.broadcasted_iota(jnp.int32, (L, uv_ref.shape[1]), 0)
    first, last = row == 0, row == L - 1
    seqs = [slice(i * L, (i + 1) * L) for i in range(nseq)]

    def short_conv(u_ref, rows, w_ref, b_ref):
        u = u_ref[rows, :].astype(F32)
        prev = jnp.where(first, 0.0, pltpu.roll(u, 1, 0))
        nxt = jnp.where(last, 0.0, pltpu.roll(u, L - 1, 0))
        return b_ref[...] + prev * w_ref[0:1, :] + u * w_ref[1:2, :] + nxt * w_ref[2:3, :]

    z = [short_conv(uv_ref, r, cwv_ref, cbv_ref) for r in seqs]
    gates = [(short_conv(ug0_ref, r, cwg0_ref, cbg0_ref), short_conv(ug1_ref, r, cwg1_ref, cbg1_ref))
             for r in seqs]
    for n in range(HY_ORDER):
        zb = [zi.astype(BF16) for zi in z]
        zre = [_dot(cos_ref[...], b) for b in zb]
        zim = [_dot(msin_ref[...], b) for b in zb]
        ka, kb, kc = ka_ref[n], kb_ref[n], kc_ref[n]
        yre = [(re * ka - im * kb).astype(BF16) for re, im in zip(zre, zim)]
        yim = [(re * kb + im * kc).astype(BF16) for re, im in zip(zre, zim)]
        conv = [_dot(cos_ref[...], a) + _dot(msint_ref[...], b) for a, b in zip(yre, yim)]
        z = [g[n] * (cv + hb_ref[n:n + 1, :] * zi) for g, cv, zi in zip(gates, conv, z)]
    for r, zi in zip(seqs, z):
        o_ref[r, :] = zi.astype(o_ref.dtype)


def _hyena(grp, p, tc, nseq, spectra, dft, conv_w, conv_b, h_bias):
    L = grp.L
    nct = HY_W // tc
    ka, kb, kc = spectra
    u_spec = lambda j: pl.BlockSpec((nseq * L, tc), lambda c, b: (b, j * nct + c))
    cw_spec = lambda j: pl.BlockSpec((SHORT_CONV, tc), lambda c, b: (0, j * nct + c))
    cb_spec = lambda j: pl.BlockSpec((1, tc), lambda c, b: (0, j * nct + c))
    k_spec = pl.BlockSpec((HY_ORDER, L, tc), lambda c, b: (0, 0, c))
    m_spec = pl.BlockSpec((L, L), lambda c, b: (0, 0))
    cb = conv_b.reshape(1, -1)
    return pl.pallas_call(
        functools.partial(_hyena_kernel, L, nseq),
        out_shape=jax.ShapeDtypeStruct((grp.rows, HY_W), BF16),
        grid=(nct, grp.nb // nseq),
        in_specs=[
            u_spec(0), u_spec(1), u_spec(2), cw_spec(0), cw_spec(1), cw_spec(2),
            cb_spec(0), cb_spec(1), cb_spec(2),
            pl.BlockSpec((HY_ORDER, tc), lambda c, b: (0, c)),
            k_spec, k_spec, k_spec, m_spec, m_spec, m_spec,
        ],
        out_specs=pl.BlockSpec((nseq * L, tc), lambda c, b: (b, c)),
        compiler_params=_cparams("parallel", "arbitrary"),
        name="hyena",
    )(p, p, p, conv_w, conv_w, conv_w, cb, cb, cb, h_bias, ka, kb, kc, *dft)


def _gqa_kernel(L, window, has_ctx, has_rope, emit_kv, *refs):
    it = iter(refs)
    q_ref, k_ref, v_ref, sink_ref = next(it), next(it), next(it), next(it)
    if has_ctx:
        kc_ref, vc_ref = next(it), next(it)
    if has_rope:
        cos_ref, sa_ref, sb_ref = next(it), next(it), next(it)
    o_ref = next(it)
    if emit_kv:
        kn_ref, vn_ref = next(it), next(it)

    qb = HEAD_DIM
    nf = HEAD_DIM // 4
    gw = ATT_GROUP * HEAD_DIM
    hps = k_ref.shape[1] // HEAD_DIM
    sink_row = sink_ref[...]
    sink_lane = lax.broadcasted_iota(jnp.int32, sink_row.shape, 1)
    band_bias = {}

    def bias_for(lo, hi, i):
        key = (lo - i * qb, hi - lo)
        if key not in band_bias:
            shape = (ATT_GROUP * qb, hi - lo)
            rel = key[0] + lax.broadcasted_iota(jnp.int32, shape, 1) - lax.broadcasted_iota(jnp.int32, shape, 0) % qb
            band_bias[key] = jnp.where(jnp.abs(rel) <= window, 0.0, -1e30)
        return band_bias[key]

    for hk in range(hps):
        head = pl.program_id(1) * hps + hk
        k = k_ref[:, hk * HEAD_DIM:(hk + 1) * HEAD_DIM]
        v = v_ref[:, hk * HEAD_DIM:(hk + 1) * HEAD_DIM]
        if emit_kv:
            kn_ref[hk] = k
            vn_ref[hk] = v
        if has_rope:
            k = _rope(k, cos_ref[...], sa_ref[...], sb_ref[...], nf)
        k = k.astype(BF16)
        v = jnp.concatenate([v.astype(BF16), jnp.ones((L, HEAD_DIM), BF16)], axis=1)
        if has_ctx:
            kc = kc_ref[hk].astype(BF16)
            vc = jnp.concatenate([vc_ref[hk].astype(BF16), jnp.ones((PAST_LEN, HEAD_DIM), BF16)], axis=1)
        sinks = [jnp.sum(jnp.where(sink_lane == head * ATT_GROUP + g, sink_row, 0.0), axis=1, keepdims=True)
                 for g in range(ATT_GROUP)]
        sink = jnp.concatenate([jnp.broadcast_to(s, (qb, 1)) for s in sinks], axis=0)

        for i in range(L // qb):
            rows = slice(i * qb, (i + 1) * qb)
            qs = []
            for g in range(ATT_GROUP):
                qg = q_ref[rows, hk * gw + g * HEAD_DIM:hk * gw + (g + 1) * HEAD_DIM].astype(F32)
                if has_rope:
                    qg = _rope(qg, cos_ref[rows, :], sa_ref[rows, :], sb_ref[rows, :], nf)
                qs.append((qg * ATT_SCALE).astype(BF16))
            q = jnp.concatenate(qs, axis=0)
            if window is None:
                lo, hi = 0, L
            else:
                lo, hi = max(0, (i - 1) * qb), min(L, (i + 2) * qb)
            s = _dot_nt(q, k[lo:hi])
            if window is not None:
                s = s + bias_for(lo, hi, i)
            m = jnp.maximum(jnp.max(s, axis=-1, keepdims=True), sink)
            if has_ctx:
                sc = _dot_nt(q, kc)
                m = jnp.maximum(m, jnp.max(sc, axis=-1, keepdims=True))
            oa = _dot(jnp.exp(s - m).astype(BF16), v[lo:hi])
            if has_ctx:
                oa = oa + _dot(jnp.exp(sc - m).astype(BF16), vc)
            o = oa[:, :HEAD_DIM] / (oa[:, HEAD_DIM:] + jnp.exp(sink - m))
            for g in range(ATT_GROUP):
                o_ref[rows, hk * gw + g * HEAD_DIM:hk * gw + (g + 1) * HEAD_DIM] = (
                    o[g * qb:(g + 1) * qb].astype(o_ref.dtype))


def _gqa(grp, p, kv, sink, hps, window=None, ctx=None, rope=None, emit_kv=False):
    L = grp.L
    qw = hps * ATT_GROUP * HEAD_DIM
    kw = hps * HEAD_DIM
    nh = ATT_KV_HEADS // hps
    in_specs = [
        pl.BlockSpec((L, qw), lambda b, h: (b, 3 * HY_W // qw + h)),
        pl.BlockSpec((L, kw), lambda b, h: (b, h)),
        pl.BlockSpec((L, kw), lambda b, h: (b, nh + h)),
        pl.BlockSpec((1, ATT_HEADS), lambda b, h: (0, 0)),
    ]
    args = [p, kv, kv, sink.reshape(1, ATT_HEADS)]
    if ctx is not None:
        kc, vc, e = ctx
        spec = pl.BlockSpec((None, None, hps, PAST_LEN, HEAD_DIM), lambda b, h: (b, e, h, 0, 0))
        in_specs += [spec, spec]
        args += [kc, vc]
    if rope is not None:
        in_specs += [pl.BlockSpec((L, HEAD_DIM), lambda b, h: (0, 0))] * 3
        args += list(rope)
    out_shape = [jax.ShapeDtypeStruct((grp.rows, ATT_HEADS * HEAD_DIM), BF16)]
    out_specs = [pl.BlockSpec((L, qw), lambda b, h: (b, h))]
    if emit_kv:
        kv_shape = jax.ShapeDtypeStruct((grp.nb, 1, ATT_KV_HEADS, L, HEAD_DIM), F32)
        kv_spec = pl.BlockSpec((None, None, hps, L, HEAD_DIM), lambda b, h: (b, 0, h, 0, 0))
        out_shape += [kv_shape, kv_shape]
        out_specs += [kv_spec, kv_spec]
    return pl.pallas_call(
        functools.partial(_gqa_kernel, L, window, ctx is not None, rope is not None, emit_kv),
        out_shape=tuple(out_shape),
        grid=(grp.nb, nh),
        in_specs=in_specs,
        out_specs=tuple(out_specs),
        compiler_params=_cparams("parallel", "parallel"),
        name="gqa",
    )(*args)


def _chunk_cumsum(x, reverse):
    n = x.shape[0]
    pos = lax.broadcasted_iota(jnp.int32, x.shape, 0) % CHUNK
    s = 1
    while s < CHUNK:
        if reverse:
            x = x + jnp.where(pos < CHUNK - s, pltpu.roll(x, n - s, 0), 0.0)
        else:
            x = x + jnp.where(pos >= s, pltpu.roll(x, s, 0), 0.0)
        s *= 2
    return x


def _hgrn_kernel(L, has_state, emit_state, *refs):
    it = iter(refs)
    q_ref, ff_ref, fb_ref, i_ref, g_ref, lb_ref, norm_ref = (next(it) for _ in range(7))
    if has_state:
        s0_ref = next(it)
    o_ref = next(it)
    if emit_state:
        s_ref = next(it)
    acc_scr = next(it)

    a = lb_ref[...]
    e = jnp.exp(a - jnp.max(a, axis=0, keepdims=True))
    lb_all = e[1] / (e[0] + e[1])
    ci = lax.broadcasted_iota(jnp.int32, (CHUNK, CHUNK), 0)
    cj = lax.broadcasted_iota(jnp.int32, (CHUNK, CHUNK), 1)
    n_chunks = L // CHUNK
    chunks = [slice(n * CHUNK, (n + 1) * CHUNK) for n in range(n_chunks)]

    for hh in range(q_ref.shape[1] // HEAD_DIM):
        hc = slice(hh * HEAD_DIM, (hh + 1) * HEAD_DIM)
        q = _silu(q_ref[:, hc].astype(F32))
        v = i_ref[:, hc]
        for d, fz_ref in enumerate((ff_ref, fb_ref)):
            lbd = lb_all[d:d + 1, hc]
            f = lbd + (1.0 - lbd) * jax.nn.sigmoid(fz_ref[:, hc].astype(F32))
            b = _chunk_cumsum(jnp.log(f), reverse=(d == 1))
            qd = (q * jnp.exp(b)).astype(BF16)
            kd32 = (1.0 - f) * jnp.exp(-b)
            kd = kd32.astype(BF16)
            keep = (cj <= ci) if d == 0 else (cj >= ci)
            decay, own = [], []
            for rows in chunks:
                end = rows.stop - 1 if d == 0 else rows.start
                dc = jnp.exp(b[end:end + 1])
                decay.append(dc)
                own.append(_dot_tn(v[rows], (kd32[rows] * dc).astype(BF16)))
            st = s0_ref[d, hh].T if has_state else jnp.zeros((HEAD_DIM, HEAD_DIM), F32)
            entering = [None] * n_chunks
            for n in (range(n_chunks) if d == 0 else range(n_chunks - 1, -1, -1)):
                entering[n] = st.astype(BF16)
                st = st * decay[n] + own[n]
            if emit_state:
                s_ref[d, hh] = st.T
            for n, rows in enumerate(chunks):
                att = jnp.where(keep, _dot_nt(qd[rows], kd[rows]), 0.0).astype(BF16)
                o = _dot(att, v[rows]) + _dot_nt(qd[rows], entering[n])
                if d == 0:
                    acc_scr[rows, :] = o
                else:
                    acc_scr[rows, :] += o
        o = _rms(acc_scr[...]) * norm_ref[...] * _silu(g_ref[:, hc].astype(F32))
        o_ref[:, hc] = o.astype(o_ref.dtype)


def _hgrn(grp, p, hg_lb, norm_g, hps, state=None, emit_state=False):
    L = grp.L
    hw = hps * HEAD_DIM
    nh = HG_HEADS // hps
    col = lambda j: pl.BlockSpec((L, hw), lambda b, h: (b, j * nh + h))
    in_specs = [col(0), col(1), col(2), col(3), col(4),
                pl.BlockSpec((hg_lb.shape[0], 2, hw), lambda b, h: (0, 0, h)),
                pl.BlockSpec((1, HEAD_DIM), lambda b, h: (0, 0))]
    args = [p, p, p, p, p, hg_lb, norm_g.reshape(1, HEAD_DIM)]
    st_spec = lambda o: pl.BlockSpec((None, None, 2, hps, HEAD_DIM, HEAD_DIM), lambda b, h: (b, o, 0, h, 0, 0))
    if state is not None:
        s0, o = state
        in_specs.append(st_spec(o))
        args.append(s0)
    out_shape = [jax.ShapeDtypeStruct((grp.rows, HG_W), BF16)]
    out_specs = [pl.BlockSpec((L, hw), lambda b, h: (b, h))]
    if emit_state:
        out_shape.append(jax.ShapeDtypeStruct((grp.nb, 1, 2, HG_HEADS, HEAD_DIM, HEAD_DIM), F32))
        out_specs.append(st_spec(0))
    return pl.pallas_call(
        functools.partial(_hgrn_kernel, L, state is not None, emit_state),
        out_shape=tuple(out_shape),
        grid=(grp.nb, nh),
        in_specs=in_specs,
        out_specs=tuple(out_specs),
        scratch_shapes=[pltpu.VMEM((L, HEAD_DIM), F32)],
        compiler_params=_cparams("parallel", "parallel"),
        name="hgrn",
    )(*args)


MLA_QW = 256
KR_W = 128
OD_TAIL = 1024


def _mla_prep_kernel(has_rope, emit_kr, *refs):
    it = iter(refs)
    ql_ref, kvl_ref, kr_ref, qn_ref, kvn_ref, wq_ref = (next(it) for _ in range(6))
    if has_rope:
        kc_ref, ksa_ref, ksb_ref = (next(it) for _ in range(3))
    q_ref, ckv_ref, kro_ref = next(it), next(it), next(it)
    if emit_kr:
        krn_ref = next(it)
        krn_ref[...] = kr_ref[:, :ROPE]

    nf = ROPE // 4
    qn = (_rms(ql_ref[...]) * qn_ref[...]).astype(BF16)
    q = _dot(qn, wq_ref[...])
    for h in range(MLA_HEADS):
        nope = slice(h * MLA_QW, h * MLA_QW + NOPE)
        rot = slice(h * MLA_QW + NOPE, (h + 1) * MLA_QW)
        q_ref[:, nope] = q[:, nope].astype(q_ref.dtype)
        qr = q[:, rot]
        if has_rope:
            qr = _rope(qr, kc_ref[...], ksa_ref[...], ksb_ref[...], nf)
        q_ref[:, rot] = qr.astype(q_ref.dtype)
    ckv_ref[...] = _rms(kvl_ref[...]) * kvn_ref[...]
    kr = kr_ref[...]
    if has_rope:
        kr = _rope(kr, kc_ref[...], ksa_ref[...], ksb_ref[...], nf)
    kro_ref[...] = kr.astype(kro_ref.dtype)


def _mla_prep(grp, tail, q_norm, kv_norm, wq, rope=None, emit_kr=False):
    tm = min(512, grp.L)
    per = grp.L // tm
    n_rows = grp.rows
    in_specs = [
        pl.BlockSpec((tm, Q_LORA), lambda i: (i, 0)),
        pl.BlockSpec((tm, KV_LORA), lambda i: (i, Q_LORA // KV_LORA)),
        pl.BlockSpec((tm, KR_W), lambda i: (i, (Q_LORA + KV_LORA) // KR_W)),
        pl.BlockSpec((1, Q_LORA), lambda i: (0, 0)),
        pl.BlockSpec((1, KV_LORA), lambda i: (0, 0)),
        pl.BlockSpec((Q_LORA, MLA_HEADS * MLA_QW), lambda i: (0, 0)),
    ]
    args = [tail, tail, tail, q_norm.reshape(1, Q_LORA), kv_norm.reshape(1, KV_LORA), wq]
    if rope is not None:
        in_specs += [pl.BlockSpec((tm, KR_W), lambda i: (i % per, 0))] * 3
        args += list(rope)
    out_shape = [jax.ShapeDtypeStruct((n_rows, MLA_HEADS * MLA_QW), BF16),
                 jax.ShapeDtypeStruct((n_rows, KV_LORA), F32),
                 jax.ShapeDtypeStruct((n_rows, KR_W), BF16)]
    out_specs = [pl.BlockSpec((tm, MLA_HEADS * MLA_QW), lambda i: (i, 0)),
                 pl.BlockSpec((tm, KV_LORA), lambda i: (i, 0)),
                 pl.BlockSpec((tm, KR_W), lambda i: (i, 0))]
    if emit_kr:
        out_shape.append(jax.ShapeDtypeStruct((n_rows, ROPE), F32))
        out_specs.append(pl.BlockSpec((tm, ROPE), lambda i: (i, 0)))
    return pl.pallas_call(
        functools.partial(_mla_prep_kernel, rope is not None, emit_kr),
        out_shape=tuple(out_shape),
        grid=(n_rows // tm,),
        in_specs=in_specs,
        out_specs=tuple(out_specs),
        compiler_params=_cparams("parallel"),
        name="mla_prep",
    )(*args)


def _mla_attn_kernel(L, has_ctx, *refs):
    it = iter(refs)
    q_ref, ckv_ref, kr_ref, wkv_ref = (next(it) for _ in range(4))
    if has_ctx:
        cckv_ref, ckr_ref = next(it), next(it)
    o_ref = next(it)

    ckv = ckv_ref[...].astype(BF16)
    kr = kr_ref[...]
    if has_ctx:
        ckv = jnp.concatenate([ckv, cckv_ref[...].astype(BF16)], axis=0)
        kr = jnp.concatenate([kr, ckr_ref[...].astype(BF16)], axis=0)
    ones = jnp.ones((ckv.shape[0], V_DIM), BF16)
    qb = min(L, 256)
    for hh in range(q_ref.shape[1] // MLA_QW):
        qc = slice(hh * MLA_QW, (hh + 1) * MLA_QW)
        oc = slice(hh * V_DIM, (hh + 1) * V_DIM)
        kv = _dot(ckv, wkv_ref[:, qc])
        kh = jnp.concatenate([kv[:, :NOPE].astype(BF16), kr], axis=1)
        vh = jnp.concatenate([kv[:, NOPE:].astype(BF16), ones], axis=1)
        for i in range(L // qb):
            rows = slice(i * qb, (i + 1) * qb)
            s = _dot_nt(q_ref[rows, qc], kh) * MLA_SCALE
            m = jnp.max(s, axis=-1, keepdims=True)
            oa = _dot(jnp.exp(s - m).astype(BF16), vh)
            o_ref[rows, oc] = (oa[:, :V_DIM] / oa[:, V_DIM:]).astype(o_ref.dtype)


def _mla_attn(grp, q, ckv, kr, wkv, hps, ctx=None):
    L = grp.L
    in_specs = [
        pl.BlockSpec((L, hps * MLA_QW), lambda b, h: (b, h)),
        pl.BlockSpec((L, KV_LORA), lambda b, h: (b, 0)),
        pl.BlockSpec((L, KR_W), lambda b, h: (b, 0)),
        pl.BlockSpec((KV_LORA, hps * (NOPE + V_DIM)), lambda b, h: (0, h)),
    ]
    args = [q, ckv, kr, wkv]
    if ctx is not None:
        cckv, ckr, o = ctx
        in_specs += [pl.BlockSpec((None, None, PAST_LEN, KV_LORA), lambda b, h: (b, o, 0, 0)),
                     pl.BlockSpec((None, None, PAST_LEN, KR_W), lambda b, h: (b, o, 0, 0))]
        args += [cckv, ckr]
    return pl.pallas_call(
        functools.partial(_mla_attn_kernel, L, ctx is not None),
        out_shape=jax.ShapeDtypeStruct((grp.rows, MLA_HEADS * V_DIM), BF16),
        grid=(grp.nb, MLA_HEADS // hps),
        in_specs=in_specs,
        out_specs=pl.BlockSpec((L, hps * V_DIM), lambda b, h: (b, h)),
        compiler_params=_cparams("parallel", "parallel"),
        name="mla_attn",
    )(*args)


def kernel(x_prompt, x_sample, c, c_ctx, cache_attn_k, cache_attn_v, cache_mla_ckv, cache_mla_krope, state_hgrn, mod_w, mod_b, norm_g, ffn_wg, ffn_wu, ffn_wd, ev_w_in, ev_w_out, hy_conv_w, hy_conv_b, hy_f_w1, hy_f_b1, hy_f_w2, hy_f_b2, hy_f_w3, hy_f_freq, hy_bias, attn_sink, od_w_in, od_w_out, hg_lb, hg_norm, mla_q_norm, mla_w_qb, mla_kv_norm, mla_w_kvb):
    depth = mod_w.shape[0]
    groups = (PROMPT, LATENT)
    xs = [x_prompt.reshape(PROMPT.rows, D_MODEL), x_sample.reshape(LATENT.rows, D_MODEL)]
    cvec = jnp.concatenate([c_ctx[None, :], c, jnp.zeros((MOD_ROWS - 1 - DEC_BATCH, D_MODEL), F32)], axis=0)
    mods_all = _modulation(cvec, mod_w, mod_b)

    wg, wu, wd = ffn_wg.astype(BF16), ffn_wu.astype(BF16), ffn_wd.astype(BF16)
    hy_tc = {SEQ: 512, DEC_SEQ: 256}
    hy_nseq = {SEQ: 4, DEC_SEQ: 2}

    new_k = new_v = new_ckv = new_kr = new_s = None
    for l in range(depth):
        mods = mods_all[l]
        xs = [_ffn(grp, x, mods, 0, norm_g[l, 0], norm_g[l, 1], wg, wu, wd, l, 0) for grp, x in zip(groups, xs)]
        if l % 2 == 0:
            e = l // 2
            w_in = ev_w_in[e].astype(BF16)
            w_out = ev_w_out[e].astype(BF16)
            kv_cols = 2 * ATT_KV_HEADS * HEAD_DIM
            mix = []
            for grp, x in zip(groups, xs):
                p, kv = _inproj(grp, x, mods, norm_g[l, 2], w_in, TN_EVEN, kv_cols)
                dft = _dft_matrices(grp.L)
                tc = hy_tc[grp.L]
                spectra = _hyena_spectra(grp.L, tc, dft, hy_f_w1[e], hy_f_b1[e], hy_f_w2[e], hy_f_b2[e],
                                         hy_f_w3[e], hy_f_freq[e])
                hy = _hyena(grp, p, tc, hy_nseq[grp.L], spectra, dft, hy_conv_w[e], hy_conv_b[e], hy_bias[e])
                if grp.latent:
                    rope = _rope_tables(grp.L, HEAD_DIM, 0, HEAD_DIM)
                    (att,) = _gqa(grp, p, kv, attn_sink[e], 1, window=WINDOW,
                                  ctx=(cache_attn_k, cache_attn_v, e), rope=rope)
                else:
                    att, new_k, new_v = _gqa(grp, p, kv, attn_sink[e], 2, emit_kv=True)
                mix.append((hy, att))
        else:
            o = l // 2
            w_in = lax.dynamic_update_slice(jnp.zeros((D_MODEL, OD_IN_PAD), BF16), od_w_in[o].astype(BF16), (0, 0))
            w_out = od_w_out[o].astype(BF16)
            wq = mla_w_qb[o].reshape(Q_LORA, MLA_HEADS, NOPE + ROPE)
            wq = jnp.pad(wq, ((0, 0), (0, 0), (0, MLA_QW - NOPE - ROPE))).reshape(Q_LORA, -1).astype(BF16)
            wkv = mla_w_kvb[o].astype(BF16)
            mix = []
            for grp, x in zip(groups, xs):
                p, tail = _inproj(grp, x, mods, norm_g[l, 2], w_in, TN_ODD, OD_TAIL)
                if grp.latent:
                    (hg,) = _hgrn(grp, p, hg_lb, hg_norm[o], 2, state=(state_hgrn, o))
                    rope = _rope_tables(grp.L, ROPE, 0, KR_W)
                    q, ckv, kr = _mla_prep(grp, tail, mla_q_norm[o], mla_kv_norm[o], wq, rope=rope)
                    ckr = jnp.pad(cache_mla_krope, ((0, 0), (0, 0), (0, 0), (0, KR_W - ROPE)))
                    att = _mla_attn(grp, q, ckv, kr, wkv, 4, ctx=(cache_mla_ckv, ckr, o))
                else:
                    hg, new_s = _hgrn(grp, p, hg_lb, hg_norm[o], 2, emit_state=True)
                    q, ckv, kr, kr_raw = _mla_prep(grp, tail, mla_q_norm[o], mla_kv_norm[o], wq, emit_kr=True)
                    att = _mla_attn(grp, q, ckv, kr, wkv, 4)
                    new_ckv = ckv.reshape(BATCH, 1, SEQ, KV_LORA)
                    new_kr = kr_raw.reshape(BATCH, 1, SEQ, ROPE)
                mix.append((hg, att))
        half = w_out.shape[0] // 2
        xs = [_outproj(grp, a, b, w_out[:half], w_out[half:], x, mods, norm_g[l, 3])
              for grp, x, (a, b) in zip(groups, xs, mix)]
        xs = [_ffn(grp, x, mods, 2, norm_g[l, 4], norm_g[l, 5], wg, wu, wd, l, 1) for grp, x in zip(groups, xs)]

    y_prompt = xs[0].reshape(BATCH, SEQ, D_MODEL)
    y_sample = xs[1].reshape(DEC_BATCH, DEC_SEQ, D_MODEL)
    return (y_prompt, y_sample, new_k, new_v, new_ckv, new_kr, new_s)
```

```python
import functools
import math
from typing import NamedTuple

import jax
import jax.numpy as jnp
from jax import lax
from jax.experimental import pallas as pl
from jax.experimental.pallas import tpu as pltpu

D_MODEL = 2048
BATCH = 16
SEQ = 256
DEC_BATCH = 8
DEC_SEQ = 1024
PAST_LEN = 512
GRID_W = 64
HEAD_DIM = 128
HY_W = 1024
HY_ORDER = 2
SHORT_CONV = 3
POS_EMB = 33
FILTER_ORDER = 64
HY_FAST_DECAY = 0.3
HY_SLOW_DECAY = 1.5
HY_TARGET = 1e-2
ATT_HEADS = 8
ATT_KV_HEADS = 2
ATT_GROUP = 4
WINDOW = 128
EV_IN = 3 * HY_W + (ATT_HEADS + 2 * ATT_KV_HEADS) * HEAD_DIM
HG_W = 1024
HG_HEADS = 8
CHUNK = 64
Q_LORA = 512
KV_LORA = 256
NOPE = 128
ROPE = 64
V_DIM = 128
MLA_HEADS = 8
OD_IN = 5 * HG_W + Q_LORA + KV_LORA + ROPE
OD_IN_PAD = 6144
D_FF = 5632
MACARON_W = 0.5
N_MOD = 9
ROPE_BASE = 10000.0
EPS = 1e-6
ATT_SCALE = HEAD_DIM ** -0.5
MLA_SCALE = (NOPE + ROPE) ** -0.5

MOD_ROWS = 16

V7X_VMEM_LIMIT = 56 * 1024 * 1024
TM = 512
TM_IN = 1024
RC = 512
TF = 512
TN_EVEN = 768
TN_ODD = 1024
MOD_TN = 1024

BF16 = jnp.bfloat16
F32 = jnp.float32
HIGHEST = lax.Precision.HIGHEST


class Group(NamedTuple):
    nb: int
    L: int
    latent: bool

    @property
    def rows(self):
        return self.nb * self.L

    def mod_row(self, i, tm):
        return 1 + i // (self.L // tm) if self.latent else 0


PROMPT = Group(BATCH, SEQ, False)
LATENT = Group(DEC_BATCH, DEC_SEQ, True)


def _cparams(*sem):
    return pltpu.CompilerParams(dimension_semantics=sem, vmem_limit_bytes=V7X_VMEM_LIMIT)


def _rms(x):
    return x * lax.rsqrt(jnp.mean(x * x, axis=-1, keepdims=True) + EPS)


def _silu(x):
    return x * jax.nn.sigmoid(x)


def _dot(a, b):
    return jnp.dot(a, b, preferred_element_type=F32)


def _dot_nt(a, b):
    return lax.dot_general(a, b, (((1,), (1,)), ((), ())), preferred_element_type=F32)


def _dot_tn(a, b):
    return lax.dot_general(a, b, (((0,), (0,)), ((), ())), preferred_element_type=F32)


def _row_chunks(n_rows, body, rc=RC):
    def step(c, carry):
        body(pl.ds(pl.multiple_of(c * rc, rc), rc))
        return carry
    lax.fori_loop(0, n_rows // rc, step, 0, unroll=True)


def _modulated_norm(x_ref, h_scr, gain, shift):
    def body(rows):
        h_scr[rows, :] = (_rms(x_ref[rows, :]) * gain + shift).astype(BF16)
    _row_chunks(x_ref.shape[0], body)


def _mod_kernel(c_ref, w_ref, b_ref, o_ref):
    s = _silu(c_ref[...]).astype(BF16)
    o_ref[...] = _dot(s, w_ref[...].astype(BF16)) + b_ref[...]


def _modulation(cvec, mod_w, mod_b):
    depth = mod_w.shape[0]
    n = N_MOD * D_MODEL
    out = pl.pallas_call(
        _mod_kernel,
        out_shape=jax.ShapeDtypeStruct((depth, MOD_ROWS, n), F32),
        grid=(depth, n // MOD_TN),
        in_specs=[
            pl.BlockSpec((MOD_ROWS, D_MODEL), lambda l, j: (0, 0)),
            pl.BlockSpec((None, D_MODEL, MOD_TN), lambda l, j: (l, 0, j)),
            pl.BlockSpec((None, 1, MOD_TN), lambda l, j: (l, 0, j)),
        ],
        out_specs=pl.BlockSpec((None, MOD_ROWS, MOD_TN), lambda l, j: (l, 0, j)),
        compiler_params=_cparams("parallel", "parallel"),
        name="modulation",
    )(cvec, mod_w, mod_b.reshape(depth, 1, n))
    return out.reshape(depth, MOD_ROWS, N_MOD, D_MODEL)


def _ffn_kernel(j, x_ref, mod_ref, gpre_ref, gpost_ref, wg_ref, wu_ref, wd_ref, o_ref, h_scr):
    f = pl.program_id(1)

    @pl.when(f == 0)
    def _():
        gain = gpre_ref[...] * (1.0 + mod_ref[3 * j + 1:3 * j + 2, :])
        _modulated_norm(x_ref, h_scr, gain, mod_ref[3 * j:3 * j + 1, :])

    def partial_down(rows):
        h = h_scr[rows, :]
        a = (_silu(_dot(h, wg_ref[...])) * _dot(h, wu_ref[...])).astype(BF16)
        return _dot(a, wd_ref[...])

    @pl.when(f == 0)
    def _():
        def first(rows):
            o_ref[rows, :] = partial_down(rows)
        _row_chunks(o_ref.shape[0], first)

    @pl.when(f > 0)
    def _():
        def accumulate(rows):
            o_ref[rows, :] += partial_down(rows)
        _row_chunks(o_ref.shape[0], accumulate)

    @pl.when(f == pl.num_programs(1) - 1)
    def _():
        gain = (MACARON_W * mod_ref[3 * j + 2:3 * j + 3, :]) * gpost_ref[...]

        def residual(rows):
            o_ref[rows, :] = x_ref[rows, :] + _rms(o_ref[rows, :]) * gain
        _row_chunks(o_ref.shape[0], residual)


def _ffn(grp, x, mods, j, g_pre, g_post, wg, wu, wd, l, s):
    return pl.pallas_call(
        functools.partial(_ffn_kernel, j),
        out_shape=jax.ShapeDtypeStruct((grp.rows, D_MODEL), F32),
        grid=(grp.rows // TM, D_FF // TF),
        in_specs=[
            pl.BlockSpec((TM, D_MODEL), lambda i, f: (i, 0)),
            pl.BlockSpec((None, N_MOD, D_MODEL), lambda i, f: (grp.mod_row(i, TM), 0, 0)),
            pl.BlockSpec((1, D_MODEL), lambda i, f: (0, 0)),
            pl.BlockSpec((1, D_MODEL), lambda i, f: (0, 0)),
            pl.BlockSpec((None, None, D_MODEL, TF), lambda i, f: (l, s, 0, f)),
            pl.BlockSpec((None, None, D_MODEL, TF), lambda i, f: (l, s, 0, f)),
            pl.BlockSpec((None, None, TF, D_MODEL), lambda i, f: (l, s, f, 0)),
        ],
        out_specs=pl.BlockSpec((TM, D_MODEL), lambda i, f: (i, 0)),
        scratch_shapes=[pltpu.VMEM((TM, D_MODEL), BF16)],
        compiler_params=_cparams("parallel", "arbitrary"),
        name="ffn",
    )(x, mods, g_pre.reshape(1, D_MODEL), g_post.reshape(1, D_MODEL), wg, wu, wd)


def _inproj_kernel(x_ref, mod_ref, g_ref, w_ref, wl_ref, o_ref, t_ref, h_scr):
    k = pl.program_id(1)
    last = pl.num_programs(1) - 1

    def project(weights_ref):
        y = _dot(h_scr[...], weights_ref[...])
        o_ref[...] = y.astype(o_ref.dtype)
        return y

    @pl.when(k == 0)
    def _():
        _modulated_norm(x_ref, h_scr, g_ref[...] * (1.0 + mod_ref[4:5, :]), mod_ref[3:4, :])
        project(w_ref)

    @pl.when((k > 0) & (k < last))
    def _():
        project(w_ref)

    @pl.when(k == last)
    def _():
        y = project(wl_ref)
        t_ref[...] = y[:, y.shape[1] - t_ref.shape[1]:]


def _inproj(grp, x, mods, g, w, w_last, n_tail):
    tn = w_last.shape[1]
    steps = w.shape[1] // tn + 1
    assert steps >= 2 and w.shape[1] % tn == 0
    tm = TM_IN
    return pl.pallas_call(
        _inproj_kernel,
        out_shape=(jax.ShapeDtypeStruct((grp.rows, steps * tn), BF16),
                   jax.ShapeDtypeStruct((grp.rows, n_tail), F32)),
        grid=(grp.rows // tm, steps),
        in_specs=[
            pl.BlockSpec((tm, D_MODEL), lambda i, k: (i, 0)),
            pl.BlockSpec((None, N_MOD, D_MODEL), lambda i, k: (grp.mod_row(i, tm), 0, 0)),
            pl.BlockSpec((1, D_MODEL), lambda i, k: (0, 0)),
            pl.BlockSpec((D_MODEL, tn), lambda i, k: (0, jnp.minimum(k, steps - 2))),
            pl.BlockSpec((D_MODEL, tn), lambda i, k: (0, 0)),
        ],
        out_specs=(pl.BlockSpec((tm, tn), lambda i, k: (i, k)),
                   pl.BlockSpec((tm, n_tail), lambda i, k: (i, 0))),
        scratch_shapes=[pltpu.VMEM((tm, D_MODEL), BF16)],
        compiler_params=_cparams("parallel", "arbitrary"),
        name="inproj",
    )(x, mods, g.reshape(1, D_MODEL), w, w_last)


def _outproj_kernel(a_ref, b_ref, wa_ref, wb_ref, x_ref, mod_ref, g_ref, o_ref):
    y = _dot(a_ref[...], wa_ref[...]) + _dot(b_ref[...], wb_ref[...])
    o_ref[...] = x_ref[...] + _rms(y) * (mod_ref[5:6, :] * g_ref[...])


def _outproj(grp, a, b, wa, wb, x, mods, g):
    half = a.shape[1]
    tm = TM
    return pl.pallas_call(
        _outproj_kernel,
        out_shape=jax.ShapeDtypeStruct((grp.rows, D_MODEL), F32),
        grid=(grp.rows // tm,),
        in_specs=[
            pl.BlockSpec((tm, half), lambda i: (i, 0)),
            pl.BlockSpec((tm, half), lambda i: (i, 0)),
            pl.BlockSpec((half, D_MODEL), lambda i: (0, 0)),
            pl.BlockSpec((half, D_MODEL), lambda i: (0, 0)),
            pl.BlockSpec((tm, D_MODEL), lambda i: (i, 0)),
            pl.BlockSpec((None, N_MOD, D_MODEL), lambda i: (grp.mod_row(i, tm), 0, 0)),
            pl.BlockSpec((1, D_MODEL), lambda i: (0, 0)),
        ],
        out_specs=pl.BlockSpec((tm, D_MODEL), lambda i: (i, 0)),
        compiler_params=_cparams("parallel"),
        name="outproj",
    )(a, b, wa, wb, x, mods, g.reshape(1, D_MODEL))


def _rope_tables(L, rot_dim, lane0, width):
    half = rot_dim // 2
    inv = ROPE_BASE ** (-jnp.arange(0, half, 2, dtype=F32) / half)
    pos = jnp.arange(L)
    ang_r = (pos // GRID_W).astype(F32)[:, None] * inv
    ang_c = (pos % GRID_W).astype(F32)[:, None] * inv
    cr, sr, cc, sc = jnp.cos(ang_r), jnp.sin(ang_r), jnp.cos(ang_c), jnp.sin(ang_c)
    z = jnp.zeros_like(sr)
    cos = jnp.concatenate([cr, cr, cc, cc], axis=-1)
    sin_a = jnp.concatenate([-sr, z, -sc, z], axis=-1)
    sin_b = jnp.concatenate([z, sr, z, sc], axis=-1)
    pad = ((0, 0), (lane0, width - lane0 - rot_dim))
    return jnp.pad(cos, pad, constant_values=1.0), jnp.pad(sin_a, pad), jnp.pad(sin_b, pad)


def _rope(x, cos, sin_a, sin_b, nf):
    w = x.shape[-1]
    return x * cos + pltpu.roll(x, w - nf, 1) * sin_a + pltpu.roll(x, nf, 1) * sin_b


DFT_SPLIT = 32


def _dft_matrices(L):
    s = jnp.arange(L, dtype=jnp.int32)[None, :]

    def trig(k):
        ang = ((k[:, None] * s) % (2 * L)).astype(F32) * (math.pi / L)
        return jnp.cos(ang), jnp.sin(ang)

    c1, s1 = trig(jnp.arange(0, L, DFT_SPLIT, dtype=jnp.int32))
    c0, s0 = trig(jnp.arange(DFT_SPLIT, dtype=jnp.int32))
    cos = (c1[:, None, :] * c0[None] - s1[:, None, :] * s0[None]).reshape(L, L)
    sin = (s1[:, None, :] * c0[None] + c1[:, None, :] * s0[None]).reshape(L, L)
    nyq = (1 - 2 * (jnp.arange(L, dtype=jnp.int32) % 2)).astype(F32)
    k = jnp.arange(L, dtype=jnp.int32)[:, None]
    msin = jnp.where(k == 0, nyq[None, :], -sin)
    msin_t = jnp.where(s == 0, nyq[:, None], -sin)
    return cos.astype(BF16), msin.astype(BF16), msin_t.astype(BF16)


def _filter_kernel(L, z_ref, w1_ref, b1_ref, w2_ref, b2_ref, fr_ref, w3f_ref, w3b_ref, dl_ref, cos_ref, msin_ref,
                   ka_ref, kb_ref, kc_ref, h_scr):
    z = z_ref[...]

    @pl.when((pl.program_id(0) == 0) & (pl.program_id(1) == 0))
    def _():
        fr = fr_ref[...]
        h1 = jnp.sin(fr * (jnp.dot(z, w1_ref[...], precision=HIGHEST, preferred_element_type=F32) + b1_ref[...]))
        h_scr[...] = jnp.sin(
            fr * (jnp.dot(h1, w2_ref[...], precision=HIGHEST, preferred_element_type=F32) + b2_ref[...]))

    h = h_scr[...]
    decay = jnp.exp(-z[:, 0:1] * dl_ref[...])
    hf = jnp.dot(h, w3f_ref[...], precision=HIGHEST, preferred_element_type=F32) * decay
    hb = jnp.dot(h, w3b_ref[...], precision=HIGHEST, preferred_element_type=F32) * decay
    row = lax.broadcasted_iota(jnp.int32, hf.shape, 0)
    row0 = row == 0
    hb = jnp.where(row0, 0.0, hb)
    even = hf + hb
    re = _dot(cos_ref[...], even.astype(BF16))
    im = _dot(msin_ref[...], (hf - hb).astype(BF16))
    nyq = jnp.sum(jnp.where(row % 2 == 0, even, -even), axis=0, keepdims=True)
    sc = jnp.where(row0, 0.5 / L, 1.0 / L)
    ka_ref[...] = re * sc
    kb_ref[...] = jnp.where(row0, 0.0, im) * sc
    kc_ref[...] = jnp.where(row0, nyq, re) * sc


def _hyena_spectra(L, tc, dft, f_w1, f_b1, f_w2, f_b2, f_w3, f_freq):
    t = jnp.linspace(0.0, 1.0, L, dtype=F32)[:, None]
    bands = (POS_EMB - 1) // 2
    w = 2.0 * math.pi * jnp.arange(L, dtype=F32)[:, None] / L
    fb = jnp.linspace(1e-4, bands - 1, bands, dtype=F32)[None, :]
    z = jnp.concatenate([t, jnp.cos(fb * w), -jnp.sin(fb * w)], axis=-1)
    z = jnp.pad(z, ((0, 0), (0, 128 - POS_EMB)))
    w1 = jnp.pad(f_w1, ((0, 128 - POS_EMB), (0, 0)))
    deltas = jnp.abs(jnp.linspace(math.log(HY_TARGET) / HY_SLOW_DECAY,
                                  math.log(HY_TARGET) / HY_FAST_DECAY, HY_W, dtype=F32))[None, :]
    nct = HY_W // tc
    fo = FILTER_ORDER
    row = lambda a: a.reshape(1, fo)
    kshape = jax.ShapeDtypeStruct((HY_ORDER, L, HY_W), F32)
    kspec = pl.BlockSpec((None, L, tc), lambda n, c: (n, 0, c))
    const = lambda shape: pl.BlockSpec(shape, lambda n, c: (0, 0))
    return pl.pallas_call(
        functools.partial(_filter_kernel, L),
        out_shape=(kshape, kshape, kshape),
        grid=(HY_ORDER, nct),
        in_specs=[
            const((L, 128)), const((128, fo)), const((1, fo)), const((fo, fo)), const((1, fo)), const((1, fo)),
            pl.BlockSpec((fo, tc), lambda n, c: (0, 2 * n * nct + c)),
            pl.BlockSpec((fo, tc), lambda n, c: (0, (2 * n + 1) * nct + c)),
            pl.BlockSpec((1, tc), lambda n, c: (0, c)),
            const((L, L)), const((L, L)),
        ],
        out_specs=(kspec, kspec, kspec),
        scratch_shapes=[pltpu.VMEM((L, fo), F32)],
        compiler_params=_cparams("arbitrary", "arbitrary"),
        name="hyena_filter",
    )(z, w1, row(f_b1), f_w2, row(f_b2), row(f_freq), f_w3, f_w3, deltas, dft[0], dft[1])


def _hyena_kernel(L, nseq, uv_ref, ug0_ref, ug1_ref, cwv_ref, cwg0_ref, cwg1_ref, cbv_ref, cbg0_ref, cbg1_ref,
                  hb_ref, ka_ref, kb_ref, kc_ref, cos_ref, msin_ref, msint_ref, o_ref):
    row = lax.broadcasted_iota(jnp.int32, (L, uv_ref.shape[1]), 0)
    first, last = row == 0, row == L - 1
    seqs = [slice(i * L, (i + 1) * L) for i in range(nseq)]

    def short_conv(u_ref, rows, w_ref, b_ref):
        u = u_ref[rows, :].astype(F32)
        prev = jnp.where(first, 0.0, pltpu.roll(u, 1, 0))
        nxt = jnp.where(last, 0.0, pltpu.roll(u, L - 1, 0))
        return b_ref[...] + prev * w_ref[0:1, :] + u * w_ref[1:2, :] + nxt * w_ref[2:3, :]

    z = [short_conv(uv_ref, r, cwv_ref, cbv_ref) for r in seqs]
    gates = [(short_conv(ug0_ref, r, cwg0_ref, cbg0_ref), short_conv(ug1_ref, r, cwg1_ref, cbg1_ref))
             for r in seqs]
    for n in range(HY_ORDER):
        zb = [zi.astype(BF16) for zi in z]
        zre = [_dot(cos_ref[...], b) for b in zb]
        zim = [_dot(msin_ref[...], b) for b in zb]
        ka, kb, kc = ka_ref[n], kb_ref[n], kc_ref[n]
        yre = [(re * ka - im * kb).astype(BF16) for re, im in zip(zre, zim)]
        yim = [(re * kb + im * kc).astype(BF16) for re, im in zip(zre, zim)]
        conv = [_dot(cos_ref[...], a) + _dot(msint_ref[...], b) for a, b in zip(yre, yim)]
        z = [g[n] * (cv + hb_ref[n:n + 1, :] * zi) for g, cv, zi in zip(gates, conv, z)]
    for r, zi in zip(seqs, z):
        o_ref[r, :] = zi.astype(o_ref.dtype)


def _hyena(grp, p, tc, nseq, spectra, dft, conv_w, conv_b, h_bias):
    L = grp.L
    nct = HY_W // tc
    ka, kb, kc = spectra
    u_spec = lambda j: pl.BlockSpec((nseq * L, tc), lambda c, b: (b, j * nct + c))
    cw_spec = lambda j: pl.BlockSpec((SHORT_CONV, tc), lambda c, b: (0, j * nct + c))
    cb_spec = lambda j: pl.BlockSpec((1, tc), lambda c, b: (0, j * nct + c))
    k_spec = pl.BlockSpec((HY_ORDER, L, tc), lambda c, b: (0, 0, c))
    m_spec = pl.BlockSpec((L, L), lambda c, b: (0, 0))
    cb = conv_b.reshape(1, -1)
    return pl.pallas_call(
        functools.partial(_hyena_kernel, L, nseq),
        out_shape=jax.ShapeDtypeStruct((grp.rows, HY_W), BF16),
        grid=(nct, grp.nb // nseq),
        in_specs=[
            u_spec(0), u_spec(1), u_spec(2), cw_spec(0), cw_spec(1), cw_spec(2),
            cb_spec(0), cb_spec(1), cb_spec(2),
            pl.BlockSpec((HY_ORDER, tc), lambda c, b: (0, c)),
            k_spec, k_spec, k_spec, m_spec, m_spec, m_spec,
        ],
        out_specs=pl.BlockSpec((nseq * L, tc), lambda c, b: (b, c)),
        compiler_params=_cparams("parallel", "arbitrary"),
        name="hyena",
    )(p, p, p, conv_w, conv_w, conv_w, cb, cb, cb, h_bias, ka, kb, kc, *dft)


def _gqa_kernel(L, window, has_ctx, has_rope, emit_kv, *refs):
    it = iter(refs)
    q_ref, k_ref, v_ref, sink_ref = next(it), next(it), next(it), next(it)
    if has_ctx:
        kc_ref, vc_ref = next(it), next(it)
    if has_rope:
        cos_ref, sa_ref, sb_ref = next(it), next(it), next(it)
    o_ref = next(it)
    if emit_kv:
        kn_ref, vn_ref = next(it), next(it)

    qb = HEAD_DIM
    nf = HEAD_DIM // 4
    gw = ATT_GROUP * HEAD_DIM
    hps = k_ref.shape[1] // HEAD_DIM
    sink_row = sink_ref[...]
    sink_lane = lax.broadcasted_iota(jnp.int32, sink_row.shape, 1)
    band_bias = {}

    def bias_for(lo, hi, i):
        key = (lo - i * qb, hi - lo)
        if key not in band_bias:
            shape = (ATT_GROUP * qb, hi - lo)
            rel = key[0] + lax.broadcasted_iota(jnp.int32, shape, 1) - lax.broadcasted_iota(jnp.int32, shape, 0) % qb
            band_bias[key] = jnp.where(jnp.abs(rel) <= window, 0.0, -1e30)
        return band_bias[key]

    for hk in range(hps):
        head = pl.program_id(1) * hps + hk
        k = k_ref[:, hk * HEAD_DIM:(hk + 1) * HEAD_DIM]
        v = v_ref[:, hk * HEAD_DIM:(hk + 1) * HEAD_DIM]
        if emit_kv:
            kn_ref[hk] = k
            vn_ref[hk] = v
        if has_rope:
            k = _rope(k, cos_ref[...], sa_ref[...], sb_ref[...], nf)
        k = k.astype(BF16)
        v = jnp.concatenate([v.astype(BF16), jnp.ones((L, HEAD_DIM), BF16)], axis=1)
        if has_ctx:
            kc = kc_ref[hk].astype(BF16)
            vc = jnp.concatenate([vc_ref[hk].astype(BF16), jnp.ones((PAST_LEN, HEAD_DIM), BF16)], axis=1)
        sinks = [jnp.sum(jnp.where(sink_lane == head * ATT_GROUP + g, sink_row, 0.0), axis=1, keepdims=True)
                 for g in range(ATT_GROUP)]
        sink = jnp.concatenate([jnp.broadcast_to(s, (qb, 1)) for s in sinks], axis=0)

        for i in range(L // qb):
            rows = slice(i * qb, (i + 1) * qb)
            qs = []
            for g in range(ATT_GROUP):
                qg = q_ref[rows, hk * gw + g * HEAD_DIM:hk * gw + (g + 1) * HEAD_DIM].astype(F32)
                if has_rope:
                    qg = _rope(qg, cos_ref[rows, :], sa_ref[rows, :], sb_ref[rows, :], nf)
                qs.append((qg * ATT_SCALE).astype(BF16))
            q = jnp.concatenate(qs, axis=0)
            if window is None:
                lo, hi = 0, L
            else:
                lo, hi = max(0, (i - 1) * qb), min(L, (i + 2) * qb)
            s = _dot_nt(q, k[lo:hi])
            if window is not None:
                s = s + bias_for(lo, hi, i)
            m = jnp.maximum(jnp.max(s, axis=-1, keepdims=True), sink)
            if has_ctx:
                sc = _dot_nt(q, kc)
                m = jnp.maximum(m, jnp.max(sc, axis=-1, keepdims=True))
            oa = _dot(jnp.exp(s - m).astype(BF16), v[lo:hi])
            if has_ctx:
                oa = oa + _dot(jnp.exp(sc - m).astype(BF16), vc)
            o = oa[:, :HEAD_DIM] / (oa[:, HEAD_DIM:] + jnp.exp(sink - m))
            for g in range(ATT_GROUP):
                o_ref[rows, hk * gw + g * HEAD_DIM:hk * gw + (g + 1) * HEAD_DIM] = (
                    o[g * qb:(g + 1) * qb].astype(o_ref.dtype))


def _gqa(grp, p, kv, sink, hps, window=None, ctx=None, rope=None, emit_kv=False):
    L = grp.L
    qw = hps * ATT_GROUP * HEAD_DIM
    kw = hps * HEAD_DIM
    nh = ATT_KV_HEADS // hps
    in_specs = [
        pl.BlockSpec((L, qw), lambda b, h: (b, 3 * HY_W // qw + h)),
        pl.BlockSpec((L, kw), lambda b, h: (b, h)),
        pl.BlockSpec((L, kw), lambda b, h: (b, nh + h)),
        pl.BlockSpec((1, ATT_HEADS), lambda b, h: (0, 0)),
    ]
    args = [p, kv, kv, sink.reshape(1, ATT_HEADS)]
    if ctx is not None:
        kc, vc, e = ctx
        spec = pl.BlockSpec((None, None, hps, PAST_LEN, HEAD_DIM), lambda b, h: (b, e, h, 0, 0))
        in_specs += [spec, spec]
        args += [kc, vc]
    if rope is not None:
        in_specs += [pl.BlockSpec((L, HEAD_DIM), lambda b, h: (0, 0))] * 3
        args += list(rope)
    out_shape = [jax.ShapeDtypeStruct((grp.rows, ATT_HEADS * HEAD_DIM), BF16)]
    out_specs = [pl.BlockSpec((L, qw), lambda b, h: (b, h))]
    if emit_kv:
        kv_shape = jax.ShapeDtypeStruct((grp.nb, 1, ATT_KV_HEADS, L, HEAD_DIM), F32)
        kv_spec = pl.BlockSpec((None, None, hps, L, HEAD_DIM), lambda b, h: (b, 0, h, 0, 0))
        out_shape += [kv_shape, kv_shape]
        out_specs += [kv_spec, kv_spec]
    return pl.pallas_call(
        functools.partial(_gqa_kernel, L, window, ctx is not None, rope is not None, emit_kv),
        out_shape=tuple(out_shape),
        grid=(grp.nb, nh),
        in_specs=in_specs,
        out_specs=tuple(out_specs),
        compiler_params=_cparams("parallel", "parallel"),
        name="gqa",
    )(*args)


def _chunk_cumsum(x, reverse):
    n = x.shape[0]
    pos = lax.broadcasted_iota(jnp.int32, x.shape, 0) % CHUNK
    s = 1
    while s < CHUNK:
        if reverse:
            x = x + jnp.where(pos < CHUNK - s, pltpu.roll(x, n - s, 0), 0.0)
        else:
            x = x + jnp.where(pos >= s, pltpu.roll(x, s, 0), 0.0)
        s *= 2
    return x


def _hgrn_kernel(L, has_state, emit_state, *refs):
    it = iter(refs)
    q_ref, ff_ref, fb_ref, i_ref, g_ref, lb_ref, norm_ref = (next(it) for _ in range(7))
    if has_state:
        s0_ref = next(it)
    o_ref = next(it)
    if emit_state:
        s_ref = next(it)
    acc_scr = next(it)

    a = lb_ref[...]
    e = jnp.exp(a - jnp.max(a, axis=0, keepdims=True))
    lb_all = e[1] / (e[0] + e[1])
    ci = lax.broadcasted_iota(jnp.int32, (CHUNK, CHUNK), 0)
    cj = lax.broadcasted_iota(jnp.int32, (CHUNK, CHUNK), 1)
    n_chunks = L // CHUNK
    chunks = [slice(n * CHUNK, (n + 1) * CHUNK) for n in range(n_chunks)]

    for hh in range(q_ref.shape[1] // HEAD_DIM):
        hc = slice(hh * HEAD_DIM, (hh + 1) * HEAD_DIM)
        q = _silu(q_ref[:, hc].astype(F32))
        for d, fz_ref in enumerate((ff_ref, fb_ref)):
            lbd = lb_all[d:d + 1, hc]
            keep = (cj <= ci) if d == 0 else (cj >= ci)
            qds, atts, decay, own = [], [], [], []
            for rows in chunks:
                f = lbd + (1.0 - lbd) * jax.nn.sigmoid(fz_ref[rows, hc].astype(F32))
                b = _chunk_cumsum(jnp.log(f), reverse=(d == 1))
                qd = (q[rows] * jnp.exp(b)).astype(BF16)
                kd32 = (1.0 - f) * jnp.exp(-b)
                dc = jnp.exp(b[CHUNK - 1:CHUNK] if d == 0 else b[0:1])
                qds.append(qd)
                decay.append(dc)
                atts.append(jnp.where(keep, _dot_nt(qd, kd32.astype(BF16)), 0.0).astype(BF16))
                own.append(_dot_tn(i_ref[rows, hc], (kd32 * dc).astype(BF16)))
            st = s0_ref[d, hh].T if has_state else jnp.zeros((HEAD_DIM, HEAD_DIM), F32)
            entering = [None] * n_chunks
            for n in (range(n_chunks) if d == 0 else range(n_chunks - 1, -1, -1)):
                entering[n] = st.astype(BF16)
                st = st * decay[n] + own[n]
            if emit_state:
                s_ref[d, hh] = st.T
            for n, rows in enumerate(chunks):
                o = _dot(atts[n], i_ref[rows, hc]) + _dot_nt(qds[n], entering[n])
                if d == 0:
                    acc_scr[rows, :] = o
                else:
                    acc_scr[rows, :] += o
        o = _rms(acc_scr[...]) * norm_ref[...] * _silu(g_ref[:, hc].astype(F32))
        o_ref[:, hc] = o.astype(o_ref.dtype)


def _hgrn(grp, p, hg_lb, norm_g, hps, state=None, emit_state=False):
    L = grp.L
    hw = hps * HEAD_DIM
    nh = HG_HEADS // hps
    col = lambda j: pl.BlockSpec((L, hw), lambda b, h: (b, j * nh + h))
    in_specs = [col(0), col(1), col(2), col(3), col(4),
                pl.BlockSpec((hg_lb.shape[0], 2, hw), lambda b, h: (0, 0, h)),
                pl.BlockSpec((1, HEAD_DIM), lambda b, h: (0, 0))]
    args = [p, p, p, p, p, hg_lb, norm_g.reshape(1, HEAD_DIM)]
    st_spec = lambda o: pl.BlockSpec((None, None, 2, hps, HEAD_DIM, HEAD_DIM), lambda b, h: (b, o, 0, h, 0, 0))
    if state is not None:
        s0, o = state
        in_specs.append(st_spec(o))
        args.append(s0)
    out_shape = [jax.ShapeDtypeStruct((grp.rows, HG_W), BF16)]
    out_specs = [pl.BlockSpec((L, hw), lambda b, h: (b, h))]
    if emit_state:
        out_shape.append(jax.ShapeDtypeStruct((grp.nb, 1, 2, HG_HEADS, HEAD_DIM, HEAD_DIM), F32))
        out_specs.append(st_spec(0))
    return pl.pallas_call(
        functools.partial(_hgrn_kernel, L, state is not None, emit_state),
        out_shape=tuple(out_shape),
        grid=(grp.nb, nh),
        in_specs=in_specs,
        out_specs=tuple(out_specs),
        scratch_shapes=[pltpu.VMEM((L, HEAD_DIM), F32)],
        compiler_params=_cparams("parallel", "parallel"),
        name="hgrn",
    )(*args)


MLA_QW = 256
KR_W = 128
OD_TAIL = 1024


def _mla_prep_kernel(has_rope, emit_kr, *refs):
    it = iter(refs)
    ql_ref, kvl_ref, kr_ref, qn_ref, kvn_ref, wq_ref = (next(it) for _ in range(6))
    if has_rope:
        kc_ref, ksa_ref, ksb_ref = (next(it) for _ in range(3))
    q_ref, ckv_ref, kro_ref = next(it), next(it), next(it)
    if emit_kr:
        krn_ref = next(it)
        krn_ref[...] = kr_ref[:, :ROPE]

    nf = ROPE // 4
    qn = (_rms(ql_ref[...]) * qn_ref[...]).astype(BF16)
    q = _dot(qn, wq_ref[...])
    for h in range(MLA_HEADS):
        nope = slice(h * MLA_QW, h * MLA_QW + NOPE)
        rot = slice(h * MLA_QW + NOPE, (h + 1) * MLA_QW)
        q_ref[:, nope] = q[:, nope].astype(q_ref.dtype)
        qr = q[:, rot]
        if has_rope:
            qr = _rope(qr, kc_ref[...], ksa_ref[...], ksb_ref[...], nf)
        q_ref[:, rot] = qr.astype(q_ref.dtype)
    ckv_ref[...] = _rms(kvl_ref[...]) * kvn_ref[...]
    kr = kr_ref[...]
    if has_rope:
        kr = _rope(kr, kc_ref[...], ksa_ref[...], ksb_ref[...], nf)
    kro_ref[...] = kr.astype(kro_ref.dtype)


def _mla_prep(grp, tail, q_norm, kv_norm, wq, rope=None, emit_kr=False):
    tm = min(512, grp.L)
    per = grp.L // tm
    n_rows = grp.rows
    in_specs = [
        pl.BlockSpec((tm, Q_LORA), lambda i: (i, 0)),
        pl.BlockSpec((tm, KV_LORA), lambda i: (i, Q_LORA // KV_LORA)),
        pl.BlockSpec((tm, KR_W), lambda i: (i, (Q_LORA + KV_LORA) // KR_W)),
        pl.BlockSpec((1, Q_LORA), lambda i: (0, 0)),
        pl.BlockSpec((1, KV_LORA), lambda i: (0, 0)),
        pl.BlockSpec((Q_LORA, MLA_HEADS * MLA_QW), lambda i: (0, 0)),
    ]
    args = [tail, tail, tail, q_norm.reshape(1, Q_LORA), kv_norm.reshape(1, KV_LORA), wq]
    if rope is not None:
        in_specs += [pl.BlockSpec((tm, KR_W), lambda i: (i % per, 0))] * 3
        args += list(rope)
    out_shape = [jax.ShapeDtypeStruct((n_rows, MLA_HEADS * MLA_QW), BF16),
                 jax.ShapeDtypeStruct((n_rows, KV_LORA), F32),
                 jax.ShapeDtypeStruct((n_rows, KR_W), BF16)]
    out_specs = [pl.BlockSpec((tm, MLA_HEADS * MLA_QW), lambda i: (i, 0)),
                 pl.BlockSpec((tm, KV_LORA), lambda i: (i, 0)),
                 pl.BlockSpec((tm, KR_W), lambda i: (i, 0))]
    if emit_kr:
        out_shape.append(jax.ShapeDtypeStruct((n_rows, ROPE), F32))
        out_specs.append(pl.BlockSpec((tm, ROPE), lambda i: (i, 0)))
    return pl.pallas_call(
        functools.partial(_mla_prep_kernel, rope is not None, emit_kr),
        out_shape=tuple(out_shape),
        grid=(n_rows // tm,),
        in_specs=in_specs,
        out_specs=tuple(out_specs),
        compiler_params=_cparams("parallel"),
        name="mla_prep",
    )(*args)


def _mla_attn_kernel(L, has_ctx, *refs):
    it = iter(refs)
    q_ref, ckv_ref, kr_ref, wkv_ref = (next(it) for _ in range(4))
    if has_ctx:
        cckv_ref, ckr_ref = next(it), next(it)
    o_ref = next(it)

    ckv = ckv_ref[...].astype(BF16)
    kr = kr_ref[...]
    if has_ctx:
        ckv = jnp.concatenate([ckv, cckv_ref[...].astype(BF16)], axis=0)
        kr = jnp.concatenate([kr, ckr_ref[...].astype(BF16)], axis=0)
    ones = jnp.ones((ckv.shape[0], V_DIM), BF16)
    qb = min(L, 256)
    for hh in range(q_ref.shape[1] // MLA_QW):
        qc = slice(hh * MLA_QW, (hh + 1) * MLA_QW)
        oc = slice(hh * V_DIM, (hh + 1) * V_DIM)
        kv = _dot(ckv, wkv_ref[:, qc])
        kh = jnp.concatenate([kv[:, :NOPE].astype(BF16), kr], axis=1)
        vh = jnp.concatenate([kv[:, NOPE:].astype(BF16), ones], axis=1)
        for i in range(L // qb):
            rows = slice(i * qb, (i + 1) * qb)
            s = _dot_nt(q_ref[rows, qc], kh) * MLA_SCALE
            m = jnp.max(s, axis=-1, keepdims=True)
            oa = _dot(jnp.exp(s - m).astype(BF16), vh)
            o_ref[rows, oc] = (oa[:, :V_DIM] / oa[:, V_DIM:]).astype(o_ref.dtype)


def _mla_attn(grp, q, ckv, kr, wkv, hps, ctx=None):
    L = grp.L
    in_specs = [
        pl.BlockSpec((L, hps * MLA_QW), lambda b, h: (b, h)),
        pl.BlockSpec((L, KV_LORA), lambda b, h: (b, 0)),
        pl.BlockSpec((L, KR_W), lambda b, h: (b, 0)),
        pl.BlockSpec((KV_LORA, hps * (NOPE + V_DIM)), lambda b, h: (0, h)),
    ]
    args = [q, ckv, kr, wkv]
    if ctx is not None:
        cckv, ckr, o = ctx
        in_specs += [pl.BlockSpec((None, None, PAST_LEN, KV_LORA), lambda b, h: (b, o, 0, 0)),
                     pl.BlockSpec((None, None, PAST_LEN, KR_W), lambda b, h: (b, o, 0, 0))]
        args += [cckv, ckr]
    return pl.pallas_call(
        functools.partial(_mla_attn_kernel, L, ctx is not None),
        out_shape=jax.ShapeDtypeStruct((grp.rows, MLA_HEADS * V_DIM), BF16),
        grid=(grp.nb, MLA_HEADS // hps),
        in_specs=in_specs,
        out_specs=pl.BlockSpec((L, hps * V_DIM), lambda b, h: (b, h)),
        compiler_params=_cparams("parallel", "parallel"),
        name="mla_attn",
    )(*args)


def kernel(x_prompt, x_sample, c, c_ctx, cache_attn_k, cache_attn_v, cache_mla_ckv, cache_mla_krope, state_hgrn, mod_w, mod_b, norm_g, ffn_wg, ffn_wu, ffn_wd, ev_w_in, ev_w_out, hy_conv_w, hy_conv_b, hy_f_w1, hy_f_b1, hy_f_w2, hy_f_b2, hy_f_w3, hy_f_freq, hy_bias, attn_sink, od_w_in, od_w_out, hg_lb, hg_norm, mla_q_norm, mla_w_qb, mla_kv_norm, mla_w_kvb):
    depth = mod_w.shape[0]
    groups = (PROMPT, LATENT)
    xs = [x_prompt.reshape(PROMPT.rows, D_MODEL), x_sample.reshape(LATENT.rows, D_MODEL)]
    cvec = jnp.concatenate([c_ctx[None, :], c, jnp.zeros((MOD_ROWS - 1 - DEC_BATCH, D_MODEL), F32)], axis=0)
    mods_all = _modulation(cvec, mod_w, mod_b)

    wg, wu, wd = ffn_wg.astype(BF16), ffn_wu.astype(BF16), ffn_wd.astype(BF16)
    hy_tc = {SEQ: 512, DEC_SEQ: 256}
    hy_nseq = {SEQ: 4, DEC_SEQ: 2}

    new_k = new_v = new_ckv = new_kr = new_s = None
    for l in range(depth):
        mods = mods_all[l]
        xs = [_ffn(grp, x, mods, 0, norm_g[l, 0], norm_g[l, 1], wg, wu, wd, l, 0) for grp, x in zip(groups, xs)]
        if l % 2 == 0:
            e = l // 2
            w_in = ev_w_in[e][:, :EV_IN - TN_EVEN].astype(BF16)
            w_in_last = ev_w_in[e][:, EV_IN - TN_EVEN:].astype(BF16)
            w_out = ev_w_out[e].astype(BF16)
            kv_cols = 2 * ATT_KV_HEADS * HEAD_DIM
            mix = []
            for grp, x in zip(groups, xs):
                p, kv = _inproj(grp, x, mods, norm_g[l, 2], w_in, w_in_last, kv_cols)
                dft = _dft_matrices(grp.L)
                tc = hy_tc[grp.L]
                spectra = _hyena_spectra(grp.L, tc, dft, hy_f_w1[e], hy_f_b1[e], hy_f_w2[e], hy_f_b2[e],
                                         hy_f_w3[e], hy_f_freq[e])
                hy = _hyena(grp, p, tc, hy_nseq[grp.L], spectra, dft, hy_conv_w[e], hy_conv_b[e], hy_bias[e])
                if grp.latent:
                    rope = _rope_tables(grp.L, HEAD_DIM, 0, HEAD_DIM)
                    (att,) = _gqa(grp, p, kv, attn_sink[e], 1, window=WINDOW,
                                  ctx=(cache_attn_k, cache_attn_v, e), rope=rope)
                else:
                    att, new_k, new_v = _gqa(grp, p, kv, attn_sink[e], 2, emit_kv=True)
                mix.append((hy, att))
        else:
            o = l // 2
            n_main = OD_IN_PAD - TN_ODD
            w_in = od_w_in[o][:, :n_main].astype(BF16)
            w_in_last = jnp.pad(od_w_in[o][:, n_main:].astype(BF16), ((0, 0), (0, OD_IN_PAD - OD_IN)))
            w_out = od_w_out[o].astype(BF16)
            wq = mla_w_qb[o].reshape(Q_LORA, MLA_HEADS, NOPE + ROPE)
            wq = jnp.pad(wq, ((0, 0), (0, 0), (0, MLA_QW - NOPE - ROPE))).reshape(Q_LORA, -1).astype(BF16)
            wkv = mla_w_kvb[o].astype(BF16)
            mix = []
            for grp, x in zip(groups, xs):
                p, tail = _inproj(grp, x, mods, norm_g[l, 2], w_in, w_in_last, OD_TAIL)
                if grp.latent:
                    (hg,) = _hgrn(grp, p, hg_lb, hg_norm[o], 2, state=(state_hgrn, o))
                    rope = _rope_tables(grp.L, ROPE, 0, KR_W)
                    q, ckv, kr = _mla_prep(grp, tail, mla_q_norm[o], mla_kv_norm[o], wq, rope=rope)
                    ckr = jnp.pad(cache_mla_krope, ((0, 0), (0, 0), (0, 0), (0, KR_W - ROPE)))
                    att = _mla_attn(grp, q, ckv, kr, wkv, 4, ctx=(cache_mla_ckv, ckr, o))
                else:
                    hg, new_s = _hgrn(grp, p, hg_lb, hg_norm[o], 2, emit_state=True)
                    q, ckv, kr, kr_raw = _mla_prep(grp, tail, mla_q_norm[o], mla_kv_norm[o], wq, emit_kr=True)
                    att = _mla_attn(grp, q, ckv, kr, wkv, 4)
                    new_ckv = ckv.reshape(BATCH, 1, SEQ, KV_LORA)
                    new_kr = kr_raw.reshape(BATCH, 1, SEQ, ROPE)
                mix.append((hg, att))
        half = w_out.shape[0] // 2
        xs = [_outproj(grp, a, b, w_out[:half], w_out[half:], x, mods, norm_g[l, 3])
              for grp, x, (a, b) in zip(groups, xs, mix)]
        xs = [_ffn(grp, x, mods, 2, norm_g[l, 4], norm_g[l, 5], wg, wu, wd, l, 1) for grp, x in zip(groups, xs)]

    y_prompt = xs[0].reshape(BATCH, SEQ, D_MODEL)
    y_sample = xs[1].reshape(DEC_BATCH, DEC_SEQ, D_MODEL)
    return (y_prompt, y_sample, new_k, new_v, new_ckv, new_kr, new_s)
```

```python
import functools
import math
from typing import NamedTuple

import jax
import jax.numpy as jnp
from jax import lax
from jax.experimental import pallas as pl
from jax.experimental.pallas import tpu as pltpu

D_MODEL = 2048
BATCH = 16
SEQ = 256
DEC_BATCH = 8
DEC_SEQ = 1024
PAST_LEN = 512
GRID_W = 64
HEAD_DIM = 128
HY_W = 1024
HY_ORDER = 2
SHORT_CONV = 3
POS_EMB = 33
FILTER_ORDER = 64
HY_FAST_DECAY = 0.3
HY_SLOW_DECAY = 1.5
HY_TARGET = 1e-2
ATT_HEADS = 8
ATT_KV_HEADS = 2
ATT_GROUP = 4
WINDOW = 128
EV_IN = 3 * HY_W + (ATT_HEADS + 2 * ATT_KV_HEADS) * HEAD_DIM
HG_W = 1024
HG_HEADS = 8
CHUNK = 64
Q_LORA = 512
KV_LORA = 256
NOPE = 128
ROPE = 64
V_DIM = 128
MLA_HEADS = 8
OD_IN = 5 * HG_W + Q_LORA + KV_LORA + ROPE
OD_IN_PAD = 6144
D_FF = 5632
MACARON_W = 0.5
N_MOD = 9
ROPE_BASE = 10000.0
EPS = 1e-6
ATT_SCALE = HEAD_DIM ** -0.5
MLA_SCALE = (NOPE + ROPE) ** -0.5

MOD_ROWS = 16

V7X_VMEM_LIMIT = 56 * 1024 * 1024
TM = 512
TM_FFN = 1024
FFN_VMEM_LIMIT = 60 * 1024 * 1024
TM_IN = 1024
RC = 512
TF = 512
TN_EVEN = 768
TN_ODD = 1024
MOD_TN = 2048

BF16 = jnp.bfloat16
F32 = jnp.float32
HIGHEST = lax.Precision.HIGHEST


class Group(NamedTuple):
    nb: int
    L: int
    latent: bool

    @property
    def rows(self):
        return self.nb * self.L

    def mod_row(self, i, tm):
        return 1 + i // (self.L // tm) if self.latent else 0


PROMPT = Group(BATCH, SEQ, False)
LATENT = Group(DEC_BATCH, DEC_SEQ, True)


def _cparams(*sem):
    return pltpu.CompilerParams(dimension_semantics=sem, vmem_limit_bytes=V7X_VMEM_LIMIT)


def _rms(x):
    return x * lax.rsqrt(jnp.mean(x * x, axis=-1, keepdims=True) + EPS)


def _silu(x):
    return x * jax.nn.sigmoid(x)


def _dot(a, b):
    return jnp.dot(a, b, preferred_element_type=F32)


def _dot_nt(a, b):
    return lax.dot_general(a, b, (((1,), (1,)), ((), ())), preferred_element_type=F32)


def _dot_tn(a, b):
    return lax.dot_general(a, b, (((0,), (0,)), ((), ())), preferred_element_type=F32)


def _row_chunks(n_rows, body, rc=RC):
    def step(c, carry):
        body(pl.ds(pl.multiple_of(c * rc, rc), rc))
        return carry
    lax.fori_loop(0, n_rows // rc, step, 0, unroll=True)


def _modulated_norm(x_ref, h_scr, gain, shift):
    def body(rows):
        h_scr[rows, :] = (_rms(x_ref[rows, :]) * gain + shift).astype(BF16)
    _row_chunks(x_ref.shape[0], body)


def _mod_kernel(c_ref, w_ref, b_ref, o_ref):
    s = _silu(c_ref[...]).astype(BF16)
    o_ref[...] = _dot(s, w_ref[...].astype(BF16)) + b_ref[...]


def _modulation(cvec, mod_w, mod_b):
    depth = mod_w.shape[0]
    n = N_MOD * D_MODEL
    out = pl.pallas_call(
        _mod_kernel,
        out_shape=jax.ShapeDtypeStruct((depth, MOD_ROWS, n), F32),
        grid=(depth, n // MOD_TN),
        in_specs=[
            pl.BlockSpec((MOD_ROWS, D_MODEL), lambda l, j: (0, 0)),
            pl.BlockSpec((None, D_MODEL, MOD_TN), lambda l, j: (l, 0, j)),
            pl.BlockSpec((None, 1, MOD_TN), lambda l, j: (l, 0, j)),
        ],
        out_specs=pl.BlockSpec((None, MOD_ROWS, MOD_TN), lambda l, j: (l, 0, j)),
        compiler_params=_cparams("parallel", "parallel"),
        name="modulation",
    )(cvec, mod_w, mod_b.reshape(depth, 1, n))
    return out.reshape(depth, MOD_ROWS, N_MOD, D_MODEL)


def _ffn_kernel(j, x_ref, mod_ref, gpre_ref, gpost_ref, wg_ref, wu_ref, wd_ref, o_ref, h_scr):
    f = pl.program_id(1)

    @pl.when(f == 0)
    def _():
        gain = gpre_ref[...] * (1.0 + mod_ref[3 * j + 1:3 * j + 2, :])
        _modulated_norm(x_ref, h_scr, gain, mod_ref[3 * j:3 * j + 1, :])

    def partial_down(rows):
        h = h_scr[rows, :]
        a = (_silu(_dot(h, wg_ref[...])) * _dot(h, wu_ref[...])).astype(BF16)
        return _dot(a, wd_ref[...])

    @pl.when(f == 0)
    def _():
        def first(rows):
            o_ref[rows, :] = partial_down(rows)
        _row_chunks(o_ref.shape[0], first)

    last = pl.num_programs(1) - 1

    @pl.when((f > 0) & (f < last))
    def _():
        def accumulate(rows):
            o_ref[rows, :] += partial_down(rows)
        _row_chunks(o_ref.shape[0], accumulate)

    @pl.when(f == last)
    def _():
        gain = (MACARON_W * mod_ref[3 * j + 2:3 * j + 3, :]) * gpost_ref[...]

        def finish(rows):
            acc = o_ref[rows, :] + partial_down(rows)
            o_ref[rows, :] = x_ref[rows, :] + _rms(acc) * gain
        _row_chunks(o_ref.shape[0], finish, RC // 2)


def _ffn(grp, x, mods, j, g_pre, g_post, wg, wu, wd, l, s):
    return pl.pallas_call(
        functools.partial(_ffn_kernel, j),
        out_shape=jax.ShapeDtypeStruct((grp.rows, D_MODEL), F32),
        grid=(grp.rows // TM_FFN, D_FF // TF),
        in_specs=[
            pl.BlockSpec((TM_FFN, D_MODEL), lambda i, f: (i, 0)),
            pl.BlockSpec((None, N_MOD, D_MODEL), lambda i, f: (grp.mod_row(i, TM_FFN), 0, 0)),
            pl.BlockSpec((1, D_MODEL), lambda i, f: (0, 0)),
            pl.BlockSpec((1, D_MODEL), lambda i, f: (0, 0)),
            pl.BlockSpec((None, None, D_MODEL, TF), lambda i, f: (l, s, 0, f)),
            pl.BlockSpec((None, None, D_MODEL, TF), lambda i, f: (l, s, 0, f)),
            pl.BlockSpec((None, None, TF, D_MODEL), lambda i, f: (l, s, f, 0)),
        ],
        out_specs=pl.BlockSpec((TM_FFN, D_MODEL), lambda i, f: (i, 0)),
        scratch_shapes=[pltpu.VMEM((TM_FFN, D_MODEL), BF16)],
        compiler_params=pltpu.CompilerParams(dimension_semantics=("parallel", "arbitrary"),
                                             vmem_limit_bytes=FFN_VMEM_LIMIT),
        name="ffn",
    )(x, mods, g_pre.reshape(1, D_MODEL), g_post.reshape(1, D_MODEL), wg, wu, wd)


def _inproj_kernel(x_ref, mod_ref, g_ref, w_ref, wl_ref, o_ref, t_ref, h_scr):
    k = pl.program_id(1)
    last = pl.num_programs(1) - 1

    def project(weights_ref):
        y = _dot(h_scr[...], weights_ref[...])
        o_ref[...] = y.astype(o_ref.dtype)
        return y

    @pl.when(k == 0)
    def _():
        _modulated_norm(x_ref, h_scr, g_ref[...] * (1.0 + mod_ref[4:5, :]), mod_ref[3:4, :])
        project(w_ref)

    @pl.when((k > 0) & (k < last))
    def _():
        project(w_ref)

    @pl.when(k == last)
    def _():
        y = project(wl_ref)
        t_ref[...] = y[:, y.shape[1] - t_ref.shape[1]:]


def _inproj(grp, x, mods, g, w, w_last, n_tail):
    tn = w_last.shape[1]
    steps = w.shape[1] // tn + 1
    assert steps >= 2 and w.shape[1] % tn == 0
    tm = TM_IN
    return pl.pallas_call(
        _inproj_kernel,
        out_shape=(jax.ShapeDtypeStruct((grp.rows, steps * tn), BF16),
                   jax.ShapeDtypeStruct((grp.rows, n_tail), F32)),
        grid=(grp.rows // tm, steps),
        in_specs=[
            pl.BlockSpec((tm, D_MODEL), lambda i, k: (i, 0)),
            pl.BlockSpec((None, N_MOD, D_MODEL), lambda i, k: (grp.mod_row(i, tm), 0, 0)),
            pl.BlockSpec((1, D_MODEL), lambda i, k: (0, 0)),
            pl.BlockSpec((D_MODEL, tn), lambda i, k: (0, jnp.minimum(k, steps - 2))),
            pl.BlockSpec((D_MODEL, tn), lambda i, k: (0, 0)),
        ],
        out_specs=(pl.BlockSpec((tm, tn), lambda i, k: (i, k)),
                   pl.BlockSpec((tm, n_tail), lambda i, k: (i, 0))),
        scratch_shapes=[pltpu.VMEM((tm, D_MODEL), BF16)],
        compiler_params=_cparams("parallel", "arbitrary"),
        name="inproj",
    )(x, mods, g.reshape(1, D_MODEL), w, w_last)


def _outproj_kernel(a_ref, b_ref, wa_ref, wb_ref, x_ref, mod_ref, g_ref, o_ref):
    y = _dot(a_ref[...], wa_ref[...]) + _dot(b_ref[...], wb_ref[...])
    o_ref[...] = x_ref[...] + _rms(y) * (mod_ref[5:6, :] * g_ref[...])


def _outproj(grp, a, b, w, x, mods, g):
    half = a.shape[1]
    tm = TM
    return pl.pallas_call(
        _outproj_kernel,
        out_shape=jax.ShapeDtypeStruct((grp.rows, D_MODEL), F32),
        grid=(grp.rows // tm,),
        in_specs=[
            pl.BlockSpec((tm, half), lambda i: (i, 0)),
            pl.BlockSpec((tm, half), lambda i: (i, 0)),
            pl.BlockSpec((half, D_MODEL), lambda i: (0, 0), pipeline_mode=pl.Buffered(1)),
            pl.BlockSpec((half, D_MODEL), lambda i: (1, 0), pipeline_mode=pl.Buffered(1)),
            pl.BlockSpec((tm, D_MODEL), lambda i: (i, 0)),
            pl.BlockSpec((None, N_MOD, D_MODEL), lambda i: (grp.mod_row(i, tm), 0, 0)),
            pl.BlockSpec((1, D_MODEL), lambda i: (0, 0)),
        ],
        out_specs=pl.BlockSpec((tm, D_MODEL), lambda i: (i, 0)),
        compiler_params=_cparams("parallel"),
        name="outproj",
    )(a, b, w, w, x, mods, g.reshape(1, D_MODEL))


def _rope_tables(L, rot_dim, lane0, width):
    half = rot_dim // 2
    inv = ROPE_BASE ** (-jnp.arange(0, half, 2, dtype=F32) / half)
    pos = jnp.arange(L)
    ang_r = (pos // GRID_W).astype(F32)[:, None] * inv
    ang_c = (pos % GRID_W).astype(F32)[:, None] * inv
    cr, sr, cc, sc = jnp.cos(ang_r), jnp.sin(ang_r), jnp.cos(ang_c), jnp.sin(ang_c)
    z = jnp.zeros_like(sr)
    cos = jnp.concatenate([cr, cr, cc, cc], axis=-1)
    sin_a = jnp.concatenate([-sr, z, -sc, z], axis=-1)
    sin_b = jnp.concatenate([z, sr, z, sc], axis=-1)
    pad = ((0, 0), (lane0, width - lane0 - rot_dim))
    return jnp.pad(cos, pad, constant_values=1.0), jnp.pad(sin_a, pad), jnp.pad(sin_b, pad)


def _rope(x, cos, sin_a, sin_b, nf):
    w = x.shape[-1]
    return x * cos + pltpu.roll(x, w - nf, 1) * sin_a + pltpu.roll(x, nf, 1) * sin_b


DFT_SPLIT = 32


def _dft_matrices(L):
    s = jnp.arange(L, dtype=jnp.int32)[None, :]

    def trig(k):
        ang = ((k[:, None] * s) % (2 * L)).astype(F32) * (math.pi / L)
        return jnp.cos(ang), jnp.sin(ang)

    c1, s1 = trig(jnp.arange(0, L, DFT_SPLIT, dtype=jnp.int32))
    c0, s0 = trig(jnp.arange(DFT_SPLIT, dtype=jnp.int32))
    cos = (c1[:, None, :] * c0[None] - s1[:, None, :] * s0[None]).reshape(L, L)
    sin = (s1[:, None, :] * c0[None] + c1[:, None, :] * s0[None]).reshape(L, L)
    nyq = (1 - 2 * (jnp.arange(L, dtype=jnp.int32) % 2)).astype(F32)
    k = jnp.arange(L, dtype=jnp.int32)[:, None]
    msin = jnp.where(k == 0, nyq[None, :], -sin)
    msin_t = jnp.where(s == 0, nyq[:, None], -sin)
    return cos.astype(BF16), msin.astype(BF16), msin_t.astype(BF16)


def _filter_kernel(L, z_ref, w1_ref, b1_ref, w2_ref, b2_ref, fr_ref, w3f_ref, w3b_ref, dl_ref, cos_ref, msin_ref,
                   ka_ref, kb_ref, kc_ref, h_scr):
    z = z_ref[...]

    @pl.when((pl.program_id(0) == 0) & (pl.program_id(1) == 0))
    def _():
        fr = fr_ref[...]
        h1 = jnp.sin(fr * (jnp.dot(z, w1_ref[...], precision=HIGHEST, preferred_element_type=F32) + b1_ref[...]))
        h_scr[...] = jnp.sin(
            fr * (jnp.dot(h1, w2_ref[...], precision=HIGHEST, preferred_element_type=F32) + b2_ref[...]))

    h = h_scr[...]
    decay = jnp.exp(-z[:, 0:1] * dl_ref[...])
    hf = jnp.dot(h, w3f_ref[...], precision=HIGHEST, preferred_element_type=F32) * decay
    hb = jnp.dot(h, w3b_ref[...], precision=HIGHEST, preferred_element_type=F32) * decay
    row = lax.broadcasted_iota(jnp.int32, hf.shape, 0)
    row0 = row == 0
    hb = jnp.where(row0, 0.0, hb)
    even = hf + hb
    re = _dot(cos_ref[...], even.astype(BF16))
    im = _dot(msin_ref[...], (hf - hb).astype(BF16))
    nyq = jnp.sum(jnp.where(row % 2 == 0, even, -even), axis=0, keepdims=True)
    sc = jnp.where(row0, 0.5 / L, 1.0 / L)
    ka_ref[...] = re * sc
    kb_ref[...] = jnp.where(row0, 0.0, im) * sc
    kc_ref[...] = jnp.where(row0, nyq, re) * sc


def _hyena_spectra(L, tc, dft, f_w1, f_b1, f_w2, f_b2, f_w3, f_freq):
    t = jnp.linspace(0.0, 1.0, L, dtype=F32)[:, None]
    bands = (POS_EMB - 1) // 2
    w = 2.0 * math.pi * jnp.arange(L, dtype=F32)[:, None] / L
    fb = jnp.linspace(1e-4, bands - 1, bands, dtype=F32)[None, :]
    z = jnp.concatenate([t, jnp.cos(fb * w), -jnp.sin(fb * w)], axis=-1)
    z = jnp.pad(z, ((0, 0), (0, 128 - POS_EMB)))
    w1 = jnp.pad(f_w1, ((0, 128 - POS_EMB), (0, 0)))
    deltas = jnp.abs(jnp.linspace(math.log(HY_TARGET) / HY_SLOW_DECAY,
                                  math.log(HY_TARGET) / HY_FAST_DECAY, HY_W, dtype=F32))[None, :]
    nct = HY_W // tc
    fo = FILTER_ORDER
    row = lambda a: a.reshape(1, fo)
    kshape = jax.ShapeDtypeStruct((HY_ORDER, L, HY_W), F32)
    kspec = pl.BlockSpec((None, L, tc), lambda n, c: (n, 0, c))
    const = lambda shape: pl.BlockSpec(shape, lambda n, c: (0, 0))
    return pl.pallas_call(
        functools.partial(_filter_kernel, L),
        out_shape=(kshape, kshape, kshape),
        grid=(HY_ORDER, nct),
        in_specs=[
            const((L, 128)), const((128, fo)), const((1, fo)), const((fo, fo)), const((1, fo)), const((1, fo)),
            pl.BlockSpec((fo, tc), lambda n, c: (0, 2 * n * nct + c)),
            pl.BlockSpec((fo, tc), lambda n, c: (0, (2 * n + 1) * nct + c)),
            pl.BlockSpec((1, tc), lambda n, c: (0, c)),
            const((L, L)), const((L, L)),
        ],
        out_specs=(kspec, kspec, kspec),
        scratch_shapes=[pltpu.VMEM((L, fo), F32)],
        compiler_params=_cparams("arbitrary", "arbitrary"),
        name="hyena_filter",
    )(z, w1, row(f_b1), f_w2, row(f_b2), row(f_freq), f_w3, f_w3, deltas, dft[0], dft[1])


def _hyena_kernel(L, nseq, uv_ref, ug0_ref, ug1_ref, cwv_ref, cwg0_ref, cwg1_ref, cbv_ref, cbg0_ref, cbg1_ref,
                  hb_ref, ka_ref, kb_ref, kc_ref, cos_ref, msin_ref, msint_ref, o_ref):
    row = lax.broadcasted_iota(jnp.int32, (L, uv_ref.shape[1]), 0)
    first, last = row == 0, row == L - 1
    seqs = [slice(i * L, (i + 1) * L) for i in range(nseq)]

    def short_conv(u_ref, rows, w_ref, b_ref):
        u = u_ref[rows, :].astype(F32)
        prev = jnp.where(first, 0.0, pltpu.roll(u, 1, 0))
        nxt = jnp.where(last, 0.0, pltpu.roll(u, L - 1, 0))
        return b_ref[...] + prev * w_ref[0:1, :] + u * w_ref[1:2, :] + nxt * w_ref[2:3, :]

    z = [short_conv(uv_ref, r, cwv_ref, cbv_ref) for r in seqs]
    gates = [(short_conv(ug0_ref, r, cwg0_ref, cbg0_ref), short_conv(ug1_ref, r, cwg1_ref, cbg1_ref))
             for r in seqs]
    for n in range(HY_ORDER):
        zb = [zi.astype(BF16) for zi in z]
        zre = [_dot(cos_ref[...], b) for b in zb]
        zim = [_dot(msin_ref[...], b) for b in zb]
        ka, kb, kc = ka_ref[n], kb_ref[n], kc_ref[n]
        yre = [(re * ka - im * kb).astype(BF16) for re, im in zip(zre, zim)]
        yim = [(re * kb + im * kc).astype(BF16) for re, im in zip(zre, zim)]
        conv = [_dot(cos_ref[...], a) + _dot(msint_ref[...], b) for a, b in zip(yre, yim)]
        z = [g[n] * (cv + hb_ref[n:n + 1, :] * zi) for g, cv, zi in zip(gates, conv, z)]
    for r, zi in zip(seqs, z):
        o_ref[r, :] = zi.astype(o_ref.dtype)


def _hyena(grp, p, tc, nseq, spectra, dft, conv_w, conv_b, h_bias):
    L = grp.L
    nct = HY_W // tc
    ka, kb, kc = spectra
    u_spec = lambda j: pl.BlockSpec((nseq * L, tc), lambda c, b: (b, j * nct + c))
    cw_spec = lambda j: pl.BlockSpec((SHORT_CONV, tc), lambda c, b: (0, j * nct + c))
    cb_spec = lambda j: pl.BlockSpec((1, tc), lambda c, b: (0, j * nct + c))
    k_spec = pl.BlockSpec((HY_ORDER, L, tc), lambda c, b: (0, 0, c))
    m_spec = pl.BlockSpec((L, L), lambda c, b: (0, 0))
    cb = conv_b.reshape(1, -1)
    return pl.pallas_call(
        functools.partial(_hyena_kernel, L, nseq),
        out_shape=jax.ShapeDtypeStruct((grp.rows, HY_W), BF16),
        grid=(nct, grp.nb // nseq),
        in_specs=[
            u_spec(0), u_spec(1), u_spec(2), cw_spec(0), cw_spec(1), cw_spec(2),
            cb_spec(0), cb_spec(1), cb_spec(2),
            pl.BlockSpec((HY_ORDER, tc), lambda c, b: (0, c)),
            k_spec, k_spec, k_spec, m_spec, m_spec, m_spec,
        ],
        out_specs=pl.BlockSpec((nseq * L, tc), lambda c, b: (b, c)),
        compiler_params=_cparams("parallel", "arbitrary"),
        name="hyena",
    )(p, p, p, conv_w, conv_w, conv_w, cb, cb, cb, h_bias, ka, kb, kc, *dft)


def _gqa_kernel(L, window, has_ctx, has_rope, emit_kv, *refs):
    it = iter(refs)
    q_ref, k_ref, v_ref, sink_ref = next(it), next(it), next(it), next(it)
    if has_ctx:
        kc_ref, vc_ref = next(it), next(it)
    if has_rope:
        cos_ref, sa_ref, sb_ref = next(it), next(it), next(it)
    o_ref = next(it)
    if emit_kv:
        kn_ref, vn_ref = next(it), next(it)

    qb = HEAD_DIM
    nf = HEAD_DIM // 4
    gw = ATT_GROUP * HEAD_DIM
    hps = k_ref.shape[1] // HEAD_DIM
    sink_row = sink_ref[...]
    sink_lane = lax.broadcasted_iota(jnp.int32, sink_row.shape, 1)
    band_bias = {}

    def bias_for(lo, hi, i):
        key = (lo - i * qb, hi - lo)
        if key not in band_bias:
            shape = (ATT_GROUP * qb, hi - lo)
            rel = key[0] + lax.broadcasted_iota(jnp.int32, shape, 1) - lax.broadcasted_iota(jnp.int32, shape, 0) % qb
            band_bias[key] = jnp.where(jnp.abs(rel) <= window, 0.0, -1e30)
        return band_bias[key]

    for hk in range(hps):
        head = pl.program_id(1) * hps + hk
        k = k_ref[:, hk * HEAD_DIM:(hk + 1) * HEAD_DIM]
        v = v_ref[:, hk * HEAD_DIM:(hk + 1) * HEAD_DIM]
        if emit_kv:
            kn_ref[hk] = k
            vn_ref[hk] = v
        if has_rope:
            k = _rope(k, cos_ref[...], sa_ref[...], sb_ref[...], nf)
        k = k.astype(BF16)
        v = jnp.concatenate([v.astype(BF16), jnp.ones((L, HEAD_DIM), BF16)], axis=1)
        if has_ctx:
            kc = kc_ref[hk].astype(BF16)
            vc = jnp.concatenate([vc_ref[hk].astype(BF16), jnp.ones((PAST_LEN, HEAD_DIM), BF16)], axis=1)
        sinks = [jnp.sum(jnp.where(sink_lane == head * ATT_GROUP + g, sink_row, 0.0), axis=1, keepdims=True)
                 for g in range(ATT_GROUP)]
        sink = jnp.concatenate([jnp.broadcast_to(s, (qb, 1)) for s in sinks], axis=0)

        for i in range(L // qb):
            rows = slice(i * qb, (i + 1) * qb)
            qs = []
            for g in range(ATT_GROUP):
                qg = q_ref[rows, hk * gw + g * HEAD_DIM:hk * gw + (g + 1) * HEAD_DIM].astype(F32)
                if has_rope:
                    qg = _rope(qg, cos_ref[rows, :], sa_ref[rows, :], sb_ref[rows, :], nf)
                qs.append((qg * ATT_SCALE).astype(BF16))
            q = jnp.concatenate(qs, axis=0)
            if window is None:
                lo, hi = 0, L
            else:
                lo, hi = max(0, (i - 1) * qb), min(L, (i + 2) * qb)
            s = _dot_nt(q, k[lo:hi])
            if window is not None:
                s = s + bias_for(lo, hi, i)
            m = jnp.maximum(jnp.max(s, axis=-1, keepdims=True), sink)
            if has_ctx:
                sc = _dot_nt(q, kc)
                m = jnp.maximum(m, jnp.max(sc, axis=-1, keepdims=True))
            oa = _dot(jnp.exp(s - m).astype(BF16), v[lo:hi])
            if has_ctx:
                oa = oa + _dot(jnp.exp(sc - m).astype(BF16), vc)
            o = oa[:, :HEAD_DIM] / (oa[:, HEAD_DIM:] + jnp.exp(sink - m))
            for g in range(ATT_GROUP):
                o_ref[rows, hk * gw + g * HEAD_DIM:hk * gw + (g + 1) * HEAD_DIM] = (
                    o[g * qb:(g + 1) * qb].astype(o_ref.dtype))


def _gqa(grp, p, kv, sink, hps, window=None, ctx=None, rope=None, emit_kv=False):
    L = grp.L
    qw = hps * ATT_GROUP * HEAD_DIM
    kw = hps * HEAD_DIM
    nh = ATT_KV_HEADS // hps
    in_specs = [
        pl.BlockSpec((L, qw), lambda b, h: (b, 3 * HY_W // qw + h)),
        pl.BlockSpec((L, kw), lambda b, h: (b, h)),
        pl.BlockSpec((L, kw), lambda b, h: (b, nh + h)),
        pl.BlockSpec((1, ATT_HEADS), lambda b, h: (0, 0)),
    ]
    args = [p, kv, kv, sink.reshape(1, ATT_HEADS)]
    if ctx is not None:
        kc, vc, e = ctx
        spec = pl.BlockSpec((None, None, hps, PAST_LEN, HEAD_DIM), lambda b, h: (b, e, h, 0, 0))
        in_specs += [spec, spec]
        args += [kc, vc]
    if rope is not None:
        in_specs += [pl.BlockSpec((L, HEAD_DIM), lambda b, h: (0, 0))] * 3
        args += list(rope)
    out_shape = [jax.ShapeDtypeStruct((grp.rows, ATT_HEADS * HEAD_DIM), BF16)]
    out_specs = [pl.BlockSpec((L, qw), lambda b, h: (b, h))]
    if emit_kv:
        kv_shape = jax.ShapeDtypeStruct((grp.nb, 1, ATT_KV_HEADS, L, HEAD_DIM), F32)
        kv_spec = pl.BlockSpec((None, None, hps, L, HEAD_DIM), lambda b, h: (b, 0, h, 0, 0))
        out_shape += [kv_shape, kv_shape]
        out_specs += [kv_spec, kv_spec]
    return pl.pallas_call(
        functools.partial(_gqa_kernel, L, window, ctx is not None, rope is not None, emit_kv),
        out_shape=tuple(out_shape),
        grid=(grp.nb, nh),
        in_specs=in_specs,
        out_specs=tuple(out_specs),
        compiler_params=_cparams("parallel", "parallel"),
        name="gqa",
    )(*args)


def _chunk_cumsum(x, reverse):
    n = x.shape[0]
    pos = lax.broadcasted_iota(jnp.int32, x.shape, 0) % CHUNK
    s = 1
    while s < CHUNK:
        if reverse:
            x = x + jnp.where(pos < CHUNK - s, pltpu.roll(x, n - s, 0), 0.0)
        else:
            x = x + jnp.where(pos >= s, pltpu.roll(x, s, 0), 0.0)
        s *= 2
    return x


def _hgrn_kernel(L, has_state, emit_state, *refs):
    it = iter(refs)
    q_ref, ff_ref, fb_ref, i_ref, g_ref, lb_ref, norm_ref = (next(it) for _ in range(7))
    if has_state:
        s0_ref = next(it)
    o_ref = next(it)
    if emit_state:
        s_ref = next(it)
    acc_scr = next(it)

    a = lb_ref[...]
    e = jnp.exp(a - jnp.max(a, axis=0, keepdims=True))
    lb_all = e[1] / (e[0] + e[1])
    ci = lax.broadcasted_iota(jnp.int32, (CHUNK, CHUNK), 0)
    cj = lax.broadcasted_iota(jnp.int32, (CHUNK, CHUNK), 1)
    n_chunks = L // CHUNK
    chunks = [slice(n * CHUNK, (n + 1) * CHUNK) for n in range(n_chunks)]

    for hh in range(q_ref.shape[1] // HEAD_DIM):
        hc = slice(hh * HEAD_DIM, (hh + 1) * HEAD_DIM)
        q = _silu(q_ref[:, hc].astype(F32))
        for d, fz_ref in enumerate((ff_ref, fb_ref)):
            lbd = lb_all[d:d + 1, hc]
            keep = (cj <= ci) if d == 0 else (cj >= ci)
            qds, atts, decay, own = [], [], [], []
            for rows in chunks:
                f = lbd + (1.0 - lbd) * jax.nn.sigmoid(fz_ref[rows, hc].astype(F32))
                b = _chunk_cumsum(jnp.log(f), reverse=(d == 1))
                qd = (q[rows] * jnp.exp(b)).astype(BF16)
                kd32 = (1.0 - f) * jnp.exp(-b)
                dc = jnp.exp(b[CHUNK - 1:CHUNK] if d == 0 else b[0:1])
                qds.append(qd)
                decay.append(dc)
                atts.append(jnp.where(keep, _dot_nt(qd, kd32.astype(BF16)), 0.0).astype(BF16))
                own.append(_dot_tn(i_ref[rows, hc], (kd32 * dc).astype(BF16)))
            st = s0_ref[d, hh].T if has_state else jnp.zeros((HEAD_DIM, HEAD_DIM), F32)
            entering = [None] * n_chunks
            for n in (range(n_chunks) if d == 0 else range(n_chunks - 1, -1, -1)):
                entering[n] = st.astype(BF16)
                st = st * decay[n] + own[n]
            if emit_state:
                s_ref[d, hh] = st.T
            for n, rows in enumerate(chunks):
                o = _dot(atts[n], i_ref[rows, hc]) + _dot_nt(qds[n], entering[n])
                if d == 0:
                    acc_scr[rows, :] = o
                else:
                    acc_scr[rows, :] += o
        o = _rms(acc_scr[...]) * norm_ref[...] * _silu(g_ref[:, hc].astype(F32))
        o_ref[:, hc] = o.astype(o_ref.dtype)


def _hgrn(grp, p, hg_lb, norm_g, hps, state=None, emit_state=False):
    L = grp.L
    hw = hps * HEAD_DIM
    nh = HG_HEADS // hps
    col = lambda j: pl.BlockSpec((L, hw), lambda b, h: (b, j * nh + h))
    in_specs = [col(0), col(1), col(2), col(3), col(4),
                pl.BlockSpec((hg_lb.shape[0], 2, hw), lambda b, h: (0, 0, h)),
                pl.BlockSpec((1, HEAD_DIM), lambda b, h: (0, 0))]
    args = [p, p, p, p, p, hg_lb, norm_g.reshape(1, HEAD_DIM)]
    st_spec = lambda o: pl.BlockSpec((None, None, 2, hps, HEAD_DIM, HEAD_DIM), lambda b, h: (b, o, 0, h, 0, 0))
    if state is not None:
        s0, o = state
        in_specs.append(st_spec(o))
        args.append(s0)
    out_shape = [jax.ShapeDtypeStruct((grp.rows, HG_W), BF16)]
    out_specs = [pl.BlockSpec((L, hw), lambda b, h: (b, h))]
    if emit_state:
        out_shape.append(jax.ShapeDtypeStruct((grp.nb, 1, 2, HG_HEADS, HEAD_DIM, HEAD_DIM), F32))
        out_specs.append(st_spec(0))
    return pl.pallas_call(
        functools.partial(_hgrn_kernel, L, state is not None, emit_state),
        out_shape=tuple(out_shape),
        grid=(grp.nb, nh),
        in_specs=in_specs,
        out_specs=tuple(out_specs),
        scratch_shapes=[pltpu.VMEM((L, HEAD_DIM), F32)],
        compiler_params=_cparams("parallel", "parallel"),
        name="hgrn",
    )(*args)


MLA_QW = 256
KR_W = 128
OD_TAIL = 1024


def _mla_prep_kernel(has_rope, emit_kr, *refs):
    it = iter(refs)
    ql_ref, kvl_ref, kr_ref, qn_ref, kvn_ref, wq_ref = (next(it) for _ in range(6))
    if has_rope:
        kc_ref, ksa_ref, ksb_ref = (next(it) for _ in range(3))
    q_ref, ckv_ref, kro_ref = next(it), next(it), next(it)
    if emit_kr:
        krn_ref = next(it)
        krn_ref[...] = kr_ref[:, :ROPE]

    nf = ROPE // 4
    qn = (_rms(ql_ref[...]) * qn_ref[...]).astype(BF16)
    q = _dot(qn, wq_ref[...])
    for h in range(MLA_HEADS):
        nope = slice(h * MLA_QW, h * MLA_QW + NOPE)
        rot = slice(h * MLA_QW + NOPE, (h + 1) * MLA_QW)
        q_ref[:, nope] = q[:, nope].astype(q_ref.dtype)
        qr = q[:, rot]
        if has_rope:
            qr = _rope(qr, kc_ref[...], ksa_ref[...], ksb_ref[...], nf)
        q_ref[:, rot] = qr.astype(q_ref.dtype)
    ckv_ref[...] = _rms(kvl_ref[...]) * kvn_ref[...]
    kr = kr_ref[...]
    if has_rope:
        kr = _rope(kr, kc_ref[...], ksa_ref[...], ksb_ref[...], nf)
    kro_ref[...] = kr.astype(kro_ref.dtype)


def _mla_prep(grp, tail, q_norm, kv_norm, wq, rope=None, emit_kr=False):
    tm = min(512, grp.L)
    per = grp.L // tm
    n_rows = grp.rows
    in_specs = [
        pl.BlockSpec((tm, Q_LORA), lambda i: (i, 0)),
        pl.BlockSpec((tm, KV_LORA), lambda i: (i, Q_LORA // KV_LORA)),
        pl.BlockSpec((tm, KR_W), lambda i: (i, (Q_LORA + KV_LORA) // KR_W)),
        pl.BlockSpec((1, Q_LORA), lambda i: (0, 0)),
        pl.BlockSpec((1, KV_LORA), lambda i: (0, 0)),
        pl.BlockSpec((Q_LORA, MLA_HEADS * MLA_QW), lambda i: (0, 0)),
    ]
    args = [tail, tail, tail, q_norm.reshape(1, Q_LORA), kv_norm.reshape(1, KV_LORA), wq]
    if rope is not None:
        in_specs += [pl.BlockSpec((tm, KR_W), lambda i: (i % per, 0))] * 3
        args += list(rope)
    out_shape = [jax.ShapeDtypeStruct((n_rows, MLA_HEADS * MLA_QW), BF16),
                 jax.ShapeDtypeStruct((n_rows, KV_LORA), F32),
                 jax.ShapeDtypeStruct((n_rows, KR_W), BF16)]
    out_specs = [pl.BlockSpec((tm, MLA_HEADS * MLA_QW), lambda i: (i, 0)),
                 pl.BlockSpec((tm, KV_LORA), lambda i: (i, 0)),
                 pl.BlockSpec((tm, KR_W), lambda i: (i, 0))]
    if emit_kr:
        out_shape.append(jax.ShapeDtypeStruct((n_rows, ROPE), F32))
        out_specs.append(pl.BlockSpec((tm, ROPE), lambda i: (i, 0)))
    return pl.pallas_call(
        functools.partial(_mla_prep_kernel, rope is not None, emit_kr),
        out_shape=tuple(out_shape),
        grid=(n_rows // tm,),
        in_specs=in_specs,
        out_specs=tuple(out_specs),
        compiler_params=_cparams("parallel"),
        name="mla_prep",
    )(*args)


def _mla_attn_kernel(L, has_ctx, *refs):
    it = iter(refs)
    q_ref, ckv_ref, kr_ref, wkv_ref = (next(it) for _ in range(4))
    if has_ctx:
        cckv_ref, ckr_ref = next(it), next(it)
    o_ref = next(it)

    ckv = ckv_ref[...].astype(BF16)
    kr = kr_ref[...]
    if has_ctx:
        ckv = jnp.concatenate([ckv, cckv_ref[...].astype(BF16)], axis=0)
        kr = jnp.concatenate([kr, ckr_ref[...].astype(BF16)], axis=0)
    ones = jnp.ones((ckv.shape[0], V_DIM), BF16)
    qb = min(L, 256)
    for hh in range(q_ref.shape[1] // MLA_QW):
        qc = slice(hh * MLA_QW, (hh + 1) * MLA_QW)
        oc = slice(hh * V_DIM, (hh + 1) * V_DIM)
        kv = _dot(ckv, wkv_ref[:, qc])
        kh = jnp.concatenate([kv[:, :NOPE].astype(BF16), kr], axis=1)
        vh = jnp.concatenate([kv[:, NOPE:].astype(BF16), ones], axis=1)
        for i in range(L // qb):
            rows = slice(i * qb, (i + 1) * qb)
            s = _dot_nt(q_ref[rows, qc], kh) * MLA_SCALE
            m = jnp.max(s, axis=-1, keepdims=True)
            oa = _dot(jnp.exp(s - m).astype(BF16), vh)
            o_ref[rows, oc] = (oa[:, :V_DIM] / oa[:, V_DIM:]).astype(o_ref.dtype)


def _mla_attn(grp, q, ckv, kr, wkv, hps, ctx=None):
    L = grp.L
    in_specs = [
        pl.BlockSpec((L, hps * MLA_QW), lambda b, h: (b, h)),
        pl.BlockSpec((L, KV_LORA), lambda b, h: (b, 0)),
        pl.BlockSpec((L, KR_W), lambda b, h: (b, 0)),
        pl.BlockSpec((KV_LORA, hps * (NOPE + V_DIM)), lambda b, h: (0, h)),
    ]
    args = [q, ckv, kr, wkv]
    if ctx is not None:
        cckv, ckr, o = ctx
        in_specs += [pl.BlockSpec((None, None, PAST_LEN, KV_LORA), lambda b, h: (b, o, 0, 0)),
                     pl.BlockSpec((None, None, PAST_LEN, KR_W), lambda b, h: (b, o, 0, 0))]
        args += [cckv, ckr]
    return pl.pallas_call(
        functools.partial(_mla_attn_kernel, L, ctx is not None),
        out_shape=jax.ShapeDtypeStruct((grp.rows, MLA_HEADS * V_DIM), BF16),
        grid=(grp.nb, MLA_HEADS // hps),
        in_specs=in_specs,
        out_specs=pl.BlockSpec((L, hps * V_DIM), lambda b, h: (b, h)),
        compiler_params=_cparams("parallel", "parallel"),
        name="mla_attn",
    )(*args)


def kernel(x_prompt, x_sample, c, c_ctx, cache_attn_k, cache_attn_v, cache_mla_ckv, cache_mla_krope, state_hgrn, mod_w, mod_b, norm_g, ffn_wg, ffn_wu, ffn_wd, ev_w_in, ev_w_out, hy_conv_w, hy_conv_b, hy_f_w1, hy_f_b1, hy_f_w2, hy_f_b2, hy_f_w3, hy_f_freq, hy_bias, attn_sink, od_w_in, od_w_out, hg_lb, hg_norm, mla_q_norm, mla_w_qb, mla_kv_norm, mla_w_kvb):
    depth = mod_w.shape[0]
    groups = (PROMPT, LATENT)
    xs = [x_prompt.reshape(PROMPT.rows, D_MODEL), x_sample.reshape(LATENT.rows, D_MODEL)]
    cvec = jnp.concatenate([c_ctx[None, :], c, jnp.zeros((MOD_ROWS - 1 - DEC_BATCH, D_MODEL), F32)], axis=0)
    mods_all = _modulation(cvec, mod_w, mod_b)

    wg, wu, wd = ffn_wg.astype(BF16), ffn_wu.astype(BF16), ffn_wd.astype(BF16)
    hy_tc = {SEQ: 512, DEC_SEQ: 256}
    hy_nseq = {SEQ: 4, DEC_SEQ: 2}

    new_k = new_v = new_ckv = new_kr = new_s = None
    for l in range(depth):
        mods = mods_all[l]
        xs = [_ffn(grp, x, mods, 0, norm_g[l, 0], norm_g[l, 1], wg, wu, wd, l, 0) for grp, x in zip(groups, xs)]
        if l % 2 == 0:
            e = l // 2
            w_in = ev_w_in[e][:, :EV_IN - TN_EVEN].astype(BF16)
            w_in_last = ev_w_in[e][:, EV_IN - TN_EVEN:].astype(BF16)
            w_out = ev_w_out[e].astype(BF16)
            kv_cols = 2 * ATT_KV_HEADS * HEAD_DIM
            mix = []
            for grp, x in zip(groups, xs):
                p, kv = _inproj(grp, x, mods, norm_g[l, 2], w_in, w_in_last, kv_cols)
                dft = _dft_matrices(grp.L)
                tc = hy_tc[grp.L]
                spectra = _hyena_spectra(grp.L, tc, dft, hy_f_w1[e], hy_f_b1[e], hy_f_w2[e], hy_f_b2[e],
                                         hy_f_w3[e], hy_f_freq[e])
                hy = _hyena(grp, p, tc, hy_nseq[grp.L], spectra, dft, hy_conv_w[e], hy_conv_b[e], hy_bias[e])
                if grp.latent:
                    rope = _rope_tables(grp.L, HEAD_DIM, 0, HEAD_DIM)
                    (att,) = _gqa(grp, p, kv, attn_sink[e], 1, window=WINDOW,
                                  ctx=(cache_attn_k, cache_attn_v, e), rope=rope)
                else:
                    att, new_k, new_v = _gqa(grp, p, kv, attn_sink[e], 2, emit_kv=True)
                mix.append((hy, att))
        else:
            o = l // 2
            n_main = OD_IN_PAD - TN_ODD
            w_in = od_w_in[o][:, :n_main].astype(BF16)
            w_in_last = jnp.pad(od_w_in[o][:, n_main:].astype(BF16), ((0, 0), (0, OD_IN_PAD - OD_IN)))
            w_out = od_w_out[o].astype(BF16)
            wq = mla_w_qb[o].reshape(Q_LORA, MLA_HEADS, NOPE + ROPE)
            wq = jnp.pad(wq, ((0, 0), (0, 0), (0, MLA_QW - NOPE - ROPE))).reshape(Q_LORA, -1).astype(BF16)
            wkv = mla_w_kvb[o].astype(BF16)
            mix = []
            for grp, x in zip(groups, xs):
                p, tail = _inproj(grp, x, mods, norm_g[l, 2], w_in, w_in_last, OD_TAIL)
                if grp.latent:
                    (hg,) = _hgrn(grp, p, hg_lb, hg_norm[o], 2, state=(state_hgrn, o))
                    rope = _rope_tables(grp.L, ROPE, 0, KR_W)
                    q, ckv, kr = _mla_prep(grp, tail, mla_q_norm[o], mla_kv_norm[o], wq, rope=rope)
                    ckr = jnp.pad(cache_mla_krope, ((0, 0), (0, 0), (0, 0), (0, KR_W - ROPE)))
                    att = _mla_attn(grp, q, ckv, kr, wkv, 4, ctx=(cache_mla_ckv, ckr, o))
                else:
                    hg, new_s = _hgrn(grp, p, hg_lb, hg_norm[o], 2, emit_state=True)
                    q, ckv, kr, kr_raw = _mla_prep(grp, tail, mla_q_norm[o], mla_kv_norm[o], wq, emit_kr=True)
                    att = _mla_attn(grp, q, ckv, kr, wkv, 4)
                    new_ckv = ckv.reshape(BATCH, 1, SEQ, KV_LORA)
                    new_kr = kr_raw.reshape(BATCH, 1, SEQ, ROPE)
                mix.append((hg, att))
        xs = [_outproj(grp, a, b, w_out, x, mods, norm_g[l, 3])
              for grp, x, (a, b) in zip(groups, xs, mix)]
        xs = [_ffn(grp, x, mods, 2, norm_g[l, 4], norm_g[l, 5], wg, wu, wd, l, 1) for grp, x in zip(groups, xs)]

    y_prompt = xs[0].reshape(BATCH, SEQ, D_MODEL)
    y_sample = xs[1].reshape(DEC_BATCH, DEC_SEQ, D_MODEL)
    return (y_prompt, y_sample, new_k, new_v, new_ckv, new_kr, new_s)
```

```python
import functools
import math
from typing import NamedTuple

import jax
import jax.numpy as jnp
from jax import lax
from jax.experimental import pallas as pl
from jax.experimental.pallas import tpu as pltpu

D_MODEL = 2048
BATCH = 16
SEQ = 256
DEC_BATCH = 8
DEC_SEQ = 1024
PAST_LEN = 512
GRID_W = 64
HEAD_DIM = 128
HY_W = 1024
HY_ORDER = 2
SHORT_CONV = 3
POS_EMB = 33
FILTER_ORDER = 64
HY_FAST_DECAY = 0.3
HY_SLOW_DECAY = 1.5
HY_TARGET = 1e-2
ATT_HEADS = 8
ATT_KV_HEADS = 2
ATT_GROUP = 4
WINDOW = 128
EV_IN = 3 * HY_W + (ATT_HEADS + 2 * ATT_KV_HEADS) * HEAD_DIM
HG_W = 1024
HG_HEADS = 8
CHUNK = 64
Q_LORA = 512
KV_LORA = 256
NOPE = 128
ROPE = 64
V_DIM = 128
MLA_HEADS = 8
OD_IN = 5 * HG_W + Q_LORA + KV_LORA + ROPE
OD_IN_PAD = 6144
D_FF = 5632
MACARON_W = 0.5
N_MOD = 9
ROPE_BASE = 10000.0
EPS = 1e-6
ATT_SCALE = HEAD_DIM ** -0.5
MLA_SCALE = (NOPE + ROPE) ** -0.5

MOD_ROWS = 16

V7X_VMEM_LIMIT = 56 * 1024 * 1024
TM = 1024
TM_FFN = 1024
BIG_TILE_VMEM_LIMIT = 60 * 1024 * 1024
TM_IN = 1024
RC = 512
TF = 512
TN_EVEN = 1536
TN_ODD = 1024
MOD_TN = 2048

BF16 = jnp.bfloat16
F32 = jnp.float32
HIGHEST = lax.Precision.HIGHEST


class Group(NamedTuple):
    nb: int
    L: int
    latent: bool

    @property
    def rows(self):
        return self.nb * self.L

    def mod_row(self, i, tm):
        return 1 + i // (self.L // tm) if self.latent else 0


PROMPT = Group(BATCH, SEQ, False)
LATENT = Group(DEC_BATCH, DEC_SEQ, True)


def _cparams(*sem):
    return pltpu.CompilerParams(dimension_semantics=sem, vmem_limit_bytes=V7X_VMEM_LIMIT)


def _rms(x):
    return x * lax.rsqrt(jnp.mean(x * x, axis=-1, keepdims=True) + EPS)


def _silu(x):
    return x * jax.nn.sigmoid(x)


def _dot(a, b):
    return jnp.dot(a, b, preferred_element_type=F32)


def _dot_nt(a, b):
    return lax.dot_general(a, b, (((1,), (1,)), ((), ())), preferred_element_type=F32)


def _dot_tn(a, b):
    return lax.dot_general(a, b, (((0,), (0,)), ((), ())), preferred_element_type=F32)


def _row_chunks(n_rows, body, rc=RC):
    def step(c, carry):
        body(pl.ds(pl.multiple_of(c * rc, rc), rc))
        return carry
    lax.fori_loop(0, n_rows // rc, step, 0, unroll=True)


def _modulated_norm(x_ref, h_scr, gain, shift):
    def body(rows):
        h_scr[rows, :] = (_rms(x_ref[rows, :]) * gain + shift).astype(BF16)
    _row_chunks(x_ref.shape[0], body)


def _mod_kernel(c_ref, w_ref, b_ref, o_ref):
    s = _silu(c_ref[...]).astype(BF16)
    o_ref[...] = _dot(s, w_ref[...].astype(BF16)) + b_ref[...]


def _modulation(cvec, mod_w, mod_b):
    depth = mod_w.shape[0]
    n = N_MOD * D_MODEL
    out = pl.pallas_call(
        _mod_kernel,
        out_shape=jax.ShapeDtypeStruct((depth, MOD_ROWS, n), F32),
        grid=(depth, n // MOD_TN),
        in_specs=[
            pl.BlockSpec((MOD_ROWS, D_MODEL), lambda l, j: (0, 0)),
            pl.BlockSpec((None, D_MODEL, MOD_TN), lambda l, j: (l, 0, j)),
            pl.BlockSpec((None, 1, MOD_TN), lambda l, j: (l, 0, j)),
        ],
        out_specs=pl.BlockSpec((None, MOD_ROWS, MOD_TN), lambda l, j: (l, 0, j)),
        compiler_params=_cparams("parallel", "parallel"),
        name="modulation",
    )(cvec, mod_w, mod_b.reshape(depth, 1, n))
    return out.reshape(depth, MOD_ROWS, N_MOD, D_MODEL)


def _ffn_kernel(j, x_ref, mod_ref, gpre_ref, gpost_ref, wg_ref, wu_ref, wd_ref, o_ref, h_scr):
    f = pl.program_id(1)

    @pl.when(f == 0)
    def _():
        gain = gpre_ref[...] * (1.0 + mod_ref[3 * j + 1:3 * j + 2, :])
        _modulated_norm(x_ref, h_scr, gain, mod_ref[3 * j:3 * j + 1, :])

    def partial_down(rows):
        h = h_scr[rows, :]
        a = (_silu(_dot(h, wg_ref[...])) * _dot(h, wu_ref[...])).astype(BF16)
        return _dot(a, wd_ref[...])

    @pl.when(f == 0)
    def _():
        def first(rows):
            o_ref[rows, :] = partial_down(rows)
        _row_chunks(o_ref.shape[0], first)

    last = pl.num_programs(1) - 1

    @pl.when((f > 0) & (f < last))
    def _():
        def accumulate(rows):
            o_ref[rows, :] += partial_down(rows)
        _row_chunks(o_ref.shape[0], accumulate)

    @pl.when(f == last)
    def _():
        gain = (MACARON_W * mod_ref[3 * j + 2:3 * j + 3, :]) * gpost_ref[...]

        def finish(rows):
            acc = o_ref[rows, :] + partial_down(rows)
            o_ref[rows, :] = x_ref[rows, :] + _rms(acc) * gain
        _row_chunks(o_ref.shape[0], finish, RC // 2)


def _ffn(grp, x, mods, j, g_pre, g_post, wg, wu, wd, l, s):
    return pl.pallas_call(
        functools.partial(_ffn_kernel, j),
        out_shape=jax.ShapeDtypeStruct((grp.rows, D_MODEL), F32),
        grid=(grp.rows // TM_FFN, D_FF // TF),
        in_specs=[
            pl.BlockSpec((TM_FFN, D_MODEL), lambda i, f: (i, 0)),
            pl.BlockSpec((None, N_MOD, D_MODEL), lambda i, f: (grp.mod_row(i, TM_FFN), 0, 0)),
            pl.BlockSpec((1, D_MODEL), lambda i, f: (0, 0)),
            pl.BlockSpec((1, D_MODEL), lambda i, f: (0, 0)),
            pl.BlockSpec((None, None, D_MODEL, TF), lambda i, f: (l, s, 0, f)),
            pl.BlockSpec((None, None, D_MODEL, TF), lambda i, f: (l, s, 0, f)),
            pl.BlockSpec((None, None, TF, D_MODEL), lambda i, f: (l, s, f, 0)),
        ],
        out_specs=pl.BlockSpec((TM_FFN, D_MODEL), lambda i, f: (i, 0)),
        scratch_shapes=[pltpu.VMEM((TM_FFN, D_MODEL), BF16)],
        compiler_params=pltpu.CompilerParams(dimension_semantics=("parallel", "arbitrary"),
                                             vmem_limit_bytes=BIG_TILE_VMEM_LIMIT),
        name="ffn",
    )(x, mods, g_pre.reshape(1, D_MODEL), g_post.reshape(1, D_MODEL), wg, wu, wd)


def _inproj_kernel(x_ref, mod_ref, g_ref, w_ref, wl_ref, o_ref, t_ref, h_scr):
    k = pl.program_id(1)
    last = pl.num_programs(1) - 1

    def project(weights_ref):
        y = _dot(h_scr[...], weights_ref[...])
        o_ref[...] = y.astype(o_ref.dtype)
        return y

    @pl.when(k == 0)
    def _():
        _modulated_norm(x_ref, h_scr, g_ref[...] * (1.0 + mod_ref[4:5, :]), mod_ref[3:4, :])
        project(w_ref)

    @pl.when((k > 0) & (k < last))
    def _():
        project(w_ref)

    @pl.when(k == last)
    def _():
        y = project(wl_ref)
        t_ref[...] = y[:, y.shape[1] - t_ref.shape[1]:]


def _inproj(grp, x, mods, g, w, w_last, n_tail):
    tn = w_last.shape[1]
    steps = w.shape[1] // tn + 1
    assert steps >= 2 and w.shape[1] % tn == 0
    tm = TM_IN
    return pl.pallas_call(
        _inproj_kernel,
        out_shape=(jax.ShapeDtypeStruct((grp.rows, steps * tn), BF16),
                   jax.ShapeDtypeStruct((grp.rows, n_tail), F32)),
        grid=(grp.rows // tm, steps),
        in_specs=[
            pl.BlockSpec((tm, D_MODEL), lambda i, k: (i, 0)),
            pl.BlockSpec((None, N_MOD, D_MODEL), lambda i, k: (grp.mod_row(i, tm), 0, 0)),
            pl.BlockSpec((1, D_MODEL), lambda i, k: (0, 0)),
            pl.BlockSpec((D_MODEL, tn), lambda i, k: (0, jnp.minimum(k, steps - 2))),
            pl.BlockSpec((D_MODEL, tn), lambda i, k: (0, 0)),
        ],
        out_specs=(pl.BlockSpec((tm, tn), lambda i, k: (i, k)),
                   pl.BlockSpec((tm, n_tail), lambda i, k: (i, 0))),
        scratch_shapes=[pltpu.VMEM((tm, D_MODEL), BF16)],
        compiler_params=pltpu.CompilerParams(dimension_semantics=("parallel", "arbitrary"),
                                             vmem_limit_bytes=BIG_TILE_VMEM_LIMIT),
        name="inproj",
    )(x, mods, g.reshape(1, D_MODEL), w, w_last)


def _outproj_kernel(a_ref, b_ref, wa_ref, wb_ref, x_ref, mod_ref, g_ref, o_ref):
    gain = mod_ref[5:6, :] * g_ref[...]

    def body(rows):
        y = _dot(a_ref[rows, :], wa_ref[...]) + _dot(b_ref[rows, :], wb_ref[...])
        o_ref[rows, :] = x_ref[rows, :] + _rms(y) * gain
    _row_chunks(o_ref.shape[0], body)


def _outproj(grp, a, b, w, x, mods, g):
    half = a.shape[1]
    tm = TM
    return pl.pallas_call(
        _outproj_kernel,
        out_shape=jax.ShapeDtypeStruct((grp.rows, D_MODEL), F32),
        grid=(grp.rows // tm,),
        in_specs=[
            pl.BlockSpec((tm, half), lambda i: (i, 0)),
            pl.BlockSpec((tm, half), lambda i: (i, 0)),
            pl.BlockSpec((half, D_MODEL), lambda i: (0, 0), pipeline_mode=pl.Buffered(1)),
            pl.BlockSpec((half, D_MODEL), lambda i: (1, 0), pipeline_mode=pl.Buffered(1)),
            pl.BlockSpec((tm, D_MODEL), lambda i: (i, 0)),
            pl.BlockSpec((None, N_MOD, D_MODEL), lambda i: (grp.mod_row(i, tm), 0, 0)),
            pl.BlockSpec((1, D_MODEL), lambda i: (0, 0)),
        ],
        out_specs=pl.BlockSpec((tm, D_MODEL), lambda i: (i, 0)),
        compiler_params=pltpu.CompilerParams(dimension_semantics=("parallel",),
                                             vmem_limit_bytes=BIG_TILE_VMEM_LIMIT),
        name="outproj",
    )(a, b, w, w, x, mods, g.reshape(1, D_MODEL))


def _rope_tables(L, rot_dim, lane0, width):
    half = rot_dim // 2
    inv = ROPE_BASE ** (-jnp.arange(0, half, 2, dtype=F32) / half)
    pos = jnp.arange(L)
    ang_r = (pos // GRID_W).astype(F32)[:, None] * inv
    ang_c = (pos % GRID_W).astype(F32)[:, None] * inv
    cr, sr, cc, sc = jnp.cos(ang_r), jnp.sin(ang_r), jnp.cos(ang_c), jnp.sin(ang_c)
    z = jnp.zeros_like(sr)
    cos = jnp.concatenate([cr, cr, cc, cc], axis=-1)
    sin_a = jnp.concatenate([-sr, z, -sc, z], axis=-1)
    sin_b = jnp.concatenate([z, sr, z, sc], axis=-1)
    pad = ((0, 0), (lane0, width - lane0 - rot_dim))
    return jnp.pad(cos, pad, constant_values=1.0), jnp.pad(sin_a, pad), jnp.pad(sin_b, pad)


def _rope(x, cos, sin_a, sin_b, nf):
    w = x.shape[-1]
    return x * cos + pltpu.roll(x, w - nf, 1) * sin_a + pltpu.roll(x, nf, 1) * sin_b


DFT_SPLIT = 32


def _dft_matrices(L):
    s = jnp.arange(L, dtype=jnp.int32)[None, :]

    def trig(k):
        ang = ((k[:, None] * s) % (2 * L)).astype(F32) * (math.pi / L)
        return jnp.cos(ang), jnp.sin(ang)

    c1, s1 = trig(jnp.arange(0, L, DFT_SPLIT, dtype=jnp.int32))
    c0, s0 = trig(jnp.arange(DFT_SPLIT, dtype=jnp.int32))
    cos = (c1[:, None, :] * c0[None] - s1[:, None, :] * s0[None]).reshape(L, L)
    sin = (s1[:, None, :] * c0[None] + c1[:, None, :] * s0[None]).reshape(L, L)
    nyq = (1 - 2 * (jnp.arange(L, dtype=jnp.int32) % 2)).astype(F32)
    k = jnp.arange(L, dtype=jnp.int32)[:, None]
    msin = jnp.where(k == 0, nyq[None, :], -sin)
    msin_t = jnp.where(s == 0, nyq[:, None], -sin)
    return cos.astype(BF16), msin.astype(BF16), msin_t.astype(BF16)


def _filter_kernel(L, z_ref, w1_ref, b1_ref, w2_ref, b2_ref, fr_ref, w3f_ref, w3b_ref, dl_ref, cos_ref, msin_ref,
                   ka_ref, kb_ref, kc_ref, h_scr):
    z = z_ref[...]

    @pl.when((pl.program_id(0) == 0) & (pl.program_id(1) == 0))
    def _():
        fr = fr_ref[...]
        h1 = jnp.sin(fr * (jnp.dot(z, w1_ref[...], precision=HIGHEST, preferred_element_type=F32) + b1_ref[...]))
        h_scr[...] = jnp.sin(
            fr * (jnp.dot(h1, w2_ref[...], precision=HIGHEST, preferred_element_type=F32) + b2_ref[...]))

    h = h_scr[...]
    decay = jnp.exp(-z[:, 0:1] * dl_ref[...])
    hf = jnp.dot(h, w3f_ref[...], precision=HIGHEST, preferred_element_type=F32) * decay
    hb = jnp.dot(h, w3b_ref[...], precision=HIGHEST, preferred_element_type=F32) * decay
    row = lax.broadcasted_iota(jnp.int32, hf.shape, 0)
    row0 = row == 0
    hb = jnp.where(row0, 0.0, hb)
    even = hf + hb
    re = _dot(cos_ref[...], even.astype(BF16))
    im = _dot(msin_ref[...], (hf - hb).astype(BF16))
    nyq = jnp.sum(jnp.where(row % 2 == 0, even, -even), axis=0, keepdims=True)
    sc = jnp.where(row0, 0.5 / L, 1.0 / L)
    ka_ref[...] = re * sc
    kb_ref[...] = jnp.where(row0, 0.0, im) * sc
    kc_ref[...] = jnp.where(row0, nyq, re) * sc


def _hyena_spectra(L, tc, dft, f_w1, f_b1, f_w2, f_b2, f_w3, f_freq):
    t = jnp.linspace(0.0, 1.0, L, dtype=F32)[:, None]
    bands = (POS_EMB - 1) // 2
    w = 2.0 * math.pi * jnp.arange(L, dtype=F32)[:, None] / L
    fb = jnp.linspace(1e-4, bands - 1, bands, dtype=F32)[None, :]
    z = jnp.concatenate([t, jnp.cos(fb * w), -jnp.sin(fb * w)], axis=-1)
    z = jnp.pad(z, ((0, 0), (0, 128 - POS_EMB)))
    w1 = jnp.pad(f_w1, ((0, 128 - POS_EMB), (0, 0)))
    deltas = jnp.abs(jnp.linspace(math.log(HY_TARGET) / HY_SLOW_DECAY,
                                  math.log(HY_TARGET) / HY_FAST_DECAY, HY_W, dtype=F32))[None, :]
    nct = HY_W // tc
    fo = FILTER_ORDER
    row = lambda a: a.reshape(1, fo)
    kshape = jax.ShapeDtypeStruct((HY_ORDER, L, HY_W), F32)
    kspec = pl.BlockSpec((None, L, tc), lambda n, c: (n, 0, c))
    const = lambda shape: pl.BlockSpec(shape, lambda n, c: (0, 0))
    return pl.pallas_call(
        functools.partial(_filter_kernel, L),
        out_shape=(kshape, kshape, kshape),
        grid=(HY_ORDER, nct),
        in_specs=[
            const((L, 128)), const((128, fo)), const((1, fo)), const((fo, fo)), const((1, fo)), const((1, fo)),
            pl.BlockSpec((fo, tc), lambda n, c: (0, 2 * n * nct + c)),
            pl.BlockSpec((fo, tc), lambda n, c: (0, (2 * n + 1) * nct + c)),
            pl.BlockSpec((1, tc), lambda n, c: (0, c)),
            const((L, L)), const((L, L)),
        ],
        out_specs=(kspec, kspec, kspec),
        scratch_shapes=[pltpu.VMEM((L, fo), F32)],
        compiler_params=_cparams("arbitrary", "arbitrary"),
        name="hyena_filter",
    )(z, w1, row(f_b1), f_w2, row(f_b2), row(f_freq), f_w3, f_w3, deltas, dft[0], dft[1])


def _hyena_kernel(L, nseq, uv_ref, ug0_ref, ug1_ref, cwv_ref, cwg0_ref, cwg1_ref, cbv_ref, cbg0_ref, cbg1_ref,
                  hb_ref, ka_ref, kb_ref, kc_ref, cos_ref, msin_ref, msint_ref, o_ref):
    row = lax.broadcasted_iota(jnp.int32, (L, uv_ref.shape[1]), 0)
    first, last = row == 0, row == L - 1
    seqs = [slice(i * L, (i + 1) * L) for i in range(nseq)]

    def short_conv(u_ref, rows, w_ref, b_ref):
        u = u_ref[rows, :].astype(F32)
        prev = jnp.where(first, 0.0, pltpu.roll(u, 1, 0))
        nxt = jnp.where(last, 0.0, pltpu.roll(u, L - 1, 0))
        return b_ref[...] + prev * w_ref[0:1, :] + u * w_ref[1:2, :] + nxt * w_ref[2:3, :]

    z = [short_conv(uv_ref, r, cwv_ref, cbv_ref) for r in seqs]
    gates = [(short_conv(ug0_ref, r, cwg0_ref, cbg0_ref), short_conv(ug1_ref, r, cwg1_ref, cbg1_ref))
             for r in seqs]
    for n in range(HY_ORDER):
        zb = [zi.astype(BF16) for zi in z]
        zre = [_dot(cos_ref[...], b) for b in zb]
        zim = [_dot(msin_ref[...], b) for b in zb]
        ka, kb, kc = ka_ref[n], kb_ref[n], kc_ref[n]
        yre = [(re * ka - im * kb).astype(BF16) for re, im in zip(zre, zim)]
        yim = [(re * kb + im * kc).astype(BF16) for re, im in zip(zre, zim)]
        conv = [_dot(cos_ref[...], a) + _dot(msint_ref[...], b) for a, b in zip(yre, yim)]
        z = [g[n] * (cv + hb_ref[n:n + 1, :] * zi) for g, cv, zi in zip(gates, conv, z)]
    for r, zi in zip(seqs, z):
        o_ref[r, :] = zi.astype(o_ref.dtype)


def _hyena(grp, p, tc, nseq, spectra, dft, conv_w, conv_b, h_bias):
    L = grp.L
    nct = HY_W // tc
    ka, kb, kc = spectra
    u_spec = lambda j: pl.BlockSpec((nseq * L, tc), lambda c, b: (b, j * nct + c))
    cw_spec = lambda j: pl.BlockSpec((SHORT_CONV, tc), lambda c, b: (0, j * nct + c))
    cb_spec = lambda j: pl.BlockSpec((1, tc), lambda c, b: (0, j * nct + c))
    k_spec = pl.BlockSpec((HY_ORDER, L, tc), lambda c, b: (0, 0, c))
    m_spec = pl.BlockSpec((L, L), lambda c, b: (0, 0))
    cb = conv_b.reshape(1, -1)
    return pl.pallas_call(
        functools.partial(_hyena_kernel, L, nseq),
        out_shape=jax.ShapeDtypeStruct((grp.rows, HY_W), BF16),
        grid=(nct, grp.nb // nseq),
        in_specs=[
            u_spec(0), u_spec(1), u_spec(2), cw_spec(0), cw_spec(1), cw_spec(2),
            cb_spec(0), cb_spec(1), cb_spec(2),
            pl.BlockSpec((HY_ORDER, tc), lambda c, b: (0, c)),
            k_spec, k_spec, k_spec, m_spec, m_spec, m_spec,
        ],
        out_specs=pl.BlockSpec((nseq * L, tc), lambda c, b: (b, c)),
        compiler_params=_cparams("parallel", "arbitrary"),
        name="hyena",
    )(p, p, p, conv_w, conv_w, conv_w, cb, cb, cb, h_bias, ka, kb, kc, *dft)


def _gqa_kernel(L, window, has_ctx, has_rope, emit_kv, *refs):
    it = iter(refs)
    q_ref, k_ref, v_ref, sink_ref = next(it), next(it), next(it), next(it)
    if has_ctx:
        kc_ref, vc_ref = next(it), next(it)
    if has_rope:
        cos_ref, sa_ref, sb_ref = next(it), next(it), next(it)
    o_ref = next(it)
    if emit_kv:
        kn_ref, vn_ref = next(it), next(it)

    qb = HEAD_DIM
    nf = HEAD_DIM // 4
    gw = ATT_GROUP * HEAD_DIM
    hps = k_ref.shape[1] // HEAD_DIM
    sink_row = sink_ref[...]
    sink_lane = lax.broadcasted_iota(jnp.int32, sink_row.shape, 1)
    band_bias = {}

    def bias_for(lo, hi, i):
        key = (lo - i * qb, hi - lo)
        if key not in band_bias:
            shape = (ATT_GROUP * qb, hi - lo)
            rel = key[0] + lax.broadcasted_iota(jnp.int32, shape, 1) - lax.broadcasted_iota(jnp.int32, shape, 0) % qb
            band_bias[key] = jnp.where(jnp.abs(rel) <= window, 0.0, -1e30)
        return band_bias[key]

    for hk in range(hps):
        head = pl.program_id(1) * hps + hk
        k = k_ref[:, hk * HEAD_DIM:(hk + 1) * HEAD_DIM]
        v = v_ref[:, hk * HEAD_DIM:(hk + 1) * HEAD_DIM]
        if emit_kv:
            kn_ref[hk] = k
            vn_ref[hk] = v
        if has_rope:
            k = _rope(k, cos_ref[...], sa_ref[...], sb_ref[...], nf)
        k = k.astype(BF16)
        v = jnp.concatenate([v.astype(BF16), jnp.ones((L, HEAD_DIM), BF16)], axis=1)
        if has_ctx:
            kc = kc_ref[hk].astype(BF16)
            vc = jnp.concatenate([vc_ref[hk].astype(BF16), jnp.ones((PAST_LEN, HEAD_DIM), BF16)], axis=1)
        sinks = [jnp.sum(jnp.where(sink_lane == head * ATT_GROUP + g, sink_row, 0.0), axis=1, keepdims=True)
                 for g in range(ATT_GROUP)]
        sink = jnp.concatenate([jnp.broadcast_to(s, (qb, 1)) for s in sinks], axis=0)

        for i in range(L // qb):
            rows = slice(i * qb, (i + 1) * qb)
            qs = []
            for g in range(ATT_GROUP):
                qg = q_ref[rows, hk * gw + g * HEAD_DIM:hk * gw + (g + 1) * HEAD_DIM].astype(F32)
                if has_rope:
                    qg = _rope(qg, cos_ref[rows, :], sa_ref[rows, :], sb_ref[rows, :], nf)
                qs.append((qg * ATT_SCALE).astype(BF16))
            q = jnp.concatenate(qs, axis=0)
            if window is None:
                lo, hi = 0, L
            else:
                lo, hi = max(0, (i - 1) * qb), min(L, (i + 2) * qb)
            s = _dot_nt(q, k[lo:hi])
            if window is not None:
                s = s + bias_for(lo, hi, i)
            m = jnp.maximum(jnp.max(s, axis=-1, keepdims=True), sink)
            if has_ctx:
                sc = _dot_nt(q, kc)
                m = jnp.maximum(m, jnp.max(sc, axis=-1, keepdims=True))
            oa = _dot(jnp.exp(s - m).astype(BF16), v[lo:hi])
            if has_ctx:
                oa = oa + _dot(jnp.exp(sc - m).astype(BF16), vc)
            o = oa[:, :HEAD_DIM] / (oa[:, HEAD_DIM:] + jnp.exp(sink - m))
            for g in range(ATT_GROUP):
                o_ref[rows, hk * gw + g * HEAD_DIM:hk * gw + (g + 1) * HEAD_DIM] = (
                    o[g * qb:(g + 1) * qb].astype(o_ref.dtype))


def _gqa(grp, p, kv, sink, hps, window=None, ctx=None, rope=None, emit_kv=False):
    L = grp.L
    qw = hps * ATT_GROUP * HEAD_DIM
    kw = hps * HEAD_DIM
    nh = ATT_KV_HEADS // hps
    in_specs = [
        pl.BlockSpec((L, qw), lambda b, h: (b, 3 * HY_W // qw + h)),
        pl.BlockSpec((L, kw), lambda b, h: (b, h)),
        pl.BlockSpec((L, kw), lambda b, h: (b, nh + h)),
        pl.BlockSpec((1, ATT_HEADS), lambda b, h: (0, 0)),
    ]
    args = [p, kv, kv, sink.reshape(1, ATT_HEADS)]
    if ctx is not None:
        kc, vc, e = ctx
        spec = pl.BlockSpec((None, None, hps, PAST_LEN, HEAD_DIM), lambda b, h: (b, e, h, 0, 0))
        in_specs += [spec, spec]
        args += [kc, vc]
    if rope is not None:
        in_specs += [pl.BlockSpec((L, HEAD_DIM), lambda b, h: (0, 0))] * 3
        args += list(rope)
    out_shape = [jax.ShapeDtypeStruct((grp.rows, ATT_HEADS * HEAD_DIM), BF16)]
    out_specs = [pl.BlockSpec((L, qw), lambda b, h: (b, h))]
    if emit_kv:
        kv_shape = jax.ShapeDtypeStruct((grp.nb, 1, ATT_KV_HEADS, L, HEAD_DIM), F32)
        kv_spec = pl.BlockSpec((None, None, hps, L, HEAD_DIM), lambda b, h: (b, 0, h, 0, 0))
        out_shape += [kv_shape, kv_shape]
        out_specs += [kv_spec, kv_spec]
    return pl.pallas_call(
        functools.partial(_gqa_kernel, L, window, ctx is not None, rope is not None, emit_kv),
        out_shape=tuple(out_shape),
        grid=(grp.nb, nh),
        in_specs=in_specs,
        out_specs=tuple(out_specs),
        compiler_params=_cparams("parallel", "parallel"),
        name="gqa",
    )(*args)


def _chunk_cumsum(x, reverse):
    n = x.shape[0]
    pos = lax.broadcasted_iota(jnp.int32, x.shape, 0) % CHUNK
    s = 1
    while s < CHUNK:
        if reverse:
            x = x + jnp.where(pos < CHUNK - s, pltpu.roll(x, n - s, 0), 0.0)
        else:
            x = x + jnp.where(pos >= s, pltpu.roll(x, s, 0), 0.0)
        s *= 2
    return x


def _hgrn_kernel(L, has_state, emit_state, *refs):
    it = iter(refs)
    q_ref, ff_ref, fb_ref, i_ref, g_ref, lb_ref, norm_ref = (next(it) for _ in range(7))
    if has_state:
        s0_ref = next(it)
    o_ref = next(it)
    if emit_state:
        s_ref = next(it)
    acc_scr = next(it)

    a = lb_ref[...]
    e = jnp.exp(a - jnp.max(a, axis=0, keepdims=True))
    lb_all = e[1] / (e[0] + e[1])
    ci = lax.broadcasted_iota(jnp.int32, (CHUNK, CHUNK), 0)
    cj = lax.broadcasted_iota(jnp.int32, (CHUNK, CHUNK), 1)
    n_chunks = L // CHUNK
    chunks = [slice(n * CHUNK, (n + 1) * CHUNK) for n in range(n_chunks)]

    for hh in range(q_ref.shape[1] // HEAD_DIM):
        hc = slice(hh * HEAD_DIM, (hh + 1) * HEAD_DIM)
        q = _silu(q_ref[:, hc].astype(F32))
        for d, fz_ref in enumerate((ff_ref, fb_ref)):
            lbd = lb_all[d:d + 1, hc]
            keep = (cj <= ci) if d == 0 else (cj >= ci)
            qds, atts, decay, own = [], [], [], []
            for rows in chunks:
                f = lbd + (1.0 - lbd) * jax.nn.sigmoid(fz_ref[rows, hc].astype(F32))
                b = _chunk_cumsum(jnp.log(f), reverse=(d == 1))
                qd = (q[rows] * jnp.exp(b)).astype(BF16)
                kd32 = (1.0 - f) * jnp.exp(-b)
                dc = jnp.exp(b[CHUNK - 1:CHUNK] if d == 0 else b[0:1])
                qds.append(qd)
                decay.append(dc)
                atts.append(jnp.where(keep, _dot_nt(qd, kd32.astype(BF16)), 0.0).astype(BF16))
                own.append(_dot_tn(i_ref[rows, hc], (kd32 * dc).astype(BF16)))
            st = s0_ref[d, hh].T if has_state else jnp.zeros((HEAD_DIM, HEAD_DIM), F32)
            entering = [None] * n_chunks
            for n in (range(n_chunks) if d == 0 else range(n_chunks - 1, -1, -1)):
                entering[n] = st.astype(BF16)
                st = st * decay[n] + own[n]
            if emit_state:
                s_ref[d, hh] = st.T
            for n, rows in enumerate(chunks):
                o = _dot(atts[n], i_ref[rows, hc]) + _dot_nt(qds[n], entering[n])
                if d == 0:
                    acc_scr[rows, :] = o
                else:
                    acc_scr[rows, :] += o
        o = _rms(acc_scr[...]) * norm_ref[...] * _silu(g_ref[:, hc].astype(F32))
        o_ref[:, hc] = o.astype(o_ref.dtype)


def _hgrn(grp, p, hg_lb, norm_g, hps, state=None, emit_state=False):
    L = grp.L
    hw = hps * HEAD_DIM
    nh = HG_HEADS // hps
    col = lambda j: pl.BlockSpec((L, hw), lambda b, h: (b, j * nh + h))
    in_specs = [col(0), col(1), col(2), col(3), col(4),
                pl.BlockSpec((hg_lb.shape[0], 2, hw), lambda b, h: (0, 0, h)),
                pl.BlockSpec((1, HEAD_DIM), lambda b, h: (0, 0))]
    args = [p, p, p, p, p, hg_lb, norm_g.reshape(1, HEAD_DIM)]
    st_spec = lambda o: pl.BlockSpec((None, None, 2, hps, HEAD_DIM, HEAD_DIM), lambda b, h: (b, o, 0, h, 0, 0))
    if state is not None:
        s0, o = state
        in_specs.append(st_spec(o))
        args.append(s0)
    out_shape = [jax.ShapeDtypeStruct((grp.rows, HG_W), BF16)]
    out_specs = [pl.BlockSpec((L, hw), lambda b, h: (b, h))]
    if emit_state:
        out_shape.append(jax.ShapeDtypeStruct((grp.nb, 1, 2, HG_HEADS, HEAD_DIM, HEAD_DIM), F32))
        out_specs.append(st_spec(0))
    return pl.pallas_call(
        functools.partial(_hgrn_kernel, L, state is not None, emit_state),
        out_shape=tuple(out_shape),
        grid=(grp.nb, nh),
        in_specs=in_specs,
        out_specs=tuple(out_specs),
        scratch_shapes=[pltpu.VMEM((L, HEAD_DIM), F32)],
        compiler_params=_cparams("parallel", "parallel"),
        name="hgrn",
    )(*args)


MLA_QW = 256
KR_W = 128
OD_TAIL = 1024


def _mla_prep_kernel(has_rope, emit_kr, *refs):
    it = iter(refs)
    ql_ref, kvl_ref, kr_ref, qn_ref, kvn_ref, wq_ref = (next(it) for _ in range(6))
    if has_rope:
        kc_ref, ksa_ref, ksb_ref = (next(it) for _ in range(3))
    q_ref, ckv_ref, kro_ref = next(it), next(it), next(it)
    if emit_kr:
        krn_ref = next(it)
        krn_ref[...] = kr_ref[:, :ROPE]

    nf = ROPE // 4
    qn = (_rms(ql_ref[...]) * qn_ref[...]).astype(BF16)
    q = _dot(qn, wq_ref[...])
    for h in range(MLA_HEADS):
        nope = slice(h * MLA_QW, h * MLA_QW + NOPE)
        rot = slice(h * MLA_QW + NOPE, (h + 1) * MLA_QW)
        q_ref[:, nope] = q[:, nope].astype(q_ref.dtype)
        qr = q[:, rot]
        if has_rope:
            qr = _rope(qr, kc_ref[...], ksa_ref[...], ksb_ref[...], nf)
        q_ref[:, rot] = qr.astype(q_ref.dtype)
    ckv_ref[...] = _rms(kvl_ref[...]) * kvn_ref[...]
    kr = kr_ref[...]
    if has_rope:
        kr = _rope(kr, kc_ref[...], ksa_ref[...], ksb_ref[...], nf)
    kro_ref[...] = kr.astype(kro_ref.dtype)


def _mla_prep(grp, tail, q_norm, kv_norm, wq, rope=None, emit_kr=False):
    tm = min(512, grp.L)
    per = grp.L // tm
    n_rows = grp.rows
    in_specs = [
        pl.BlockSpec((tm, Q_LORA), lambda i: (i, 0)),
        pl.BlockSpec((tm, KV_LORA), lambda i: (i, Q_LORA // KV_LORA)),
        pl.BlockSpec((tm, KR_W), lambda i: (i, (Q_LORA + KV_LORA) // KR_W)),
        pl.BlockSpec((1, Q_LORA), lambda i: (0, 0)),
        pl.BlockSpec((1, KV_LORA), lambda i: (0, 0)),
        pl.BlockSpec((Q_LORA, MLA_HEADS * MLA_QW), lambda i: (0, 0)),
    ]
    args = [tail, tail, tail, q_norm.reshape(1, Q_LORA), kv_norm.reshape(1, KV_LORA), wq]
    if rope is not None:
        in_specs += [pl.BlockSpec((tm, KR_W), lambda i: (i % per, 0))] * 3
        args += list(rope)
    out_shape = [jax.ShapeDtypeStruct((n_rows, MLA_HEADS * MLA_QW), BF16),
                 jax.ShapeDtypeStruct((n_rows, KV_LORA), F32),
                 jax.ShapeDtypeStruct((n_rows, KR_W), BF16)]
    out_specs = [pl.BlockSpec((tm, MLA_HEADS * MLA_QW), lambda i: (i, 0)),
                 pl.BlockSpec((tm, KV_LORA), lambda i: (i, 0)),
                 pl.BlockSpec((tm, KR_W), lambda i: (i, 0))]
    if emit_kr:
        out_shape.append(jax.ShapeDtypeStruct((n_rows, ROPE), F32))
        out_specs.append(pl.BlockSpec((tm, ROPE), lambda i: (i, 0)))
    return pl.pallas_call(
        functools.partial(_mla_prep_kernel, rope is not None, emit_kr),
        out_shape=tuple(out_shape),
        grid=(n_rows // tm,),
        in_specs=in_specs,
        out_specs=tuple(out_specs),
        compiler_params=_cparams("parallel"),
        name="mla_prep",
    )(*args)


def _mla_attn_kernel(L, has_ctx, *refs):
    it = iter(refs)
    q_ref, ckv_ref, kr_ref, wkv_ref = (next(it) for _ in range(4))
    if has_ctx:
        cckv_ref, ckr_ref = next(it), next(it)
    o_ref = next(it)

    ckv = ckv_ref[...].astype(BF16)
    kr = kr_ref[...]
    if has_ctx:
        ckv = jnp.concatenate([ckv, cckv_ref[...].astype(BF16)], axis=0)
        kr = jnp.concatenate([kr, ckr_ref[...].astype(BF16)], axis=0)
    ones = jnp.ones((ckv.shape[0], V_DIM), BF16)
    qb = min(L, 256)
    for hh in range(q_ref.shape[1] // MLA_QW):
        qc = slice(hh * MLA_QW, (hh + 1) * MLA_QW)
        oc = slice(hh * V_DIM, (hh + 1) * V_DIM)
        kv = _dot(ckv, wkv_ref[:, qc])
        kh = jnp.concatenate([kv[:, :NOPE].astype(BF16), kr], axis=1)
        vh = jnp.concatenate([kv[:, NOPE:].astype(BF16), ones], axis=1)
        for i in range(L // qb):
            rows = slice(i * qb, (i + 1) * qb)
            s = _dot_nt(q_ref[rows, qc], kh) * MLA_SCALE
            m = jnp.max(s, axis=-1, keepdims=True)
            oa = _dot(jnp.exp(s - m).astype(BF16), vh)
            o_ref[rows, oc] = (oa[:, :V_DIM] / oa[:, V_DIM:]).astype(o_ref.dtype)


def _mla_attn(grp, q, ckv, kr, wkv, hps, ctx=None):
    L = grp.L
    in_specs = [
        pl.BlockSpec((L, hps * MLA_QW), lambda b, h: (b, h)),
        pl.BlockSpec((L, KV_LORA), lambda b, h: (b, 0)),
        pl.BlockSpec((L, KR_W), lambda b, h: (b, 0)),
        pl.BlockSpec((KV_LORA, hps * (NOPE + V_DIM)), lambda b, h: (0, h)),
    ]
    args = [q, ckv, kr, wkv]
    if ctx is not None:
        cckv, ckr, o = ctx
        in_specs += [pl.BlockSpec((None, None, PAST_LEN, KV_LORA), lambda b, h: (b, o, 0, 0)),
                     pl.BlockSpec((None, None, PAST_LEN, KR_W), lambda b, h: (b, o, 0, 0))]
        args += [cckv, ckr]
    return pl.pallas_call(
        functools.partial(_mla_attn_kernel, L, ctx is not None),
        out_shape=jax.ShapeDtypeStruct((grp.rows, MLA_HEADS * V_DIM), BF16),
        grid=(grp.nb, MLA_HEADS // hps),
        in_specs=in_specs,
        out_specs=pl.BlockSpec((L, hps * V_DIM), lambda b, h: (b, h)),
        compiler_params=_cparams("parallel", "parallel"),
        name="mla_attn",
    )(*args)


def kernel(x_prompt, x_sample, c, c_ctx, cache_attn_k, cache_attn_v, cache_mla_ckv, cache_mla_krope, state_hgrn, mod_w, mod_b, norm_g, ffn_wg, ffn_wu, ffn_wd, ev_w_in, ev_w_out, hy_conv_w, hy_conv_b, hy_f_w1, hy_f_b1, hy_f_w2, hy_f_b2, hy_f_w3, hy_f_freq, hy_bias, attn_sink, od_w_in, od_w_out, hg_lb, hg_norm, mla_q_norm, mla_w_qb, mla_kv_norm, mla_w_kvb):
    depth = mod_w.shape[0]
    groups = (PROMPT, LATENT)
    xs = [x_prompt.reshape(PROMPT.rows, D_MODEL), x_sample.reshape(LATENT.rows, D_MODEL)]
    cvec = jnp.concatenate([c_ctx[None, :], c, jnp.zeros((MOD_ROWS - 1 - DEC_BATCH, D_MODEL), F32)], axis=0)
    mods_all = _modulation(cvec, mod_w, mod_b)

    wg, wu, wd = ffn_wg.astype(BF16), ffn_wu.astype(BF16), ffn_wd.astype(BF16)
    hy_tc = {SEQ: 512, DEC_SEQ: 256}
    hy_nseq = {SEQ: 4, DEC_SEQ: 2}

    new_k = new_v = new_ckv = new_kr = new_s = None
    for l in range(depth):
        mods = mods_all[l]
        xs = [_ffn(grp, x, mods, 0, norm_g[l, 0], norm_g[l, 1], wg, wu, wd, l, 0) for grp, x in zip(groups, xs)]
        if l % 2 == 0:
            e = l // 2
            w_in = ev_w_in[e][:, :EV_IN - TN_EVEN].astype(BF16)
            w_in_last = ev_w_in[e][:, EV_IN - TN_EVEN:].astype(BF16)
            w_out = ev_w_out[e].astype(BF16)
            kv_cols = 2 * ATT_KV_HEADS * HEAD_DIM
            mix = []
            for grp, x in zip(groups, xs):
                p, kv = _inproj(grp, x, mods, norm_g[l, 2], w_in, w_in_last, kv_cols)
                dft = _dft_matrices(grp.L)
                tc = hy_tc[grp.L]
                spectra = _hyena_spectra(grp.L, tc, dft, hy_f_w1[e], hy_f_b1[e], hy_f_w2[e], hy_f_b2[e],
                                         hy_f_w3[e], hy_f_freq[e])
                hy = _hyena(grp, p, tc, hy_nseq[grp.L], spectra, dft, hy_conv_w[e], hy_conv_b[e], hy_bias[e])
                if grp.latent:
                    rope = _rope_tables(grp.L, HEAD_DIM, 0, HEAD_DIM)
                    (att,) = _gqa(grp, p, kv, attn_sink[e], 1, window=WINDOW,
                                  ctx=(cache_attn_k, cache_attn_v, e), rope=rope)
                else:
                    att, new_k, new_v = _gqa(grp, p, kv, attn_sink[e], 2, emit_kv=True)
                mix.append((hy, att))
        else:
            o = l // 2
            n_main = OD_IN_PAD - TN_ODD
            w_in = od_w_in[o][:, :n_main].astype(BF16)
            w_in_last = jnp.pad(od_w_in[o][:, n_main:].astype(BF16), ((0, 0), (0, OD_IN_PAD - OD_IN)))
            w_out = od_w_out[o].astype(BF16)
            wq = mla_w_qb[o].reshape(Q_LORA, MLA_HEADS, NOPE + ROPE)
            wq = jnp.pad(wq, ((0, 0), (0, 0), (0, MLA_QW - NOPE - ROPE))).reshape(Q_LORA, -1).astype(BF16)
            wkv = mla_w_kvb[o].astype(BF16)
            mix = []
            for grp, x in zip(groups, xs):
                p, tail = _inproj(grp, x, mods, norm_g[l, 2], w_in, w_in_last, OD_TAIL)
                if grp.latent:
                    (hg,) = _hgrn(grp, p, hg_lb, hg_norm[o], 2, state=(state_hgrn, o))
                    rope = _rope_tables(grp.L, ROPE, 0, KR_W)
                    q, ckv, kr = _mla_prep(grp, tail, mla_q_norm[o], mla_kv_norm[o], wq, rope=rope)
                    ckr = jnp.pad(cache_mla_krope, ((0, 0), (0, 0), (0, 0), (0, KR_W - ROPE)))
                    att = _mla_attn(grp, q, ckv, kr, wkv, 4, ctx=(cache_mla_ckv, ckr, o))
                else:
                    hg, new_s = _hgrn(grp, p, hg_lb, hg_norm[o], 2, emit_state=True)
                    q, ckv, kr, kr_raw = _mla_prep(grp, tail, mla_q_norm[o], mla_kv_norm[o], wq, emit_kr=True)
                    att = _mla_attn(grp, q, ckv, kr, wkv, 4)
                    new_ckv = ckv.reshape(BATCH, 1, SEQ, KV_LORA)
                    new_kr = kr_raw.reshape(BATCH, 1, SEQ, ROPE)
                mix.append((hg, att))
        xs = [_outproj(grp, a, b, w_out, x, mods, norm_g[l, 3])
              for grp, x, (a, b) in zip(groups, xs, mix)]
        xs = [_ffn(grp, x, mods, 2, norm_g[l, 4], norm_g[l, 5], wg, wu, wd, l, 1) for grp, x in zip(groups, xs)]

    y_prompt = xs[0].reshape(BATCH, SEQ, D_MODEL)
    y_sample = xs[1].reshape(DEC_BATCH, DEC_SEQ, D_MODEL)
    return (y_prompt, y_sample, new_k, new_v, new_ckv, new_kr, new_s)
```

```python
import functools
import math
from typing import NamedTuple

import jax
import jax.numpy as jnp
from jax import lax
from jax.experimental import pallas as pl
from jax.experimental.pallas import tpu as pltpu

D_MODEL = 2048
BATCH = 16
SEQ = 256
DEC_BATCH = 8
DEC_SEQ = 1024
PAST_LEN = 512
GRID_W = 64
HEAD_DIM = 128
HY_W = 1024
HY_ORDER = 2
SHORT_CONV = 3
POS_EMB = 33
FILTER_ORDER = 64
HY_FAST_DECAY = 0.3
HY_SLOW_DECAY = 1.5
HY_TARGET = 1e-2
ATT_HEADS = 8
ATT_KV_HEADS = 2
ATT_GROUP = 4
WINDOW = 128
EV_IN = 3 * HY_W + (ATT_HEADS + 2 * ATT_KV_HEADS) * HEAD_DIM
HG_W = 1024
HG_HEADS = 8
CHUNK = 64
Q_LORA = 512
KV_LORA = 256
NOPE = 128
ROPE = 64
V_DIM = 128
MLA_HEADS = 8
OD_IN = 5 * HG_W + Q_LORA + KV_LORA + ROPE
OD_IN_PAD = 6144
D_FF = 5632
MACARON_W = 0.5
N_MOD = 9
ROPE_BASE = 10000.0
EPS = 1e-6
ATT_SCALE = HEAD_DIM ** -0.5
MLA_SCALE = (NOPE + ROPE) ** -0.5

MOD_ROWS = 16

V7X_VMEM_LIMIT = 56 * 1024 * 1024
TM = 512
TM_FFN = 1024
BIG_TILE_VMEM_LIMIT = 60 * 1024 * 1024
TM_IN = 1024
RC = 512
TF = 512
TN_EVEN = 1536
TN_ODD = 1024
MOD_TN = 2048

BF16 = jnp.bfloat16
F32 = jnp.float32
HIGHEST = lax.Precision.HIGHEST


class Group(NamedTuple):
    nb: int
    L: int
    latent: bool

    @property
    def rows(self):
        return self.nb * self.L

    def mod_row(self, i, tm):
        return 1 + i // (self.L // tm) if self.latent else 0


PROMPT = Group(BATCH, SEQ, False)
LATENT = Group(DEC_BATCH, DEC_SEQ, True)


def _cparams(*sem):
    return pltpu.CompilerParams(dimension_semantics=sem, vmem_limit_bytes=V7X_VMEM_LIMIT)


def _rms(x):
    return x * lax.rsqrt(jnp.mean(x * x, axis=-1, keepdims=True) + EPS)


def _silu(x):
    return x * jax.nn.sigmoid(x)


def _dot(a, b):
    return jnp.dot(a, b, preferred_element_type=F32)


def _dot_nt(a, b):
    return lax.dot_general(a, b, (((1,), (1,)), ((), ())), preferred_element_type=F32)


def _dot_tn(a, b):
    return lax.dot_general(a, b, (((0,), (0,)), ((), ())), preferred_element_type=F32)


def _row_chunks(n_rows, body, rc=RC):
    def step(c, carry):
        body(pl.ds(pl.multiple_of(c * rc, rc), rc))
        return carry
    lax.fori_loop(0, n_rows // rc, step, 0, unroll=True)


def _modulated_norm(x_ref, h_scr, gain, shift):
    def body(rows):
        h_scr[rows, :] = (_rms(x_ref[rows, :]) * gain + shift).astype(BF16)
    _row_chunks(x_ref.shape[0], body)


def _mod_kernel(c_ref, w_ref, b_ref, o_ref):
    s = _silu(c_ref[...]).astype(BF16)
    o_ref[...] = _dot(s, w_ref[...].astype(BF16)) + b_ref[...]


def _modulation(cvec, mod_w, mod_b):
    depth = mod_w.shape[0]
    n = N_MOD * D_MODEL
    out = pl.pallas_call(
        _mod_kernel,
        out_shape=jax.ShapeDtypeStruct((depth, MOD_ROWS, n), F32),
        grid=(depth, n // MOD_TN),
        in_specs=[
            pl.BlockSpec((MOD_ROWS, D_MODEL), lambda l, j: (0, 0)),
            pl.BlockSpec((None, D_MODEL, MOD_TN), lambda l, j: (l, 0, j)),
            pl.BlockSpec((None, 1, MOD_TN), lambda l, j: (l, 0, j)),
        ],
        out_specs=pl.BlockSpec((None, MOD_ROWS, MOD_TN), lambda l, j: (l, 0, j)),
        compiler_params=_cparams("parallel", "parallel"),
        name="modulation",
    )(cvec, mod_w, mod_b.reshape(depth, 1, n))
    return out.reshape(depth, MOD_ROWS, N_MOD, D_MODEL)


def _ffn_kernel(j, x_ref, mod_ref, gpre_ref, gpost_ref, wg_ref, wu_ref, wd_ref, o_ref, h_scr):
    f = pl.program_id(1)

    @pl.when(f == 0)
    def _():
        gain = gpre_ref[...] * (1.0 + mod_ref[3 * j + 1:3 * j + 2, :])
        _modulated_norm(x_ref, h_scr, gain, mod_ref[3 * j:3 * j + 1, :])

    def partial_down(rows):
        h = h_scr[rows, :]
        a = (_silu(_dot(h, wg_ref[...])) * _dot(h, wu_ref[...])).astype(BF16)
        return _dot(a, wd_ref[...])

    @pl.when(f == 0)
    def _():
        def first(rows):
            o_ref[rows, :] = partial_down(rows)
        _row_chunks(o_ref.shape[0], first)

    last = pl.num_programs(1) - 1

    @pl.when((f > 0) & (f < last))
    def _():
        def accumulate(rows):
            o_ref[rows, :] += partial_down(rows)
        _row_chunks(o_ref.shape[0], accumulate)

    @pl.when(f == last)
    def _():
        gain = (MACARON_W * mod_ref[3 * j + 2:3 * j + 3, :]) * gpost_ref[...]

        def finish(rows):
            acc = o_ref[rows, :] + partial_down(rows)
            o_ref[rows, :] = x_ref[rows, :] + _rms(acc) * gain
        _row_chunks(o_ref.shape[0], finish, RC // 2)


def _ffn(grp, x, mods, j, g_pre, g_post, wg, wu, wd, l, s):
    return pl.pallas_call(
        functools.partial(_ffn_kernel, j),
        out_shape=jax.ShapeDtypeStruct((grp.rows, D_MODEL), F32),
        grid=(grp.rows // TM_FFN, D_FF // TF),
        in_specs=[
            pl.BlockSpec((TM_FFN, D_MODEL), lambda i, f: (i, 0)),
            pl.BlockSpec((None, N_MOD, D_MODEL), lambda i, f: (grp.mod_row(i, TM_FFN), 0, 0)),
            pl.BlockSpec((1, D_MODEL), lambda i, f: (0, 0)),
            pl.BlockSpec((1, D_MODEL), lambda i, f: (0, 0)),
            pl.BlockSpec((None, None, D_MODEL, TF), lambda i, f: (l, s, 0, f)),
            pl.BlockSpec((None, None, D_MODEL, TF), lambda i, f: (l, s, 0, f)),
            pl.BlockSpec((None, None, TF, D_MODEL), lambda i, f: (l, s, f, 0)),
        ],
        out_specs=pl.BlockSpec((TM_FFN, D_MODEL), lambda i, f: (i, 0)),
        scratch_shapes=[pltpu.VMEM((TM_FFN, D_MODEL), BF16)],
        compiler_params=pltpu.CompilerParams(dimension_semantics=("parallel", "arbitrary"),
                                             vmem_limit_bytes=BIG_TILE_VMEM_LIMIT),
        name="ffn",
    )(x, mods, g_pre.reshape(1, D_MODEL), g_post.reshape(1, D_MODEL), wg, wu, wd)


def _inproj_kernel(x_ref, mod_ref, g_ref, w_ref, wl_ref, o_ref, t_ref, h_scr):
    k = pl.program_id(1)
    last = pl.num_programs(1) - 1

    def project(weights_ref):
        y = _dot(h_scr[...], weights_ref[...])
        o_ref[...] = y.astype(o_ref.dtype)
        return y

    @pl.when(k == 0)
    def _():
        _modulated_norm(x_ref, h_scr, g_ref[...] * (1.0 + mod_ref[4:5, :]), mod_ref[3:4, :])
        project(w_ref)

    @pl.when((k > 0) & (k < last))
    def _():
        project(w_ref)

    @pl.when(k == last)
    def _():
        y = project(wl_ref)
        t_ref[...] = y[:, y.shape[1] - t_ref.shape[1]:]


def _inproj(grp, x, mods, g, w, w_last, n_tail):
    tn = w_last.shape[1]
    steps = w.shape[1] // tn + 1
    assert steps >= 2 and w.shape[1] % tn == 0
    tm = TM_IN
    return pl.pallas_call(
        _inproj_kernel,
        out_shape=(jax.ShapeDtypeStruct((grp.rows, steps * tn), BF16),
                   jax.ShapeDtypeStruct((grp.rows, n_tail), F32)),
        grid=(grp.rows // tm, steps),
        in_specs=[
            pl.BlockSpec((tm, D_MODEL), lambda i, k: (i, 0)),
            pl.BlockSpec((None, N_MOD, D_MODEL), lambda i, k: (grp.mod_row(i, tm), 0, 0)),
            pl.BlockSpec((1, D_MODEL), lambda i, k: (0, 0)),
            pl.BlockSpec((D_MODEL, tn), lambda i, k: (0, jnp.minimum(k, steps - 2))),
            pl.BlockSpec((D_MODEL, tn), lambda i, k: (0, 0)),
        ],
        out_specs=(pl.BlockSpec((tm, tn), lambda i, k: (i, k)),
                   pl.BlockSpec((tm, n_tail), lambda i, k: (i, 0))),
        scratch_shapes=[pltpu.VMEM((tm, D_MODEL), BF16)],
        compiler_params=pltpu.CompilerParams(dimension_semantics=("parallel", "arbitrary"),
                                             vmem_limit_bytes=BIG_TILE_VMEM_LIMIT),
        name="inproj",
    )(x, mods, g.reshape(1, D_MODEL), w, w_last)


def _outproj_kernel(a_ref, b_ref, wa_ref, wb_ref, x_ref, mod_ref, g_ref, o_ref):
    y = _dot(a_ref[...], wa_ref[...]) + _dot(b_ref[...], wb_ref[...])
    o_ref[...] = x_ref[...] + _rms(y) * (mod_ref[5:6, :] * g_ref[...])


def _outproj(grp, a, b, w, x, mods, g):
    half = a.shape[1]
    tm = TM
    return pl.pallas_call(
        _outproj_kernel,
        out_shape=jax.ShapeDtypeStruct((grp.rows, D_MODEL), F32),
        grid=(grp.rows // tm,),
        in_specs=[
            pl.BlockSpec((tm, half), lambda i: (i, 0)),
            pl.BlockSpec((tm, half), lambda i: (i, 0)),
            pl.BlockSpec((half, D_MODEL), lambda i: (0, 0), pipeline_mode=pl.Buffered(1)),
            pl.BlockSpec((half, D_MODEL), lambda i: (1, 0), pipeline_mode=pl.Buffered(1)),
            pl.BlockSpec((tm, D_MODEL), lambda i: (i, 0)),
            pl.BlockSpec((None, N_MOD, D_MODEL), lambda i: (grp.mod_row(i, tm), 0, 0)),
            pl.BlockSpec((1, D_MODEL), lambda i: (0, 0)),
        ],
        out_specs=pl.BlockSpec((tm, D_MODEL), lambda i: (i, 0)),
        compiler_params=_cparams("parallel"),
        name="outproj",
    )(a, b, w, w, x, mods, g.reshape(1, D_MODEL))


def _rope_tables(L, rot_dim, lane0, width):
    half = rot_dim // 2
    inv = ROPE_BASE ** (-jnp.arange(0, half, 2, dtype=F32) / half)
    pos = jnp.arange(L)
    ang_r = (pos // GRID_W).astype(F32)[:, None] * inv
    ang_c = (pos % GRID_W).astype(F32)[:, None] * inv
    cr, sr, cc, sc = jnp.cos(ang_r), jnp.sin(ang_r), jnp.cos(ang_c), jnp.sin(ang_c)
    z = jnp.zeros_like(sr)
    cos = jnp.concatenate([cr, cr, cc, cc], axis=-1)
    sin_a = jnp.concatenate([-sr, z, -sc, z], axis=-1)
    sin_b = jnp.concatenate([z, sr, z, sc], axis=-1)
    pad = ((0, 0), (lane0, width - lane0 - rot_dim))
    return jnp.pad(cos, pad, constant_values=1.0), jnp.pad(sin_a, pad), jnp.pad(sin_b, pad)


def _rope(x, cos, sin_a, sin_b, nf):
    w = x.shape[-1]
    return x * cos + pltpu.roll(x, w - nf, 1) * sin_a + pltpu.roll(x, nf, 1) * sin_b


DFT_SPLIT = 32


def _dft_matrices(L):
    s = jnp.arange(L, dtype=jnp.int32)[None, :]

    def trig(k):
        ang = ((k[:, None] * s) % (2 * L)).astype(F32) * (math.pi / L)
        return jnp.cos(ang), jnp.sin(ang)

    c1, s1 = trig(jnp.arange(0, L, DFT_SPLIT, dtype=jnp.int32))
    c0, s0 = trig(jnp.arange(DFT_SPLIT, dtype=jnp.int32))
    cos = (c1[:, None, :] * c0[None] - s1[:, None, :] * s0[None]).reshape(L, L)
    sin = (s1[:, None, :] * c0[None] + c1[:, None, :] * s0[None]).reshape(L, L)
    nyq = (1 - 2 * (jnp.arange(L, dtype=jnp.int32) % 2)).astype(F32)
    k = jnp.arange(L, dtype=jnp.int32)[:, None]
    msin = jnp.where(k == 0, nyq[None, :], -sin)
    msin_t = jnp.where(s == 0, nyq[:, None], -sin)
    return cos.astype(BF16), msin.astype(BF16), msin_t.astype(BF16)


def _filter_kernel(L, z_ref, w1_ref, b1_ref, w2_ref, b2_ref, fr_ref, w3f_ref, w3b_ref, dl_ref, cos_ref, msin_ref,
                   ka_ref, kb_ref, kc_ref, h_scr):
    z = z_ref[...]

    @pl.when((pl.program_id(0) == 0) & (pl.program_id(1) == 0))
    def _():
        fr = fr_ref[...]
        h1 = jnp.sin(fr * (jnp.dot(z, w1_ref[...], precision=HIGHEST, preferred_element_type=F32) + b1_ref[...]))
        h_scr[...] = jnp.sin(
            fr * (jnp.dot(h1, w2_ref[...], precision=HIGHEST, preferred_element_type=F32) + b2_ref[...]))

    h = h_scr[...]
    decay = jnp.exp(-z[:, 0:1] * dl_ref[...])
    hf = jnp.dot(h, w3f_ref[...], precision=HIGHEST, preferred_element_type=F32) * decay
    hb = jnp.dot(h, w3b_ref[...], precision=HIGHEST, preferred_element_type=F32) * decay
    row = lax.broadcasted_iota(jnp.int32, hf.shape, 0)
    row0 = row == 0
    hb = jnp.where(row0, 0.0, hb)
    even = hf + hb
    re = _dot(cos_ref[...], even.astype(BF16))
    im = _dot(msin_ref[...], (hf - hb).astype(BF16))
    nyq = jnp.sum(jnp.where(row % 2 == 0, even, -even), axis=0, keepdims=True)
    sc = jnp.where(row0, 0.5 / L, 1.0 / L)
    ka_ref[...] = re * sc
    kb_ref[...] = jnp.where(row0, 0.0, im) * sc
    kc_ref[...] = jnp.where(row0, nyq, re) * sc


def _hyena_spectra(L, tc, dft, f_w1, f_b1, f_w2, f_b2, f_w3, f_freq):
    t = jnp.linspace(0.0, 1.0, L, dtype=F32)[:, None]
    bands = (POS_EMB - 1) // 2
    w = 2.0 * math.pi * jnp.arange(L, dtype=F32)[:, None] / L
    fb = jnp.linspace(1e-4, bands - 1, bands, dtype=F32)[None, :]
    z = jnp.concatenate([t, jnp.cos(fb * w), -jnp.sin(fb * w)], axis=-1)
    z = jnp.pad(z, ((0, 0), (0, 128 - POS_EMB)))
    w1 = jnp.pad(f_w1, ((0, 128 - POS_EMB), (0, 0)))
    deltas = jnp.abs(jnp.linspace(math.log(HY_TARGET) / HY_SLOW_DECAY,
                                  math.log(HY_TARGET) / HY_FAST_DECAY, HY_W, dtype=F32))[None, :]
    nct = HY_W // tc
    fo = FILTER_ORDER
    row = lambda a: a.reshape(1, fo)
    kshape = jax.ShapeDtypeStruct((HY_ORDER, L, HY_W), F32)
    kspec = pl.BlockSpec((None, L, tc), lambda n, c: (n, 0, c))
    const = lambda shape: pl.BlockSpec(shape, lambda n, c: (0, 0))
    return pl.pallas_call(
        functools.partial(_filter_kernel, L),
        out_shape=(kshape, kshape, kshape),
        grid=(HY_ORDER, nct),
        in_specs=[
            const((L, 128)), const((128, fo)), const((1, fo)), const((fo, fo)), const((1, fo)), const((1, fo)),
            pl.BlockSpec((fo, tc), lambda n, c: (0, 2 * n * nct + c)),
            pl.BlockSpec((fo, tc), lambda n, c: (0, (2 * n + 1) * nct + c)),
            pl.BlockSpec((1, tc), lambda n, c: (0, c)),
            const((L, L)), const((L, L)),
        ],
        out_specs=(kspec, kspec, kspec),
        scratch_shapes=[pltpu.VMEM((L, fo), F32)],
        compiler_params=_cparams("arbitrary", "arbitrary"),
        name="hyena_filter",
    )(z, w1, row(f_b1), f_w2, row(f_b2), row(f_freq), f_w3, f_w3, deltas, dft[0], dft[1])


def _hyena_kernel(L, nseq, uv_ref, ug0_ref, ug1_ref, cwv_ref, cwg0_ref, cwg1_ref, cbv_ref, cbg0_ref, cbg1_ref,
                  hb_ref, ka_ref, kb_ref, kc_ref, cos_ref, msin_ref, msint_ref, o_ref):
    row = lax.broadcasted_iota(jnp.int32, (L, uv_ref.shape[1]), 0)
    first, last = row == 0, row == L - 1
    seqs = [slice(i * L, (i + 1) * L) for i in range(nseq)]

    def short_conv(u_ref, rows, w_ref, b_ref):
        u = u_ref[rows, :].astype(F32)
        prev = jnp.where(first, 0.0, pltpu.roll(u, 1, 0))
        nxt = jnp.where(last, 0.0, pltpu.roll(u, L - 1, 0))
        return b_ref[...] + prev * w_ref[0:1, :] + u * w_ref[1:2, :] + nxt * w_ref[2:3, :]

    z = [short_conv(uv_ref, r, cwv_ref, cbv_ref) for r in seqs]
    gates = [(short_conv(ug0_ref, r, cwg0_ref, cbg0_ref), short_conv(ug1_ref, r, cwg1_ref, cbg1_ref))
             for r in seqs]
    for n in range(HY_ORDER):
        zb = [zi.astype(BF16) for zi in z]
        zre = [_dot(cos_ref[...], b) for b in zb]
        zim = [_dot(msin_ref[...], b) for b in zb]
        ka, kb, kc = ka_ref[n], kb_ref[n], kc_ref[n]
        yre = [(re * ka - im * kb).astype(BF16) for re, im in zip(zre, zim)]
        yim = [(re * kb + im * kc).astype(BF16) for re, im in zip(zre, zim)]
        conv = [_dot(cos_ref[...], a) + _dot(msint_ref[...], b) for a, b in zip(yre, yim)]
        z = [g[n] * (cv + hb_ref[n:n + 1, :] * zi) for g, cv, zi in zip(gates, conv, z)]
    for r, zi in zip(seqs, z):
        o_ref[r, :] = zi.astype(o_ref.dtype)


def _hyena(grp, p, tc, nseq, spectra, dft, conv_w, conv_b, h_bias):
    L = grp.L
    nct = HY_W // tc
    ka, kb, kc = spectra
    u_spec = lambda j: pl.BlockSpec((nseq * L, tc), lambda c, b: (b, j * nct + c))
    cw_spec = lambda j: pl.BlockSpec((SHORT_CONV, tc), lambda c, b: (0, j * nct + c))
    cb_spec = lambda j: pl.BlockSpec((1, tc), lambda c, b: (0, j * nct + c))
    k_spec = pl.BlockSpec((HY_ORDER, L, tc), lambda c, b: (0, 0, c))
    m_spec = pl.BlockSpec((L, L), lambda c, b: (0, 0))
    cb = conv_b.reshape(1, -1)
    return pl.pallas_call(
        functools.partial(_hyena_kernel, L, nseq),
        out_shape=jax.ShapeDtypeStruct((grp.rows, HY_W), BF16),
        grid=(nct, grp.nb // nseq),
        in_specs=[
            u_spec(0), u_spec(1), u_spec(2), cw_spec(0), cw_spec(1), cw_spec(2),
            cb_spec(0), cb_spec(1), cb_spec(2),
            pl.BlockSpec((HY_ORDER, tc), lambda c, b: (0, c)),
            k_spec, k_spec, k_spec, m_spec, m_spec, m_spec,
        ],
        out_specs=pl.BlockSpec((nseq * L, tc), lambda c, b: (b, c)),
        compiler_params=_cparams("parallel", "arbitrary"),
        name="hyena",
    )(p, p, p, conv_w, conv_w, conv_w, cb, cb, cb, h_bias, ka, kb, kc, *dft)


def _gqa_kernel(L, nseq, window, has_ctx, has_rope, emit_kv, *refs):
    it = iter(refs)
    q_ref, k_ref, v_ref, sink_ref = next(it), next(it), next(it), next(it)
    if has_ctx:
        kc_ref, vc_ref = next(it), next(it)
    if has_rope:
        cos_ref, sa_ref, sb_ref = next(it), next(it), next(it)
    o_ref = next(it)
    if emit_kv:
        kn_ref, vn_ref = next(it), next(it)

    qb = HEAD_DIM
    nf = HEAD_DIM // 4
    gw = ATT_GROUP * HEAD_DIM
    hps = k_ref.shape[1] // HEAD_DIM
    sink_row = sink_ref[...]
    sink_lane = lax.broadcasted_iota(jnp.int32, sink_row.shape, 1)
    band_bias = {}

    def bias_for(lo, hi, i):
        key = (lo - i * qb, hi - lo)
        if key not in band_bias:
            shape = (ATT_GROUP * qb, hi - lo)
            rel = key[0] + lax.broadcasted_iota(jnp.int32, shape, 1) - lax.broadcasted_iota(jnp.int32, shape, 0) % qb
            band_bias[key] = jnp.where(jnp.abs(rel) <= window, 0.0, -1e30)
        return band_bias[key]

    for sq in range(nseq):
        r0 = sq * L
        for hk in range(hps):
            head = pl.program_id(1) * hps + hk
            k = k_ref[r0:r0 + L, hk * HEAD_DIM:(hk + 1) * HEAD_DIM]
            v = v_ref[r0:r0 + L, hk * HEAD_DIM:(hk + 1) * HEAD_DIM]
            if emit_kv:
                kn_ref[sq, hk] = k
                vn_ref[sq, hk] = v
            if has_rope:
                k = _rope(k, cos_ref[...], sa_ref[...], sb_ref[...], nf)
            k = k.astype(BF16)
            v = jnp.concatenate([v.astype(BF16), jnp.ones((L, HEAD_DIM), BF16)], axis=1)
            if has_ctx:
                kc = kc_ref[hk].astype(BF16)
                vc = jnp.concatenate([vc_ref[hk].astype(BF16), jnp.ones((PAST_LEN, HEAD_DIM), BF16)], axis=1)
            sinks = [jnp.sum(jnp.where(sink_lane == head * ATT_GROUP + g, sink_row, 0.0), axis=1, keepdims=True)
                     for g in range(ATT_GROUP)]
            sink = jnp.concatenate([jnp.broadcast_to(s, (qb, 1)) for s in sinks], axis=0)

            for i in range(L // qb):
                rows = slice(i * qb, (i + 1) * qb)
                out_rows = slice(r0 + i * qb, r0 + (i + 1) * qb)
                qs = []
                for g in range(ATT_GROUP):
                    qg = q_ref[out_rows, hk * gw + g * HEAD_DIM:hk * gw + (g + 1) * HEAD_DIM].astype(F32)
                    if has_rope:
                        qg = _rope(qg, cos_ref[rows, :], sa_ref[rows, :], sb_ref[rows, :], nf)
                    qs.append((qg * ATT_SCALE).astype(BF16))
                q = jnp.concatenate(qs, axis=0)
                if window is None:
                    lo, hi = 0, L
                else:
                    lo, hi = max(0, (i - 1) * qb), min(L, (i + 2) * qb)
                s = _dot_nt(q, k[lo:hi])
                if window is not None:
                    s = s + bias_for(lo, hi, i)
                m = jnp.maximum(jnp.max(s, axis=-1, keepdims=True), sink)
                if has_ctx:
                    sc = _dot_nt(q, kc)
                    m = jnp.maximum(m, jnp.max(sc, axis=-1, keepdims=True))
                oa = _dot(jnp.exp(s - m).astype(BF16), v[lo:hi])
                if has_ctx:
                    oa = oa + _dot(jnp.exp(sc - m).astype(BF16), vc)
                o = oa[:, :HEAD_DIM] / (oa[:, HEAD_DIM:] + jnp.exp(sink - m))
                for g in range(ATT_GROUP):
                    o_ref[out_rows, hk * gw + g * HEAD_DIM:hk * gw + (g + 1) * HEAD_DIM] = (
                        o[g * qb:(g + 1) * qb].astype(o_ref.dtype))


def _gqa(grp, p, kv, sink, hps, nseq=1, window=None, ctx=None, rope=None, emit_kv=False):
    assert nseq == 1 or (ctx is None and rope is None)
    L = grp.L
    rows = nseq * L
    qw = hps * ATT_GROUP * HEAD_DIM
    kw = hps * HEAD_DIM
    nh = ATT_KV_HEADS // hps
    in_specs = [
        pl.BlockSpec((rows, qw), lambda b, h: (b, 3 * HY_W // qw + h)),
        pl.BlockSpec((rows, kw), lambda b, h: (b, h)),
        pl.BlockSpec((rows, kw), lambda b, h: (b, nh + h)),
        pl.BlockSpec((1, ATT_HEADS), lambda b, h: (0, 0)),
    ]
    args = [p, kv, kv, sink.reshape(1, ATT_HEADS)]
    if ctx is not None:
        kc, vc, e = ctx
        spec = pl.BlockSpec((None, None, hps, PAST_LEN, HEAD_DIM), lambda b, h: (b, e, h, 0, 0))
        in_specs += [spec, spec]
        args += [kc, vc]
    if rope is not None:
        in_specs += [pl.BlockSpec((L, HEAD_DIM), lambda b, h: (0, 0))] * 3
        args += list(rope)
    out_shape = [jax.ShapeDtypeStruct((grp.rows, ATT_HEADS * HEAD_DIM), BF16)]
    out_specs = [pl.BlockSpec((rows, qw), lambda b, h: (b, h))]
    if emit_kv:
        kv_shape = jax.ShapeDtypeStruct((grp.nb, 1, ATT_KV_HEADS, L, HEAD_DIM), F32)
        kv_spec = pl.BlockSpec((nseq, None, hps, L, HEAD_DIM), lambda b, h: (b, 0, h, 0, 0))
        out_shape += [kv_shape, kv_shape]
        out_specs += [kv_spec, kv_spec]
    return pl.pallas_call(
        functools.partial(_gqa_kernel, L, nseq, window, ctx is not None, rope is not None, emit_kv),
        out_shape=tuple(out_shape),
        grid=(grp.nb // nseq, nh),
        in_specs=in_specs,
        out_specs=tuple(out_specs),
        compiler_params=_cparams("parallel", "parallel"),
        name="gqa",
    )(*args)


def _chunk_cumsum(x, reverse):
    n = x.shape[0]
    pos = lax.broadcasted_iota(jnp.int32, x.shape, 0) % CHUNK
    s = 1
    while s < CHUNK:
        if reverse:
            x = x + jnp.where(pos < CHUNK - s, pltpu.roll(x, n - s, 0), 0.0)
        else:
            x = x + jnp.where(pos >= s, pltpu.roll(x, s, 0), 0.0)
        s *= 2
    return x


def _hgrn_kernel(L, has_state, emit_state, *refs):
    it = iter(refs)
    q_ref, ff_ref, fb_ref, i_ref, g_ref, lb_ref, norm_ref = (next(it) for _ in range(7))
    if has_state:
        s0_ref = next(it)
    o_ref = next(it)
    if emit_state:
        s_ref = next(it)
    acc_scr = next(it)

    a = lb_ref[...]
    e = jnp.exp(a - jnp.max(a, axis=0, keepdims=True))
    lb_all = e[1] / (e[0] + e[1])
    ci = lax.broadcasted_iota(jnp.int32, (CHUNK, CHUNK), 0)
    cj = lax.broadcasted_iota(jnp.int32, (CHUNK, CHUNK), 1)
    n_chunks = L // CHUNK
    chunks = [slice(n * CHUNK, (n + 1) * CHUNK) for n in range(n_chunks)]

    for hh in range(q_ref.shape[1] // HEAD_DIM):
        hc = slice(hh * HEAD_DIM, (hh + 1) * HEAD_DIM)
        q = _silu(q_ref[:, hc].astype(F32))
        for d, fz_ref in enumerate((ff_ref, fb_ref)):
            lbd = lb_all[d:d + 1, hc]
            keep = (cj <= ci) if d == 0 else (cj >= ci)
            qds, atts, decay, own = [], [], [], []
            for rows in chunks:
                f = lbd + (1.0 - lbd) * jax.nn.sigmoid(fz_ref[rows, hc].astype(F32))
                b = _chunk_cumsum(jnp.log(f), reverse=(d == 1))
                qd = (q[rows] * jnp.exp(b)).astype(BF16)
                kd32 = (1.0 - f) * jnp.exp(-b)
                dc = jnp.exp(b[CHUNK - 1:CHUNK] if d == 0 else b[0:1])
                qds.append(qd)
                decay.append(dc)
                atts.append(jnp.where(keep, _dot_nt(qd, kd32.astype(BF16)), 0.0).astype(BF16))
                own.append(_dot_tn(i_ref[rows, hc], (kd32 * dc).astype(BF16)))
            st = s0_ref[d, hh].T if has_state else jnp.zeros((HEAD_DIM, HEAD_DIM), F32)
            entering = [None] * n_chunks
            for n in (range(n_chunks) if d == 0 else range(n_chunks - 1, -1, -1)):
                entering[n] = st.astype(BF16)
                st = st * decay[n] + own[n]
            if emit_state:
                s_ref[d, hh] = st.T
            for n, rows in enumerate(chunks):
                o = _dot(atts[n], i_ref[rows, hc]) + _dot_nt(qds[n], entering[n])
                if d == 0:
                    acc_scr[rows, :] = o
                else:
                    acc_scr[rows, :] += o
        o = _rms(acc_scr[...]) * norm_ref[...] * _silu(g_ref[:, hc].astype(F32))
        o_ref[:, hc] = o.astype(o_ref.dtype)


def _hgrn(grp, p, hg_lb, norm_g, hps, state=None, emit_state=False):
    L = grp.L
    hw = hps * HEAD_DIM
    nh = HG_HEADS // hps
    col = lambda j: pl.BlockSpec((L, hw), lambda b, h: (b, j * nh + h))
    in_specs = [col(0), col(1), col(2), col(3), col(4),
                pl.BlockSpec((hg_lb.shape[0], 2, hw), lambda b, h: (0, 0, h)),
                pl.BlockSpec((1, HEAD_DIM), lambda b, h: (0, 0))]
    args = [p, p, p, p, p, hg_lb, norm_g.reshape(1, HEAD_DIM)]
    st_spec = lambda o: pl.BlockSpec((None, None, 2, hps, HEAD_DIM, HEAD_DIM), lambda b, h: (b, o, 0, h, 0, 0))
    if state is not None:
        s0, o = state
        in_specs.append(st_spec(o))
        args.append(s0)
    out_shape = [jax.ShapeDtypeStruct((grp.rows, HG_W), BF16)]
    out_specs = [pl.BlockSpec((L, hw), lambda b, h: (b, h))]
    if emit_state:
        out_shape.append(jax.ShapeDtypeStruct((grp.nb, 1, 2, HG_HEADS, HEAD_DIM, HEAD_DIM), F32))
        out_specs.append(st_spec(0))
    return pl.pallas_call(
        functools.partial(_hgrn_kernel, L, state is not None, emit_state),
        out_shape=tuple(out_shape),
        grid=(grp.nb, nh),
        in_specs=in_specs,
        out_specs=tuple(out_specs),
        scratch_shapes=[pltpu.VMEM((L, HEAD_DIM), F32)],
        compiler_params=_cparams("parallel", "parallel"),
        name="hgrn",
    )(*args)


MLA_QW = 256
KR_W = 128
OD_TAIL = 1024


def _mla_prep_kernel(has_rope, emit_kr, *refs):
    it = iter(refs)
    ql_ref, kvl_ref, kr_ref, qn_ref, kvn_ref, wq_ref = (next(it) for _ in range(6))
    if has_rope:
        kc_ref, ksa_ref, ksb_ref = (next(it) for _ in range(3))
    q_ref, ckv_ref, kro_ref = next(it), next(it), next(it)
    if emit_kr:
        krn_ref = next(it)
        krn_ref[...] = kr_ref[:, :ROPE]

    nf = ROPE // 4
    qn = (_rms(ql_ref[...]) * qn_ref[...]).astype(BF16)
    q = _dot(qn, wq_ref[...])
    for h in range(MLA_HEADS):
        nope = slice(h * MLA_QW, h * MLA_QW + NOPE)
        rot = slice(h * MLA_QW + NOPE, (h + 1) * MLA_QW)
        q_ref[:, nope] = q[:, nope].astype(q_ref.dtype)
        qr = q[:, rot]
        if has_rope:
            qr = _rope(qr, kc_ref[...], ksa_ref[...], ksb_ref[...], nf)
        q_ref[:, rot] = qr.astype(q_ref.dtype)
    ckv_ref[...] = _rms(kvl_ref[...]) * kvn_ref[...]
    kr = kr_ref[...]
    if has_rope:
        kr = _rope(kr, kc_ref[...], ksa_ref[...], ksb_ref[...], nf)
    kro_ref[...] = kr.astype(kro_ref.dtype)


def _mla_prep(grp, tail, q_norm, kv_norm, wq, rope=None, emit_kr=False):
    tm = min(512, grp.L)
    per = grp.L // tm
    n_rows = grp.rows
    in_specs = [
        pl.BlockSpec((tm, Q_LORA), lambda i: (i, 0)),
        pl.BlockSpec((tm, KV_LORA), lambda i: (i, Q_LORA // KV_LORA)),
        pl.BlockSpec((tm, KR_W), lambda i: (i, (Q_LORA + KV_LORA) // KR_W)),
        pl.BlockSpec((1, Q_LORA), lambda i: (0, 0)),
        pl.BlockSpec((1, KV_LORA), lambda i: (0, 0)),
        pl.BlockSpec((Q_LORA, MLA_HEADS * MLA_QW), lambda i: (0, 0)),
    ]
    args = [tail, tail, tail, q_norm.reshape(1, Q_LORA), kv_norm.reshape(1, KV_LORA), wq]
    if rope is not None:
        in_specs += [pl.BlockSpec((tm, KR_W), lambda i: (i % per, 0))] * 3
        args += list(rope)
    out_shape = [jax.ShapeDtypeStruct((n_rows, MLA_HEADS * MLA_QW), BF16),
                 jax.ShapeDtypeStruct((n_rows, KV_LORA), F32),
                 jax.ShapeDtypeStruct((n_rows, KR_W), BF16)]
    out_specs = [pl.BlockSpec((tm, MLA_HEADS * MLA_QW), lambda i: (i, 0)),
                 pl.BlockSpec((tm, KV_LORA), lambda i: (i, 0)),
                 pl.BlockSpec((tm, KR_W), lambda i: (i, 0))]
    if emit_kr:
        out_shape.append(jax.ShapeDtypeStruct((n_rows, ROPE), F32))
        out_specs.append(pl.BlockSpec((tm, ROPE), lambda i: (i, 0)))
    return pl.pallas_call(
        functools.partial(_mla_prep_kernel, rope is not None, emit_kr),
        out_shape=tuple(out_shape),
        grid=(n_rows // tm,),
        in_specs=in_specs,
        out_specs=tuple(out_specs),
        compiler_params=_cparams("parallel"),
        name="mla_prep",
    )(*args)


def _mla_attn_kernel(L, has_ctx, *refs):
    it = iter(refs)
    q_ref, ckv_ref, kr_ref, wkv_ref = (next(it) for _ in range(4))
    if has_ctx:
        cckv_ref, ckr_ref = next(it), next(it)
    o_ref = next(it)

    ckv = ckv_ref[...].astype(BF16)
    kr = kr_ref[...]
    if has_ctx:
        ckv = jnp.concatenate([ckv, cckv_ref[...].astype(BF16)], axis=0)
        kr = jnp.concatenate([kr, ckr_ref[...].astype(BF16)], axis=0)
    ones = jnp.ones((ckv.shape[0], V_DIM), BF16)
    qb = min(L, 256)
    for hh in range(q_ref.shape[1] // MLA_QW):
        qc = slice(hh * MLA_QW, (hh + 1) * MLA_QW)
        oc = slice(hh * V_DIM, (hh + 1) * V_DIM)
        kv = _dot(ckv, wkv_ref[:, qc])
        kh = jnp.concatenate([kv[:, :NOPE].astype(BF16), kr], axis=1)
        vh = jnp.concatenate([kv[:, NOPE:].astype(BF16), ones], axis=1)
        for i in range(L // qb):
            rows = slice(i * qb, (i + 1) * qb)
            s = _dot_nt(q_ref[rows, qc], kh) * MLA_SCALE
            m = jnp.max(s, axis=-1, keepdims=True)
            oa = _dot(jnp.exp(s - m).astype(BF16), vh)
            o_ref[rows, oc] = (oa[:, :V_DIM] / oa[:, V_DIM:]).astype(o_ref.dtype)


def _mla_attn(grp, q, ckv, kr, wkv, hps, ctx=None):
    L = grp.L
    in_specs = [
        pl.BlockSpec((L, hps * MLA_QW), lambda b, h: (b, h)),
        pl.BlockSpec((L, KV_LORA), lambda b, h: (b, 0)),
        pl.BlockSpec((L, KR_W), lambda b, h: (b, 0)),
        pl.BlockSpec((KV_LORA, hps * (NOPE + V_DIM)), lambda b, h: (0, h)),
    ]
    args = [q, ckv, kr, wkv]
    if ctx is not None:
        cckv, ckr, o = ctx
        in_specs += [pl.BlockSpec((None, None, PAST_LEN, KV_LORA), lambda b, h: (b, o, 0, 0)),
                     pl.BlockSpec((None, None, PAST_LEN, KR_W), lambda b, h: (b, o, 0, 0))]
        args += [cckv, ckr]
    return pl.pallas_call(
        functools.partial(_mla_attn_kernel, L, ctx is not None),
        out_shape=jax.ShapeDtypeStruct((grp.rows, MLA_HEADS * V_DIM), BF16),
        grid=(grp.nb, MLA_HEADS // hps),
        in_specs=in_specs,
        out_specs=pl.BlockSpec((L, hps * V_DIM), lambda b, h: (b, h)),
        compiler_params=_cparams("parallel", "parallel"),
        name="mla_attn",
    )(*args)


def kernel(x_prompt, x_sample, c, c_ctx, cache_attn_k, cache_attn_v, cache_mla_ckv, cache_mla_krope, state_hgrn, mod_w, mod_b, norm_g, ffn_wg, ffn_wu, ffn_wd, ev_w_in, ev_w_out, hy_conv_w, hy_conv_b, hy_f_w1, hy_f_b1, hy_f_w2, hy_f_b2, hy_f_w3, hy_f_freq, hy_bias, attn_sink, od_w_in, od_w_out, hg_lb, hg_norm, mla_q_norm, mla_w_qb, mla_kv_norm, mla_w_kvb):
    depth = mod_w.shape[0]
    groups = (PROMPT, LATENT)
    xs = [x_prompt.reshape(PROMPT.rows, D_MODEL), x_sample.reshape(LATENT.rows, D_MODEL)]
    cvec = jnp.concatenate([c_ctx[None, :], c, jnp.zeros((MOD_ROWS - 1 - DEC_BATCH, D_MODEL), F32)], axis=0)
    mods_all = _modulation(cvec, mod_w, mod_b)

    wg, wu, wd = ffn_wg.astype(BF16), ffn_wu.astype(BF16), ffn_wd.astype(BF16)
    hy_tc = {SEQ: 512, DEC_SEQ: 256}
    hy_nseq = {SEQ: 4, DEC_SEQ: 2}

    new_k = new_v = new_ckv = new_kr = new_s = None
    for l in range(depth):
        mods = mods_all[l]
        xs = [_ffn(grp, x, mods, 0, norm_g[l, 0], norm_g[l, 1], wg, wu, wd, l, 0) for grp, x in zip(groups, xs)]
        if l % 2 == 0:
            e = l // 2
            w_in = ev_w_in[e][:, :EV_IN - TN_EVEN].astype(BF16)
            w_in_last = ev_w_in[e][:, EV_IN - TN_EVEN:].astype(BF16)
            w_out = ev_w_out[e].astype(BF16)
            kv_cols = 2 * ATT_KV_HEADS * HEAD_DIM
            mix = []
            for grp, x in zip(groups, xs):
                p, kv = _inproj(grp, x, mods, norm_g[l, 2], w_in, w_in_last, kv_cols)
                dft = _dft_matrices(grp.L)
                tc = hy_tc[grp.L]
                spectra = _hyena_spectra(grp.L, tc, dft, hy_f_w1[e], hy_f_b1[e], hy_f_w2[e], hy_f_b2[e],
                                         hy_f_w3[e], hy_f_freq[e])
                hy = _hyena(grp, p, tc, hy_nseq[grp.L], spectra, dft, hy_conv_w[e], hy_conv_b[e], hy_bias[e])
                if grp.latent:
                    rope = _rope_tables(grp.L, HEAD_DIM, 0, HEAD_DIM)
                    (att,) = _gqa(grp, p, kv, attn_sink[e], 1, window=WINDOW,
                                  ctx=(cache_attn_k, cache_attn_v, e), rope=rope)
                else:
                    att, new_k, new_v = _gqa(grp, p, kv, attn_sink[e], 2, nseq=2, emit_kv=True)
                mix.append((hy, att))
        else:
            o = l // 2
            n_main = OD_IN_PAD - TN_ODD
            w_in = od_w_in[o][:, :n_main].astype(BF16)
            w_in_last = jnp.pad(od_w_in[o][:, n_main:].astype(BF16), ((0, 0), (0, OD_IN_PAD - OD_IN)))
            w_out = od_w_out[o].astype(BF16)
            wq = mla_w_qb[o].reshape(Q_LORA, MLA_HEADS, NOPE + ROPE)
            wq = jnp.pad(wq, ((0, 0), (0, 0), (0, MLA_QW - NOPE - ROPE))).reshape(Q_LORA, -1).astype(BF16)
            wkv = mla_w_kvb[o].astype(BF16)
            mix = []
            for grp, x in zip(groups, xs):
                p, tail = _inproj(grp, x, mods, norm_g[l, 2], w_in, w_in_last, OD_TAIL)
                if grp.latent:
                    (hg,) = _hgrn(grp, p, hg_lb, hg_norm[o], 2, state=(state_hgrn, o))
                    rope = _rope_tables(grp.L, ROPE, 0, KR_W)
                    q, ckv, kr = _mla_prep(grp, tail, mla_q_norm[o], mla_kv_norm[o], wq, rope=rope)
                    ckr = jnp.pad(cache_mla_krope, ((0, 0), (0, 0), (0, 0), (0, KR_W - ROPE)))
                    att = _mla_attn(grp, q, ckv, kr, wkv, 4, ctx=(cache_mla_ckv, ckr, o))
                else:
                    hg, new_s = _hgrn(grp, p, hg_lb, hg_norm[o], 4, emit_state=True)
                    q, ckv, kr, kr_raw = _mla_prep(grp, tail, mla_q_norm[o], mla_kv_norm[o], wq, emit_kr=True)
                    att = _mla_attn(grp, q, ckv, kr, wkv, 8)
                    new_ckv = ckv.reshape(BATCH, 1, SEQ, KV_LORA)
                    new_kr = kr_raw.reshape(BATCH, 1, SEQ, ROPE)
                mix.append((hg, att))
        xs = [_outproj(grp, a, b, w_out, x, mods, norm_g[l, 3])
              for grp, x, (a, b) in zip(groups, xs, mix)]
        xs = [_ffn(grp, x, mods, 2, norm_g[l, 4], norm_g[l, 5], wg, wu, wd, l, 1) for grp, x in zip(groups, xs)]

    y_prompt = xs[0].reshape(BATCH, SEQ, D_MODEL)
    y_sample = xs[1].reshape(DEC_BATCH, DEC_SEQ, D_MODEL)
    return (y_prompt, y_sample, new_k, new_v, new_ckv, new_kr, new_s)
```

```python
import functools
import math
from typing import NamedTuple

import jax
import jax.numpy as jnp
from jax import lax
from jax.experimental import pallas as pl
from jax.experimental.pallas import tpu as pltpu

D_MODEL = 2048
BATCH = 16
SEQ = 256
DEC_BATCH = 8
DEC_SEQ = 1024
PAST_LEN = 512
GRID_W = 64
HEAD_DIM = 128
HY_W = 1024
HY_ORDER = 2
SHORT_CONV = 3
POS_EMB = 33
FILTER_ORDER = 64
HY_FAST_DECAY = 0.3
HY_SLOW_DECAY = 1.5
HY_TARGET = 1e-2
ATT_HEADS = 8
ATT_KV_HEADS = 2
ATT_GROUP = 4
WINDOW = 128
EV_IN = 3 * HY_W + (ATT_HEADS + 2 * ATT_KV_HEADS) * HEAD_DIM
HG_W = 1024
HG_HEADS = 8
CHUNK = 64
Q_LORA = 512
KV_LORA = 256
NOPE = 128
ROPE = 64
V_DIM = 128
MLA_HEADS = 8
OD_IN = 5 * HG_W + Q_LORA + KV_LORA + ROPE
OD_IN_PAD = 6144
D_FF = 5632
MACARON_W = 0.5
N_MOD = 9
ROPE_BASE = 10000.0
EPS = 1e-6
ATT_SCALE = HEAD_DIM ** -0.5
MLA_SCALE = (NOPE + ROPE) ** -0.5

MOD_ROWS = 16

V7X_VMEM_LIMIT = 56 * 1024 * 1024
TM = 512
TM_FFN = 1024
BIG_TILE_VMEM_LIMIT = 60 * 1024 * 1024
TM_IN = 1024
RC = 512
TF = 512
TN_EVEN = 1536
TN_ODD = 1024
MOD_TN = 2048

BF16 = jnp.bfloat16
F32 = jnp.float32
HIGHEST = lax.Precision.HIGHEST


class Group(NamedTuple):
    nb: int
    L: int
    latent: bool

    @property
    def rows(self):
        return self.nb * self.L

    def mod_row(self, i, tm):
        return 1 + i // (self.L // tm) if self.latent else 0


PROMPT = Group(BATCH, SEQ, False)
LATENT = Group(DEC_BATCH, DEC_SEQ, True)


def _cparams(*sem):
    return pltpu.CompilerParams(dimension_semantics=sem, vmem_limit_bytes=V7X_VMEM_LIMIT)


def _rms(x):
    return x * lax.rsqrt(jnp.mean(x * x, axis=-1, keepdims=True) + EPS)


def _silu(x):
    return x * jax.nn.sigmoid(x)


def _dot(a, b):
    return jnp.dot(a, b, preferred_element_type=F32)


def _dot_nt(a, b):
    return lax.dot_general(a, b, (((1,), (1,)), ((), ())), preferred_element_type=F32)


def _dot_tn(a, b):
    return lax.dot_general(a, b, (((0,), (0,)), ((), ())), preferred_element_type=F32)


def _row_chunks(n_rows, body, rc=RC):
    def step(c, carry):
        body(pl.ds(pl.multiple_of(c * rc, rc), rc))
        return carry
    lax.fori_loop(0, n_rows // rc, step, 0, unroll=True)


def _modulated_norm(x_ref, h_scr, gain, shift):
    def body(rows):
        h_scr[rows, :] = (_rms(x_ref[rows, :]) * gain + shift).astype(BF16)
    _row_chunks(x_ref.shape[0], body)


def _mod_kernel(c_ref, w_ref, b_ref, o_ref):
    s = _silu(c_ref[...]).astype(BF16)
    o_ref[...] = _dot(s, w_ref[...].astype(BF16)) + b_ref[...]


def _modulation(cvec, mod_w, mod_b):
    depth = mod_w.shape[0]
    n = N_MOD * D_MODEL
    out = pl.pallas_call(
        _mod_kernel,
        out_shape=jax.ShapeDtypeStruct((depth, MOD_ROWS, n), F32),
        grid=(depth, n // MOD_TN),
        in_specs=[
            pl.BlockSpec((MOD_ROWS, D_MODEL), lambda l, j: (0, 0)),
            pl.BlockSpec((None, D_MODEL, MOD_TN), lambda l, j: (l, 0, j)),
            pl.BlockSpec((None, 1, MOD_TN), lambda l, j: (l, 0, j)),
        ],
        out_specs=pl.BlockSpec((None, MOD_ROWS, MOD_TN), lambda l, j: (l, 0, j)),
        compiler_params=_cparams("parallel", "parallel"),
        name="modulation",
    )(cvec, mod_w, mod_b.reshape(depth, 1, n))
    return out.reshape(depth, MOD_ROWS, N_MOD, D_MODEL)


def _ffn_kernel(j, x_ref, mod_ref, gpre_ref, gpost_ref, wg_ref, wu_ref, wd_ref, o_ref, h_scr):
    f = pl.program_id(1)

    @pl.when(f == 0)
    def _():
        gain = gpre_ref[...] * (1.0 + mod_ref[3 * j + 1:3 * j + 2, :])
        _modulated_norm(x_ref, h_scr, gain, mod_ref[3 * j:3 * j + 1, :])

    def partial_down(rows):
        h = h_scr[rows, :]
        a = (_silu(_dot(h, wg_ref[...])) * _dot(h, wu_ref[...])).astype(BF16)
        return _dot(a, wd_ref[...])

    @pl.when(f == 0)
    def _():
        def first(rows):
            o_ref[rows, :] = partial_down(rows)
        _row_chunks(o_ref.shape[0], first)

    last = pl.num_programs(1) - 1

    @pl.when((f > 0) & (f < last))
    def _():
        def accumulate(rows):
            o_ref[rows, :] += partial_down(rows)
        _row_chunks(o_ref.shape[0], accumulate)

    @pl.when(f == last)
    def _():
        gain = (MACARON_W * mod_ref[3 * j + 2:3 * j + 3, :]) * gpost_ref[...]

        def finish(rows):
            acc = o_ref[rows, :] + partial_down(rows)
            o_ref[rows, :] = x_ref[rows, :] + _rms(acc) * gain
        _row_chunks(o_ref.shape[0], finish, RC // 2)


def _ffn(grp, x, mods, j, g_pre, g_post, wg, wu, wd, l, s):
    return pl.pallas_call(
        functools.partial(_ffn_kernel, j),
        out_shape=jax.ShapeDtypeStruct((grp.rows, D_MODEL), F32),
        grid=(grp.rows // TM_FFN, D_FF // TF),
        in_specs=[
            pl.BlockSpec((TM_FFN, D_MODEL), lambda i, f: (i, 0)),
            pl.BlockSpec((None, N_MOD, D_MODEL), lambda i, f: (grp.mod_row(i, TM_FFN), 0, 0)),
            pl.BlockSpec((1, D_MODEL), lambda i, f: (0, 0)),
            pl.BlockSpec((1, D_MODEL), lambda i, f: (0, 0)),
            pl.BlockSpec((None, None, D_MODEL, TF), lambda i, f: (l, s, 0, f)),
            pl.BlockSpec((None, None, D_MODEL, TF), lambda i, f: (l, s, 0, f)),
            pl.BlockSpec((None, None, TF, D_MODEL), lambda i, f: (l, s, f, 0)),
        ],
        out_specs=pl.BlockSpec((TM_FFN, D_MODEL), lambda i, f: (i, 0)),
        scratch_shapes=[pltpu.VMEM((TM_FFN, D_MODEL), BF16)],
        compiler_params=pltpu.CompilerParams(dimension_semantics=("parallel", "arbitrary"),
                                             vmem_limit_bytes=BIG_TILE_VMEM_LIMIT),
        name="ffn",
    )(x, mods, g_pre.reshape(1, D_MODEL), g_post.reshape(1, D_MODEL), wg, wu, wd)


def _inproj_kernel(w_transposed, x_ref, mod_ref, g_ref, w_ref, wl_ref, o_ref, t_ref, h_scr):
    k = pl.program_id(1)
    last = pl.num_programs(1) - 1

    def project(weights_ref):
        y = (_dot_nt if w_transposed else _dot)(h_scr[...], weights_ref[...])
        o_ref[...] = y.astype(o_ref.dtype)
        return y

    @pl.when(k == 0)
    def _():
        _modulated_norm(x_ref, h_scr, g_ref[...] * (1.0 + mod_ref[4:5, :]), mod_ref[3:4, :])
        project(w_ref)

    @pl.when((k > 0) & (k < last))
    def _():
        project(w_ref)

    @pl.when(k == last)
    def _():
        y = project(wl_ref)
        t_ref[...] = y[:, y.shape[1] - t_ref.shape[1]:]


def _inproj(grp, x, mods, g, w, w_last, n_tail, w_transposed=False):
    col = 0 if w_transposed else 1
    tn = w_last.shape[col]
    steps = w.shape[col] // tn + 1
    assert steps >= 2 and w.shape[col] % tn == 0
    tm = TM_IN
    if w_transposed:
        w_specs = [pl.BlockSpec((tn, D_MODEL), lambda i, k: (jnp.minimum(k, steps - 2), 0)),
                   pl.BlockSpec((tn, D_MODEL), lambda i, k: (0, 0))]
    else:
        w_specs = [pl.BlockSpec((D_MODEL, tn), lambda i, k: (0, jnp.minimum(k, steps - 2))),
                   pl.BlockSpec((D_MODEL, tn), lambda i, k: (0, 0))]
    return pl.pallas_call(
        functools.partial(_inproj_kernel, w_transposed),
        out_shape=(jax.ShapeDtypeStruct((grp.rows, steps * tn), BF16),
                   jax.ShapeDtypeStruct((grp.rows, n_tail), F32)),
        grid=(grp.rows // tm, steps),
        in_specs=[
            pl.BlockSpec((tm, D_MODEL), lambda i, k: (i, 0)),
            pl.BlockSpec((None, N_MOD, D_MODEL), lambda i, k: (grp.mod_row(i, tm), 0, 0)),
            pl.BlockSpec((1, D_MODEL), lambda i, k: (0, 0)),
            *w_specs,
        ],
        out_specs=(pl.BlockSpec((tm, tn), lambda i, k: (i, k)),
                   pl.BlockSpec((tm, n_tail), lambda i, k: (i, 0))),
        scratch_shapes=[pltpu.VMEM((tm, D_MODEL), BF16)],
        compiler_params=pltpu.CompilerParams(dimension_semantics=("parallel", "arbitrary"),
                                             vmem_limit_bytes=BIG_TILE_VMEM_LIMIT),
        name="inproj",
    )(x, mods, g.reshape(1, D_MODEL), w, w_last)


def _outproj_kernel(a_ref, b_ref, wa_ref, wb_ref, x_ref, mod_ref, g_ref, o_ref):
    y = _dot(a_ref[...], wa_ref[...]) + _dot(b_ref[...], wb_ref[...])
    o_ref[...] = x_ref[...] + _rms(y) * (mod_ref[5:6, :] * g_ref[...])


def _outproj(grp, a, b, w, x, mods, g):
    half = a.shape[1]
    tm = TM
    return pl.pallas_call(
        _outproj_kernel,
        out_shape=jax.ShapeDtypeStruct((grp.rows, D_MODEL), F32),
        grid=(grp.rows // tm,),
        in_specs=[
            pl.BlockSpec((tm, half), lambda i: (i, 0)),
            pl.BlockSpec((tm, half), lambda i: (i, 0)),
            pl.BlockSpec((half, D_MODEL), lambda i: (0, 0), pipeline_mode=pl.Buffered(1)),
            pl.BlockSpec((half, D_MODEL), lambda i: (1, 0), pipeline_mode=pl.Buffered(1)),
            pl.BlockSpec((tm, D_MODEL), lambda i: (i, 0)),
            pl.BlockSpec((None, N_MOD, D_MODEL), lambda i: (grp.mod_row(i, tm), 0, 0)),
            pl.BlockSpec((1, D_MODEL), lambda i: (0, 0)),
        ],
        out_specs=pl.BlockSpec((tm, D_MODEL), lambda i: (i, 0)),
        compiler_params=_cparams("parallel"),
        name="outproj",
    )(a, b, w, w, x, mods, g.reshape(1, D_MODEL))


def _rope_tables(L, rot_dim, lane0, width):
    half = rot_dim // 2
    inv = ROPE_BASE ** (-jnp.arange(0, half, 2, dtype=F32) / half)
    pos = jnp.arange(L)
    ang_r = (pos // GRID_W).astype(F32)[:, None] * inv
    ang_c = (pos % GRID_W).astype(F32)[:, None] * inv
    cr, sr, cc, sc = jnp.cos(ang_r), jnp.sin(ang_r), jnp.cos(ang_c), jnp.sin(ang_c)
    z = jnp.zeros_like(sr)
    cos = jnp.concatenate([cr, cr, cc, cc], axis=-1)
    sin_a = jnp.concatenate([-sr, z, -sc, z], axis=-1)
    sin_b = jnp.concatenate([z, sr, z, sc], axis=-1)
    pad = ((0, 0), (lane0, width - lane0 - rot_dim))
    return jnp.pad(cos, pad, constant_values=1.0), jnp.pad(sin_a, pad), jnp.pad(sin_b, pad)


def _rope(x, cos, sin_a, sin_b, nf):
    w = x.shape[-1]
    return x * cos + pltpu.roll(x, w - nf, 1) * sin_a + pltpu.roll(x, nf, 1) * sin_b


DFT_SPLIT = 32


def _dft_matrices(L):
    s = jnp.arange(L, dtype=jnp.int32)[None, :]

    def trig(k):
        ang = ((k[:, None] * s) % (2 * L)).astype(F32) * (math.pi / L)
        return jnp.cos(ang), jnp.sin(ang)

    c1, s1 = trig(jnp.arange(0, L, DFT_SPLIT, dtype=jnp.int32))
    c0, s0 = trig(jnp.arange(DFT_SPLIT, dtype=jnp.int32))
    cos = (c1[:, None, :] * c0[None] - s1[:, None, :] * s0[None]).reshape(L, L)
    sin = (s1[:, None, :] * c0[None] + c1[:, None, :] * s0[None]).reshape(L, L)
    nyq = (1 - 2 * (jnp.arange(L, dtype=jnp.int32) % 2)).astype(F32)
    k = jnp.arange(L, dtype=jnp.int32)[:, None]
    msin = jnp.where(k == 0, nyq[None, :], -sin)
    msin_t = jnp.where(s == 0, nyq[:, None], -sin)
    return cos.astype(BF16), msin.astype(BF16), msin_t.astype(BF16)


def _filter_kernel(L, z_ref, w1_ref, b1_ref, w2_ref, b2_ref, fr_ref, w3f_ref, w3b_ref, dl_ref, cos_ref, msin_ref,
                   ka_ref, kb_ref, kc_ref, h_scr):
    z = z_ref[...]

    @pl.when((pl.program_id(0) == 0) & (pl.program_id(1) == 0))
    def _():
        fr = fr_ref[...]
        h1 = jnp.sin(fr * (jnp.dot(z, w1_ref[...], precision=HIGHEST, preferred_element_type=F32) + b1_ref[...]))
        h_scr[...] = jnp.sin(
            fr * (jnp.dot(h1, w2_ref[...], precision=HIGHEST, preferred_element_type=F32) + b2_ref[...]))

    h = h_scr[...]
    decay = jnp.exp(-z[:, 0:1] * dl_ref[...])
    hf = jnp.dot(h, w3f_ref[...], precision=HIGHEST, preferred_element_type=F32) * decay
    hb = jnp.dot(h, w3b_ref[...], precision=HIGHEST, preferred_element_type=F32) * decay
    row = lax.broadcasted_iota(jnp.int32, hf.shape, 0)
    row0 = row == 0
    hb = jnp.where(row0, 0.0, hb)
    even = hf + hb
    re = _dot(cos_ref[...], even.astype(BF16))
    im = _dot(msin_ref[...], (hf - hb).astype(BF16))
    nyq = jnp.sum(jnp.where(row % 2 == 0, even, -even), axis=0, keepdims=True)
    sc = jnp.where(row0, 0.5 / L, 1.0 / L)
    ka_ref[...] = re * sc
    kb_ref[...] = jnp.where(row0, 0.0, im) * sc
    kc_ref[...] = jnp.where(row0, nyq, re) * sc


def _hyena_spectra(L, tc, dft, f_w1, f_b1, f_w2, f_b2, f_w3, f_freq):
    t = jnp.linspace(0.0, 1.0, L, dtype=F32)[:, None]
    bands = (POS_EMB - 1) // 2
    w = 2.0 * math.pi * jnp.arange(L, dtype=F32)[:, None] / L
    fb = jnp.linspace(1e-4, bands - 1, bands, dtype=F32)[None, :]
    z = jnp.concatenate([t, jnp.cos(fb * w), -jnp.sin(fb * w)], axis=-1)
    z = jnp.pad(z, ((0, 0), (0, 128 - POS_EMB)))
    w1 = jnp.pad(f_w1, ((0, 128 - POS_EMB), (0, 0)))
    deltas = jnp.abs(jnp.linspace(math.log(HY_TARGET) / HY_SLOW_DECAY,
                                  math.log(HY_TARGET) / HY_FAST_DECAY, HY_W, dtype=F32))[None, :]
    nct = HY_W // tc
    fo = FILTER_ORDER
    row = lambda a: a.reshape(1, fo)
    kshape = jax.ShapeDtypeStruct((HY_ORDER, L, HY_W), F32)
    kspec = pl.BlockSpec((None, L, tc), lambda n, c: (n, 0, c))
    const = lambda shape: pl.BlockSpec(shape, lambda n, c: (0, 0))
    return pl.pallas_call(
        functools.partial(_filter_kernel, L),
        out_shape=(kshape, kshape, kshape),
        grid=(HY_ORDER, nct),
        in_specs=[
            const((L, 128)), const((128, fo)), const((1, fo)), const((fo, fo)), const((1, fo)), const((1, fo)),
            pl.BlockSpec((fo, tc), lambda n, c: (0, 2 * n * nct + c)),
            pl.BlockSpec((fo, tc), lambda n, c: (0, (2 * n + 1) * nct + c)),
            pl.BlockSpec((1, tc), lambda n, c: (0, c)),
            const((L, L)), const((L, L)),
        ],
        out_specs=(kspec, kspec, kspec),
        scratch_shapes=[pltpu.VMEM((L, fo), F32)],
        compiler_params=_cparams("arbitrary", "arbitrary"),
        name="hyena_filter",
    )(z, w1, row(f_b1), f_w2, row(f_b2), row(f_freq), f_w3, f_w3, deltas, dft[0], dft[1])


def _hyena_kernel(L, nseq, uv_ref, ug0_ref, ug1_ref, cwv_ref, cwg0_ref, cwg1_ref, cbv_ref, cbg0_ref, cbg1_ref,
                  hb_ref, ka_ref, kb_ref, kc_ref, cos_ref, msin_ref, msint_ref, o_ref):
    row = lax.broadcasted_iota(jnp.int32, (L, uv_ref.shape[1]), 0)
    first, last = row == 0, row == L - 1
    seqs = [slice(i * L, (i + 1) * L) for i in range(nseq)]

    def short_conv(u_ref, rows, w_ref, b_ref):
        u = u_ref[rows, :].astype(F32)
        prev = jnp.where(first, 0.0, pltpu.roll(u, 1, 0))
        nxt = jnp.where(last, 0.0, pltpu.roll(u, L - 1, 0))
        return b_ref[...] + prev * w_ref[0:1, :] + u * w_ref[1:2, :] + nxt * w_ref[2:3, :]

    z = [short_conv(uv_ref, r, cwv_ref, cbv_ref) for r in seqs]
    gates = [(short_conv(ug0_ref, r, cwg0_ref, cbg0_ref), short_conv(ug1_ref, r, cwg1_ref, cbg1_ref))
             for r in seqs]
    for n in range(HY_ORDER):
        zb = [zi.astype(BF16) for zi in z]
        zre = [_dot(cos_ref[...], b) for b in zb]
        zim = [_dot(msin_ref[...], b) for b in zb]
        ka, kb, kc = ka_ref[n], kb_ref[n], kc_ref[n]
        yre = [(re * ka - im * kb).astype(BF16) for re, im in zip(zre, zim)]
        yim = [(re * kb + im * kc).astype(BF16) for re, im in zip(zre, zim)]
        conv = [_dot(cos_ref[...], a) + _dot(msint_ref[...], b) for a, b in zip(yre, yim)]
        z = [g[n] * (cv + hb_ref[n:n + 1, :] * zi) for g, cv, zi in zip(gates, conv, z)]
    for r, zi in zip(seqs, z):
        o_ref[r, :] = zi.astype(o_ref.dtype)


def _hyena(grp, p, tc, nseq, spectra, dft, conv_w, conv_b, h_bias):
    L = grp.L
    nct = HY_W // tc
    ka, kb, kc = spectra
    u_spec = lambda j: pl.BlockSpec((nseq * L, tc), lambda c, b: (b, j * nct + c))
    cw_spec = lambda j: pl.BlockSpec((SHORT_CONV, tc), lambda c, b: (0, j * nct + c))
    cb_spec = lambda j: pl.BlockSpec((1, tc), lambda c, b: (0, j * nct + c))
    k_spec = pl.BlockSpec((HY_ORDER, L, tc), lambda c, b: (0, 0, c))
    m_spec = pl.BlockSpec((L, L), lambda c, b: (0, 0))
    cb = conv_b.reshape(1, -1)
    return pl.pallas_call(
        functools.partial(_hyena_kernel, L, nseq),
        out_shape=jax.ShapeDtypeStruct((grp.rows, HY_W), BF16),
        grid=(nct, grp.nb // nseq),
        in_specs=[
            u_spec(0), u_spec(1), u_spec(2), cw_spec(0), cw_spec(1), cw_spec(2),
            cb_spec(0), cb_spec(1), cb_spec(2),
            pl.BlockSpec((HY_ORDER, tc), lambda c, b: (0, c)),
            k_spec, k_spec, k_spec, m_spec, m_spec, m_spec,
        ],
        out_specs=pl.BlockSpec((nseq * L, tc), lambda c, b: (b, c)),
        compiler_params=_cparams("parallel", "arbitrary"),
        name="hyena",
    )(p, p, p, conv_w, conv_w, conv_w, cb, cb, cb, h_bias, ka, kb, kc, *dft)


def _gqa_kernel(L, nseq, window, has_ctx, has_rope, emit_kv, *refs):
    it = iter(refs)
    q_ref, k_ref, v_ref, sink_ref = next(it), next(it), next(it), next(it)
    if has_ctx:
        kc_ref, vc_ref = next(it), next(it)
    if has_rope:
        cos_ref, sa_ref, sb_ref = next(it), next(it), next(it)
    o_ref = next(it)
    if emit_kv:
        kn_ref, vn_ref = next(it), next(it)

    qb = HEAD_DIM
    nf = HEAD_DIM // 4
    gw = ATT_GROUP * HEAD_DIM
    hps = k_ref.shape[1] // HEAD_DIM
    sink_row = sink_ref[...]
    sink_lane = lax.broadcasted_iota(jnp.int32, sink_row.shape, 1)
    band_bias = {}

    def bias_for(lo, hi, i):
        key = (lo - i * qb, hi - lo)
        if key not in band_bias:
            shape = (ATT_GROUP * qb, hi - lo)
            rel = key[0] + lax.broadcasted_iota(jnp.int32, shape, 1) - lax.broadcasted_iota(jnp.int32, shape, 0) % qb
            band_bias[key] = jnp.where(jnp.abs(rel) <= window, 0.0, -1e30)
        return band_bias[key]

    for sq in range(nseq):
        r0 = sq * L
        for hk in range(hps):
            head = pl.program_id(1) * hps + hk
            k = k_ref[r0:r0 + L, hk * HEAD_DIM:(hk + 1) * HEAD_DIM]
            v = v_ref[r0:r0 + L, hk * HEAD_DIM:(hk + 1) * HEAD_DIM]
            if emit_kv:
                kn_ref[sq, hk] = k
                vn_ref[sq, hk] = v
            if has_rope:
                k = _rope(k, cos_ref[...], sa_ref[...], sb_ref[...], nf)
            k = k.astype(BF16)
            v = jnp.concatenate([v.astype(BF16), jnp.ones((L, HEAD_DIM), BF16)], axis=1)
            if has_ctx:
                kc = kc_ref[hk].astype(BF16)
                vc = jnp.concatenate([vc_ref[hk].astype(BF16), jnp.ones((PAST_LEN, HEAD_DIM), BF16)], axis=1)
            sinks = [jnp.sum(jnp.where(sink_lane == head * ATT_GROUP + g, sink_row, 0.0), axis=1, keepdims=True)
                     for g in range(ATT_GROUP)]
            sink = jnp.concatenate([jnp.broadcast_to(s, (qb, 1)) for s in sinks], axis=0)

            for i in range(L // qb):
                rows = slice(i * qb, (i + 1) * qb)
                out_rows = slice(r0 + i * qb, r0 + (i + 1) * qb)
                qs = []
                for g in range(ATT_GROUP):
                    qg = q_ref[out_rows, hk * gw + g * HEAD_DIM:hk * gw + (g + 1) * HEAD_DIM].astype(F32)
                    if has_rope:
                        qg = _rope(qg, cos_ref[rows, :], sa_ref[rows, :], sb_ref[rows, :], nf)
                    qs.append((qg * ATT_SCALE).astype(BF16))
                q = jnp.concatenate(qs, axis=0)
                if window is None:
                    lo, hi = 0, L
                else:
                    lo, hi = max(0, (i - 1) * qb), min(L, (i + 2) * qb)
                s = _dot_nt(q, k[lo:hi])
                if window is not None:
                    s = s + bias_for(lo, hi, i)
                m = jnp.maximum(jnp.max(s, axis=-1, keepdims=True), sink)
                if has_ctx:
                    sc = _dot_nt(q, kc)
                    m = jnp.maximum(m, jnp.max(sc, axis=-1, keepdims=True))
                oa = _dot(jnp.exp(s - m).astype(BF16), v[lo:hi])
                if has_ctx:
                    oa = oa + _dot(jnp.exp(sc - m).astype(BF16), vc)
                o = oa[:, :HEAD_DIM] / (oa[:, HEAD_DIM:] + jnp.exp(sink - m))
                for g in range(ATT_GROUP):
                    o_ref[out_rows, hk * gw + g * HEAD_DIM:hk * gw + (g + 1) * HEAD_DIM] = (
                        o[g * qb:(g + 1) * qb].astype(o_ref.dtype))


def _gqa(grp, p, kv, sink, hps, nseq=1, window=None, ctx=None, rope=None, emit_kv=False):
    assert nseq == 1 or (ctx is None and rope is None)
    L = grp.L
    rows = nseq * L
    qw = hps * ATT_GROUP * HEAD_DIM
    kw = hps * HEAD_DIM
    nh = ATT_KV_HEADS // hps
    in_specs = [
        pl.BlockSpec((rows, qw), lambda b, h: (b, 3 * HY_W // qw + h)),
        pl.BlockSpec((rows, kw), lambda b, h: (b, h)),
        pl.BlockSpec((rows, kw), lambda b, h: (b, nh + h)),
        pl.BlockSpec((1, ATT_HEADS), lambda b, h: (0, 0)),
    ]
    args = [p, kv, kv, sink.reshape(1, ATT_HEADS)]
    if ctx is not None:
        kc, vc, e = ctx
        spec = pl.BlockSpec((None, None, hps, PAST_LEN, HEAD_DIM), lambda b, h: (b, e, h, 0, 0))
        in_specs += [spec, spec]
        args += [kc, vc]
    if rope is not None:
        in_specs += [pl.BlockSpec((L, HEAD_DIM), lambda b, h: (0, 0))] * 3
        args += list(rope)
    out_shape = [jax.ShapeDtypeStruct((grp.rows, ATT_HEADS * HEAD_DIM), BF16)]
    out_specs = [pl.BlockSpec((rows, qw), lambda b, h: (b, h))]
    if emit_kv:
        kv_shape = jax.ShapeDtypeStruct((grp.nb, 1, ATT_KV_HEADS, L, HEAD_DIM), F32)
        kv_spec = pl.BlockSpec((nseq, None, hps, L, HEAD_DIM), lambda b, h: (b, 0, h, 0, 0))
        out_shape += [kv_shape, kv_shape]
        out_specs += [kv_spec, kv_spec]
    return pl.pallas_call(
        functools.partial(_gqa_kernel, L, nseq, window, ctx is not None, rope is not None, emit_kv),
        out_shape=tuple(out_shape),
        grid=(grp.nb // nseq, nh),
        in_specs=in_specs,
        out_specs=tuple(out_specs),
        compiler_params=_cparams("parallel", "parallel"),
        name="gqa",
    )(*args)


def _chunk_cumsum(x, reverse):
    n = x.shape[0]
    pos = lax.broadcasted_iota(jnp.int32, x.shape, 0) % CHUNK
    s = 1
    while s < CHUNK:
        if reverse:
            x = x + jnp.where(pos < CHUNK - s, pltpu.roll(x, n - s, 0), 0.0)
        else:
            x = x + jnp.where(pos >= s, pltpu.roll(x, s, 0), 0.0)
        s *= 2
    return x


def _hgrn_kernel(L, has_state, emit_state, *refs):
    it = iter(refs)
    q_ref, ff_ref, fb_ref, i_ref, g_ref, lb_ref, norm_ref = (next(it) for _ in range(7))
    if has_state:
        s0_ref = next(it)
    o_ref = next(it)
    if emit_state:
        s_ref = next(it)
    acc_scr = next(it)

    a = lb_ref[...]
    e = jnp.exp(a - jnp.max(a, axis=0, keepdims=True))
    lb_all = e[1] / (e[0] + e[1])
    ci = lax.broadcasted_iota(jnp.int32, (CHUNK, CHUNK), 0)
    cj = lax.broadcasted_iota(jnp.int32, (CHUNK, CHUNK), 1)
    n_chunks = L // CHUNK
    chunks = [slice(n * CHUNK, (n + 1) * CHUNK) for n in range(n_chunks)]

    for hh in range(q_ref.shape[1] // HEAD_DIM):
        hc = slice(hh * HEAD_DIM, (hh + 1) * HEAD_DIM)
        q = _silu(q_ref[:, hc].astype(F32))
        for d, fz_ref in enumerate((ff_ref, fb_ref)):
            lbd = lb_all[d:d + 1, hc]
            keep = (cj <= ci) if d == 0 else (cj >= ci)
            qds, atts, decay, own = [], [], [], []
            for rows in chunks:
                f = lbd + (1.0 - lbd) * jax.nn.sigmoid(fz_ref[rows, hc].astype(F32))
                b = _chunk_cumsum(jnp.log(f), reverse=(d == 1))
                qd = (q[rows] * jnp.exp(b)).astype(BF16)
                kd32 = (1.0 - f) * jnp.exp(-b)
                dc = jnp.exp(b[CHUNK - 1:CHUNK] if d == 0 else b[0:1])
                qds.append(qd)
                decay.append(dc)
                atts.append(jnp.where(keep, _dot_nt(qd, kd32.astype(BF16)), 0.0).astype(BF16))
                own.append(_dot_tn(i_ref[rows, hc], (kd32 * dc).astype(BF16)))
            st = s0_ref[d, hh].T if has_state else jnp.zeros((HEAD_DIM, HEAD_DIM), F32)
            entering = [None] * n_chunks
            for n in (range(n_chunks) if d == 0 else range(n_chunks - 1, -1, -1)):
                entering[n] = st.astype(BF16)
                st = st * decay[n] + own[n]
            if emit_state:
                s_ref[d, hh] = st.T
            for n, rows in enumerate(chunks):
                o = _dot(atts[n], i_ref[rows, hc]) + _dot_nt(qds[n], entering[n])
                if d == 0:
                    acc_scr[rows, :] = o
                else:
                    acc_scr[rows, :] += o
        o = _rms(acc_scr[...]) * norm_ref[...] * _silu(g_ref[:, hc].astype(F32))
        o_ref[:, hc] = o.astype(o_ref.dtype)


def _hgrn(grp, p, hg_lb, norm_g, hps, state=None, emit_state=False):
    L = grp.L
    hw = hps * HEAD_DIM
    nh = HG_HEADS // hps
    col = lambda j: pl.BlockSpec((L, hw), lambda b, h: (b, j * nh + h))
    in_specs = [col(0), col(1), col(2), col(3), col(4),
                pl.BlockSpec((hg_lb.shape[0], 2, hw), lambda b, h: (0, 0, h)),
                pl.BlockSpec((1, HEAD_DIM), lambda b, h: (0, 0))]
    args = [p, p, p, p, p, hg_lb, norm_g.reshape(1, HEAD_DIM)]
    st_spec = lambda o: pl.BlockSpec((None, None, 2, hps, HEAD_DIM, HEAD_DIM), lambda b, h: (b, o, 0, h, 0, 0))
    if state is not None:
        s0, o = state
        in_specs.append(st_spec(o))
        args.append(s0)
    out_shape = [jax.ShapeDtypeStruct((grp.rows, HG_W), BF16)]
    out_specs = [pl.BlockSpec((L, hw), lambda b, h: (b, h))]
    if emit_state:
        out_shape.append(jax.ShapeDtypeStruct((grp.nb, 1, 2, HG_HEADS, HEAD_DIM, HEAD_DIM), F32))
        out_specs.append(st_spec(0))
    return pl.pallas_call(
        functools.partial(_hgrn_kernel, L, state is not None, emit_state),
        out_shape=tuple(out_shape),
        grid=(grp.nb, nh),
        in_specs=in_specs,
        out_specs=tuple(out_specs),
        scratch_shapes=[pltpu.VMEM((L, HEAD_DIM), F32)],
        compiler_params=_cparams("parallel", "parallel"),
        name="hgrn",
    )(*args)


MLA_QW = 256
KR_W = 128
OD_TAIL = 1024


def _mla_prep_kernel(has_rope, emit_kr, *refs):
    it = iter(refs)
    ql_ref, kvl_ref, kr_ref, qn_ref, kvn_ref, wq_ref = (next(it) for _ in range(6))
    if has_rope:
        kc_ref, ksa_ref, ksb_ref = (next(it) for _ in range(3))
    q_ref, ckv_ref, kro_ref = next(it), next(it), next(it)
    if emit_kr:
        krn_ref = next(it)
        krn_ref[...] = kr_ref[:, :ROPE]

    nf = ROPE // 4
    qn = (_rms(ql_ref[...]) * qn_ref[...]).astype(BF16)
    q = _dot(qn, wq_ref[...])
    for h in range(MLA_HEADS):
        nope = slice(h * MLA_QW, h * MLA_QW + NOPE)
        rot = slice(h * MLA_QW + NOPE, (h + 1) * MLA_QW)
        q_ref[:, nope] = q[:, nope].astype(q_ref.dtype)
        qr = q[:, rot]
        if has_rope:
            qr = _rope(qr, kc_ref[...], ksa_ref[...], ksb_ref[...], nf)
        q_ref[:, rot] = qr.astype(q_ref.dtype)
    ckv_ref[...] = _rms(kvl_ref[...]) * kvn_ref[...]
    kr = kr_ref[...]
    if has_rope:
        kr = _rope(kr, kc_ref[...], ksa_ref[...], ksb_ref[...], nf)
    kro_ref[...] = kr.astype(kro_ref.dtype)


def _mla_prep(grp, tail, q_norm, kv_norm, wq, rope=None, emit_kr=False):
    tm = min(512, grp.L)
    per = grp.L // tm
    n_rows = grp.rows
    in_specs = [
        pl.BlockSpec((tm, Q_LORA), lambda i: (i, 0)),
        pl.BlockSpec((tm, KV_LORA), lambda i: (i, Q_LORA // KV_LORA)),
        pl.BlockSpec((tm, KR_W), lambda i: (i, (Q_LORA + KV_LORA) // KR_W)),
        pl.BlockSpec((1, Q_LORA), lambda i: (0, 0)),
        pl.BlockSpec((1, KV_LORA), lambda i: (0, 0)),
        pl.BlockSpec((Q_LORA, MLA_HEADS * MLA_QW), lambda i: (0, 0)),
    ]
    args = [tail, tail, tail, q_norm.reshape(1, Q_LORA), kv_norm.reshape(1, KV_LORA), wq]
    if rope is not None:
        in_specs += [pl.BlockSpec((tm, KR_W), lambda i: (i % per, 0))] * 3
        args += list(rope)
    out_shape = [jax.ShapeDtypeStruct((n_rows, MLA_HEADS * MLA_QW), BF16),
                 jax.ShapeDtypeStruct((n_rows, KV_LORA), F32),
                 jax.ShapeDtypeStruct((n_rows, KR_W), BF16)]
    out_specs = [pl.BlockSpec((tm, MLA_HEADS * MLA_QW), lambda i: (i, 0)),
                 pl.BlockSpec((tm, KV_LORA), lambda i: (i, 0)),
                 pl.BlockSpec((tm, KR_W), lambda i: (i, 0))]
    if emit_kr:
        out_shape.append(jax.ShapeDtypeStruct((n_rows, ROPE), F32))
        out_specs.append(pl.BlockSpec((tm, ROPE), lambda i: (i, 0)))
    return pl.pallas_call(
        functools.partial(_mla_prep_kernel, rope is not None, emit_kr),
        out_shape=tuple(out_shape),
        grid=(n_rows // tm,),
        in_specs=in_specs,
        out_specs=tuple(out_specs),
        compiler_params=_cparams("parallel"),
        name="mla_prep",
    )(*args)


def _mla_attn_kernel(L, has_ctx, *refs):
    it = iter(refs)
    q_ref, ckv_ref, kr_ref, wkv_ref = (next(it) for _ in range(4))
    if has_ctx:
        cckv_ref, ckr_ref = next(it), next(it)
    o_ref = next(it)

    ckv = ckv_ref[...].astype(BF16)
    kr = kr_ref[...]
    if has_ctx:
        ckv = jnp.concatenate([ckv, cckv_ref[...].astype(BF16)], axis=0)
        kr = jnp.concatenate([kr, ckr_ref[...].astype(BF16)], axis=0)
    ones = jnp.ones((ckv.shape[0], V_DIM), BF16)
    qb = min(L, 256)
    for hh in range(q_ref.shape[1] // MLA_QW):
        qc = slice(hh * MLA_QW, (hh + 1) * MLA_QW)
        oc = slice(hh * V_DIM, (hh + 1) * V_DIM)
        kv = _dot(ckv, wkv_ref[:, qc])
        kh = jnp.concatenate([kv[:, :NOPE].astype(BF16), kr], axis=1)
        vh = jnp.concatenate([kv[:, NOPE:].astype(BF16), ones], axis=1)
        for i in range(L // qb):
            rows = slice(i * qb, (i + 1) * qb)
            s = _dot_nt(q_ref[rows, qc], kh) * MLA_SCALE
            m = jnp.max(s, axis=-1, keepdims=True)
            oa = _dot(jnp.exp(s - m).astype(BF16), vh)
            o_ref[rows, oc] = (oa[:, :V_DIM] / oa[:, V_DIM:]).astype(o_ref.dtype)


def _mla_attn(grp, q, ckv, kr, wkv, hps, ctx=None):
    L = grp.L
    in_specs = [
        pl.BlockSpec((L, hps * MLA_QW), lambda b, h: (b, h)),
        pl.BlockSpec((L, KV_LORA), lambda b, h: (b, 0)),
        pl.BlockSpec((L, KR_W), lambda b, h: (b, 0)),
        pl.BlockSpec((KV_LORA, hps * (NOPE + V_DIM)), lambda b, h: (0, h)),
    ]
    args = [q, ckv, kr, wkv]
    if ctx is not None:
        cckv, ckr, o = ctx
        in_specs += [pl.BlockSpec((None, None, PAST_LEN, KV_LORA), lambda b, h: (b, o, 0, 0)),
                     pl.BlockSpec((None, None, PAST_LEN, KR_W), lambda b, h: (b, o, 0, 0))]
        args += [cckv, ckr]
    return pl.pallas_call(
        functools.partial(_mla_attn_kernel, L, ctx is not None),
        out_shape=jax.ShapeDtypeStruct((grp.rows, MLA_HEADS * V_DIM), BF16),
        grid=(grp.nb, MLA_HEADS // hps),
        in_specs=in_specs,
        out_specs=pl.BlockSpec((L, hps * V_DIM), lambda b, h: (b, h)),
        compiler_params=_cparams("parallel", "parallel"),
        name="mla_attn",
    )(*args)


def kernel(x_prompt, x_sample, c, c_ctx, cache_attn_k, cache_attn_v, cache_mla_ckv, cache_mla_krope, state_hgrn, mod_w, mod_b, norm_g, ffn_wg, ffn_wu, ffn_wd, ev_w_in, ev_w_out, hy_conv_w, hy_conv_b, hy_f_w1, hy_f_b1, hy_f_w2, hy_f_b2, hy_f_w3, hy_f_freq, hy_bias, attn_sink, od_w_in, od_w_out, hg_lb, hg_norm, mla_q_norm, mla_w_qb, mla_kv_norm, mla_w_kvb):
    depth = mod_w.shape[0]
    groups = (PROMPT, LATENT)
    xs = [x_prompt.reshape(PROMPT.rows, D_MODEL), x_sample.reshape(LATENT.rows, D_MODEL)]
    cvec = jnp.concatenate([c_ctx[None, :], c, jnp.zeros((MOD_ROWS - 1 - DEC_BATCH, D_MODEL), F32)], axis=0)
    mods_all = _modulation(cvec, mod_w, mod_b)

    wg, wu, wd = ffn_wg.astype(BF16), ffn_wu.astype(BF16), ffn_wd.astype(BF16)
    hy_tc = {SEQ: 512, DEC_SEQ: 256}
    hy_nseq = {SEQ: 4, DEC_SEQ: 2}

    new_k = new_v = new_ckv = new_kr = new_s = None
    for l in range(depth):
        mods = mods_all[l]
        xs = [_ffn(grp, x, mods, 0, norm_g[l, 0], norm_g[l, 1], wg, wu, wd, l, 0) for grp, x in zip(groups, xs)]
        if l % 2 == 0:
            e = l // 2
            w_in = ev_w_in[e][:, :EV_IN - TN_EVEN].astype(BF16)
            w_in_last = ev_w_in[e][:, EV_IN - TN_EVEN:].astype(BF16)
            w_out = ev_w_out[e].astype(BF16)
            kv_cols = 2 * ATT_KV_HEADS * HEAD_DIM
            mix = []
            for grp, x in zip(groups, xs):
                p, kv = _inproj(grp, x, mods, norm_g[l, 2], w_in, w_in_last, kv_cols)
                dft = _dft_matrices(grp.L)
                tc = hy_tc[grp.L]
                spectra = _hyena_spectra(grp.L, tc, dft, hy_f_w1[e], hy_f_b1[e], hy_f_w2[e], hy_f_b2[e],
                                         hy_f_w3[e], hy_f_freq[e])
                hy = _hyena(grp, p, tc, hy_nseq[grp.L], spectra, dft, hy_conv_w[e], hy_conv_b[e], hy_bias[e])
                if grp.latent:
                    rope = _rope_tables(grp.L, HEAD_DIM, 0, HEAD_DIM)
                    (att,) = _gqa(grp, p, kv, attn_sink[e], 1, window=WINDOW,
                                  ctx=(cache_attn_k, cache_attn_v, e), rope=rope)
                else:
                    att, new_k, new_v = _gqa(grp, p, kv, attn_sink[e], 2, nseq=2, emit_kv=True)
                mix.append((hy, att))
        else:
            o = l // 2
            n_main = OD_IN_PAD - TN_ODD
            w_t = od_w_in[o].T
            w_in = w_t[:n_main].astype(BF16)
            w_in_last = jnp.pad(w_t[n_main:].astype(BF16), ((0, OD_IN_PAD - OD_IN), (0, 0)))
            w_out = od_w_out[o].astype(BF16)
            wq = mla_w_qb[o].reshape(Q_LORA, MLA_HEADS, NOPE + ROPE)
            wq = jnp.pad(wq, ((0, 0), (0, 0), (0, MLA_QW - NOPE - ROPE))).reshape(Q_LORA, -1).astype(BF16)
            wkv = mla_w_kvb[o].astype(BF16)
            mix = []
            for grp, x in zip(groups, xs):
                p, tail = _inproj(grp, x, mods, norm_g[l, 2], w_in, w_in_last, OD_TAIL, w_transposed=True)
                if grp.latent:
                    (hg,) = _hgrn(grp, p, hg_lb, hg_norm[o], 2, state=(state_hgrn, o))
                    rope = _rope_tables(grp.L, ROPE, 0, KR_W)
                    q, ckv, kr = _mla_prep(grp, tail, mla_q_norm[o], mla_kv_norm[o], wq, rope=rope)
                    ckr = jnp.pad(cache_mla_krope, ((0, 0), (0, 0), (0, 0), (0, KR_W - ROPE)))
                    att = _mla_attn(grp, q, ckv, kr, wkv, 4, ctx=(cache_mla_ckv, ckr, o))
                else:
                    hg, new_s = _hgrn(grp, p, hg_lb, hg_norm[o], 4, emit_state=True)
                    q, ckv, kr, kr_raw = _mla_prep(grp, tail, mla_q_norm[o], mla_kv_norm[o], wq, emit_kr=True)
                    att = _mla_attn(grp, q, ckv, kr, wkv, 8)
                    new_ckv = ckv.reshape(BATCH, 1, SEQ, KV_LORA)
                    new_kr = kr_raw.reshape(BATCH, 1, SEQ, ROPE)
                mix.append((hg, att))
        xs = [_outproj(grp, a, b, w_out, x, mods, norm_g[l, 3])
              for grp, x, (a, b) in zip(groups, xs, mix)]
        xs = [_ffn(grp, x, mods, 2, norm_g[l, 4], norm_g[l, 5], wg, wu, wd, l, 1) for grp, x in zip(groups, xs)]

    y_prompt = xs[0].reshape(BATCH, SEQ, D_MODEL)
    y_sample = xs[1].reshape(DEC_BATCH, DEC_SEQ, D_MODEL)
    return (y_prompt, y_sample, new_k, new_v, new_ckv, new_kr, new_s)
```

```python
import functools
import math
from typing import NamedTuple

import jax
import jax.numpy as jnp
from jax import lax
from jax.experimental import pallas as pl
from jax.experimental.pallas import tpu as pltpu

D_MODEL = 2048
BATCH = 16
SEQ = 256
DEC_BATCH = 8
DEC_SEQ = 1024
PAST_LEN = 512
GRID_W = 64
HEAD_DIM = 128
HY_W = 1024
HY_ORDER = 2
SHORT_CONV = 3
POS_EMB = 33
FILTER_ORDER = 64
HY_FAST_DECAY = 0.3
HY_SLOW_DECAY = 1.5
HY_TARGET = 1e-2
ATT_HEADS = 8
ATT_KV_HEADS = 2
ATT_GROUP = 4
WINDOW = 128
EV_IN = 3 * HY_W + (ATT_HEADS + 2 * ATT_KV_HEADS) * HEAD_DIM
HG_W = 1024
HG_HEADS = 8
CHUNK = 64
Q_LORA = 512
KV_LORA = 256
NOPE = 128
ROPE = 64
V_DIM = 128
MLA_HEADS = 8
OD_IN = 5 * HG_W + Q_LORA + KV_LORA + ROPE
OD_IN_PAD = 6144
D_FF = 5632
MACARON_W = 0.5
N_MOD = 9
ROPE_BASE = 10000.0
EPS = 1e-6
ATT_SCALE = HEAD_DIM ** -0.5
MLA_SCALE = (NOPE + ROPE) ** -0.5

MOD_ROWS = 16

V7X_VMEM_LIMIT = 56 * 1024 * 1024
TM = 512
TM_FFN = 1024
BIG_TILE_VMEM_LIMIT = 60 * 1024 * 1024
TM_IN = 1024
RC = 512
TF = 512
TN_EVEN = 1536
TN_ODD = 1024
MOD_TN = 2048

BF16 = jnp.bfloat16
F32 = jnp.float32
HIGHEST = lax.Precision.HIGHEST


class Group(NamedTuple):
    nb: int
    L: int
    latent: bool

    @property
    def rows(self):
        return self.nb * self.L

    def mod_row(self, i, tm):
        return 1 + i // (self.L // tm) if self.latent else 0


PROMPT = Group(BATCH, SEQ, False)
LATENT = Group(DEC_BATCH, DEC_SEQ, True)


def _cparams(*sem):
    return pltpu.CompilerParams(dimension_semantics=sem, vmem_limit_bytes=V7X_VMEM_LIMIT)


def _rms(x):
    return x * lax.rsqrt(jnp.mean(x * x, axis=-1, keepdims=True) + EPS)


def _silu(x):
    return x * jax.nn.sigmoid(x)


def _dot(a, b):
    return jnp.dot(a, b, preferred_element_type=F32)


def _dot_nt(a, b):
    return lax.dot_general(a, b, (((1,), (1,)), ((), ())), preferred_element_type=F32)


def _dot_tn(a, b):
    return lax.dot_general(a, b, (((0,), (0,)), ((), ())), preferred_element_type=F32)


def _row_chunks(n_rows, body, rc=RC):
    def step(c, carry):
        body(pl.ds(pl.multiple_of(c * rc, rc), rc))
        return carry
    lax.fori_loop(0, n_rows // rc, step, 0, unroll=True)


def _modulated_norm(x_ref, h_scr, gain, shift):
    def body(rows):
        h_scr[rows, :] = (_rms(x_ref[rows, :]) * gain + shift).astype(BF16)
    _row_chunks(x_ref.shape[0], body)


def _mod_kernel(c_ref, w_ref, b_ref, o_ref):
    s = _silu(c_ref[...]).astype(BF16)
    o_ref[...] = _dot(s, w_ref[...].astype(BF16)) + b_ref[...]


def _modulation(cvec, mod_w, mod_b):
    depth = mod_w.shape[0]
    n = N_MOD * D_MODEL
    out = pl.pallas_call(
        _mod_kernel,
        out_shape=jax.ShapeDtypeStruct((depth, MOD_ROWS, n), F32),
        grid=(depth, n // MOD_TN),
        in_specs=[
            pl.BlockSpec((MOD_ROWS, D_MODEL), lambda l, j: (0, 0)),
            pl.BlockSpec((None, D_MODEL, MOD_TN), lambda l, j: (l, 0, j)),
            pl.BlockSpec((None, 1, MOD_TN), lambda l, j: (l, 0, j)),
        ],
        out_specs=pl.BlockSpec((None, MOD_ROWS, MOD_TN), lambda l, j: (l, 0, j)),
        compiler_params=_cparams("parallel", "parallel"),
        name="modulation",
    )(cvec, mod_w, mod_b.reshape(depth, 1, n))
    return out.reshape(depth, MOD_ROWS, N_MOD, D_MODEL)


def _ffn_kernel(j, x_ref, mod_ref, gpre_ref, gpost_ref, wg_ref, wu_ref, wd_ref, o_ref, h_scr):
    f = pl.program_id(1)

    @pl.when(f == 0)
    def _():
        gain = gpre_ref[...] * (1.0 + mod_ref[3 * j + 1:3 * j + 2, :])
        _modulated_norm(x_ref, h_scr, gain, mod_ref[3 * j:3 * j + 1, :])

    def partial_down(rows):
        h = h_scr[rows, :]
        a = (_silu(_dot(h, wg_ref[...])) * _dot(h, wu_ref[...])).astype(BF16)
        return _dot(a, wd_ref[...])

    @pl.when(f == 0)
    def _():
        def first(rows):
            o_ref[rows, :] = partial_down(rows)
        _row_chunks(o_ref.shape[0], first)

    last = pl.num_programs(1) - 1

    @pl.when((f > 0) & (f < last))
    def _():
        def accumulate(rows):
            o_ref[rows, :] += partial_down(rows)
        _row_chunks(o_ref.shape[0], accumulate)

    @pl.when(f == last)
    def _():
        gain = (MACARON_W * mod_ref[3 * j + 2:3 * j + 3, :]) * gpost_ref[...]

        def finish(rows):
            acc = o_ref[rows, :] + partial_down(rows)
            o_ref[rows, :] = x_ref[rows, :] + _rms(acc) * gain
        _row_chunks(o_ref.shape[0], finish, RC // 2)


def _ffn(grp, x, mods, j, g_pre, g_post, wg, wu, wd, l, s):
    return pl.pallas_call(
        functools.partial(_ffn_kernel, j),
        out_shape=jax.ShapeDtypeStruct((grp.rows, D_MODEL), F32),
        grid=(grp.rows // TM_FFN, D_FF // TF),
        in_specs=[
            pl.BlockSpec((TM_FFN, D_MODEL), lambda i, f: (i, 0)),
            pl.BlockSpec((None, N_MOD, D_MODEL), lambda i, f: (grp.mod_row(i, TM_FFN), 0, 0)),
            pl.BlockSpec((1, D_MODEL), lambda i, f: (0, 0)),
            pl.BlockSpec((1, D_MODEL), lambda i, f: (0, 0)),
            pl.BlockSpec((None, None, D_MODEL, TF), lambda i, f: (l, s, 0, f)),
            pl.BlockSpec((None, None, D_MODEL, TF), lambda i, f: (l, s, 0, f)),
            pl.BlockSpec((None, None, TF, D_MODEL), lambda i, f: (l, s, f, 0)),
        ],
        out_specs=pl.BlockSpec((TM_FFN, D_MODEL), lambda i, f: (i, 0)),
        scratch_shapes=[pltpu.VMEM((TM_FFN, D_MODEL), BF16)],
        compiler_params=pltpu.CompilerParams(dimension_semantics=("parallel", "arbitrary"),
                                             vmem_limit_bytes=BIG_TILE_VMEM_LIMIT),
        name="ffn",
    )(x, mods, g_pre.reshape(1, D_MODEL), g_post.reshape(1, D_MODEL), wg, wu, wd)


def _inproj_kernel(w_transposed, x_ref, mod_ref, g_ref, w_ref, wl_ref, o_ref, t_ref, h_scr):
    k = pl.program_id(1)
    last = pl.num_programs(1) - 1

    def project(weights_ref):
        y = (_dot_nt if w_transposed else _dot)(h_scr[...], weights_ref[...])
        o_ref[...] = y.astype(o_ref.dtype)
        return y

    @pl.when(k == 0)
    def _():
        _modulated_norm(x_ref, h_scr, g_ref[...] * (1.0 + mod_ref[4:5, :]), mod_ref[3:4, :])
        project(w_ref)

    @pl.when((k > 0) & (k < last))
    def _():
        project(w_ref)

    @pl.when(k == last)
    def _():
        y = project(wl_ref)
        t_ref[...] = y[:, y.shape[1] - t_ref.shape[1]:]


def _inproj(grp, x, mods, g, w, w_last, last_block, tn, steps, n_tail, w_transposed=False):
    assert steps >= 2
    tm = TM_IN
    if w_transposed:
        w_specs = [pl.BlockSpec((tn, D_MODEL), lambda i, k: (jnp.minimum(k, steps - 2), 0)),
                   pl.BlockSpec((tn, D_MODEL), lambda i, k: (last_block, 0), pipeline_mode=pl.Buffered(1))]
    else:
        w_specs = [pl.BlockSpec((D_MODEL, tn), lambda i, k: (0, jnp.minimum(k, steps - 2))),
                   pl.BlockSpec((D_MODEL, tn), lambda i, k: (0, last_block), pipeline_mode=pl.Buffered(1))]
    return pl.pallas_call(
        functools.partial(_inproj_kernel, w_transposed),
        out_shape=(jax.ShapeDtypeStruct((grp.rows, steps * tn), BF16),
                   jax.ShapeDtypeStruct((grp.rows, n_tail), F32)),
        grid=(grp.rows // tm, steps),
        in_specs=[
            pl.BlockSpec((tm, D_MODEL), lambda i, k: (i, 0)),
            pl.BlockSpec((None, N_MOD, D_MODEL), lambda i, k: (grp.mod_row(i, tm), 0, 0)),
            pl.BlockSpec((1, D_MODEL), lambda i, k: (0, 0)),
            *w_specs,
        ],
        out_specs=(pl.BlockSpec((tm, tn), lambda i, k: (i, k)),
                   pl.BlockSpec((tm, n_tail), lambda i, k: (i, 0))),
        scratch_shapes=[pltpu.VMEM((tm, D_MODEL), BF16)],
        compiler_params=pltpu.CompilerParams(dimension_semantics=("parallel", "arbitrary"),
                                             vmem_limit_bytes=BIG_TILE_VMEM_LIMIT),
        name="inproj",
    )(x, mods, g.reshape(1, D_MODEL), w, w_last)


def _outproj_kernel(a_ref, b_ref, wa_ref, wb_ref, x_ref, mod_ref, g_ref, o_ref):
    y = _dot(a_ref[...], wa_ref[...]) + _dot(b_ref[...], wb_ref[...])
    o_ref[...] = x_ref[...] + _rms(y) * (mod_ref[5:6, :] * g_ref[...])


def _outproj(grp, a, b, w, x, mods, g):
    half = a.shape[1]
    tm = TM
    return pl.pallas_call(
        _outproj_kernel,
        out_shape=jax.ShapeDtypeStruct((grp.rows, D_MODEL), F32),
        grid=(grp.rows // tm,),
        in_specs=[
            pl.BlockSpec((tm, half), lambda i: (i, 0)),
            pl.BlockSpec((tm, half), lambda i: (i, 0)),
            pl.BlockSpec((half, D_MODEL), lambda i: (0, 0), pipeline_mode=pl.Buffered(1)),
            pl.BlockSpec((half, D_MODEL), lambda i: (1, 0), pipeline_mode=pl.Buffered(1)),
            pl.BlockSpec((tm, D_MODEL), lambda i: (i, 0)),
            pl.BlockSpec((None, N_MOD, D_MODEL), lambda i: (grp.mod_row(i, tm), 0, 0)),
            pl.BlockSpec((1, D_MODEL), lambda i: (0, 0)),
        ],
        out_specs=pl.BlockSpec((tm, D_MODEL), lambda i: (i, 0)),
        compiler_params=_cparams("parallel"),
        name="outproj",
    )(a, b, w, w, x, mods, g.reshape(1, D_MODEL))


def _rope_tables(L, rot_dim, lane0, width):
    half = rot_dim // 2
    inv = ROPE_BASE ** (-jnp.arange(0, half, 2, dtype=F32) / half)
    pos = jnp.arange(L)
    ang_r = (pos // GRID_W).astype(F32)[:, None] * inv
    ang_c = (pos % GRID_W).astype(F32)[:, None] * inv
    cr, sr, cc, sc = jnp.cos(ang_r), jnp.sin(ang_r), jnp.cos(ang_c), jnp.sin(ang_c)
    z = jnp.zeros_like(sr)
    cos = jnp.concatenate([cr, cr, cc, cc], axis=-1)
    sin_a = jnp.concatenate([-sr, z, -sc, z], axis=-1)
    sin_b = jnp.concatenate([z, sr, z, sc], axis=-1)
    pad = ((0, 0), (lane0, width - lane0 - rot_dim))
    return jnp.pad(cos, pad, constant_values=1.0), jnp.pad(sin_a, pad), jnp.pad(sin_b, pad)


def _rope(x, cos, sin_a, sin_b, nf):
    w = x.shape[-1]
    return x * cos + pltpu.roll(x, w - nf, 1) * sin_a + pltpu.roll(x, nf, 1) * sin_b


DFT_SPLIT = 32


def _dft_matrices(L):
    s = jnp.arange(L, dtype=jnp.int32)[None, :]

    def trig(k):
        ang = ((k[:, None] * s) % (2 * L)).astype(F32) * (math.pi / L)
        return jnp.cos(ang), jnp.sin(ang)

    c1, s1 = trig(jnp.arange(0, L, DFT_SPLIT, dtype=jnp.int32))
    c0, s0 = trig(jnp.arange(DFT_SPLIT, dtype=jnp.int32))
    cos = (c1[:, None, :] * c0[None] - s1[:, None, :] * s0[None]).reshape(L, L)
    sin = (s1[:, None, :] * c0[None] + c1[:, None, :] * s0[None]).reshape(L, L)
    nyq = (1 - 2 * (jnp.arange(L, dtype=jnp.int32) % 2)).astype(F32)
    k = jnp.arange(L, dtype=jnp.int32)[:, None]
    msin = jnp.where(k == 0, nyq[None, :], -sin)
    msin_t = jnp.where(s == 0, nyq[:, None], -sin)
    return cos.astype(BF16), msin.astype(BF16), msin_t.astype(BF16)


def _filter_kernel(L, z_ref, w1_ref, b1_ref, w2_ref, b2_ref, fr_ref, w3f_ref, w3b_ref, dl_ref, cos_ref, msin_ref,
                   ka_ref, kb_ref, kc_ref, h_scr):
    z = z_ref[...]

    @pl.when((pl.program_id(0) == 0) & (pl.program_id(1) == 0))
    def _():
        fr = fr_ref[...]
        h1 = jnp.sin(fr * (jnp.dot(z, w1_ref[...], precision=HIGHEST, preferred_element_type=F32) + b1_ref[...]))
        h_scr[...] = jnp.sin(
            fr * (jnp.dot(h1, w2_ref[...], precision=HIGHEST, preferred_element_type=F32) + b2_ref[...]))

    h = h_scr[...]
    decay = jnp.exp(-z[:, 0:1] * dl_ref[...])
    hf = jnp.dot(h, w3f_ref[...], precision=HIGHEST, preferred_element_type=F32) * decay
    hb = jnp.dot(h, w3b_ref[...], precision=HIGHEST, preferred_element_type=F32) * decay
    row = lax.broadcasted_iota(jnp.int32, hf.shape, 0)
    row0 = row == 0
    hb = jnp.where(row0, 0.0, hb)
    even = hf + hb
    re = _dot(cos_ref[...], even.astype(BF16))
    im = _dot(msin_ref[...], (hf - hb).astype(BF16))
    nyq = jnp.sum(jnp.where(row % 2 == 0, even, -even), axis=0, keepdims=True)
    sc = jnp.where(row0, 0.5 / L, 1.0 / L)
    ka_ref[...] = re * sc
    kb_ref[...] = jnp.where(row0, 0.0, im) * sc
    kc_ref[...] = jnp.where(row0, nyq, re) * sc


def _hyena_spectra(L, tc, dft, f_w1, f_b1, f_w2, f_b2, f_w3, f_freq):
    t = jnp.linspace(0.0, 1.0, L, dtype=F32)[:, None]
    bands = (POS_EMB - 1) // 2
    w = 2.0 * math.pi * jnp.arange(L, dtype=F32)[:, None] / L
    fb = jnp.linspace(1e-4, bands - 1, bands, dtype=F32)[None, :]
    z = jnp.concatenate([t, jnp.cos(fb * w), -jnp.sin(fb * w)], axis=-1)
    z = jnp.pad(z, ((0, 0), (0, 128 - POS_EMB)))
    w1 = jnp.pad(f_w1, ((0, 128 - POS_EMB), (0, 0)))
    deltas = jnp.abs(jnp.linspace(math.log(HY_TARGET) / HY_SLOW_DECAY,
                                  math.log(HY_TARGET) / HY_FAST_DECAY, HY_W, dtype=F32))[None, :]
    nct = HY_W // tc
    fo = FILTER_ORDER
    row = lambda a: a.reshape(1, fo)
    kshape = jax.ShapeDtypeStruct((HY_ORDER, L, HY_W), F32)
    kspec = pl.BlockSpec((None, L, tc), lambda n, c: (n, 0, c))
    const = lambda shape: pl.BlockSpec(shape, lambda n, c: (0, 0))
    return pl.pallas_call(
        functools.partial(_filter_kernel, L),
        out_shape=(kshape, kshape, kshape),
        grid=(HY_ORDER, nct),
        in_specs=[
            const((L, 128)), const((128, fo)), const((1, fo)), const((fo, fo)), const((1, fo)), const((1, fo)),
            pl.BlockSpec((fo, tc), lambda n, c: (0, 2 * n * nct + c)),
            pl.BlockSpec((fo, tc), lambda n, c: (0, (2 * n + 1) * nct + c)),
            pl.BlockSpec((1, tc), lambda n, c: (0, c)),
            const((L, L)), const((L, L)),
        ],
        out_specs=(kspec, kspec, kspec),
        scratch_shapes=[pltpu.VMEM((L, fo), F32)],
        compiler_params=_cparams("arbitrary", "arbitrary"),
        name="hyena_filter",
    )(z, w1, row(f_b1), f_w2, row(f_b2), row(f_freq), f_w3, f_w3, deltas, dft[0], dft[1])


def _hyena_kernel(L, nseq, uv_ref, ug0_ref, ug1_ref, cwv_ref, cwg0_ref, cwg1_ref, cbv_ref, cbg0_ref, cbg1_ref,
                  hb_ref, ka_ref, kb_ref, kc_ref, cos_ref, msin_ref, msint_ref, o_ref):
    row = lax.broadcasted_iota(jnp.int32, (L, uv_ref.shape[1]), 0)
    first, last = row == 0, row == L - 1
    seqs = [slice(i * L, (i + 1) * L) for i in range(nseq)]

    def short_conv(u_ref, rows, w_ref, b_ref):
        u = u_ref[rows, :].astype(F32)
        prev = jnp.where(first, 0.0, pltpu.roll(u, 1, 0))
        nxt = jnp.where(last, 0.0, pltpu.roll(u, L - 1, 0))
        return b_ref[...] + prev * w_ref[0:1, :] + u * w_ref[1:2, :] + nxt * w_ref[2:3, :]

    z = [short_conv(uv_ref, r, cwv_ref, cbv_ref) for r in seqs]
    gates = [(short_conv(ug0_ref, r, cwg0_ref, cbg0_ref), short_conv(ug1_ref, r, cwg1_ref, cbg1_ref))
             for r in seqs]
    for n in range(HY_ORDER):
        zb = [zi.astype(BF16) for zi in z]
        zre = [_dot(cos_ref[...], b) for b in zb]
        zim = [_dot(msin_ref[...], b) for b in zb]
        ka, kb, kc = ka_ref[n], kb_ref[n], kc_ref[n]
        yre = [(re * ka - im * kb).astype(BF16) for re, im in zip(zre, zim)]
        yim = [(re * kb + im * kc).astype(BF16) for re, im in zip(zre, zim)]
        conv = [_dot(cos_ref[...], a) + _dot(msint_ref[...], b) for a, b in zip(yre, yim)]
        z = [g[n] * (cv + hb_ref[n:n + 1, :] * zi) for g, cv, zi in zip(gates, conv, z)]
    for r, zi in zip(seqs, z):
        o_ref[r, :] = zi.astype(o_ref.dtype)


def _hyena(grp, p, tc, nseq, spectra, dft, conv_w, conv_b, h_bias):
    L = grp.L
    nct = HY_W // tc
    ka, kb, kc = spectra
    u_spec = lambda j: pl.BlockSpec((nseq * L, tc), lambda c, b: (b, j * nct + c))
    cw_spec = lambda j: pl.BlockSpec((SHORT_CONV, tc), lambda c, b: (0, j * nct + c))
    cb_spec = lambda j: pl.BlockSpec((1, tc), lambda c, b: (0, j * nct + c))
    k_spec = pl.BlockSpec((HY_ORDER, L, tc), lambda c, b: (0, 0, c))
    m_spec = pl.BlockSpec((L, L), lambda c, b: (0, 0))
    cb = conv_b.reshape(1, -1)
    return pl.pallas_call(
        functools.partial(_hyena_kernel, L, nseq),
        out_shape=jax.ShapeDtypeStruct((grp.rows, HY_W), BF16),
        grid=(nct, grp.nb // nseq),
        in_specs=[
            u_spec(0), u_spec(1), u_spec(2), cw_spec(0), cw_spec(1), cw_spec(2),
            cb_spec(0), cb_spec(1), cb_spec(2),
            pl.BlockSpec((HY_ORDER, tc), lambda c, b: (0, c)),
            k_spec, k_spec, k_spec, m_spec, m_spec, m_spec,
        ],
        out_specs=pl.BlockSpec((nseq * L, tc), lambda c, b: (b, c)),
        compiler_params=_cparams("parallel", "arbitrary"),
        name="hyena",
    )(p, p, p, conv_w, conv_w, conv_w, cb, cb, cb, h_bias, ka, kb, kc, *dft)


def _gqa_kernel(L, nseq, window, has_ctx, has_rope, emit_kv, *refs):
    it = iter(refs)
    q_ref, k_ref, v_ref, sink_ref = next(it), next(it), next(it), next(it)
    if has_ctx:
        kc_ref, vc_ref = next(it), next(it)
    if has_rope:
        cos_ref, sa_ref, sb_ref = next(it), next(it), next(it)
    o_ref = next(it)
    if emit_kv:
        kn_ref, vn_ref = next(it), next(it)

    qb = HEAD_DIM
    nf = HEAD_DIM // 4
    gw = ATT_GROUP * HEAD_DIM
    hps = k_ref.shape[1] // HEAD_DIM
    sink_row = sink_ref[...]
    sink_lane = lax.broadcasted_iota(jnp.int32, sink_row.shape, 1)
    band_bias = {}

    def bias_for(lo, hi, i):
        key = (lo - i * qb, hi - lo)
        if key not in band_bias:
            shape = (ATT_GROUP * qb, hi - lo)
            rel = key[0] + lax.broadcasted_iota(jnp.int32, shape, 1) - lax.broadcasted_iota(jnp.int32, shape, 0) % qb
            band_bias[key] = jnp.where(jnp.abs(rel) <= window, 0.0, -1e30)
        return band_bias[key]

    for sq in range(nseq):
        r0 = sq * L
        for hk in range(hps):
            head = pl.program_id(1) * hps + hk
            k = k_ref[r0:r0 + L, hk * HEAD_DIM:(hk + 1) * HEAD_DIM]
            v = v_ref[r0:r0 + L, hk * HEAD_DIM:(hk + 1) * HEAD_DIM]
            if emit_kv:
                kn_ref[sq, hk] = k
                vn_ref[sq, hk] = v
            if has_rope:
                k = _rope(k, cos_ref[...], sa_ref[...], sb_ref[...], nf)
            k = k.astype(BF16)
            v = jnp.concatenate([v.astype(BF16), jnp.ones((L, HEAD_DIM), BF16)], axis=1)
            if has_ctx:
                kc = kc_ref[hk].astype(BF16)
                vc = jnp.concatenate([vc_ref[hk].astype(BF16), jnp.ones((PAST_LEN, HEAD_DIM), BF16)], axis=1)
            sinks = [jnp.sum(jnp.where(sink_lane == head * ATT_GROUP + g, sink_row, 0.0), axis=1, keepdims=True)
                     for g in range(ATT_GROUP)]
            sink = jnp.concatenate([jnp.broadcast_to(s, (qb, 1)) for s in sinks], axis=0)

            for i in range(L // qb):
                rows = slice(i * qb, (i + 1) * qb)
                out_rows = slice(r0 + i * qb, r0 + (i + 1) * qb)
                qs = []
                for g in range(ATT_GROUP):
                    qg = q_ref[out_rows, hk * gw + g * HEAD_DIM:hk * gw + (g + 1) * HEAD_DIM].astype(F32)
                    if has_rope:
                        qg = _rope(qg, cos_ref[rows, :], sa_ref[rows, :], sb_ref[rows, :], nf)
                    qs.append((qg * ATT_SCALE).astype(BF16))
                q = jnp.concatenate(qs, axis=0)
                if window is None:
                    lo, hi = 0, L
                else:
                    lo, hi = max(0, (i - 1) * qb), min(L, (i + 2) * qb)
                s = _dot_nt(q, k[lo:hi])
                if window is not None:
                    s = s + bias_for(lo, hi, i)
                m = jnp.maximum(jnp.max(s, axis=-1, keepdims=True), sink)
                if has_ctx:
                    sc = _dot_nt(q, kc)
                    m = jnp.maximum(m, jnp.max(sc, axis=-1, keepdims=True))
                oa = _dot(jnp.exp(s - m).astype(BF16), v[lo:hi])
                if has_ctx:
                    oa = oa + _dot(jnp.exp(sc - m).astype(BF16), vc)
                o = oa[:, :HEAD_DIM] / (oa[:, HEAD_DIM:] + jnp.exp(sink - m))
                for g in range(ATT_GROUP):
                    o_ref[out_rows, hk * gw + g * HEAD_DIM:hk * gw + (g + 1) * HEAD_DIM] = (
                        o[g * qb:(g + 1) * qb].astype(o_ref.dtype))


def _gqa(grp, p, kv, sink, hps, nseq=1, window=None, ctx=None, rope=None, emit_kv=False):
    assert nseq == 1 or (ctx is None and rope is None)
    L = grp.L
    rows = nseq * L
    qw = hps * ATT_GROUP * HEAD_DIM
    kw = hps * HEAD_DIM
    nh = ATT_KV_HEADS // hps
    in_specs = [
        pl.BlockSpec((rows, qw), lambda b, h: (b, 3 * HY_W // qw + h)),
        pl.BlockSpec((rows, kw), lambda b, h: (b, h)),
        pl.BlockSpec((rows, kw), lambda b, h: (b, nh + h)),
        pl.BlockSpec((1, ATT_HEADS), lambda b, h: (0, 0)),
    ]
    args = [p, kv, kv, sink.reshape(1, ATT_HEADS)]
    if ctx is not None:
        kc, vc, e = ctx
        spec = pl.BlockSpec((None, None, hps, PAST_LEN, HEAD_DIM), lambda b, h: (b, e, h, 0, 0))
        in_specs += [spec, spec]
        args += [kc, vc]
    if rope is not None:
        in_specs += [pl.BlockSpec((L, HEAD_DIM), lambda b, h: (0, 0))] * 3
        args += list(rope)
    out_shape = [jax.ShapeDtypeStruct((grp.rows, ATT_HEADS * HEAD_DIM), BF16)]
    out_specs = [pl.BlockSpec((rows, qw), lambda b, h: (b, h))]
    if emit_kv:
        kv_shape = jax.ShapeDtypeStruct((grp.nb, 1, ATT_KV_HEADS, L, HEAD_DIM), F32)
        kv_spec = pl.BlockSpec((nseq, None, hps, L, HEAD_DIM), lambda b, h: (b, 0, h, 0, 0))
        out_shape += [kv_shape, kv_shape]
        out_specs += [kv_spec, kv_spec]
    return pl.pallas_call(
        functools.partial(_gqa_kernel, L, nseq, window, ctx is not None, rope is not None, emit_kv),
        out_shape=tuple(out_shape),
        grid=(grp.nb // nseq, nh),
        in_specs=in_specs,
        out_specs=tuple(out_specs),
        compiler_params=_cparams("parallel", "parallel"),
        name="gqa",
    )(*args)


def _chunk_cumsum(x, reverse):
    n = x.shape[0]
    pos = lax.broadcasted_iota(jnp.int32, x.shape, 0) % CHUNK
    s = 1
    while s < CHUNK:
        if reverse:
            x = x + jnp.where(pos < CHUNK - s, pltpu.roll(x, n - s, 0), 0.0)
        else:
            x = x + jnp.where(pos >= s, pltpu.roll(x, s, 0), 0.0)
        s *= 2
    return x


def _hgrn_kernel(L, has_state, emit_state, *refs):
    it = iter(refs)
    q_ref, ff_ref, fb_ref, i_ref, g_ref, lb_ref, norm_ref = (next(it) for _ in range(7))
    if has_state:
        s0_ref = next(it)
    o_ref = next(it)
    if emit_state:
        s_ref = next(it)
    acc_scr = next(it)

    a = lb_ref[...]
    e = jnp.exp(a - jnp.max(a, axis=0, keepdims=True))
    lb_all = e[1] / (e[0] + e[1])
    ci = lax.broadcasted_iota(jnp.int32, (CHUNK, CHUNK), 0)
    cj = lax.broadcasted_iota(jnp.int32, (CHUNK, CHUNK), 1)
    n_chunks = L // CHUNK
    chunks = [slice(n * CHUNK, (n + 1) * CHUNK) for n in range(n_chunks)]

    for hh in range(q_ref.shape[1] // HEAD_DIM):
        hc = slice(hh * HEAD_DIM, (hh + 1) * HEAD_DIM)
        q = _silu(q_ref[:, hc].astype(F32))
        for d, fz_ref in enumerate((ff_ref, fb_ref)):
            lbd = lb_all[d:d + 1, hc]
            keep = (cj <= ci) if d == 0 else (cj >= ci)
            qds, atts, decay, own = [], [], [], []
            for rows in chunks:
                f = lbd + (1.0 - lbd) * jax.nn.sigmoid(fz_ref[rows, hc].astype(F32))
                b = _chunk_cumsum(jnp.log(f), reverse=(d == 1))
                qd = (q[rows] * jnp.exp(b)).astype(BF16)
                kd32 = (1.0 - f) * jnp.exp(-b)
                dc = jnp.exp(b[CHUNK - 1:CHUNK] if d == 0 else b[0:1])
                qds.append(qd)
                decay.append(dc)
                atts.append(jnp.where(keep, _dot_nt(qd, kd32.astype(BF16)), 0.0).astype(BF16))
                own.append(_dot_tn(i_ref[rows, hc], (kd32 * dc).astype(BF16)))
            st = s0_ref[d, hh].T if has_state else jnp.zeros((HEAD_DIM, HEAD_DIM), F32)
            entering = [None] * n_chunks
            for n in (range(n_chunks) if d == 0 else range(n_chunks - 1, -1, -1)):
                entering[n] = st.astype(BF16)
                st = st * decay[n] + own[n]
            if emit_state:
                s_ref[d, hh] = st.T
            for n, rows in enumerate(chunks):
                o = _dot(atts[n], i_ref[rows, hc]) + _dot_nt(qds[n], entering[n])
                if d == 0:
                    acc_scr[rows, :] = o
                else:
                    acc_scr[rows, :] += o
        o = _rms(acc_scr[...]) * norm_ref[...] * _silu(g_ref[:, hc].astype(F32))
        o_ref[:, hc] = o.astype(o_ref.dtype)


def _hgrn(grp, p, hg_lb, norm_g, hps, state=None, emit_state=False):
    L = grp.L
    hw = hps * HEAD_DIM
    nh = HG_HEADS // hps
    col = lambda j: pl.BlockSpec((L, hw), lambda b, h: (b, j * nh + h))
    in_specs = [col(0), col(1), col(2), col(3), col(4),
                pl.BlockSpec((hg_lb.shape[0], 2, hw), lambda b, h: (0, 0, h)),
                pl.BlockSpec((1, HEAD_DIM), lambda b, h: (0, 0))]
    args = [p, p, p, p, p, hg_lb, norm_g.reshape(1, HEAD_DIM)]
    st_spec = lambda o: pl.BlockSpec((None, None, 2, hps, HEAD_DIM, HEAD_DIM), lambda b, h: (b, o, 0, h, 0, 0))
    if state is not None:
        s0, o = state
        in_specs.append(st_spec(o))
        args.append(s0)
    out_shape = [jax.ShapeDtypeStruct((grp.rows, HG_W), BF16)]
    out_specs = [pl.BlockSpec((L, hw), lambda b, h: (b, h))]
    if emit_state:
        out_shape.append(jax.ShapeDtypeStruct((grp.nb, 1, 2, HG_HEADS, HEAD_DIM, HEAD_DIM), F32))
        out_specs.append(st_spec(0))
    return pl.pallas_call(
        functools.partial(_hgrn_kernel, L, state is not None, emit_state),
        out_shape=tuple(out_shape),
        grid=(grp.nb, nh),
        in_specs=in_specs,
        out_specs=tuple(out_specs),
        scratch_shapes=[pltpu.VMEM((L, HEAD_DIM), F32)],
        compiler_params=_cparams("parallel", "parallel"),
        name="hgrn",
    )(*args)


MLA_QW = 256
KR_W = 128
OD_TAIL = 1024


def _mla_prep_kernel(has_rope, emit_kr, *refs):
    it = iter(refs)
    ql_ref, kvl_ref, kr_ref, qn_ref, kvn_ref, wq_ref = (next(it) for _ in range(6))
    if has_rope:
        kc_ref, ksa_ref, ksb_ref = (next(it) for _ in range(3))
    q_ref, ckv_ref, kro_ref = next(it), next(it), next(it)
    if emit_kr:
        krn_ref = next(it)
        krn_ref[...] = kr_ref[:, :ROPE]

    nf = ROPE // 4
    qn = (_rms(ql_ref[...]) * qn_ref[...]).astype(BF16)
    q = _dot(qn, wq_ref[...])
    for h in range(MLA_HEADS):
        nope = slice(h * MLA_QW, h * MLA_QW + NOPE)
        rot = slice(h * MLA_QW + NOPE, (h + 1) * MLA_QW)
        q_ref[:, nope] = q[:, nope].astype(q_ref.dtype)
        qr = q[:, rot]
        if has_rope:
            qr = _rope(qr, kc_ref[...], ksa_ref[...], ksb_ref[...], nf)
        q_ref[:, rot] = qr.astype(q_ref.dtype)
    ckv_ref[...] = _rms(kvl_ref[...]) * kvn_ref[...]
    kr = kr_ref[...]
    if has_rope:
        kr = _rope(kr, kc_ref[...], ksa_ref[...], ksb_ref[...], nf)
    kro_ref[...] = kr.astype(kro_ref.dtype)


def _mla_prep(grp, tail, q_norm, kv_norm, wq, rope=None, emit_kr=False):
    tm = min(512, grp.L)
    per = grp.L // tm
    n_rows = grp.rows
    in_specs = [
        pl.BlockSpec((tm, Q_LORA), lambda i: (i, 0)),
        pl.BlockSpec((tm, KV_LORA), lambda i: (i, Q_LORA // KV_LORA)),
        pl.BlockSpec((tm, KR_W), lambda i: (i, (Q_LORA + KV_LORA) // KR_W)),
        pl.BlockSpec((1, Q_LORA), lambda i: (0, 0)),
        pl.BlockSpec((1, KV_LORA), lambda i: (0, 0)),
        pl.BlockSpec((Q_LORA, MLA_HEADS * MLA_QW), lambda i: (0, 0)),
    ]
    args = [tail, tail, tail, q_norm.reshape(1, Q_LORA), kv_norm.reshape(1, KV_LORA), wq]
    if rope is not None:
        in_specs += [pl.BlockSpec((tm, KR_W), lambda i: (i % per, 0))] * 3
        args += list(rope)
    out_shape = [jax.ShapeDtypeStruct((n_rows, MLA_HEADS * MLA_QW), BF16),
                 jax.ShapeDtypeStruct((n_rows, KV_LORA), F32),
                 jax.ShapeDtypeStruct((n_rows, KR_W), BF16)]
    out_specs = [pl.BlockSpec((tm, MLA_HEADS * MLA_QW), lambda i: (i, 0)),
                 pl.BlockSpec((tm, KV_LORA), lambda i: (i, 0)),
                 pl.BlockSpec((tm, KR_W), lambda i: (i, 0))]
    if emit_kr:
        out_shape.append(jax.ShapeDtypeStruct((n_rows, ROPE), F32))
        out_specs.append(pl.BlockSpec((tm, ROPE), lambda i: (i, 0)))
    return pl.pallas_call(
        functools.partial(_mla_prep_kernel, rope is not None, emit_kr),
        out_shape=tuple(out_shape),
        grid=(n_rows // tm,),
        in_specs=in_specs,
        out_specs=tuple(out_specs),
        compiler_params=_cparams("parallel"),
        name="mla_prep",
    )(*args)


def _mla_attn_kernel(L, has_ctx, *refs):
    it = iter(refs)
    q_ref, ckv_ref, kr_ref, wkv_ref = (next(it) for _ in range(4))
    if has_ctx:
        cckv_ref, ckr_ref = next(it), next(it)
    o_ref = next(it)

    ckv = ckv_ref[...].astype(BF16)
    kr = kr_ref[...]
    if has_ctx:
        ckv = jnp.concatenate([ckv, cckv_ref[...].astype(BF16)], axis=0)
        kr = jnp.concatenate([kr, ckr_ref[...].astype(BF16)], axis=0)
    ones = jnp.ones((ckv.shape[0], V_DIM), BF16)
    qb = min(L, 256)
    for hh in range(q_ref.shape[1] // MLA_QW):
        qc = slice(hh * MLA_QW, (hh + 1) * MLA_QW)
        oc = slice(hh * V_DIM, (hh + 1) * V_DIM)
        kv = _dot(ckv, wkv_ref[:, qc])
        kh = jnp.concatenate([kv[:, :NOPE].astype(BF16), kr], axis=1)
        vh = jnp.concatenate([kv[:, NOPE:].astype(BF16), ones], axis=1)
        for i in range(L // qb):
            rows = slice(i * qb, (i + 1) * qb)
            s = _dot_nt(q_ref[rows, qc], kh) * MLA_SCALE
            m = jnp.max(s, axis=-1, keepdims=True)
            oa = _dot(jnp.exp(s - m).astype(BF16), vh)
            o_ref[rows, oc] = (oa[:, :V_DIM] / oa[:, V_DIM:]).astype(o_ref.dtype)


def _mla_attn(grp, q, ckv, kr, wkv, hps, ctx=None):
    L = grp.L
    in_specs = [
        pl.BlockSpec((L, hps * MLA_QW), lambda b, h: (b, h)),
        pl.BlockSpec((L, KV_LORA), lambda b, h: (b, 0)),
        pl.BlockSpec((L, KR_W), lambda b, h: (b, 0)),
        pl.BlockSpec((KV_LORA, hps * (NOPE + V_DIM)), lambda b, h: (0, h)),
    ]
    args = [q, ckv, kr, wkv]
    if ctx is not None:
        cckv, ckr, o = ctx
        in_specs += [pl.BlockSpec((None, None, PAST_LEN, KV_LORA), lambda b, h: (b, o, 0, 0)),
                     pl.BlockSpec((None, None, PAST_LEN, KR_W), lambda b, h: (b, o, 0, 0))]
        args += [cckv, ckr]
    return pl.pallas_call(
        functools.partial(_mla_attn_kernel, L, ctx is not None),
        out_shape=jax.ShapeDtypeStruct((grp.rows, MLA_HEADS * V_DIM), BF16),
        grid=(grp.nb, MLA_HEADS // hps),
        in_specs=in_specs,
        out_specs=pl.BlockSpec((L, hps * V_DIM), lambda b, h: (b, h)),
        compiler_params=_cparams("parallel", "parallel"),
        name="mla_attn",
    )(*args)


def kernel(x_prompt, x_sample, c, c_ctx, cache_attn_k, cache_attn_v, cache_mla_ckv, cache_mla_krope, state_hgrn, mod_w, mod_b, norm_g, ffn_wg, ffn_wu, ffn_wd, ev_w_in, ev_w_out, hy_conv_w, hy_conv_b, hy_f_w1, hy_f_b1, hy_f_w2, hy_f_b2, hy_f_w3, hy_f_freq, hy_bias, attn_sink, od_w_in, od_w_out, hg_lb, hg_norm, mla_q_norm, mla_w_qb, mla_kv_norm, mla_w_kvb):
    depth = mod_w.shape[0]
    groups = (PROMPT, LATENT)
    xs = [x_prompt.reshape(PROMPT.rows, D_MODEL), x_sample.reshape(LATENT.rows, D_MODEL)]
    cvec = jnp.concatenate([c_ctx[None, :], c, jnp.zeros((MOD_ROWS - 1 - DEC_BATCH, D_MODEL), F32)], axis=0)
    mods_all = _modulation(cvec, mod_w, mod_b)

    wg, wu, wd = ffn_wg.astype(BF16), ffn_wu.astype(BF16), ffn_wd.astype(BF16)
    hy_tc = {SEQ: 512, DEC_SEQ: 256}
    hy_nseq = {SEQ: 4, DEC_SEQ: 2}

    new_k = new_v = new_ckv = new_kr = new_s = None
    for l in range(depth):
        mods = mods_all[l]
        xs = [_ffn(grp, x, mods, 0, norm_g[l, 0], norm_g[l, 1], wg, wu, wd, l, 0) for grp, x in zip(groups, xs)]
        if l % 2 == 0:
            e = l // 2
            w_in = ev_w_in[e].astype(BF16)
            ev_steps = EV_IN // TN_EVEN
            w_out = ev_w_out[e].astype(BF16)
            kv_cols = 2 * ATT_KV_HEADS * HEAD_DIM
            mix = []
            for grp, x in zip(groups, xs):
                p, kv = _inproj(grp, x, mods, norm_g[l, 2], w_in, w_in, ev_steps - 1, TN_EVEN, ev_steps, kv_cols)
                dft = _dft_matrices(grp.L)
                tc = hy_tc[grp.L]
                spectra = _hyena_spectra(grp.L, tc, dft, hy_f_w1[e], hy_f_b1[e], hy_f_w2[e], hy_f_b2[e],
                                         hy_f_w3[e], hy_f_freq[e])
                hy = _hyena(grp, p, tc, hy_nseq[grp.L], spectra, dft, hy_conv_w[e], hy_conv_b[e], hy_bias[e])
                if grp.latent:
                    rope = _rope_tables(grp.L, HEAD_DIM, 0, HEAD_DIM)
                    (att,) = _gqa(grp, p, kv, attn_sink[e], 1, window=WINDOW,
                                  ctx=(cache_attn_k, cache_attn_v, e), rope=rope)
                else:
                    att, new_k, new_v = _gqa(grp, p, kv, attn_sink[e], 2, nseq=2, emit_kv=True)
                mix.append((hy, att))
        else:
            o = l // 2
            n_main = OD_IN_PAD - TN_ODD
            w_in = od_w_in[o].T.astype(BF16)
            w_in_last = jnp.pad(w_in[n_main:], ((0, OD_IN_PAD - OD_IN), (0, 0)))
            od_steps = OD_IN_PAD // TN_ODD
            w_out = od_w_out[o].astype(BF16)
            wq = mla_w_qb[o].reshape(Q_LORA, MLA_HEADS, NOPE + ROPE)
            wq = jnp.pad(wq, ((0, 0), (0, 0), (0, MLA_QW - NOPE - ROPE))).reshape(Q_LORA, -1).astype(BF16)
            wkv = mla_w_kvb[o].astype(BF16)
            mix = []
            for grp, x in zip(groups, xs):
                p, tail = _inproj(grp, x, mods, norm_g[l, 2], w_in, w_in_last, 0, TN_ODD, od_steps, OD_TAIL,
                                  w_transposed=True)
                if grp.latent:
                    (hg,) = _hgrn(grp, p, hg_lb, hg_norm[o], 2, state=(state_hgrn, o))
                    rope = _rope_tables(grp.L, ROPE, 0, KR_W)
                    q, ckv, kr = _mla_prep(grp, tail, mla_q_norm[o], mla_kv_norm[o], wq, rope=rope)
                    ckr = jnp.pad(cache_mla_krope, ((0, 0), (0, 0), (0, 0), (0, KR_W - ROPE)))
                    att = _mla_attn(grp, q, ckv, kr, wkv, 4, ctx=(cache_mla_ckv, ckr, o))
                else:
                    hg, new_s = _hgrn(grp, p, hg_lb, hg_norm[o], 4, emit_state=True)
                    q, ckv, kr, kr_raw = _mla_prep(grp, tail, mla_q_norm[o], mla_kv_norm[o], wq, emit_kr=True)
                    att = _mla_attn(grp, q, ckv, kr, wkv, 8)
                    new_ckv = ckv.reshape(BATCH, 1, SEQ, KV_LORA)
                    new_kr = kr_raw.reshape(BATCH, 1, SEQ, ROPE)
                mix.append((hg, att))
        xs = [_outproj(grp, a, b, w_out, x, mods, norm_g[l, 3])
              for grp, x, (a, b) in zip(groups, xs, mix)]
        xs = [_ffn(grp, x, mods, 2, norm_g[l, 4], norm_g[l, 5], wg, wu, wd, l, 1) for grp, x in zip(groups, xs)]

    y_prompt = xs[0].reshape(BATCH, SEQ, D_MODEL)
    y_sample = xs[1].reshape(DEC_BATCH, DEC_SEQ, D_MODEL)
    return (y_prompt, y_sample, new_k, new_v, new_ckv, new_kr, new_s)
```

```python
import functools
import math
from typing import NamedTuple

import jax
import jax.numpy as jnp
from jax import lax
from jax.experimental import pallas as pl
from jax.experimental.pallas import tpu as pltpu

D_MODEL = 2048
BATCH = 16
SEQ = 256
DEC_BATCH = 8
DEC_SEQ = 1024
PAST_LEN = 512
GRID_W = 64
HEAD_DIM = 128
HY_W = 1024
HY_ORDER = 2
SHORT_CONV = 3
POS_EMB = 33
FILTER_ORDER = 64
HY_FAST_DECAY = 0.3
HY_SLOW_DECAY = 1.5
HY_TARGET = 1e-2
ATT_HEADS = 8
ATT_KV_HEADS = 2
ATT_GROUP = 4
WINDOW = 128
EV_IN = 3 * HY_W + (ATT_HEADS + 2 * ATT_KV_HEADS) * HEAD_DIM
HG_W = 1024
HG_HEADS = 8
CHUNK = 64
Q_LORA = 512
KV_LORA = 256
NOPE = 128
ROPE = 64
V_DIM = 128
MLA_HEADS = 8
OD_IN = 5 * HG_W + Q_LORA + KV_LORA + ROPE
OD_IN_PAD = 6144
D_FF = 5632
MACARON_W = 0.5
N_MOD = 9
ROPE_BASE = 10000.0
EPS = 1e-6
ATT_SCALE = HEAD_DIM ** -0.5
MLA_SCALE = (NOPE + ROPE) ** -0.5

MOD_ROWS = 16

V7X_VMEM_LIMIT = 56 * 1024 * 1024
TM = 512
TM_FFN = 1024
BIG_TILE_VMEM_LIMIT = 60 * 1024 * 1024
TM_IN = 1024
RC = 512
TF = 512
TN_EVEN = 1536
TN_ODD = 1536
MOD_TN = 2048

BF16 = jnp.bfloat16
F32 = jnp.float32
HIGHEST = lax.Precision.HIGHEST


class Group(NamedTuple):
    nb: int
    L: int
    latent: bool

    @property
    def rows(self):
        return self.nb * self.L

    def mod_row(self, i, tm):
        return 1 + i // (self.L // tm) if self.latent else 0


PROMPT = Group(BATCH, SEQ, False)
LATENT = Group(DEC_BATCH, DEC_SEQ, True)


def _cparams(*sem):
    return pltpu.CompilerParams(dimension_semantics=sem, vmem_limit_bytes=V7X_VMEM_LIMIT)


def _rms(x):
    return x * lax.rsqrt(jnp.mean(x * x, axis=-1, keepdims=True) + EPS)


def _silu(x):
    return x * jax.nn.sigmoid(x)


def _dot(a, b):
    return jnp.dot(a, b, preferred_element_type=F32)


def _dot_nt(a, b):
    return lax.dot_general(a, b, (((1,), (1,)), ((), ())), preferred_element_type=F32)


def _dot_tn(a, b):
    return lax.dot_general(a, b, (((0,), (0,)), ((), ())), preferred_element_type=F32)


def _row_chunks(n_rows, body, rc=RC):
    def step(c, carry):
        body(pl.ds(pl.multiple_of(c * rc, rc), rc))
        return carry
    lax.fori_loop(0, n_rows // rc, step, 0, unroll=True)


def _modulated_norm(x_ref, h_scr, gain, shift):
    def body(rows):
        h_scr[rows, :] = (_rms(x_ref[rows, :]) * gain + shift).astype(BF16)
    _row_chunks(x_ref.shape[0], body)


def _mod_kernel(c_ref, w_ref, b_ref, o_ref):
    s = _silu(c_ref[...]).astype(BF16)
    o_ref[...] = _dot(s, w_ref[...].astype(BF16)) + b_ref[...]


def _modulation(cvec, mod_w, mod_b):
    depth = mod_w.shape[0]
    n = N_MOD * D_MODEL
    out = pl.pallas_call(
        _mod_kernel,
        out_shape=jax.ShapeDtypeStruct((depth, MOD_ROWS, n), F32),
        grid=(depth, n // MOD_TN),
        in_specs=[
            pl.BlockSpec((MOD_ROWS, D_MODEL), lambda l, j: (0, 0)),
            pl.BlockSpec((None, D_MODEL, MOD_TN), lambda l, j: (l, 0, j)),
            pl.BlockSpec((None, 1, MOD_TN), lambda l, j: (l, 0, j)),
        ],
        out_specs=pl.BlockSpec((None, MOD_ROWS, MOD_TN), lambda l, j: (l, 0, j)),
        compiler_params=_cparams("parallel", "parallel"),
        name="modulation",
    )(cvec, mod_w, mod_b.reshape(depth, 1, n))
    return out.reshape(depth, MOD_ROWS, N_MOD, D_MODEL)


def _ffn_kernel(j, x_ref, mod_ref, gpre_ref, gpost_ref, wg_ref, wu_ref, wd_ref, o_ref, h_scr):
    f = pl.program_id(1)

    @pl.when(f == 0)
    def _():
        gain = gpre_ref[...] * (1.0 + mod_ref[3 * j + 1:3 * j + 2, :])
        _modulated_norm(x_ref, h_scr, gain, mod_ref[3 * j:3 * j + 1, :])

    def partial_down(rows):
        h = h_scr[rows, :]
        a = (_silu(_dot(h, wg_ref[...])) * _dot(h, wu_ref[...])).astype(BF16)
        return _dot(a, wd_ref[...])

    @pl.when(f == 0)
    def _():
        def first(rows):
            o_ref[rows, :] = partial_down(rows)
        _row_chunks(o_ref.shape[0], first)

    last = pl.num_programs(1) - 1

    @pl.when((f > 0) & (f < last))
    def _():
        def accumulate(rows):
            o_ref[rows, :] += partial_down(rows)
        _row_chunks(o_ref.shape[0], accumulate)

    @pl.when(f == last)
    def _():
        gain = (MACARON_W * mod_ref[3 * j + 2:3 * j + 3, :]) * gpost_ref[...]

        def finish(rows):
            acc = o_ref[rows, :] + partial_down(rows)
            o_ref[rows, :] = x_ref[rows, :] + _rms(acc) * gain
        _row_chunks(o_ref.shape[0], finish, RC // 2)


def _ffn(grp, x, mods, j, g_pre, g_post, wg, wu, wd, l, s):
    return pl.pallas_call(
        functools.partial(_ffn_kernel, j),
        out_shape=jax.ShapeDtypeStruct((grp.rows, D_MODEL), F32),
        grid=(grp.rows // TM_FFN, D_FF // TF),
        in_specs=[
            pl.BlockSpec((TM_FFN, D_MODEL), lambda i, f: (i, 0)),
            pl.BlockSpec((None, N_MOD, D_MODEL), lambda i, f: (grp.mod_row(i, TM_FFN), 0, 0)),
            pl.BlockSpec((1, D_MODEL), lambda i, f: (0, 0)),
            pl.BlockSpec((1, D_MODEL), lambda i, f: (0, 0)),
            pl.BlockSpec((None, None, D_MODEL, TF), lambda i, f: (l, s, 0, f)),
            pl.BlockSpec((None, None, D_MODEL, TF), lambda i, f: (l, s, 0, f)),
            pl.BlockSpec((None, None, TF, D_MODEL), lambda i, f: (l, s, f, 0)),
        ],
        out_specs=pl.BlockSpec((TM_FFN, D_MODEL), lambda i, f: (i, 0)),
        scratch_shapes=[pltpu.VMEM((TM_FFN, D_MODEL), BF16)],
        compiler_params=pltpu.CompilerParams(dimension_semantics=("parallel", "arbitrary"),
                                             vmem_limit_bytes=BIG_TILE_VMEM_LIMIT),
        name="ffn",
    )(x, mods, g_pre.reshape(1, D_MODEL), g_post.reshape(1, D_MODEL), wg, wu, wd)


def _inproj_kernel(w_transposed, x_ref, mod_ref, g_ref, w_ref, wl_ref, o_ref, t_ref, h_scr):
    k = pl.program_id(1)
    last = pl.num_programs(1) - 1

    def project(weights_ref):
        y = (_dot_nt if w_transposed else _dot)(h_scr[...], weights_ref[...])
        o_ref[...] = y.astype(o_ref.dtype)
        return y

    @pl.when(k == 0)
    def _():
        _modulated_norm(x_ref, h_scr, g_ref[...] * (1.0 + mod_ref[4:5, :]), mod_ref[3:4, :])
        project(w_ref)

    @pl.when((k > 0) & (k < last))
    def _():
        project(w_ref)

    @pl.when(k == last)
    def _():
        y = project(wl_ref)
        t_ref[...] = y[:, y.shape[1] - t_ref.shape[1]:]


def _inproj(grp, x, mods, g, w, w_last, last_block, tn, steps, n_tail, w_transposed=False):
    assert steps >= 2
    tm = TM_IN
    if w_transposed:
        w_specs = [pl.BlockSpec((tn, D_MODEL), lambda i, k: (jnp.minimum(k, steps - 2), 0)),
                   pl.BlockSpec((tn, D_MODEL), lambda i, k: (last_block, 0), pipeline_mode=pl.Buffered(1))]
    else:
        w_specs = [pl.BlockSpec((D_MODEL, tn), lambda i, k: (0, jnp.minimum(k, steps - 2))),
                   pl.BlockSpec((D_MODEL, tn), lambda i, k: (0, last_block), pipeline_mode=pl.Buffered(1))]
    return pl.pallas_call(
        functools.partial(_inproj_kernel, w_transposed),
        out_shape=(jax.ShapeDtypeStruct((grp.rows, steps * tn), BF16),
                   jax.ShapeDtypeStruct((grp.rows, n_tail), F32)),
        grid=(grp.rows // tm, steps),
        in_specs=[
            pl.BlockSpec((tm, D_MODEL), lambda i, k: (i, 0)),
            pl.BlockSpec((None, N_MOD, D_MODEL), lambda i, k: (grp.mod_row(i, tm), 0, 0)),
            pl.BlockSpec((1, D_MODEL), lambda i, k: (0, 0)),
            *w_specs,
        ],
        out_specs=(pl.BlockSpec((tm, tn), lambda i, k: (i, k)),
                   pl.BlockSpec((tm, n_tail), lambda i, k: (i, 0))),
        scratch_shapes=[pltpu.VMEM((tm, D_MODEL), BF16)],
        compiler_params=pltpu.CompilerParams(dimension_semantics=("parallel", "arbitrary"),
                                             vmem_limit_bytes=BIG_TILE_VMEM_LIMIT),
        name="inproj",
    )(x, mods, g.reshape(1, D_MODEL), w, w_last)


def _outproj_kernel(a_ref, b_ref, wa_ref, wb_ref, x_ref, mod_ref, g_ref, o_ref):
    y = _dot(a_ref[...], wa_ref[...]) + _dot(b_ref[...], wb_ref[...])
    o_ref[...] = x_ref[...] + _rms(y) * (mod_ref[5:6, :] * g_ref[...])


def _outproj(grp, a, b, w, x, mods, g):
    half = a.shape[1]
    tm = TM
    return pl.pallas_call(
        _outproj_kernel,
        out_shape=jax.ShapeDtypeStruct((grp.rows, D_MODEL), F32),
        grid=(grp.rows // tm,),
        in_specs=[
            pl.BlockSpec((tm, half), lambda i: (i, 0)),
            pl.BlockSpec((tm, half), lambda i: (i, 0)),
            pl.BlockSpec((half, D_MODEL), lambda i: (0, 0), pipeline_mode=pl.Buffered(1)),
            pl.BlockSpec((half, D_MODEL), lambda i: (1, 0), pipeline_mode=pl.Buffered(1)),
            pl.BlockSpec((tm, D_MODEL), lambda i: (i, 0)),
            pl.BlockSpec((None, N_MOD, D_MODEL), lambda i: (grp.mod_row(i, tm), 0, 0)),
            pl.BlockSpec((1, D_MODEL), lambda i: (0, 0)),
        ],
        out_specs=pl.BlockSpec((tm, D_MODEL), lambda i: (i, 0)),
        compiler_params=_cparams("parallel"),
        name="outproj",
    )(a, b, w, w, x, mods, g.reshape(1, D_MODEL))


def _rope_tables(L, rot_dim, lane0, width):
    half = rot_dim // 2
    inv = ROPE_BASE ** (-jnp.arange(0, half, 2, dtype=F32) / half)
    pos = jnp.arange(L)
    ang_r = (pos // GRID_W).astype(F32)[:, None] * inv
    ang_c = (pos % GRID_W).astype(F32)[:, None] * inv
    cr, sr, cc, sc = jnp.cos(ang_r), jnp.sin(ang_r), jnp.cos(ang_c), jnp.sin(ang_c)
    z = jnp.zeros_like(sr)
    cos = jnp.concatenate([cr, cr, cc, cc], axis=-1)
    sin_a = jnp.concatenate([-sr, z, -sc, z], axis=-1)
    sin_b = jnp.concatenate([z, sr, z, sc], axis=-1)
    pad = ((0, 0), (lane0, width - lane0 - rot_dim))
    return jnp.pad(cos, pad, constant_values=1.0), jnp.pad(sin_a, pad), jnp.pad(sin_b, pad)


def _rope(x, cos, sin_a, sin_b, nf):
    w = x.shape[-1]
    return x * cos + pltpu.roll(x, w - nf, 1) * sin_a + pltpu.roll(x, nf, 1) * sin_b


DFT_SPLIT = 32


def _dft_matrices(L):
    s = jnp.arange(L, dtype=jnp.int32)[None, :]

    def trig(k):
        ang = ((k[:, None] * s) % (2 * L)).astype(F32) * (math.pi / L)
        return jnp.cos(ang), jnp.sin(ang)

    c1, s1 = trig(jnp.arange(0, L, DFT_SPLIT, dtype=jnp.int32))
    c0, s0 = trig(jnp.arange(DFT_SPLIT, dtype=jnp.int32))
    cos = (c1[:, None, :] * c0[None] - s1[:, None, :] * s0[None]).reshape(L, L)
    sin = (s1[:, None, :] * c0[None] + c1[:, None, :] * s0[None]).reshape(L, L)
    nyq = (1 - 2 * (jnp.arange(L, dtype=jnp.int32) % 2)).astype(F32)
    k = jnp.arange(L, dtype=jnp.int32)[:, None]
    msin = jnp.where(k == 0, nyq[None, :], -sin)
    msin_t = jnp.where(s == 0, nyq[:, None], -sin)
    return cos.astype(BF16), msin.astype(BF16), msin_t.astype(BF16)


def _filter_kernel(L, z_ref, w1_ref, b1_ref, w2_ref, b2_ref, fr_ref, w3f_ref, w3b_ref, dl_ref, cos_ref, msin_ref,
                   ka_ref, kb_ref, kc_ref, h_scr):
    z = z_ref[...]

    @pl.when((pl.program_id(0) == 0) & (pl.program_id(1) == 0))
    def _():
        fr = fr_ref[...]
        h1 = jnp.sin(fr * (jnp.dot(z, w1_ref[...], precision=HIGHEST, preferred_element_type=F32) + b1_ref[...]))
        h_scr[...] = jnp.sin(
            fr * (jnp.dot(h1, w2_ref[...], precision=HIGHEST, preferred_element_type=F32) + b2_ref[...]))

    h = h_scr[...]
    decay = jnp.exp(-z[:, 0:1] * dl_ref[...])
    hf = jnp.dot(h, w3f_ref[...], precision=HIGHEST, preferred_element_type=F32) * decay
    hb = jnp.dot(h, w3b_ref[...], precision=HIGHEST, preferred_element_type=F32) * decay
    row = lax.broadcasted_iota(jnp.int32, hf.shape, 0)
    row0 = row == 0
    hb = jnp.where(row0, 0.0, hb)
    even = hf + hb
    re = _dot(cos_ref[...], even.astype(BF16))
    im = _dot(msin_ref[...], (hf - hb).astype(BF16))
    nyq = jnp.sum(jnp.where(row % 2 == 0, even, -even), axis=0, keepdims=True)
    sc = jnp.where(row0, 0.5 / L, 1.0 / L)
    ka_ref[...] = re * sc
    kb_ref[...] = jnp.where(row0, 0.0, im) * sc
    kc_ref[...] = jnp.where(row0, nyq, re) * sc


def _hyena_spectra(L, tc, dft, f_w1, f_b1, f_w2, f_b2, f_w3, f_freq):
    t = jnp.linspace(0.0, 1.0, L, dtype=F32)[:, None]
    bands = (POS_EMB - 1) // 2
    w = 2.0 * math.pi * jnp.arange(L, dtype=F32)[:, None] / L
    fb = jnp.linspace(1e-4, bands - 1, bands, dtype=F32)[None, :]
    z = jnp.concatenate([t, jnp.cos(fb * w), -jnp.sin(fb * w)], axis=-1)
    z = jnp.pad(z, ((0, 0), (0, 128 - POS_EMB)))
    w1 = jnp.pad(f_w1, ((0, 128 - POS_EMB), (0, 0)))
    deltas = jnp.abs(jnp.linspace(math.log(HY_TARGET) / HY_SLOW_DECAY,
                                  math.log(HY_TARGET) / HY_FAST_DECAY, HY_W, dtype=F32))[None, :]
    nct = HY_W // tc
    fo = FILTER_ORDER
    row = lambda a: a.reshape(1, fo)
    kshape = jax.ShapeDtypeStruct((HY_ORDER, L, HY_W), F32)
    kspec = pl.BlockSpec((None, L, tc), lambda n, c: (n, 0, c))
    const = lambda shape: pl.BlockSpec(shape, lambda n, c: (0, 0))
    return pl.pallas_call(
        functools.partial(_filter_kernel, L),
        out_shape=(kshape, kshape, kshape),
        grid=(HY_ORDER, nct),
        in_specs=[
            const((L, 128)), const((128, fo)), const((1, fo)), const((fo, fo)), const((1, fo)), const((1, fo)),
            pl.BlockSpec((fo, tc), lambda n, c: (0, 2 * n * nct + c)),
            pl.BlockSpec((fo, tc), lambda n, c: (0, (2 * n + 1) * nct + c)),
            pl.BlockSpec((1, tc), lambda n, c: (0, c)),
            const((L, L)), const((L, L)),
        ],
        out_specs=(kspec, kspec, kspec),
        scratch_shapes=[pltpu.VMEM((L, fo), F32)],
        compiler_params=_cparams("arbitrary", "arbitrary"),
        name="hyena_filter",
    )(z, w1, row(f_b1), f_w2, row(f_b2), row(f_freq), f_w3, f_w3, deltas, dft[0], dft[1])


def _hyena_kernel(L, nseq, uv_ref, ug0_ref, ug1_ref, cwv_ref, cwg0_ref, cwg1_ref, cbv_ref, cbg0_ref, cbg1_ref,
                  hb_ref, ka_ref, kb_ref, kc_ref, cos_ref, msin_ref, msint_ref, o_ref):
    row = lax.broadcasted_iota(jnp.int32, (L, uv_ref.shape[1]), 0)
    first, last = row == 0, row == L - 1
    seqs = [slice(i * L, (i + 1) * L) for i in range(nseq)]

    def short_conv(u_ref, rows, w_ref, b_ref):
        u = u_ref[rows, :].astype(F32)
        prev = jnp.where(first, 0.0, pltpu.roll(u, 1, 0))
        nxt = jnp.where(last, 0.0, pltpu.roll(u, L - 1, 0))
        return b_ref[...] + prev * w_ref[0:1, :] + u * w_ref[1:2, :] + nxt * w_ref[2:3, :]

    z = [short_conv(uv_ref, r, cwv_ref, cbv_ref) for r in seqs]
    gates = [(short_conv(ug0_ref, r, cwg0_ref, cbg0_ref), short_conv(ug1_ref, r, cwg1_ref, cbg1_ref))
             for r in seqs]
    for n in range(HY_ORDER):
        zb = [zi.astype(BF16) for zi in z]
        zre = [_dot(cos_ref[...], b) for b in zb]
        zim = [_dot(msin_ref[...], b) for b in zb]
        ka, kb, kc = ka_ref[n], kb_ref[n], kc_ref[n]
        yre = [(re * ka - im * kb).astype(BF16) for re, im in zip(zre, zim)]
        yim = [(re * kb + im * kc).astype(BF16) for re, im in zip(zre, zim)]
        conv = [_dot(cos_ref[...], a) + _dot(msint_ref[...], b) for a, b in zip(yre, yim)]
        z = [g[n] * (cv + hb_ref[n:n + 1, :] * zi) for g, cv, zi in zip(gates, conv, z)]
    for r, zi in zip(seqs, z):
        o_ref[r, :] = zi.astype(o_ref.dtype)


def _hyena(grp, p, tc, nseq, spectra, dft, conv_w, conv_b, h_bias):
    L = grp.L
    nct = HY_W // tc
    ka, kb, kc = spectra
    u_spec = lambda j: pl.BlockSpec((nseq * L, tc), lambda c, b: (b, j * nct + c))
    cw_spec = lambda j: pl.BlockSpec((SHORT_CONV, tc), lambda c, b: (0, j * nct + c))
    cb_spec = lambda j: pl.BlockSpec((1, tc), lambda c, b: (0, j * nct + c))
    k_spec = pl.BlockSpec((HY_ORDER, L, tc), lambda c, b: (0, 0, c))
    m_spec = pl.BlockSpec((L, L), lambda c, b: (0, 0))
    cb = conv_b.reshape(1, -1)
    return pl.pallas_call(
        functools.partial(_hyena_kernel, L, nseq),
        out_shape=jax.ShapeDtypeStruct((grp.rows, HY_W), BF16),
        grid=(nct, grp.nb // nseq),
        in_specs=[
            u_spec(0), u_spec(1), u_spec(2), cw_spec(0), cw_spec(1), cw_spec(2),
            cb_spec(0), cb_spec(1), cb_spec(2),
            pl.BlockSpec((HY_ORDER, tc), lambda c, b: (0, c)),
            k_spec, k_spec, k_spec, m_spec, m_spec, m_spec,
        ],
        out_specs=pl.BlockSpec((nseq * L, tc), lambda c, b: (b, c)),
        compiler_params=_cparams("parallel", "arbitrary"),
        name="hyena",
    )(p, p, p, conv_w, conv_w, conv_w, cb, cb, cb, h_bias, ka, kb, kc, *dft)


def _gqa_kernel(L, nseq, window, has_ctx, has_rope, emit_kv, *refs):
    it = iter(refs)
    q_ref, k_ref, v_ref, sink_ref = next(it), next(it), next(it), next(it)
    if has_ctx:
        kc_ref, vc_ref = next(it), next(it)
    if has_rope:
        cos_ref, sa_ref, sb_ref = next(it), next(it), next(it)
    o_ref = next(it)
    if emit_kv:
        kn_ref, vn_ref = next(it), next(it)

    qb = HEAD_DIM
    nf = HEAD_DIM // 4
    gw = ATT_GROUP * HEAD_DIM
    hps = k_ref.shape[1] // HEAD_DIM
    sink_row = sink_ref[...]
    sink_lane = lax.broadcasted_iota(jnp.int32, sink_row.shape, 1)
    band_bias = {}

    def bias_for(lo, hi, i):
        key = (lo - i * qb, hi - lo)
        if key not in band_bias:
            shape = (ATT_GROUP * qb, hi - lo)
            rel = key[0] + lax.broadcasted_iota(jnp.int32, shape, 1) - lax.broadcasted_iota(jnp.int32, shape, 0) % qb
            band_bias[key] = jnp.where(jnp.abs(rel) <= window, 0.0, -1e30)
        return band_bias[key]

    for sq in range(nseq):
        r0 = sq * L
        for hk in range(hps):
            head = pl.program_id(1) * hps + hk
            k = k_ref[r0:r0 + L, hk * HEAD_DIM:(hk + 1) * HEAD_DIM]
            v = v_ref[r0:r0 + L, hk * HEAD_DIM:(hk + 1) * HEAD_DIM]
            if emit_kv:
                kn_ref[sq, hk] = k
                vn_ref[sq, hk] = v
            if has_rope:
                k = _rope(k, cos_ref[...], sa_ref[...], sb_ref[...], nf)
            k = k.astype(BF16)
            v = jnp.concatenate([v.astype(BF16), jnp.ones((L, HEAD_DIM), BF16)], axis=1)
            if has_ctx:
                kc = kc_ref[hk].astype(BF16)
                vc = jnp.concatenate([vc_ref[hk].astype(BF16), jnp.ones((PAST_LEN, HEAD_DIM), BF16)], axis=1)
            sinks = [jnp.sum(jnp.where(sink_lane == head * ATT_GROUP + g, sink_row, 0.0), axis=1, keepdims=True)
                     for g in range(ATT_GROUP)]
            sink = jnp.concatenate([jnp.broadcast_to(s, (qb, 1)) for s in sinks], axis=0)

            for i in range(L // qb):
                rows = slice(i * qb, (i + 1) * qb)
                out_rows = slice(r0 + i * qb, r0 + (i + 1) * qb)
                qs = []
                for g in range(ATT_GROUP):
                    qg = q_ref[out_rows, hk * gw + g * HEAD_DIM:hk * gw + (g + 1) * HEAD_DIM].astype(F32)
                    if has_rope:
                        qg = _rope(qg, cos_ref[rows, :], sa_ref[rows, :], sb_ref[rows, :], nf)
                    qs.append((qg * ATT_SCALE).astype(BF16))
                q = jnp.concatenate(qs, axis=0)
                if window is None:
                    lo, hi = 0, L
                else:
                    lo, hi = max(0, (i - 1) * qb), min(L, (i + 2) * qb)
                s = _dot_nt(q, k[lo:hi])
                if window is not None:
                    s = s + bias_for(lo, hi, i)
                m = jnp.maximum(jnp.max(s, axis=-1, keepdims=True), sink)
                if has_ctx:
                    sc = _dot_nt(q, kc)
                    m = jnp.maximum(m, jnp.max(sc, axis=-1, keepdims=True))
                oa = _dot(jnp.exp(s - m).astype(BF16), v[lo:hi])
                if has_ctx:
                    oa = oa + _dot(jnp.exp(sc - m).astype(BF16), vc)
                o = oa[:, :HEAD_DIM] / (oa[:, HEAD_DIM:] + jnp.exp(sink - m))
                for g in range(ATT_GROUP):
                    o_ref[out_rows, hk * gw + g * HEAD_DIM:hk * gw + (g + 1) * HEAD_DIM] = (
                        o[g * qb:(g + 1) * qb].astype(o_ref.dtype))


def _gqa(grp, p, kv, sink, hps, nseq=1, window=None, ctx=None, rope=None, emit_kv=False):
    assert nseq == 1 or (ctx is None and rope is None)
    L = grp.L
    rows = nseq * L
    qw = hps * ATT_GROUP * HEAD_DIM
    kw = hps * HEAD_DIM
    nh = ATT_KV_HEADS // hps
    in_specs = [
        pl.BlockSpec((rows, qw), lambda b, h: (b, 3 * HY_W // qw + h)),
        pl.BlockSpec((rows, kw), lambda b, h: (b, h)),
        pl.BlockSpec((rows, kw), lambda b, h: (b, nh + h)),
        pl.BlockSpec((1, ATT_HEADS), lambda b, h: (0, 0)),
    ]
    args = [p, kv, kv, sink.reshape(1, ATT_HEADS)]
    if ctx is not None:
        kc, vc, e = ctx
        spec = pl.BlockSpec((None, None, hps, PAST_LEN, HEAD_DIM), lambda b, h: (b, e, h, 0, 0))
        in_specs += [spec, spec]
        args += [kc, vc]
    if rope is not None:
        in_specs += [pl.BlockSpec((L, HEAD_DIM), lambda b, h: (0, 0))] * 3
        args += list(rope)
    out_shape = [jax.ShapeDtypeStruct((grp.rows, ATT_HEADS * HEAD_DIM), BF16)]
    out_specs = [pl.BlockSpec((rows, qw), lambda b, h: (b, h))]
    if emit_kv:
        kv_shape = jax.ShapeDtypeStruct((grp.nb, 1, ATT_KV_HEADS, L, HEAD_DIM), F32)
        kv_spec = pl.BlockSpec((nseq, None, hps, L, HEAD_DIM), lambda b, h: (b, 0, h, 0, 0))
        out_shape += [kv_shape, kv_shape]
        out_specs += [kv_spec, kv_spec]
    return pl.pallas_call(
        functools.partial(_gqa_kernel, L, nseq, window, ctx is not None, rope is not None, emit_kv),
        out_shape=tuple(out_shape),
        grid=(grp.nb // nseq, nh),
        in_specs=in_specs,
        out_specs=tuple(out_specs),
        compiler_params=_cparams("parallel", "parallel"),
        name="gqa",
    )(*args)


def _chunk_cumsum(x, reverse):
    n = x.shape[0]
    pos = lax.broadcasted_iota(jnp.int32, x.shape, 0) % CHUNK
    s = 1
    while s < CHUNK:
        if reverse:
            x = x + jnp.where(pos < CHUNK - s, pltpu.roll(x, n - s, 0), 0.0)
        else:
            x = x + jnp.where(pos >= s, pltpu.roll(x, s, 0), 0.0)
        s *= 2
    return x


def _hgrn_kernel(L, has_state, emit_state, *refs):
    it = iter(refs)
    q_ref, ff_ref, fb_ref, i_ref, g_ref, lb_ref, norm_ref = (next(it) for _ in range(7))
    if has_state:
        s0_ref = next(it)
    o_ref = next(it)
    if emit_state:
        s_ref = next(it)
    acc_scr = next(it)

    a = lb_ref[...]
    e = jnp.exp(a - jnp.max(a, axis=0, keepdims=True))
    lb_all = e[1] / (e[0] + e[1])
    ci = lax.broadcasted_iota(jnp.int32, (CHUNK, CHUNK), 0)
    cj = lax.broadcasted_iota(jnp.int32, (CHUNK, CHUNK), 1)
    n_chunks = L // CHUNK
    chunks = [slice(n * CHUNK, (n + 1) * CHUNK) for n in range(n_chunks)]

    for hh in range(q_ref.shape[1] // HEAD_DIM):
        hc = slice(hh * HEAD_DIM, (hh + 1) * HEAD_DIM)
        q = _silu(q_ref[:, hc].astype(F32))
        for d, fz_ref in enumerate((ff_ref, fb_ref)):
            lbd = lb_all[d:d + 1, hc]
            keep = (cj <= ci) if d == 0 else (cj >= ci)
            qds, atts, decay, own = [], [], [], []
            for rows in chunks:
                f = lbd + (1.0 - lbd) * jax.nn.sigmoid(fz_ref[rows, hc].astype(F32))
                b = _chunk_cumsum(jnp.log(f), reverse=(d == 1))
                qd = (q[rows] * jnp.exp(b)).astype(BF16)
                kd32 = (1.0 - f) * jnp.exp(-b)
                dc = jnp.exp(b[CHUNK - 1:CHUNK] if d == 0 else b[0:1])
                qds.append(qd)
                decay.append(dc)
                atts.append(jnp.where(keep, _dot_nt(qd, kd32.astype(BF16)), 0.0).astype(BF16))
                own.append(_dot_tn(i_ref[rows, hc], (kd32 * dc).astype(BF16)))
            st = s0_ref[d, hh].T if has_state else jnp.zeros((HEAD_DIM, HEAD_DIM), F32)
            entering = [None] * n_chunks
            for n in (range(n_chunks) if d == 0 else range(n_chunks - 1, -1, -1)):
                entering[n] = st.astype(BF16)
                st = st * decay[n] + own[n]
            if emit_state:
                s_ref[d, hh] = st.T
            for n, rows in enumerate(chunks):
                o = _dot(atts[n], i_ref[rows, hc]) + _dot_nt(qds[n], entering[n])
                if d == 0:
                    acc_scr[rows, :] = o
                else:
                    acc_scr[rows, :] += o
        o = _rms(acc_scr[...]) * norm_ref[...] * _silu(g_ref[:, hc].astype(F32))
        o_ref[:, hc] = o.astype(o_ref.dtype)


def _hgrn(grp, p, hg_lb, norm_g, hps, state=None, emit_state=False):
    L = grp.L
    hw = hps * HEAD_DIM
    nh = HG_HEADS // hps
    col = lambda j: pl.BlockSpec((L, hw), lambda b, h: (b, j * nh + h))
    in_specs = [col(0), col(1), col(2), col(3), col(4),
                pl.BlockSpec((hg_lb.shape[0], 2, hw), lambda b, h: (0, 0, h)),
                pl.BlockSpec((1, HEAD_DIM), lambda b, h: (0, 0))]
    args = [p, p, p, p, p, hg_lb, norm_g.reshape(1, HEAD_DIM)]
    st_spec = lambda o: pl.BlockSpec((None, None, 2, hps, HEAD_DIM, HEAD_DIM), lambda b, h: (b, o, 0, h, 0, 0))
    if state is not None:
        s0, o = state
        in_specs.append(st_spec(o))
        args.append(s0)
    out_shape = [jax.ShapeDtypeStruct((grp.rows, HG_W), BF16)]
    out_specs = [pl.BlockSpec((L, hw), lambda b, h: (b, h))]
    if emit_state:
        out_shape.append(jax.ShapeDtypeStruct((grp.nb, 1, 2, HG_HEADS, HEAD_DIM, HEAD_DIM), F32))
        out_specs.append(st_spec(0))
    return pl.pallas_call(
        functools.partial(_hgrn_kernel, L, state is not None, emit_state),
        out_shape=tuple(out_shape),
        grid=(grp.nb, nh),
        in_specs=in_specs,
        out_specs=tuple(out_specs),
        scratch_shapes=[pltpu.VMEM((L, HEAD_DIM), F32)],
        compiler_params=_cparams("parallel", "parallel"),
        name="hgrn",
    )(*args)


MLA_QW = 256
KR_W = 128
OD_TAIL = 1024


def _mla_prep_kernel(has_rope, emit_kr, *refs):
    it = iter(refs)
    ql_ref, kvl_ref, kr_ref, qn_ref, kvn_ref, wq_ref = (next(it) for _ in range(6))
    if has_rope:
        kc_ref, ksa_ref, ksb_ref = (next(it) for _ in range(3))
    q_ref, ckv_ref, kro_ref = next(it), next(it), next(it)
    if emit_kr:
        krn_ref = next(it)
        krn_ref[...] = kr_ref[:, :ROPE]

    nf = ROPE // 4
    qn = (_rms(ql_ref[...]) * qn_ref[...]).astype(BF16)
    q = _dot(qn, wq_ref[...])
    for h in range(MLA_HEADS):
        nope = slice(h * MLA_QW, h * MLA_QW + NOPE)
        rot = slice(h * MLA_QW + NOPE, (h + 1) * MLA_QW)
        q_ref[:, nope] = q[:, nope].astype(q_ref.dtype)
        qr = q[:, rot]
        if has_rope:
            qr = _rope(qr, kc_ref[...], ksa_ref[...], ksb_ref[...], nf)
        q_ref[:, rot] = qr.astype(q_ref.dtype)
    ckv_ref[...] = _rms(kvl_ref[...]) * kvn_ref[...]
    kr = kr_ref[...]
    if has_rope:
        kr = _rope(kr, kc_ref[...], ksa_ref[...], ksb_ref[...], nf)
    kro_ref[...] = kr.astype(kro_ref.dtype)


def _mla_prep(grp, tail, q_norm, kv_norm, wq, rope=None, emit_kr=False):
    tm = min(512, grp.L)
    per = grp.L // tm
    n_rows = grp.rows
    in_specs = [
        pl.BlockSpec((tm, Q_LORA), lambda i: (i, 0)),
        pl.BlockSpec((tm, KV_LORA), lambda i: (i, Q_LORA // KV_LORA)),
        pl.BlockSpec((tm, KR_W), lambda i: (i, (Q_LORA + KV_LORA) // KR_W)),
        pl.BlockSpec((1, Q_LORA), lambda i: (0, 0)),
        pl.BlockSpec((1, KV_LORA), lambda i: (0, 0)),
        pl.BlockSpec((Q_LORA, MLA_HEADS * MLA_QW), lambda i: (0, 0)),
    ]
    args = [tail, tail, tail, q_norm.reshape(1, Q_LORA), kv_norm.reshape(1, KV_LORA), wq]
    if rope is not None:
        in_specs += [pl.BlockSpec((tm, KR_W), lambda i: (i % per, 0))] * 3
        args += list(rope)
    out_shape = [jax.ShapeDtypeStruct((n_rows, MLA_HEADS * MLA_QW), BF16),
                 jax.ShapeDtypeStruct((n_rows, KV_LORA), F32),
                 jax.ShapeDtypeStruct((n_rows, KR_W), BF16)]
    out_specs = [pl.BlockSpec((tm, MLA_HEADS * MLA_QW), lambda i: (i, 0)),
                 pl.BlockSpec((tm, KV_LORA), lambda i: (i, 0)),
                 pl.BlockSpec((tm, KR_W), lambda i: (i, 0))]
    if emit_kr:
        out_shape.append(jax.ShapeDtypeStruct((n_rows, ROPE), F32))
        out_specs.append(pl.BlockSpec((tm, ROPE), lambda i: (i, 0)))
    return pl.pallas_call(
        functools.partial(_mla_prep_kernel, rope is not None, emit_kr),
        out_shape=tuple(out_shape),
        grid=(n_rows // tm,),
        in_specs=in_specs,
        out_specs=tuple(out_specs),
        compiler_params=_cparams("parallel"),
        name="mla_prep",
    )(*args)


def _mla_attn_kernel(L, has_ctx, *refs):
    it = iter(refs)
    q_ref, ckv_ref, kr_ref, wkv_ref = (next(it) for _ in range(4))
    if has_ctx:
        cckv_ref, ckr_ref = next(it), next(it)
    o_ref = next(it)

    ckv = ckv_ref[...].astype(BF16)
    kr = kr_ref[...]
    if has_ctx:
        ckv = jnp.concatenate([ckv, cckv_ref[...].astype(BF16)], axis=0)
        kr = jnp.concatenate([kr, ckr_ref[...].astype(BF16)], axis=0)
    ones = jnp.ones((ckv.shape[0], V_DIM), BF16)
    qb = min(L, 256)
    for hh in range(q_ref.shape[1] // MLA_QW):
        qc = slice(hh * MLA_QW, (hh + 1) * MLA_QW)
        oc = slice(hh * V_DIM, (hh + 1) * V_DIM)
        kv = _dot(ckv, wkv_ref[:, qc])
        kh = jnp.concatenate([kv[:, :NOPE].astype(BF16), kr], axis=1)
        vh = jnp.concatenate([kv[:, NOPE:].astype(BF16), ones], axis=1)
        for i in range(L // qb):
            rows = slice(i * qb, (i + 1) * qb)
            s = _dot_nt(q_ref[rows, qc], kh) * MLA_SCALE
            m = jnp.max(s, axis=-1, keepdims=True)
            oa = _dot(jnp.exp(s - m).astype(BF16), vh)
            o_ref[rows, oc] = (oa[:, :V_DIM] / oa[:, V_DIM:]).astype(o_ref.dtype)


def _mla_attn(grp, q, ckv, kr, wkv, hps, ctx=None):
    L = grp.L
    in_specs = [
        pl.BlockSpec((L, hps * MLA_QW), lambda b, h: (b, h)),
        pl.BlockSpec((L, KV_LORA), lambda b, h: (b, 0)),
        pl.BlockSpec((L, KR_W), lambda b, h: (b, 0)),
        pl.BlockSpec((KV_LORA, hps * (NOPE + V_DIM)), lambda b, h: (0, h)),
    ]
    args = [q, ckv, kr, wkv]
    if ctx is not None:
        cckv, ckr, o = ctx
        in_specs += [pl.BlockSpec((None, None, PAST_LEN, KV_LORA), lambda b, h: (b, o, 0, 0)),
                     pl.BlockSpec((None, None, PAST_LEN, KR_W), lambda b, h: (b, o, 0, 0))]
        args += [cckv, ckr]
    return pl.pallas_call(
        functools.partial(_mla_attn_kernel, L, ctx is not None),
        out_shape=jax.ShapeDtypeStruct((grp.rows, MLA_HEADS * V_DIM), BF16),
        grid=(grp.nb, MLA_HEADS // hps),
        in_specs=in_specs,
        out_specs=pl.BlockSpec((L, hps * V_DIM), lambda b, h: (b, h)),
        compiler_params=_cparams("parallel", "parallel"),
        name="mla_attn",
    )(*args)


def kernel(x_prompt, x_sample, c, c_ctx, cache_attn_k, cache_attn_v, cache_mla_ckv, cache_mla_krope, state_hgrn, mod_w, mod_b, norm_g, ffn_wg, ffn_wu, ffn_wd, ev_w_in, ev_w_out, hy_conv_w, hy_conv_b, hy_f_w1, hy_f_b1, hy_f_w2, hy_f_b2, hy_f_w3, hy_f_freq, hy_bias, attn_sink, od_w_in, od_w_out, hg_lb, hg_norm, mla_q_norm, mla_w_qb, mla_kv_norm, mla_w_kvb):
    depth = mod_w.shape[0]
    groups = (PROMPT, LATENT)
    xs = [x_prompt.reshape(PROMPT.rows, D_MODEL), x_sample.reshape(LATENT.rows, D_MODEL)]
    cvec = jnp.concatenate([c_ctx[None, :], c, jnp.zeros((MOD_ROWS - 1 - DEC_BATCH, D_MODEL), F32)], axis=0)
    mods_all = _modulation(cvec, mod_w, mod_b)

    wg, wu, wd = ffn_wg.astype(BF16), ffn_wu.astype(BF16), ffn_wd.astype(BF16)
    hy_tc = {SEQ: 512, DEC_SEQ: 256}
    hy_nseq = {SEQ: 4, DEC_SEQ: 2}

    new_k = new_v = new_ckv = new_kr = new_s = None
    for l in range(depth):
        mods = mods_all[l]
        xs = [_ffn(grp, x, mods, 0, norm_g[l, 0], norm_g[l, 1], wg, wu, wd, l, 0) for grp, x in zip(groups, xs)]
        if l % 2 == 0:
            e = l // 2
            w_in = ev_w_in[e].astype(BF16)
            ev_steps = EV_IN // TN_EVEN
            w_out = ev_w_out[e].astype(BF16)
            kv_cols = 2 * ATT_KV_HEADS * HEAD_DIM
            mix = []
            for grp, x in zip(groups, xs):
                p, kv = _inproj(grp, x, mods, norm_g[l, 2], w_in, w_in, ev_steps - 1, TN_EVEN, ev_steps, kv_cols)
                dft = _dft_matrices(grp.L)
                tc = hy_tc[grp.L]
                spectra = _hyena_spectra(grp.L, tc, dft, hy_f_w1[e], hy_f_b1[e], hy_f_w2[e], hy_f_b2[e],
                                         hy_f_w3[e], hy_f_freq[e])
                hy = _hyena(grp, p, tc, hy_nseq[grp.L], spectra, dft, hy_conv_w[e], hy_conv_b[e], hy_bias[e])
                if grp.latent:
                    rope = _rope_tables(grp.L, HEAD_DIM, 0, HEAD_DIM)
                    (att,) = _gqa(grp, p, kv, attn_sink[e], 1, window=WINDOW,
                                  ctx=(cache_attn_k, cache_attn_v, e), rope=rope)
                else:
                    att, new_k, new_v = _gqa(grp, p, kv, attn_sink[e], 2, nseq=2, emit_kv=True)
                mix.append((hy, att))
        else:
            o = l // 2
            n_main = OD_IN_PAD - TN_ODD
            w_in = od_w_in[o].T.astype(BF16)
            w_in_last = jnp.pad(w_in[n_main:], ((0, OD_IN_PAD - OD_IN), (0, 0)))
            od_steps = OD_IN_PAD // TN_ODD
            w_out = od_w_out[o].astype(BF16)
            wq = mla_w_qb[o].reshape(Q_LORA, MLA_HEADS, NOPE + ROPE)
            wq = jnp.pad(wq, ((0, 0), (0, 0), (0, MLA_QW - NOPE - ROPE))).reshape(Q_LORA, -1).astype(BF16)
            wkv = mla_w_kvb[o].astype(BF16)
            mix = []
            for grp, x in zip(groups, xs):
                p, tail = _inproj(grp, x, mods, norm_g[l, 2], w_in, w_in_last, 0, TN_ODD, od_steps, OD_TAIL,
                                  w_transposed=True)
                if grp.latent:
                    (hg,) = _hgrn(grp, p, hg_lb, hg_norm[o], 4, state=(state_hgrn, o))
                    rope = _rope_tables(grp.L, ROPE, 0, KR_W)
                    q, ckv, kr = _mla_prep(grp, tail, mla_q_norm[o], mla_kv_norm[o], wq, rope=rope)
                    ckr = jnp.pad(cache_mla_krope, ((0, 0), (0, 0), (0, 0), (0, KR_W - ROPE)))
                    att = _mla_attn(grp, q, ckv, kr, wkv, 8, ctx=(cache_mla_ckv, ckr, o))
                else:
                    hg, new_s = _hgrn(grp, p, hg_lb, hg_norm[o], 4, emit_state=True)
                    q, ckv, kr, kr_raw = _mla_prep(grp, tail, mla_q_norm[o], mla_kv_norm[o], wq, emit_kr=True)
                    att = _mla_attn(grp, q, ckv, kr, wkv, 8)
                    new_ckv = ckv.reshape(BATCH, 1, SEQ, KV_LORA)
                    new_kr = kr_raw.reshape(BATCH, 1, SEQ, ROPE)
                mix.append((hg, att))
        xs = [_outproj(grp, a, b, w_out, x, mods, norm_g[l, 3])
              for grp, x, (a, b) in zip(groups, xs, mix)]
        xs = [_ffn(grp, x, mods, 2, norm_g[l, 4], norm_g[l, 5], wg, wu, wd, l, 1) for grp, x in zip(groups, xs)]

    y_prompt = xs[0].reshape(BATCH, SEQ, D_MODEL)
    y_sample = xs[1].reshape(DEC_BATCH, DEC_SEQ, D_MODEL)
    return (y_prompt, y_sample, new_k, new_v, new_ckv, new_kr, new_s)
```

```python
import functools
import math
from typing import NamedTuple

import jax
import jax.numpy as jnp
from jax import lax
from jax.experimental import pallas as pl
from jax.experimental.pallas import tpu as pltpu

D_MODEL = 2048
BATCH = 16
SEQ = 256
DEC_BATCH = 8
DEC_SEQ = 1024
PAST_LEN = 512
GRID_W = 64
HEAD_DIM = 128
HY_W = 1024
HY_ORDER = 2
SHORT_CONV = 3
POS_EMB = 33
FILTER_ORDER = 64
HY_FAST_DECAY = 0.3
HY_SLOW_DECAY = 1.5
HY_TARGET = 1e-2
ATT_HEADS = 8
ATT_KV_HEADS = 2
ATT_GROUP = 4
WINDOW = 128
EV_IN = 3 * HY_W + (ATT_HEADS + 2 * ATT_KV_HEADS) * HEAD_DIM
HG_W = 1024
HG_HEADS = 8
CHUNK = 64
Q_LORA = 512
KV_LORA = 256
NOPE = 128
ROPE = 64
V_DIM = 128
MLA_HEADS = 8
OD_IN = 5 * HG_W + Q_LORA + KV_LORA + ROPE
OD_IN_PAD = 6144
D_FF = 5632
MACARON_W = 0.5
N_MOD = 9
ROPE_BASE = 10000.0
EPS = 1e-6
ATT_SCALE = HEAD_DIM ** -0.5
MLA_SCALE = (NOPE + ROPE) ** -0.5

MOD_ROWS = 16

V7X_VMEM_LIMIT = 56 * 1024 * 1024
TM = 512
TM_FFN = 1024
BIG_TILE_VMEM_LIMIT = 60 * 1024 * 1024
TM_IN = 1024
RC = 512
TF = 256
TN_EVEN = 1536
TN_ODD = 1536
MOD_TN = 2048

BF16 = jnp.bfloat16
F32 = jnp.float32
HIGHEST = lax.Precision.HIGHEST


class Group(NamedTuple):
    nb: int
    L: int
    latent: bool

    @property
    def rows(self):
        return self.nb * self.L

    def mod_row(self, i, tm):
        return 1 + i // (self.L // tm) if self.latent else 0


PROMPT = Group(BATCH, SEQ, False)
LATENT = Group(DEC_BATCH, DEC_SEQ, True)


def _cparams(*sem):
    return pltpu.CompilerParams(dimension_semantics=sem, vmem_limit_bytes=V7X_VMEM_LIMIT)


def _rms(x):
    return x * lax.rsqrt(jnp.mean(x * x, axis=-1, keepdims=True) + EPS)


def _silu(x):
    return x * jax.nn.sigmoid(x)


def _dot(a, b):
    return jnp.dot(a, b, preferred_element_type=F32)


def _dot_nt(a, b):
    return lax.dot_general(a, b, (((1,), (1,)), ((), ())), preferred_element_type=F32)


def _dot_tn(a, b):
    return lax.dot_general(a, b, (((0,), (0,)), ((), ())), preferred_element_type=F32)


def _row_chunks(n_rows, body, rc=RC):
    def step(c, carry):
        body(pl.ds(pl.multiple_of(c * rc, rc), rc))
        return carry
    lax.fori_loop(0, n_rows // rc, step, 0, unroll=True)


def _modulated_norm(x_ref, h_scr, gain, shift):
    def body(rows):
        h_scr[rows, :] = (_rms(x_ref[rows, :]) * gain + shift).astype(BF16)
    _row_chunks(x_ref.shape[0], body)


def _mod_kernel(c_ref, w_ref, b_ref, o_ref):
    s = _silu(c_ref[...]).astype(BF16)
    o_ref[...] = _dot(s, w_ref[...].astype(BF16)) + b_ref[...]


def _modulation(cvec, mod_w, mod_b):
    depth = mod_w.shape[0]
    n = N_MOD * D_MODEL
    out = pl.pallas_call(
        _mod_kernel,
        out_shape=jax.ShapeDtypeStruct((depth, MOD_ROWS, n), F32),
        grid=(depth, n // MOD_TN),
        in_specs=[
            pl.BlockSpec((MOD_ROWS, D_MODEL), lambda l, j: (0, 0)),
            pl.BlockSpec((None, D_MODEL, MOD_TN), lambda l, j: (l, 0, j)),
            pl.BlockSpec((None, 1, MOD_TN), lambda l, j: (l, 0, j)),
        ],
        out_specs=pl.BlockSpec((None, MOD_ROWS, MOD_TN), lambda l, j: (l, 0, j)),
        compiler_params=_cparams("parallel", "parallel"),
        name="modulation",
    )(cvec, mod_w, mod_b.reshape(depth, 1, n))
    return out.reshape(depth, MOD_ROWS, N_MOD, D_MODEL)


def _ffn_kernel(j, x_ref, mod_ref, gpre_ref, gpost_ref, wg_ref, wu_ref, wd_ref, o_ref, h_scr):
    f = pl.program_id(1)

    @pl.when(f == 0)
    def _():
        gain = gpre_ref[...] * (1.0 + mod_ref[3 * j + 1:3 * j + 2, :])
        _modulated_norm(x_ref, h_scr, gain, mod_ref[3 * j:3 * j + 1, :])

    def partial_down(rows):
        h = h_scr[rows, :]
        a = (_silu(_dot(h, wg_ref[...].astype(BF16))) * _dot(h, wu_ref[...].astype(BF16))).astype(BF16)
        return _dot(a, wd_ref[...].astype(BF16))

    @pl.when(f == 0)
    def _():
        def first(rows):
            o_ref[rows, :] = partial_down(rows)
        _row_chunks(o_ref.shape[0], first)

    last = pl.num_programs(1) - 1

    @pl.when((f > 0) & (f < last))
    def _():
        def accumulate(rows):
            o_ref[rows, :] += partial_down(rows)
        _row_chunks(o_ref.shape[0], accumulate)

    @pl.when(f == last)
    def _():
        gain = (MACARON_W * mod_ref[3 * j + 2:3 * j + 3, :]) * gpost_ref[...]

        def finish(rows):
            acc = o_ref[rows, :] + partial_down(rows)
            o_ref[rows, :] = x_ref[rows, :] + _rms(acc) * gain
        _row_chunks(o_ref.shape[0], finish, RC // 2)


def _ffn(grp, x, mods, j, g_pre, g_post, wg, wu, wd, l, s):
    return pl.pallas_call(
        functools.partial(_ffn_kernel, j),
        out_shape=jax.ShapeDtypeStruct((grp.rows, D_MODEL), F32),
        grid=(grp.rows // TM_FFN, D_FF // TF),
        in_specs=[
            pl.BlockSpec((TM_FFN, D_MODEL), lambda i, f: (i, 0)),
            pl.BlockSpec((None, N_MOD, D_MODEL), lambda i, f: (grp.mod_row(i, TM_FFN), 0, 0)),
            pl.BlockSpec((1, D_MODEL), lambda i, f: (0, 0)),
            pl.BlockSpec((1, D_MODEL), lambda i, f: (0, 0)),
            pl.BlockSpec((None, None, D_MODEL, TF), lambda i, f: (l, s, 0, f)),
            pl.BlockSpec((None, None, D_MODEL, TF), lambda i, f: (l, s, 0, f)),
            pl.BlockSpec((None, None, TF, D_MODEL), lambda i, f: (l, s, f, 0)),
        ],
        out_specs=pl.BlockSpec((TM_FFN, D_MODEL), lambda i, f: (i, 0)),
        scratch_shapes=[pltpu.VMEM((TM_FFN, D_MODEL), BF16)],
        compiler_params=pltpu.CompilerParams(dimension_semantics=("parallel", "arbitrary"),
                                             vmem_limit_bytes=BIG_TILE_VMEM_LIMIT),
        name="ffn",
    )(x, mods, g_pre.reshape(1, D_MODEL), g_post.reshape(1, D_MODEL), wg, wu, wd)


def _inproj_kernel(w_transposed, x_ref, mod_ref, g_ref, w_ref, wl_ref, o_ref, t_ref, h_scr):
    k = pl.program_id(1)
    last = pl.num_programs(1) - 1

    def project(weights_ref):
        y = (_dot_nt if w_transposed else _dot)(h_scr[...], weights_ref[...])
        o_ref[...] = y.astype(o_ref.dtype)
        return y

    @pl.when(k == 0)
    def _():
        _modulated_norm(x_ref, h_scr, g_ref[...] * (1.0 + mod_ref[4:5, :]), mod_ref[3:4, :])
        project(w_ref)

    @pl.when((k > 0) & (k < last))
    def _():
        project(w_ref)

    @pl.when(k == last)
    def _():
        y = project(wl_ref)
        t_ref[...] = y[:, y.shape[1] - t_ref.shape[1]:]


def _inproj(grp, x, mods, g, w, w_last, last_block, tn, steps, n_tail, w_transposed=False):
    assert steps >= 2
    tm = TM_IN
    if w_transposed:
        w_specs = [pl.BlockSpec((tn, D_MODEL), lambda i, k: (jnp.minimum(k, steps - 2), 0)),
                   pl.BlockSpec((tn, D_MODEL), lambda i, k: (last_block, 0), pipeline_mode=pl.Buffered(1))]
    else:
        w_specs = [pl.BlockSpec((D_MODEL, tn), lambda i, k: (0, jnp.minimum(k, steps - 2))),
                   pl.BlockSpec((D_MODEL, tn), lambda i, k: (0, last_block), pipeline_mode=pl.Buffered(1))]
    return pl.pallas_call(
        functools.partial(_inproj_kernel, w_transposed),
        out_shape=(jax.ShapeDtypeStruct((grp.rows, steps * tn), BF16),
                   jax.ShapeDtypeStruct((grp.rows, n_tail), F32)),
        grid=(grp.rows // tm, steps),
        in_specs=[
            pl.BlockSpec((tm, D_MODEL), lambda i, k: (i, 0)),
            pl.BlockSpec((None, N_MOD, D_MODEL), lambda i, k: (grp.mod_row(i, tm), 0, 0)),
            pl.BlockSpec((1, D_MODEL), lambda i, k: (0, 0)),
            *w_specs,
        ],
        out_specs=(pl.BlockSpec((tm, tn), lambda i, k: (i, k)),
                   pl.BlockSpec((tm, n_tail), lambda i, k: (i, 0))),
        scratch_shapes=[pltpu.VMEM((tm, D_MODEL), BF16)],
        compiler_params=pltpu.CompilerParams(dimension_semantics=("parallel", "arbitrary"),
                                             vmem_limit_bytes=BIG_TILE_VMEM_LIMIT),
        name="inproj",
    )(x, mods, g.reshape(1, D_MODEL), w, w_last)


def _outproj_kernel(a_ref, b_ref, wa_ref, wb_ref, x_ref, mod_ref, g_ref, o_ref):
    y = _dot(a_ref[...], wa_ref[...]) + _dot(b_ref[...], wb_ref[...])
    o_ref[...] = x_ref[...] + _rms(y) * (mod_ref[5:6, :] * g_ref[...])


def _outproj(grp, a, b, w, x, mods, g):
    half = a.shape[1]
    tm = TM
    return pl.pallas_call(
        _outproj_kernel,
        out_shape=jax.ShapeDtypeStruct((grp.rows, D_MODEL), F32),
        grid=(grp.rows // tm,),
        in_specs=[
            pl.BlockSpec((tm, half), lambda i: (i, 0)),
            pl.BlockSpec((tm, half), lambda i: (i, 0)),
            pl.BlockSpec((half, D_MODEL), lambda i: (0, 0), pipeline_mode=pl.Buffered(1)),
            pl.BlockSpec((half, D_MODEL), lambda i: (1, 0), pipeline_mode=pl.Buffered(1)),
            pl.BlockSpec((tm, D_MODEL), lambda i: (i, 0)),
            pl.BlockSpec((None, N_MOD, D_MODEL), lambda i: (grp.mod_row(i, tm), 0, 0)),
            pl.BlockSpec((1, D_MODEL), lambda i: (0, 0)),
        ],
        out_specs=pl.BlockSpec((tm, D_MODEL), lambda i: (i, 0)),
        compiler_params=_cparams("parallel"),
        name="outproj",
    )(a, b, w, w, x, mods, g.reshape(1, D_MODEL))


def _rope_tables(L, rot_dim, lane0, width):
    half = rot_dim // 2
    inv = ROPE_BASE ** (-jnp.arange(0, half, 2, dtype=F32) / half)
    pos = jnp.arange(L)
    ang_r = (pos // GRID_W).astype(F32)[:, None] * inv
    ang_c = (pos % GRID_W).astype(F32)[:, None] * inv
    cr, sr, cc, sc = jnp.cos(ang_r), jnp.sin(ang_r), jnp.cos(ang_c), jnp.sin(ang_c)
    z = jnp.zeros_like(sr)
    cos = jnp.concatenate([cr, cr, cc, cc], axis=-1)
    sin_a = jnp.concatenate([-sr, z, -sc, z], axis=-1)
    sin_b = jnp.concatenate([z, sr, z, sc], axis=-1)
    pad = ((0, 0), (lane0, width - lane0 - rot_dim))
    return jnp.pad(cos, pad, constant_values=1.0), jnp.pad(sin_a, pad), jnp.pad(sin_b, pad)


def _rope(x, cos, sin_a, sin_b, nf):
    w = x.shape[-1]
    return x * cos + pltpu.roll(x, w - nf, 1) * sin_a + pltpu.roll(x, nf, 1) * sin_b


DFT_SPLIT = 32


def _dft_matrices(L):
    s = jnp.arange(L, dtype=jnp.int32)[None, :]

    def trig(k):
        ang = ((k[:, None] * s) % (2 * L)).astype(F32) * (math.pi / L)
        return jnp.cos(ang), jnp.sin(ang)

    c1, s1 = trig(jnp.arange(0, L, DFT_SPLIT, dtype=jnp.int32))
    c0, s0 = trig(jnp.arange(DFT_SPLIT, dtype=jnp.int32))
    cos = (c1[:, None, :] * c0[None] - s1[:, None, :] * s0[None]).reshape(L, L)
    sin = (s1[:, None, :] * c0[None] + c1[:, None, :] * s0[None]).reshape(L, L)
    nyq = (1 - 2 * (jnp.arange(L, dtype=jnp.int32) % 2)).astype(F32)
    k = jnp.arange(L, dtype=jnp.int32)[:, None]
    msin = jnp.where(k == 0, nyq[None, :], -sin)
    msin_t = jnp.where(s == 0, nyq[:, None], -sin)
    return cos.astype(BF16), msin.astype(BF16), msin_t.astype(BF16)


def _filter_kernel(L, z_ref, w1_ref, b1_ref, w2_ref, b2_ref, fr_ref, w3f_ref, w3b_ref, dl_ref, cos_ref, msin_ref,
                   ka_ref, kb_ref, kc_ref, h_scr):
    z = z_ref[...]

    @pl.when((pl.program_id(0) == 0) & (pl.program_id(1) == 0))
    def _():
        fr = fr_ref[...]
        h1 = jnp.sin(fr * (jnp.dot(z, w1_ref[...], precision=HIGHEST, preferred_element_type=F32) + b1_ref[...]))
        h_scr[...] = jnp.sin(
            fr * (jnp.dot(h1, w2_ref[...], precision=HIGHEST, preferred_element_type=F32) + b2_ref[...]))

    h = h_scr[...]
    decay = jnp.exp(-z[:, 0:1] * dl_ref[...])
    hf = jnp.dot(h, w3f_ref[...], precision=HIGHEST, preferred_element_type=F32) * decay
    hb = jnp.dot(h, w3b_ref[...], precision=HIGHEST, preferred_element_type=F32) * decay
    row = lax.broadcasted_iota(jnp.int32, hf.shape, 0)
    row0 = row == 0
    hb = jnp.where(row0, 0.0, hb)
    even = hf + hb
    re = _dot(cos_ref[...], even.astype(BF16))
    im = _dot(msin_ref[...], (hf - hb).astype(BF16))
    nyq = jnp.sum(jnp.where(row % 2 == 0, even, -even), axis=0, keepdims=True)
    sc = jnp.where(row0, 0.5 / L, 1.0 / L)
    ka_ref[...] = re * sc
    kb_ref[...] = jnp.where(row0, 0.0, im) * sc
    kc_ref[...] = jnp.where(row0, nyq, re) * sc


def _hyena_spectra(L, tc, dft, f_w1, f_b1, f_w2, f_b2, f_w3, f_freq):
    t = jnp.linspace(0.0, 1.0, L, dtype=F32)[:, None]
    bands = (POS_EMB - 1) // 2
    w = 2.0 * math.pi * jnp.arange(L, dtype=F32)[:, None] / L
    fb = jnp.linspace(1e-4, bands - 1, bands, dtype=F32)[None, :]
    z = jnp.concatenate([t, jnp.cos(fb * w), -jnp.sin(fb * w)], axis=-1)
    z = jnp.pad(z, ((0, 0), (0, 128 - POS_EMB)))
    w1 = jnp.pad(f_w1, ((0, 128 - POS_EMB), (0, 0)))
    deltas = jnp.abs(jnp.linspace(math.log(HY_TARGET) / HY_SLOW_DECAY,
                                  math.log(HY_TARGET) / HY_FAST_DECAY, HY_W, dtype=F32))[None, :]
    nct = HY_W // tc
    fo = FILTER_ORDER
    row = lambda a: a.reshape(1, fo)
    kshape = jax.ShapeDtypeStruct((HY_ORDER, L, HY_W), F32)
    kspec = pl.BlockSpec((None, L, tc), lambda n, c: (n, 0, c))
    const = lambda shape: pl.BlockSpec(shape, lambda n, c: (0, 0))
    return pl.pallas_call(
        functools.partial(_filter_kernel, L),
        out_shape=(kshape, kshape, kshape),
        grid=(HY_ORDER, nct),
        in_specs=[
            const((L, 128)), const((128, fo)), const((1, fo)), const((fo, fo)), const((1, fo)), const((1, fo)),
            pl.BlockSpec((fo, tc), lambda n, c: (0, 2 * n * nct + c)),
            pl.BlockSpec((fo, tc), lambda n, c: (0, (2 * n + 1) * nct + c)),
            pl.BlockSpec((1, tc), lambda n, c: (0, c)),
            const((L, L)), const((L, L)),
        ],
        out_specs=(kspec, kspec, kspec),
        scratch_shapes=[pltpu.VMEM((L, fo), F32)],
        compiler_params=_cparams("arbitrary", "arbitrary"),
        name="hyena_filter",
    )(z, w1, row(f_b1), f_w2, row(f_b2), row(f_freq), f_w3, f_w3, deltas, dft[0], dft[1])


def _hyena_kernel(L, nseq, uv_ref, ug0_ref, ug1_ref, cwv_ref, cwg0_ref, cwg1_ref, cbv_ref, cbg0_ref, cbg1_ref,
                  hb_ref, ka_ref, kb_ref, kc_ref, cos_ref, msin_ref, msint_ref, o_ref):
    row = lax.broadcasted_iota(jnp.int32, (L, uv_ref.shape[1]), 0)
    first, last = row == 0, row == L - 1
    seqs = [slice(i * L, (i + 1) * L) for i in range(nseq)]

    def short_conv(u_ref, rows, w_ref, b_ref):
        u = u_ref[rows, :].astype(F32)
        prev = jnp.where(first, 0.0, pltpu.roll(u, 1, 0))
        nxt = jnp.where(last, 0.0, pltpu.roll(u, L - 1, 0))
        return b_ref[...] + prev * w_ref[0:1, :] + u * w_ref[1:2, :] + nxt * w_ref[2:3, :]

    z = [short_conv(uv_ref, r, cwv_ref, cbv_ref) for r in seqs]
    gates = [(short_conv(ug0_ref, r, cwg0_ref, cbg0_ref), short_conv(ug1_ref, r, cwg1_ref, cbg1_ref))
             for r in seqs]
    for n in range(HY_ORDER):
        zb = [zi.astype(BF16) for zi in z]
        zre = [_dot(cos_ref[...], b) for b in zb]
        zim = [_dot(msin_ref[...], b) for b in zb]
        ka, kb, kc = ka_ref[n], kb_ref[n], kc_ref[n]
        yre = [(re * ka - im * kb).astype(BF16) for re, im in zip(zre, zim)]
        yim = [(re * kb + im * kc).astype(BF16) for re, im in zip(zre, zim)]
        conv = [_dot(cos_ref[...], a) + _dot(msint_ref[...], b) for a, b in zip(yre, yim)]
        z = [g[n] * (cv + hb_ref[n:n + 1, :] * zi) for g, cv, zi in zip(gates, conv, z)]
    for r, zi in zip(seqs, z):
        o_ref[r, :] = zi.astype(o_ref.dtype)


def _hyena(grp, p, tc, nseq, spectra, dft, conv_w, conv_b, h_bias):
    L = grp.L
    nct = HY_W // tc
    ka, kb, kc = spectra
    u_spec = lambda j: pl.BlockSpec((nseq * L, tc), lambda c, b: (b, j * nct + c))
    cw_spec = lambda j: pl.BlockSpec((SHORT_CONV, tc), lambda c, b: (0, j * nct + c))
    cb_spec = lambda j: pl.BlockSpec((1, tc), lambda c, b: (0, j * nct + c))
    k_spec = pl.BlockSpec((HY_ORDER, L, tc), lambda c, b: (0, 0, c))
    m_spec = pl.BlockSpec((L, L), lambda c, b: (0, 0))
    cb = conv_b.reshape(1, -1)
    return pl.pallas_call(
        functools.partial(_hyena_kernel, L, nseq),
        out_shape=jax.ShapeDtypeStruct((grp.rows, HY_W), BF16),
        grid=(nct, grp.nb // nseq),
        in_specs=[
            u_spec(0), u_spec(1), u_spec(2), cw_spec(0), cw_spec(1), cw_spec(2),
            cb_spec(0), cb_spec(1), cb_spec(2),
            pl.BlockSpec((HY_ORDER, tc), lambda c, b: (0, c)),
            k_spec, k_spec, k_spec, m_spec, m_spec, m_spec,
        ],
        out_specs=pl.BlockSpec((nseq * L, tc), lambda c, b: (b, c)),
        compiler_params=_cparams("parallel", "arbitrary"),
        name="hyena",
    )(p, p, p, conv_w, conv_w, conv_w, cb, cb, cb, h_bias, ka, kb, kc, *dft)


def _gqa_kernel(L, nseq, window, has_ctx, has_rope, emit_kv, *refs):
    it = iter(refs)
    q_ref, k_ref, v_ref, sink_ref = next(it), next(it), next(it), next(it)
    if has_ctx:
        kc_ref, vc_ref = next(it), next(it)
    if has_rope:
        cos_ref, sa_ref, sb_ref = next(it), next(it), next(it)
    o_ref = next(it)
    if emit_kv:
        kn_ref, vn_ref = next(it), next(it)

    qb = HEAD_DIM
    nf = HEAD_DIM // 4
    gw = ATT_GROUP * HEAD_DIM
    hps = k_ref.shape[1] // HEAD_DIM
    sink_row = sink_ref[...]
    sink_lane = lax.broadcasted_iota(jnp.int32, sink_row.shape, 1)
    band_bias = {}

    def bias_for(lo, hi, i):
        key = (lo - i * qb, hi - lo)
        if key not in band_bias:
            shape = (ATT_GROUP * qb, hi - lo)
            rel = key[0] + lax.broadcasted_iota(jnp.int32, shape, 1) - lax.broadcasted_iota(jnp.int32, shape, 0) % qb
            band_bias[key] = jnp.where(jnp.abs(rel) <= window, 0.0, -1e30)
        return band_bias[key]

    for sq in range(nseq):
        r0 = sq * L
        for hk in range(hps):
            head = pl.program_id(1) * hps + hk
            k = k_ref[r0:r0 + L, hk * HEAD_DIM:(hk + 1) * HEAD_DIM]
            v = v_ref[r0:r0 + L, hk * HEAD_DIM:(hk + 1) * HEAD_DIM]
            if emit_kv:
                kn_ref[sq, hk] = k
                vn_ref[sq, hk] = v
            if has_rope:
                k = _rope(k, cos_ref[...], sa_ref[...], sb_ref[...], nf)
            k = k.astype(BF16)
            v = jnp.concatenate([v.astype(BF16), jnp.ones((L, HEAD_DIM), BF16)], axis=1)
            if has_ctx:
                kc = kc_ref[hk].astype(BF16)
                vc = jnp.concatenate([vc_ref[hk].astype(BF16), jnp.ones((PAST_LEN, HEAD_DIM), BF16)], axis=1)
            sinks = [jnp.sum(jnp.where(sink_lane == head * ATT_GROUP + g, sink_row, 0.0), axis=1, keepdims=True)
                     for g in range(ATT_GROUP)]
            sink = jnp.concatenate([jnp.broadcast_to(s, (qb, 1)) for s in sinks], axis=0)

            for i in range(L // qb):
                rows = slice(i * qb, (i + 1) * qb)
                out_rows = slice(r0 + i * qb, r0 + (i + 1) * qb)
                qs = []
                for g in range(ATT_GROUP):
                    qg = q_ref[out_rows, hk * gw + g * HEAD_DIM:hk * gw + (g + 1) * HEAD_DIM].astype(F32)
                    if has_rope:
                        qg = _rope(qg, cos_ref[rows, :], sa_ref[rows, :], sb_ref[rows, :], nf)
                    qs.append((qg * ATT_SCALE).astype(BF16))
                q = jnp.concatenate(qs, axis=0)
                if window is None:
                    lo, hi = 0, L
                else:
                    lo, hi = max(0, (i - 1) * qb), min(L, (i + 2) * qb)
                s = _dot_nt(q, k[lo:hi])
                if window is not None:
                    s = s + bias_for(lo, hi, i)
                m = jnp.maximum(jnp.max(s, axis=-1, keepdims=True), sink)
                if has_ctx:
                    sc = _dot_nt(q, kc)
                    m = jnp.maximum(m, jnp.max(sc, axis=-1, keepdims=True))
                oa = _dot(jnp.exp(s - m).astype(BF16), v[lo:hi])
                if has_ctx:
                    oa = oa + _dot(jnp.exp(sc - m).astype(BF16), vc)
                o = oa[:, :HEAD_DIM] / (oa[:, HEAD_DIM:] + jnp.exp(sink - m))
                for g in range(ATT_GROUP):
                    o_ref[out_rows, hk * gw + g * HEAD_DIM:hk * gw + (g + 1) * HEAD_DIM] = (
                        o[g * qb:(g + 1) * qb].astype(o_ref.dtype))


def _gqa(grp, p, kv, sink, hps, nseq=1, window=None, ctx=None, rope=None, emit_kv=False):
    assert nseq == 1 or (ctx is None and rope is None)
    L = grp.L
    rows = nseq * L
    qw = hps * ATT_GROUP * HEAD_DIM
    kw = hps * HEAD_DIM
    nh = ATT_KV_HEADS // hps
    in_specs = [
        pl.BlockSpec((rows, qw), lambda b, h: (b, 3 * HY_W // qw + h)),
        pl.BlockSpec((rows, kw), lambda b, h: (b, h)),
        pl.BlockSpec((rows, kw), lambda b, h: (b, nh + h)),
        pl.BlockSpec((1, ATT_HEADS), lambda b, h: (0, 0)),
    ]
    args = [p, kv, kv, sink.reshape(1, ATT_HEADS)]
    if ctx is not None:
        kc, vc, e = ctx
        spec = pl.BlockSpec((None, None, hps, PAST_LEN, HEAD_DIM), lambda b, h: (b, e, h, 0, 0))
        in_specs += [spec, spec]
        args += [kc, vc]
    if rope is not None:
        in_specs += [pl.BlockSpec((L, HEAD_DIM), lambda b, h: (0, 0))] * 3
        args += list(rope)
    out_shape = [jax.ShapeDtypeStruct((grp.rows, ATT_HEADS * HEAD_DIM), BF16)]
    out_specs = [pl.BlockSpec((rows, qw), lambda b, h: (b, h))]
    if emit_kv:
        kv_shape = jax.ShapeDtypeStruct((grp.nb, 1, ATT_KV_HEADS, L, HEAD_DIM), F32)
        kv_spec = pl.BlockSpec((nseq, None, hps, L, HEAD_DIM), lambda b, h: (b, 0, h, 0, 0))
        out_shape += [kv_shape, kv_shape]
        out_specs += [kv_spec, kv_spec]
    return pl.pallas_call(
        functools.partial(_gqa_kernel, L, nseq, window, ctx is not None, rope is not None, emit_kv),
        out_shape=tuple(out_shape),
        grid=(grp.nb // nseq, nh),
        in_specs=in_specs,
        out_specs=tuple(out_specs),
        compiler_params=_cparams("parallel", "parallel"),
        name="gqa",
    )(*args)


def _chunk_cumsum(x, reverse):
    n = x.shape[0]
    pos = lax.broadcasted_iota(jnp.int32, x.shape, 0) % CHUNK
    s = 1
    while s < CHUNK:
        if reverse:
            x = x + jnp.where(pos < CHUNK - s, pltpu.roll(x, n - s, 0), 0.0)
        else:
            x = x + jnp.where(pos >= s, pltpu.roll(x, s, 0), 0.0)
        s *= 2
    return x


def _hgrn_kernel(L, has_state, emit_state, *refs):
    it = iter(refs)
    q_ref, ff_ref, fb_ref, i_ref, g_ref, lb_ref, norm_ref = (next(it) for _ in range(7))
    if has_state:
        s0_ref = next(it)
    o_ref = next(it)
    if emit_state:
        s_ref = next(it)
    acc_scr = next(it)

    a = lb_ref[...]
    e = jnp.exp(a - jnp.max(a, axis=0, keepdims=True))
    lb_all = e[1] / (e[0] + e[1])
    ci = lax.broadcasted_iota(jnp.int32, (CHUNK, CHUNK), 0)
    cj = lax.broadcasted_iota(jnp.int32, (CHUNK, CHUNK), 1)
    n_chunks = L // CHUNK
    chunks = [slice(n * CHUNK, (n + 1) * CHUNK) for n in range(n_chunks)]

    for hh in range(q_ref.shape[1] // HEAD_DIM):
        hc = slice(hh * HEAD_DIM, (hh + 1) * HEAD_DIM)
        q = _silu(q_ref[:, hc].astype(F32))
        for d, fz_ref in enumerate((ff_ref, fb_ref)):
            lbd = lb_all[d:d + 1, hc]
            keep = (cj <= ci) if d == 0 else (cj >= ci)
            qds, atts, decay, own = [], [], [], []
            for rows in chunks:
                f = lbd + (1.0 - lbd) * jax.nn.sigmoid(fz_ref[rows, hc].astype(F32))
                b = _chunk_cumsum(jnp.log(f), reverse=(d == 1))
                qd = (q[rows] * jnp.exp(b)).astype(BF16)
                kd32 = (1.0 - f) * jnp.exp(-b)
                dc = jnp.exp(b[CHUNK - 1:CHUNK] if d == 0 else b[0:1])
                qds.append(qd)
                decay.append(dc)
                atts.append(jnp.where(keep, _dot_nt(qd, kd32.astype(BF16)), 0.0).astype(BF16))
                own.append(_dot_tn(i_ref[rows, hc], (kd32 * dc).astype(BF16)))
            st = s0_ref[d, hh].T if has_state else jnp.zeros((HEAD_DIM, HEAD_DIM), F32)
            entering = [None] * n_chunks
            for n in (range(n_chunks) if d == 0 else range(n_chunks - 1, -1, -1)):
                entering[n] = st.astype(BF16)
                st = st * decay[n] + own[n]
            if emit_state:
                s_ref[d, hh] = st.T
            for n, rows in enumerate(chunks):
                o = _dot(atts[n], i_ref[rows, hc]) + _dot_nt(qds[n], entering[n])
                if d == 0:
                    acc_scr[rows, :] = o
                else:
                    acc_scr[rows, :] += o
        o = _rms(acc_scr[...]) * norm_ref[...] * _silu(g_ref[:, hc].astype(F32))
        o_ref[:, hc] = o.astype(o_ref.dtype)


def _hgrn(grp, p, hg_lb, norm_g, hps, state=None, emit_state=False):
    L = grp.L
    hw = hps * HEAD_DIM
    nh = HG_HEADS // hps
    col = lambda j: pl.BlockSpec((L, hw), lambda b, h: (b, j * nh + h))
    in_specs = [col(0), col(1), col(2), col(3), col(4),
                pl.BlockSpec((hg_lb.shape[0], 2, hw), lambda b, h: (0, 0, h)),
                pl.BlockSpec((1, HEAD_DIM), lambda b, h: (0, 0))]
    args = [p, p, p, p, p, hg_lb, norm_g.reshape(1, HEAD_DIM)]
    st_spec = lambda o: pl.BlockSpec((None, None, 2, hps, HEAD_DIM, HEAD_DIM), lambda b, h: (b, o, 0, h, 0, 0))
    if state is not None:
        s0, o = state
        in_specs.append(st_spec(o))
        args.append(s0)
    out_shape = [jax.ShapeDtypeStruct((grp.rows, HG_W), BF16)]
    out_specs = [pl.BlockSpec((L, hw), lambda b, h: (b, h))]
    if emit_state:
        out_shape.append(jax.ShapeDtypeStruct((grp.nb, 1, 2, HG_HEADS, HEAD_DIM, HEAD_DIM), F32))
        out_specs.append(st_spec(0))
    return pl.pallas_call(
        functools.partial(_hgrn_kernel, L, state is not None, emit_state),
        out_shape=tuple(out_shape),
        grid=(grp.nb, nh),
        in_specs=in_specs,
        out_specs=tuple(out_specs),
        scratch_shapes=[pltpu.VMEM((L, HEAD_DIM), F32)],
        compiler_params=_cparams("parallel", "parallel"),
        name="hgrn",
    )(*args)


MLA_QW = 256
KR_W = 128
OD_TAIL = 1024


def _mla_prep_kernel(has_rope, emit_kr, *refs):
    it = iter(refs)
    ql_ref, kvl_ref, kr_ref, qn_ref, kvn_ref, wq_ref = (next(it) for _ in range(6))
    if has_rope:
        kc_ref, ksa_ref, ksb_ref = (next(it) for _ in range(3))
    q_ref, ckv_ref, kro_ref = next(it), next(it), next(it)
    if emit_kr:
        krn_ref = next(it)
        krn_ref[...] = kr_ref[:, :ROPE]

    nf = ROPE // 4
    qn = (_rms(ql_ref[...]) * qn_ref[...]).astype(BF16)
    q = _dot(qn, wq_ref[...])
    for h in range(MLA_HEADS):
        nope = slice(h * MLA_QW, h * MLA_QW + NOPE)
        rot = slice(h * MLA_QW + NOPE, (h + 1) * MLA_QW)
        q_ref[:, nope] = q[:, nope].astype(q_ref.dtype)
        qr = q[:, rot]
        if has_rope:
            qr = _rope(qr, kc_ref[...], ksa_ref[...], ksb_ref[...], nf)
        q_ref[:, rot] = qr.astype(q_ref.dtype)
    ckv_ref[...] = _rms(kvl_ref[...]) * kvn_ref[...]
    kr = kr_ref[...]
    if has_rope:
        kr = _rope(kr, kc_ref[...], ksa_ref[...], ksb_ref[...], nf)
    kro_ref[...] = kr.astype(kro_ref.dtype)


def _mla_prep(grp, tail, q_norm, kv_norm, wq, rope=None, emit_kr=False):
    tm = min(512, grp.L)
    per = grp.L // tm
    n_rows = grp.rows
    in_specs = [
        pl.BlockSpec((tm, Q_LORA), lambda i: (i, 0)),
        pl.BlockSpec((tm, KV_LORA), lambda i: (i, Q_LORA // KV_LORA)),
        pl.BlockSpec((tm, KR_W), lambda i: (i, (Q_LORA + KV_LORA) // KR_W)),
        pl.BlockSpec((1, Q_LORA), lambda i: (0, 0)),
        pl.BlockSpec((1, KV_LORA), lambda i: (0, 0)),
        pl.BlockSpec((Q_LORA, MLA_HEADS * MLA_QW), lambda i: (0, 0)),
    ]
    args = [tail, tail, tail, q_norm.reshape(1, Q_LORA), kv_norm.reshape(1, KV_LORA), wq]
    if rope is not None:
        in_specs += [pl.BlockSpec((tm, KR_W), lambda i: (i % per, 0))] * 3
        args += list(rope)
    out_shape = [jax.ShapeDtypeStruct((n_rows, MLA_HEADS * MLA_QW), BF16),
                 jax.ShapeDtypeStruct((n_rows, KV_LORA), F32),
                 jax.ShapeDtypeStruct((n_rows, KR_W), BF16)]
    out_specs = [pl.BlockSpec((tm, MLA_HEADS * MLA_QW), lambda i: (i, 0)),
                 pl.BlockSpec((tm, KV_LORA), lambda i: (i, 0)),
                 pl.BlockSpec((tm, KR_W), lambda i: (i, 0))]
    if emit_kr:
        out_shape.append(jax.ShapeDtypeStruct((n_rows, ROPE), F32))
        out_specs.append(pl.BlockSpec((tm, ROPE), lambda i: (i, 0)))
    return pl.pallas_call(
        functools.partial(_mla_prep_kernel, rope is not None, emit_kr),
        out_shape=tuple(out_shape),
        grid=(n_rows // tm,),
        in_specs=in_specs,
        out_specs=tuple(out_specs),
        compiler_params=_cparams("parallel"),
        name="mla_prep",
    )(*args)


def _mla_attn_kernel(L, has_ctx, *refs):
    it = iter(refs)
    q_ref, ckv_ref, kr_ref, wkv_ref = (next(it) for _ in range(4))
    if has_ctx:
        cckv_ref, ckr_ref = next(it), next(it)
    o_ref = next(it)

    ckv = ckv_ref[...].astype(BF16)
    kr = kr_ref[...]
    if has_ctx:
        ckv = jnp.concatenate([ckv, cckv_ref[...].astype(BF16)], axis=0)
        kr = jnp.concatenate([kr, ckr_ref[...].astype(BF16)], axis=0)
    ones = jnp.ones((ckv.shape[0], V_DIM), BF16)
    qb = min(L, 256)
    for hh in range(q_ref.shape[1] // MLA_QW):
        qc = slice(hh * MLA_QW, (hh + 1) * MLA_QW)
        oc = slice(hh * V_DIM, (hh + 1) * V_DIM)
        kv = _dot(ckv, wkv_ref[:, qc])
        kh = jnp.concatenate([kv[:, :NOPE].astype(BF16), kr], axis=1)
        vh = jnp.concatenate([kv[:, NOPE:].astype(BF16), ones], axis=1)
        for i in range(L // qb):
            rows = slice(i * qb, (i + 1) * qb)
            s = _dot_nt(q_ref[rows, qc], kh) * MLA_SCALE
            m = jnp.max(s, axis=-1, keepdims=True)
            oa = _dot(jnp.exp(s - m).astype(BF16), vh)
            o_ref[rows, oc] = (oa[:, :V_DIM] / oa[:, V_DIM:]).astype(o_ref.dtype)


def _mla_attn(grp, q, ckv, kr, wkv, hps, ctx=None):
    L = grp.L
    in_specs = [
        pl.BlockSpec((L, hps * MLA_QW), lambda b, h: (b, h)),
        pl.BlockSpec((L, KV_LORA), lambda b, h: (b, 0)),
        pl.BlockSpec((L, KR_W), lambda b, h: (b, 0)),
        pl.BlockSpec((KV_LORA, hps * (NOPE + V_DIM)), lambda b, h: (0, h)),
    ]
    args = [q, ckv, kr, wkv]
    if ctx is not None:
        cckv, ckr, o = ctx
        in_specs += [pl.BlockSpec((None, None, PAST_LEN, KV_LORA), lambda b, h: (b, o, 0, 0)),
                     pl.BlockSpec((None, None, PAST_LEN, KR_W), lambda b, h: (b, o, 0, 0))]
        args += [cckv, ckr]
    return pl.pallas_call(
        functools.partial(_mla_attn_kernel, L, ctx is not None),
        out_shape=jax.ShapeDtypeStruct((grp.rows, MLA_HEADS * V_DIM), BF16),
        grid=(grp.nb, MLA_HEADS // hps),
        in_specs=in_specs,
        out_specs=pl.BlockSpec((L, hps * V_DIM), lambda b, h: (b, h)),
        compiler_params=_cparams("parallel", "parallel"),
        name="mla_attn",
    )(*args)


def kernel(x_prompt, x_sample, c, c_ctx, cache_attn_k, cache_attn_v, cache_mla_ckv, cache_mla_krope, state_hgrn, mod_w, mod_b, norm_g, ffn_wg, ffn_wu, ffn_wd, ev_w_in, ev_w_out, hy_conv_w, hy_conv_b, hy_f_w1, hy_f_b1, hy_f_w2, hy_f_b2, hy_f_w3, hy_f_freq, hy_bias, attn_sink, od_w_in, od_w_out, hg_lb, hg_norm, mla_q_norm, mla_w_qb, mla_kv_norm, mla_w_kvb):
    depth = mod_w.shape[0]
    groups = (PROMPT, LATENT)
    xs = [x_prompt.reshape(PROMPT.rows, D_MODEL), x_sample.reshape(LATENT.rows, D_MODEL)]
    cvec = jnp.concatenate([c_ctx[None, :], c, jnp.zeros((MOD_ROWS - 1 - DEC_BATCH, D_MODEL), F32)], axis=0)
    mods_all = _modulation(cvec, mod_w, mod_b)

    wg, wu, wd = ffn_wg, ffn_wu, ffn_wd
    hy_tc = {SEQ: 512, DEC_SEQ: 256}
    hy_nseq = {SEQ: 4, DEC_SEQ: 2}

    new_k = new_v = new_ckv = new_kr = new_s = None
    for l in range(depth):
        mods = mods_all[l]
        xs = [_ffn(grp, x, mods, 0, norm_g[l, 0], norm_g[l, 1], wg, wu, wd, l, 0) for grp, x in zip(groups, xs)]
        if l % 2 == 0:
            e = l // 2
            w_in = ev_w_in[e].astype(BF16)
            ev_steps = EV_IN // TN_EVEN
            w_out = ev_w_out[e].astype(BF16)
            kv_cols = 2 * ATT_KV_HEADS * HEAD_DIM
            mix = []
            for grp, x in zip(groups, xs):
                p, kv = _inproj(grp, x, mods, norm_g[l, 2], w_in, w_in, ev_steps - 1, TN_EVEN, ev_steps, kv_cols)
                dft = _dft_matrices(grp.L)
                tc = hy_tc[grp.L]
                spectra = _hyena_spectra(grp.L, tc, dft, hy_f_w1[e], hy_f_b1[e], hy_f_w2[e], hy_f_b2[e],
                                         hy_f_w3[e], hy_f_freq[e])
                hy = _hyena(grp, p, tc, hy_nseq[grp.L], spectra, dft, hy_conv_w[e], hy_conv_b[e], hy_bias[e])
                if grp.latent:
                    rope = _rope_tables(grp.L, HEAD_DIM, 0, HEAD_DIM)
                    (att,) = _gqa(grp, p, kv, attn_sink[e], 1, window=WINDOW,
                                  ctx=(cache_attn_k, cache_attn_v, e), rope=rope)
                else:
                    att, new_k, new_v = _gqa(grp, p, kv, attn_sink[e], 2, nseq=2, emit_kv=True)
                mix.append((hy, att))
        else:
            o = l // 2
            n_main = OD_IN_PAD - TN_ODD
            w_in = od_w_in[o].T.astype(BF16)
            w_in_last = jnp.pad(w_in[n_main:], ((0, OD_IN_PAD - OD_IN), (0, 0)))
            od_steps = OD_IN_PAD // TN_ODD
            w_out = od_w_out[o].astype(BF16)
            wq = mla_w_qb[o].reshape(Q_LORA, MLA_HEADS, NOPE + ROPE)
            wq = jnp.pad(wq, ((0, 0), (0, 0), (0, MLA_QW - NOPE - ROPE))).reshape(Q_LORA, -1).astype(BF16)
            wkv = mla_w_kvb[o].astype(BF16)
            mix = []
            for grp, x in zip(groups, xs):
                p, tail = _inproj(grp, x, mods, norm_g[l, 2], w_in, w_in_last, 0, TN_ODD, od_steps, OD_TAIL,
                                  w_transposed=True)
                if grp.latent:
                    (hg,) = _hgrn(grp, p, hg_lb, hg_norm[o], 4, state=(state_hgrn, o))
                    rope = _rope_tables(grp.L, ROPE, 0, KR_W)
                    q, ckv, kr = _mla_prep(grp, tail, mla_q_norm[o], mla_kv_norm[o], wq, rope=rope)
                    ckr = jnp.pad(cache_mla_krope, ((0, 0), (0, 0), (0, 0), (0, KR_W - ROPE)))
                    att = _mla_attn(grp, q, ckv, kr, wkv, 8, ctx=(cache_mla_ckv, ckr, o))
                else:
                    hg, new_s = _hgrn(grp, p, hg_lb, hg_norm[o], 4, emit_state=True)
                    q, ckv, kr, kr_raw = _mla_prep(grp, tail, mla_q_norm[o], mla_kv_norm[o], wq, emit_kr=True)
                    att = _mla_attn(grp, q, ckv, kr, wkv, 8)
                    new_ckv = ckv.reshape(BATCH, 1, SEQ, KV_LORA)
                    new_kr = kr_raw.reshape(BATCH, 1, SEQ, ROPE)
                mix.append((hg, att))
        xs = [_outproj(grp, a, b, w_out, x, mods, norm_g[l, 3])
              for grp, x, (a, b) in zip(groups, xs, mix)]
        xs = [_ffn(grp, x, mods, 2, norm_g[l, 4], norm_g[l, 5], wg, wu, wd, l, 1) for grp, x in zip(groups, xs)]

    y_prompt = xs[0].reshape(BATCH, SEQ, D_MODEL)
    y_sample = xs[1].reshape(DEC_BATCH, DEC_SEQ, D_MODEL)
    return (y_prompt, y_sample, new_k, new_v, new_ckv, new_kr, new_s)
```
